```python
import math
import jax, jax.numpy as jnp
from jax import lax
import numpy as np

D_MODEL = 1024
BATCH = 8
SEQ = 2048
DEPTH = 2
DEC_BATCH = 128
DEC_SEQ = 1
PAST_LEN = 16384
PAGE_SIZE = 128

MIX_WIDTH = D_MODEL
LRU_WIDTH = (3 * MIX_WIDTH) // 8
LRU_HEADS = 6
LRU_HEAD_DIM = LRU_WIDTH // LRU_HEADS
CONV_WIDTH = 4
LRU_C = 8.0
MLSTM_WIDTH = (3 * MIX_WIDTH) // 8
MLSTM_HEADS = 4
MLSTM_HEAD_DIM = MLSTM_WIDTH // MLSTM_HEADS
MLSTM_CHUNK = 128
POOL_WIDTH = MIX_WIDTH - LRU_WIDTH - MLSTM_WIDTH
POOL_WINDOWS = (2, 4, 8, 16)
POOL_GROUPS = 4
POOL_GROUP_DIM = POOL_WIDTH // POOL_GROUPS
POOL_BUF = 15
D_FF = 4 * D_MODEL
EPS = 1e-6
IN_COLS = 2 * LRU_WIDTH + 4 * MLSTM_WIDTH + 2 * MLSTM_HEADS + POOL_WIDTH

kernel_name = "hybrid_rglru_mlstm_pool_decode_step"


def _in_splits():
    widths = (LRU_WIDTH, LRU_WIDTH, MLSTM_WIDTH, MLSTM_WIDTH, MLSTM_WIDTH, MLSTM_WIDTH,
              MLSTM_HEADS, MLSTM_HEADS, POOL_WIDTH)
    return tuple(int(s) for s in np.cumsum(widths)[:-1])


def rmsnorm(x, g):
    xf = x.astype(jnp.float32)
    y = xf * lax.rsqrt(jnp.mean(xf * xf, axis=-1, keepdims=True) + EPS)
    return (y * g).astype(x.dtype)


def causal_dwconv(x, buf, w, b):
    L = x.shape[1]
    xp = jnp.concatenate([buf.astype(x.dtype), x], axis=1)
    y = b
    for j in range(CONV_WIDTH):
        y = y + w[j] * xp[:, j:j + L]
    return y, xp[:, L:]


def rglru(x, h0, w_a, b_a, w_x, b_x, lam, is_start):
    B, L, W = x.shape
    xf = x.astype(jnp.float32)
    xh = xf.reshape(B, L, LRU_HEADS, LRU_HEAD_DIM)
    r = jax.nn.sigmoid(jnp.einsum('blhi,hij->blhj', xh, w_a) + b_a).reshape(B, L, W)
    i = jax.nn.sigmoid(jnp.einsum('blhi,hij->blhj', xh, w_x) + b_x).reshape(B, L, W)
    log_a = LRU_C * r * jax.nn.log_sigmoid(lam.astype(jnp.float32))
    a = jnp.exp(log_a)
    mult = jnp.sqrt(-jnp.expm1(2.0 * log_a))
    if is_start:
        mult = mult.at[:, 0].set(1.0)
    bterm = mult * i * xf
    bterm = bterm.at[:, 0].add(a[:, 0] * h0.astype(jnp.float32))

    def combine(lhs, rhs):
        a1, b1 = lhs
        a2, b2 = rhs
        return a1 * a2, a2 * b1 + b2

    _, h = lax.associative_scan(combine, (a, bterm), axis=1)
    return h, h[:, -1]


def mlstm(q, k, v, ig, fg, C0, n0, m0):
    B, L, H, d = q.shape
    Lc = MLSTM_CHUNK if L % MLSTM_CHUNK == 0 else L
    NC = L // Lc
    f32 = jnp.float32

    def chunks(t):
        return t.astype(f32).reshape(B, NC, Lc, H, d).transpose(1, 0, 3, 2, 4)

    def gchunks(t):
        return t.astype(f32).reshape(B, NC, Lc, H).transpose(1, 0, 3, 2)

    qc = chunks(q)
    kc = chunks(k) / math.sqrt(d)
    vc = chunks(v)
    ic = gchunks(ig)
    lfc = gchunks(jax.nn.log_sigmoid(fg.astype(f32)))
    mask = jnp.tril(jnp.ones((Lc, Lc), dtype=bool))

    def step(carry, inp):
        C, n, m = carry
        qb, kb, vb, ib, lfb = inp
        bcum = jnp.cumsum(lfb, axis=-1)
        D = bcum[..., :, None] - bcum[..., None, :] + ib[..., None, :]
        D = jnp.where(mask, D, -jnp.inf)
        g = bcum + m[..., None]
        m_t = jnp.maximum(g, jnp.max(D, axis=-1))
        inter = jnp.exp(g - m_t)
        S = jnp.einsum('bhtd,bhsd->bhts', qb, kb) * jnp.exp(D - m_t[..., None])
        num = jnp.einsum('bhts,bhsd->bhtd', S, vb) + inter[..., None] * jnp.einsum('bhvk,bhtk->bhtv', C, qb)
        den = jnp.sum(S, axis=-1) + inter * jnp.einsum('bhk,bhtk->bht', n, qb)
        h = num / jnp.maximum(jnp.abs(den), jnp.exp(-m_t))[..., None]
        m_new = m_t[..., -1]
        decay = jnp.exp(bcum[..., -1] + m - m_new)
        w = jnp.exp(bcum[..., -1:] - bcum + ib - m_new[..., None])
        C_new = decay[..., None, None] * C + jnp.einsum('bhs,bhsv,bhsk->bhvk', w, vb, kb)
        n_new = decay[..., None] * n + jnp.einsum('bhs,bhsk->bhk', w, kb)
        return (C_new, n_new, m_new), h

    (C1, n1, m1), hs = lax.scan(step, (C0.astype(f32), n0.astype(f32), m0.astype(f32)),
                                (qc, kc, vc, ic, lfc))
    h = hs.transpose(1, 0, 3, 2, 4).reshape(B, L, H, d)
    return h, C1, n1, m1


def pool_mix(u, buf, pool_w, pool_scale, start_pos):
    B, L, C = u.shape
    up = jnp.concatenate([buf.astype(u.dtype), u], axis=1)
    cs = jnp.concatenate([jnp.zeros((B, 1, C), jnp.float32),
                          jnp.cumsum(up.astype(jnp.float32), axis=1)], axis=1)
    pos = start_pos + jnp.arange(L)
    outs = []
    for gi, win in enumerate(POOL_WINDOWS):
        sl = slice(gi * POOL_GROUP_DIM, (gi + 1) * POOL_GROUP_DIM)
        cg = cs[..., sl]
        s = cg[:, POOL_BUF + 1:POOL_BUF + 1 + L] - cg[:, POOL_BUF + 1 - win:POOL_BUF + 1 - win + L]
        cnt = jnp.minimum(pos + 1, win).astype(jnp.float32)
        diff = s / cnt[None, :, None] - u[..., sl].astype(jnp.float32)
        outs.append(jnp.einsum('blc,cd->bld', diff, pool_w[gi]) * pool_scale[gi])
    return jnp.concatenate(outs, axis=-1).astype(u.dtype), up[:, L:]


def layer(x, c, start_pos, lru_h0, conv_buf, C0, n0, m0, pool_buf,
          norm1_g, norm2_g, w_ada, b_ada, w_in, conv_w, conv_b, lru_wa, lru_ba, lru_wx, lru_bx,
          lru_lam, mlstm_bi, mlstm_bf, mlstm_norm_g, pool_w, pool_scale, w_out, w_ff1, w_ff2):
    B, L, _ = x.shape
    mod = jax.nn.silu(c) @ w_ada + b_ada
    sh1, sc1, gt1, sh2, sc2, gt2 = jnp.split(mod[:, None, :], 6, axis=-1)
    h = rmsnorm(x, norm1_g) * (1.0 + sc1) + sh1
    proj = h @ w_in
    xr, gr, q, k, v, o, ig, fg, u = jnp.split(proj, _in_splits(), axis=-1)
    xc, conv_new = causal_dwconv(xr, conv_buf, conv_w, conv_b)
    hseq, h_last = rglru(xc, lru_h0, lru_wa, lru_ba, lru_wx, lru_bx, lru_lam, start_pos == 0)
    y_a = hseq.astype(x.dtype) * jax.nn.gelu(gr)
    hm, C1, n1, m1 = mlstm(q.reshape(B, L, MLSTM_HEADS, MLSTM_HEAD_DIM),
                           k.reshape(B, L, MLSTM_HEADS, MLSTM_HEAD_DIM),
                           v.reshape(B, L, MLSTM_HEADS, MLSTM_HEAD_DIM),
                           ig + mlstm_bi, fg + mlstm_bf, C0, n0, m0)
    hm = rmsnorm(hm, mlstm_norm_g).astype(x.dtype)
    y_b = jax.nn.sigmoid(o) * hm.reshape(B, L, MLSTM_WIDTH)
    y_c, pool_new = pool_mix(u, pool_buf, pool_w, pool_scale, start_pos)
    mix = jnp.concatenate([y_a, y_b, y_c], axis=-1) @ w_out
    x = x + gt1 * mix
    h2 = rmsnorm(x, norm2_g) * (1.0 + sc2) + sh2
    x = x + gt2 * (jnp.square(jax.nn.relu(h2 @ w_ff1)) @ w_ff2)
    return x, (h_last, conv_new, C1, n1, m1, pool_new)


def setup_inputs(seed: int = 0) -> dict:
    key = jax.random.key(seed)
    ks = jax.random.split(key, 40)
    nrm = lambda i, shape, s: jax.random.normal(ks[i], shape, jnp.float32) * s
    a0 = jax.random.uniform(ks[20], (DEPTH, LRU_WIDTH), jnp.float32, 0.9, 0.999)
    return {
        "x_prompt": nrm(0, (BATCH, SEQ, D_MODEL), 1.0),
        "x_sample": nrm(1, (DEC_BATCH, DEC_SEQ, D_MODEL), 1.0),
        "state_lru_h": nrm(2, (DEPTH, DEC_BATCH, LRU_WIDTH), 0.5),
        "state_lru_conv": nrm(3, (DEPTH, DEC_BATCH, CONV_WIDTH - 1, LRU_WIDTH), 1.0),
        "state_mlstm_C": nrm(4, (DEPTH, DEC_BATCH, MLSTM_HEADS, MLSTM_HEAD_DIM, MLSTM_HEAD_DIM), 0.1),
        "state_mlstm_n": nrm(5, (DEPTH, DEC_BATCH, MLSTM_HEADS, MLSTM_HEAD_DIM), 0.3),
        "state_mlstm_m": jax.random.uniform(ks[6], (DEPTH, DEC_BATCH, MLSTM_HEADS), jnp.float32, 0.0, 3.0),
        "state_pool": nrm(7, (DEPTH, DEC_BATCH, POOL_BUF, POOL_WIDTH), 1.0),
        "c_prompt": nrm(8, (BATCH, D_MODEL), 1.0),
        "c_sample": nrm(9, (DEC_BATCH, D_MODEL), 1.0),
        "norm1_g": 1.0 + nrm(10, (DEPTH, D_MODEL), 0.02),
        "norm2_g": 1.0 + nrm(11, (DEPTH, D_MODEL), 0.02),
        "w_ada": nrm(12, (DEPTH, D_MODEL, 6 * D_MODEL), 0.5 * D_MODEL ** -0.5),
        "b_ada": nrm(13, (DEPTH, 6 * D_MODEL), 0.02),
        "w_in": nrm(14, (DEPTH, D_MODEL, IN_COLS), D_MODEL ** -0.5),
        "conv_w": nrm(15, (DEPTH, CONV_WIDTH, LRU_WIDTH), CONV_WIDTH ** -0.5),
        "conv_b": nrm(16, (DEPTH, LRU_WIDTH), 0.02),
        "lru_wa": nrm(17, (DEPTH, LRU_HEADS, LRU_HEAD_DIM, LRU_HEAD_DIM), LRU_HEAD_DIM ** -0.5),
        "lru_ba": nrm(18, (DEPTH, LRU_HEADS, LRU_HEAD_DIM), 0.02),
        "lru_wx": nrm(19, (DEPTH, LRU_HEADS, LRU_HEAD_DIM, LRU_HEAD_DIM), LRU_HEAD_DIM ** -0.5),
        "lru_bx": nrm(21, (DEPTH, LRU_HEADS, LRU_HEAD_DIM), 0.02),
        "lru_lam": jnp.log(a0) - jnp.log1p(-a0),
        "mlstm_bi": nrm(22, (DEPTH, MLSTM_HEADS), 0.1),
        "mlstm_bf": jnp.linspace(3.0, 6.0, MLSTM_HEADS, dtype=jnp.float32)[None, :] + nrm(23, (DEPTH, MLSTM_HEADS), 0.1),
        "mlstm_norm_g": 1.0 + nrm(24, (DEPTH, MLSTM_HEADS, MLSTM_HEAD_DIM), 0.02),
        "pool_w": nrm(25, (DEPTH, POOL_GROUPS, POOL_GROUP_DIM, POOL_GROUP_DIM), POOL_GROUP_DIM ** -0.5),
        "pool_scale": 0.5 + nrm(26, (DEPTH, POOL_GROUPS, POOL_GROUP_DIM), 0.05),
        "w_out": nrm(27, (DEPTH, MIX_WIDTH, D_MODEL), MIX_WIDTH ** -0.5),
        "w_ff1": nrm(28, (DEPTH, D_MODEL, D_FF), D_MODEL ** -0.5),
        "w_ff2": nrm(29, (DEPTH, D_FF, D_MODEL), D_FF ** -0.5),
        "final_g": 1.0 + nrm(30, (D_MODEL,), 0.02),
    }


def reference(x_prompt, x_sample, state_lru_h, state_lru_conv, state_mlstm_C, state_mlstm_n,
              state_mlstm_m, state_pool, c_prompt, c_sample, norm1_g, norm2_g, w_ada, b_ada, w_in,
              conv_w, conv_b, lru_wa, lru_ba, lru_wx, lru_bx, lru_lam, mlstm_bi, mlstm_bf,
              mlstm_norm_g, pool_w, pool_scale, w_out, w_ff1, w_ff2, final_g):
    Bp = x_prompt.shape[0]
    f32 = jnp.float32
    xp, xs = x_prompt, x_sample
    p_states, s_states = [], []
    for l in range(DEPTH):
        params = (norm1_g[l], norm2_g[l], w_ada[l], b_ada[l], w_in[l], conv_w[l], conv_b[l],
                  lru_wa[l], lru_ba[l], lru_wx[l], lru_bx[l], lru_lam[l], mlstm_bi[l], mlstm_bf[l],
                  mlstm_norm_g[l], pool_w[l], pool_scale[l], w_out[l], w_ff1[l], w_ff2[l])
        xp, st_p = layer(xp, c_prompt, 0,
                         jnp.zeros((Bp, LRU_WIDTH), f32),
                         jnp.zeros((Bp, CONV_WIDTH - 1, LRU_WIDTH), xp.dtype),
                         jnp.zeros((Bp, MLSTM_HEADS, MLSTM_HEAD_DIM, MLSTM_HEAD_DIM), f32),
                         jnp.zeros((Bp, MLSTM_HEADS, MLSTM_HEAD_DIM), f32),
                         jnp.zeros((Bp, MLSTM_HEADS), f32),
                         jnp.zeros((Bp, POOL_BUF, POOL_WIDTH), xp.dtype),
                         *params)
        xs, st_s = layer(xs, c_sample, PAST_LEN, state_lru_h[l], state_lru_conv[l], state_mlstm_C[l],
                         state_mlstm_n[l], state_mlstm_m[l], state_pool[l], *params)
        p_states.append(st_p)
        s_states.append(st_s)
    y_prompt = rmsnorm(xp, final_g)
    y_sample = rmsnorm(xs, final_g)
    stk = lambda sts, j: jnp.stack([st[j] for st in sts], axis=0)
    return (y_prompt, y_sample,
            stk(p_states, 0), stk(p_states, 1), stk(p_states, 2), stk(p_states, 3), stk(p_states, 4), stk(p_states, 5),
            stk(s_states, 0), stk(s_states, 1), stk(s_states, 2), stk(s_states, 3), stk(s_states, 4), stk(s_states, 5))
```

```python
import functools
import math

import jax
import jax.numpy as jnp
from jax import lax
from jax.experimental import pallas as pl
from jax.experimental.pallas import tpu as pltpu

F32 = jnp.float32
BF16 = jnp.bfloat16

D_MODEL = 1024
LRU_W = 384
LRU_HEADS = 6
LRU_C = 8.0
NH = 4
HD = 96
HP = 128
MW = NH * HD
MWP = NH * HP
CHUNK = 128
PW = 256
POOL_WINDOWS = (2, 4, 8, 16)
POOL_GROUP = 64
POOL_BUF = 15
D_FF = 4096
EPS = 1e-6
PAST_LEN = 16384

C_XR, C_GR, C_Q, C_K, C_V, C_O, C_U, C_G = 0, 384, 768, 1280, 1792, 2304, 2816, 3072
N_IN = 3200
K_OUT = LRU_W + MWP + PW

VMEM_LIMIT = 56 * 1024 * 1024
TM_MIX = 256
TM_FFN = 512
BT_STATE = 8


def _log_sigmoid(x):
    return jnp.minimum(x, 0.0) - jnp.log1p(jnp.exp(-jnp.abs(x)))


def _sigmoid(x):
    return 1.0 / (1.0 + jnp.exp(-x))


def _gelu_tanh(x):
    c = math.sqrt(2.0 / math.pi)
    return 0.5 * x * (1.0 + jnp.tanh(c * (x + 0.044715 * (x * x * x))))


def _rms(x, g):
    ms = jnp.mean(x * x, axis=-1, keepdims=True)
    return x * lax.rsqrt(ms + EPS) * g


def _dot(a, b):
    return jnp.dot(a, b, preferred_element_type=F32)


def _dot_nt(a, b):
    return lax.dot_general(a, b, (((1,), (1,)), ((), ())), preferred_element_type=F32)


def _lru_coeffs(xc, wgate, bgate, lam):
    gates = _dot(xc.astype(BF16), wgate) + bgate
    r = _sigmoid(gates[:, :LRU_W])
    i = _sigmoid(gates[:, LRU_W:])
    log_a = LRU_C * r * _log_sigmoid(lam)
    a = jnp.exp(log_a)
    th = jnp.tanh(log_a)
    mult = jnp.sqrt(-2.0 * th / (1.0 - th))
    return a, mult, i


def _scan_rows(a, b):
    n = a.shape[0]
    row = lax.broadcasted_iota(jnp.int32, a.shape, 0)
    s = 1
    while s < n:
        keep = row >= s
        b = a * jnp.where(keep, pltpu.roll(b, s, 0), 0.0) + b
        if 2 * s < n:
            a = a * jnp.where(keep, pltpu.roll(a, s, 0), 1.0)
        s *= 2
    return b


def _head_norm_gate(hout, o, ng):
    ms = jnp.sum(hout * hout, axis=-1, keepdims=True) * (1.0 / HD)
    return _sigmoid(o) * (hout * lax.rsqrt(ms + EPS) * ng)


def _pool_select(s2, s4, s8, s16):
    lane = lax.broadcasted_iota(jnp.int32, s2.shape, 1)
    return jnp.where(lane < 64, s2, jnp.where(lane < 128, s4, jnp.where(lane < 192, s8, s16)))


def _pool_window_lanes(shape):
    lane = lax.broadcasted_iota(jnp.int32, shape, 1)
    return jnp.where(lane < 64, 2, jnp.where(lane < 128, 4, jnp.where(lane < 192, 8, 16)))


def _mod_kernel(c_ref, w_ref, b_ref, modp_ref, mods_ref, *, n_prompt):
    c = c_ref[...]
    act = (c * _sigmoid(c)).astype(BF16)
    out = _dot(act, w_ref[...].astype(BF16)) + b_ref[...]
    for r in range(n_prompt):
        modp_ref[r] = out[r:r + 1]
    mods_ref[...] = out[n_prompt:]


def _modulation(c_all, w_ada, b_ada, n_prompt):
    depth = w_ada.shape[0]
    n_all = c_all.shape[0]
    n_mod = w_ada.shape[2]
    tn = 1536
    return pl.pallas_call(
        functools.partial(_mod_kernel, n_prompt=n_prompt),
        grid=(depth, n_mod // tn),
        in_specs=[
            pl.BlockSpec((n_all, D_MODEL), lambda l, j: (0, 0)),
            pl.BlockSpec((None, D_MODEL, tn), lambda l, j: (l, 0, j)),
            pl.BlockSpec((None, 1, tn), lambda l, j: (l, 0, j)),
        ],
        out_specs=[
            pl.BlockSpec((None, n_prompt, 1, tn), lambda l, j: (l, 0, 0, j)),
            pl.BlockSpec((None, n_all - n_prompt, tn), lambda l, j: (l, 0, j)),
        ],
        out_shape=[
            jax.ShapeDtypeStruct((depth, n_prompt, 1, n_mod), F32),
            jax.ShapeDtypeStruct((depth, n_all - n_prompt, n_mod), F32),
        ],
        compiler_params=pltpu.CompilerParams(
            dimension_semantics=("arbitrary", "arbitrary"), vmem_limit_bytes=VMEM_LIMIT),
        name="adaln_modulation",
    )(c_all, w_ada, b_ada)


def _mlstm_chunk(q, k, v, ib_col, lf_col, ib_row, lf_row, c_prev, n_prev, m_prev):
    t_idx = lax.broadcasted_iota(jnp.int32, (CHUNK, CHUNK), 0)
    s_idx = lax.broadcasted_iota(jnp.int32, (CHUNK, CHUNK), 1)
    causal = s_idx <= t_idx
    bcum_col = jnp.sum(jnp.where(causal, lf_row, 0.0), axis=1, keepdims=True)
    bcum_row = jnp.sum(jnp.where(t_idx <= s_idx, lf_col, 0.0), axis=0, keepdims=True)
    dmat = jnp.where(causal, bcum_col - bcum_row + ib_row, -jnp.inf)
    g_col = bcum_col + m_prev
    m_t = jnp.maximum(g_col, jnp.max(dmat, axis=1, keepdims=True))
    inter = jnp.exp(g_col - m_t)
    qb = q.astype(BF16)
    ks = k * (1.0 / math.sqrt(HD))
    kb = ks.astype(BF16)
    s_mat = _dot_nt(qb, kb) * jnp.exp(dmat - m_t)
    num = _dot(s_mat.astype(BF16), v.astype(BF16)) + inter * _dot_nt(qb, c_prev.astype(BF16))
    den = jnp.sum(s_mat, axis=1, keepdims=True) + inter * jnp.sum(q * n_prev, axis=1, keepdims=True)
    hout = num / jnp.maximum(jnp.abs(den), jnp.exp(-m_t))
    b_last = bcum_col[CHUNK - 1:CHUNK, :]
    m_new = m_t[CHUNK - 1:CHUNK, :]
    decay = jnp.exp(b_last + m_prev - m_new)
    w_col = jnp.exp(b_last - bcum_col + ib_col - m_new)
    wv_t = (w_col * v).T.astype(BF16)
    c_new = decay * c_prev + _dot(wv_t, kb)
    n_new = decay * n_prev + jnp.sum(w_col * ks, axis=0, keepdims=True)
    return hout, c_new, n_new, m_new


def _prompt_mixer_kernel(x_ref, mod_ref, g1_ref, win_ref, convw_ref, convb_ref, wgate_ref, bgate_ref,
                         lam_ref, gbias_ref, ng_ref, poolw_ref, pscale_ref, wout_ref,
                         xo_ref, hlast_ref, convo_ref, co_ref, no_ref, mo_ref, poolo_ref,
                         h_sc, xbuf, ubuf, c_sc, n_sc, m_sc, *, tm):
    j = pl.program_id(1)

    @pl.when(j == 0)
    def _():
        h_sc[...] = jnp.zeros_like(h_sc)
        xbuf[0:8, :] = jnp.zeros((8, LRU_W), F32)
        ubuf[0:16, :] = jnp.zeros((16, PW), F32)
        c_sc[...] = jnp.zeros_like(c_sc)
        n_sc[...] = jnp.zeros_like(n_sc)
        m_sc[...] = jnp.zeros_like(m_sc)

    x = x_ref[0]
    mod = mod_ref[...]
    sh1 = mod[:, 0:D_MODEL]
    sc1 = mod[:, D_MODEL:2 * D_MODEL]
    gt1 = mod[:, 2 * D_MODEL:3 * D_MODEL]
    hn = _rms(x, g1_ref[...]) * (1.0 + sc1) + sh1
    proj = _dot(hn.astype(BF16), win_ref[...])
    row = lax.broadcasted_iota(jnp.int32, (tm, 1), 0)

    xr = proj[:, C_XR:C_XR + LRU_W]
    gr = proj[:, C_GR:C_GR + LRU_W]
    xbuf[8:8 + tm, :] = xr
    cw = convw_ref[...]
    xc = (convb_ref[...] + cw[3:4] * xr + cw[2:3] * xbuf[7:7 + tm, :]
          + cw[1:2] * xbuf[6:6 + tm, :] + cw[0:1] * xbuf[5:5 + tm, :])
    tail = xbuf[tm:tm + 8, :]
    xbuf[0:8, :] = tail
    convo_ref[0] = tail
    a, mult, gate_i = _lru_coeffs(xc, wgate_ref[...], bgate_ref[...], lam_ref[...])
    first = jnp.logical_and(row == 0, j == 0)
    mult = jnp.where(first, 1.0, mult)
    bterm = mult * gate_i * xc + jnp.where(row == 0, a * h_sc[...], 0.0)
    hseq = _scan_rows(a, bterm)
    h_last = hseq[tm - 1:tm, :]
    h_sc[...] = h_last
    hlast_ref[0] = h_last
    y_a = hseq * _gelu_tanh(gr)

    gcol = proj[:, C_G:C_G + HP] + gbias_ref[...]
    lane = lax.broadcasted_iota(jnp.int32, gcol.shape, 1)
    gl = jnp.where(jnp.logical_and(lane >= NH, lane < 2 * NH), _log_sigmoid(gcol), gcol)
    gl_t = gl.T
    ng = ng_ref[...]
    y_b_heads = []
    for h in range(NH):
        c_h = c_sc[h]
        n_h = n_sc[h:h + 1, :]
        m_h = m_sc[h:h + 1, 0:1]
        outs = []
        for c in range(tm // CHUNK):
            r0 = c * CHUNK
            q = proj[r0:r0 + CHUNK, C_Q + h * HP:C_Q + (h + 1) * HP]
            k = proj[r0:r0 + CHUNK, C_K + h * HP:C_K + (h + 1) * HP]
            v = proj[r0:r0 + CHUNK, C_V + h * HP:C_V + (h + 1) * HP]
            hout, c_h, n_h, m_h = _mlstm_chunk(
                q, k, v,
                gl[r0:r0 + CHUNK, h:h + 1], gl[r0:r0 + CHUNK, NH + h:NH + h + 1],
                gl_t[h:h + 1, r0:r0 + CHUNK], gl_t[NH + h:NH + h + 1, r0:r0 + CHUNK],
                c_h, n_h, m_h)
            outs.append(hout)
        c_sc[h] = c_h
        n_sc[h:h + 1, :] = n_h
        m_sc[h:h + 1, :] = jnp.broadcast_to(m_h, (1, HP))
        co_ref[0, h] = c_h[0:HD, 0:HD]
        hout_all = jnp.concatenate(outs, axis=0) if len(outs) > 1 else outs[0]
        o = proj[:, C_O + h * HP:C_O + (h + 1) * HP]
        y_b_heads.append(_head_norm_gate(hout_all, o, ng[:, h * HP:(h + 1) * HP]))
    no_ref[0] = n_sc[...]
    mo_ref[0] = m_sc[...]

    u = proj[:, C_U:C_U + PW]
    ubuf[16:16 + tm, :] = u
    ext = ubuf[...]
    s2 = ext + pltpu.roll(ext, 1, 0)
    s4 = s2 + pltpu.roll(s2, 2, 0)
    s8 = s4 + pltpu.roll(s4, 4, 0)
    s16 = s8 + pltpu.roll(s8, 8, 0)
    wsum = _pool_select(s2[16:], s4[16:], s8[16:], s16[16:])
    pos = row + j * tm
    cnt = jnp.minimum(pos + 1, _pool_window_lanes((tm, PW))).astype(F32)
    diff = wsum / cnt - u
    y_c = _dot(diff.astype(BF16), poolw_ref[...]) * pscale_ref[...]
    ptail = ubuf[tm:tm + 16, :]
    ubuf[0:16, :] = ptail
    poolo_ref[0] = ptail

    mix = jnp.concatenate([y_a] + y_b_heads + [y_c], axis=1).astype(BF16)
    xo_ref[0] = x + gt1 * _dot(mix, wout_ref[...])


def _prompt_mixer(x, modp, l, wts, tm):
    nb, seq, _ = x.shape
    lsel3 = lambda b, j: (l, 0, 0)
    wspec = lambda a: pl.BlockSpec((None,) + a.shape[1:], lsel3)
    names = ("g1", "w_in", "conv_w", "conv_b", "w_gate", "b_gate", "lam", "gbias", "ng", "pool_w",
             "pool_scale", "w_out")
    out_shapes = [
        jax.ShapeDtypeStruct((nb, seq, D_MODEL), F32),
        jax.ShapeDtypeStruct((nb, 1, LRU_W), F32),
        jax.ShapeDtypeStruct((nb, 8, LRU_W), F32),
        jax.ShapeDtypeStruct((nb, NH, HD, HD), F32),
        jax.ShapeDtypeStruct((nb, 8, HP), F32),
        jax.ShapeDtypeStruct((nb, 8, HP), F32),
        jax.ShapeDtypeStruct((nb, 16, PW), F32),
    ]
    out_specs = [
        pl.BlockSpec((1, tm, D_MODEL), lambda b, j: (b, j, 0)),
        pl.BlockSpec((1, 1, LRU_W), lambda b, j: (b, 0, 0)),
        pl.BlockSpec((1, 8, LRU_W), lambda b, j: (b, 0, 0)),
        pl.BlockSpec((1, NH, HD, HD), lambda b, j: (b, 0, 0, 0)),
        pl.BlockSpec((1, 8, HP), lambda b, j: (b, 0, 0)),
        pl.BlockSpec((1, 8, HP), lambda b, j: (b, 0, 0)),
        pl.BlockSpec((1, 16, PW), lambda b, j: (b, 0, 0)),
    ]
    return pl.pallas_call(
        functools.partial(_prompt_mixer_kernel, tm=tm),
        grid=(nb, seq // tm),
        in_specs=[
            pl.BlockSpec((1, tm, D_MODEL), lambda b, j: (b, j, 0)),
            pl.BlockSpec((None, None, 1, 3 * D_MODEL), lambda b, j: (l, b, 0, 0)),
        ] + [wspec(wts[n]) for n in names],
        out_specs=out_specs,
        out_shape=out_shapes,
        scratch_shapes=[
            pltpu.VMEM((1, LRU_W), F32),
            pltpu.VMEM((tm + 8, LRU_W), F32),
            pltpu.VMEM((tm + 16, PW), F32),
            pltpu.VMEM((NH, HP, HP), F32),
            pltpu.VMEM((8, HP), F32),
            pltpu.VMEM((8, HP), F32),
        ],
        compiler_params=pltpu.CompilerParams(
            dimension_semantics=("arbitrary", "arbitrary"), vmem_limit_bytes=VMEM_LIMIT),
        name="prompt_mixer",
    )(x, modp, *[wts[n] for n in names])


def _ffn_kernel(x_ref, mod_ref, g2_ref, w1_ref, w2_ref, gf_ref, o_ref, *, final):
    x = x_ref[...]
    mod = mod_ref[...]
    sh2 = mod[:, 0:D_MODEL]
    sc2 = mod[:, D_MODEL:2 * D_MODEL]
    gt2 = mod[:, 2 * D_MODEL:3 * D_MODEL]
    h2 = (_rms(x, g2_ref[...]) * (1.0 + sc2) + sh2).astype(BF16)
    acc = jnp.zeros(x.shape, F32)
    step = 1024
    for c in range(D_FF // step):
        hid = _dot(h2, w1_ref[:, c * step:(c + 1) * step])
        hid = jnp.square(jnp.maximum(hid, 0.0)).astype(BF16)
        acc = acc + _dot(hid, w2_ref[c * step:(c + 1) * step, :])
    y = x + gt2 * acc
    if final:
        y = _rms(y, gf_ref[...])
    o_ref[...] = y


def _ffn(x2d, mod, mod_spec, l, wts, final_g, tm, final):
    n_tok = x2d.shape[0]
    lsel3 = lambda i: (l, 0, 0)
    wspec = lambda a: pl.BlockSpec((None,) + a.shape[1:], lsel3, pipeline_mode=pl.Buffered(1))
    return pl.pallas_call(
        functools.partial(_ffn_kernel, final=final),
        grid=(n_tok // tm,),
        in_specs=[
            pl.BlockSpec((tm, D_MODEL), lambda i: (i, 0)),
            mod_spec,
            pl.BlockSpec((None, 1, D_MODEL), lsel3),
            wspec(wts["w_ff1"]),
            wspec(wts["w_ff2"]),
            pl.BlockSpec((1, D_MODEL), lambda i: (0, 0)),
        ],
        out_specs=pl.BlockSpec((tm, D_MODEL), lambda i: (i, 0)),
        out_shape=jax.ShapeDtypeStruct((n_tok, D_MODEL), F32),
        compiler_params=pltpu.CompilerParams(
            dimension_semantics=("arbitrary",), vmem_limit_bytes=VMEM_LIMIT),
        name="ffn",
    )(x2d, mod, wts["g2"], wts["w_ff1"], wts["w_ff2"], final_g)


def _sample_gate_terms(gcol, m0):
    ib = gcol
    lf = pltpu.roll(_log_sigmoid(gcol), HP - NH, 1)
    g = lf + m0
    m_t = jnp.maximum(g, ib)
    inter = jnp.exp(g - m_t)
    p = jnp.exp(ib - m_t)
    return m_t, inter, p


def _expand_heads(t):
    rows = t.shape[0]
    return jnp.concatenate([jnp.broadcast_to(t[:, h:h + 1], (rows, HP)) for h in range(NH)], axis=1)


def _sample_in_kernel(x_ref, mod_ref, g1_ref, win_ref, gbias_ref, m0_ref, proj_ref, wvt_ref, dec_ref):
    x = x_ref[...]
    mod = mod_ref[...]
    hn = _rms(x, g1_ref[...]) * (1.0 + mod[:, D_MODEL:2 * D_MODEL]) + mod[:, 0:D_MODEL]
    proj = _dot(hn.astype(BF16), win_ref[...])
    proj_ref[...] = proj
    _, inter, p = _sample_gate_terms(proj[:, C_G:C_G + HP] + gbias_ref[...], m0_ref[...])
    dec_ref[...] = _expand_heads(inter)
    wv = _expand_heads(p) * proj[:, C_V:C_V + MWP]
    wvt_ref[...] = wv.T


def _sample_in(xs, mods, l, wts, m0p):
    n = xs.shape[0]
    lsel3 = lambda i: (l, 0, 0)
    return pl.pallas_call(
        _sample_in_kernel,
        grid=(1,),
        in_specs=[
            pl.BlockSpec((n, D_MODEL), lambda i: (0, 0)),
            pl.BlockSpec((None, n, 2 * D_MODEL), lsel3),
            pl.BlockSpec((None, 1, D_MODEL), lsel3),
            pl.BlockSpec((None, D_MODEL, N_IN), lsel3),
            pl.BlockSpec((None, 1, HP), lsel3),
            pl.BlockSpec((None, n, HP), lsel3),
        ],
        out_specs=[
            pl.BlockSpec((n, N_IN), lambda i: (0, 0)),
            pl.BlockSpec((MWP, n), lambda i: (0, 0)),
            pl.BlockSpec((n, MWP), lambda i: (0, 0)),
        ],
        out_shape=[
            jax.ShapeDtypeStruct((n, N_IN), F32),
            jax.ShapeDtypeStruct((MWP, n), F32),
            jax.ShapeDtypeStruct((n, MWP), F32),
        ],
        compiler_params=pltpu.CompilerParams(
            dimension_semantics=("arbitrary",), vmem_limit_bytes=VMEM_LIMIT),
        name="sample_in_proj",
    )(xs, mods, wts["g1"], wts["w_in"], wts["gbias"], m0p)


def _sample_state_kernel(proj_ref, wvt_ref, dec_ref, c0_ref, cnew_ref, cqt_ref, *, bt):
    i = pl.program_id(0)

    @pl.when(i == 0)
    def _():
        cqt_ref[...] = jnp.zeros_like(cqt_ref)

    n_tok = wvt_ref.shape[1]
    lane = lax.broadcasted_iota(jnp.int32, (HD, n_tok), 1)
    scale = 1.0 / math.sqrt(HD)

    for h in range(NH):
        q_rows = proj_ref[:, C_Q + h * HP:C_Q + h * HP + HD]
        k_rows = proj_ref[:, C_K + h * HP:C_K + h * HP + HD] * scale
        d_rows = dec_ref[:, h * HP:h * HP + HD]
        wvt_h = wvt_ref[h * HP:h * HP + HD, :]
        cq_acc = cqt_ref[h * HP:h * HP + HD, :]
        for jj in range(bt):
            sel = lane == i * bt + jj
            c_old = c0_ref[jj, h]
            cq = jnp.sum(c_old * q_rows[jj:jj + 1], axis=1, keepdims=True)
            wv_col = jnp.sum(jnp.where(sel, wvt_h, 0.0), axis=1, keepdims=True)
            cnew_ref[jj, h] = d_rows[jj:jj + 1] * c_old + wv_col * k_rows[jj:jj + 1]
            cq_acc = jnp.where(sel, cq, cq_acc)
        cqt_ref[h * HP:h * HP + HD, :] = cq_acc


def _sample_state(proj, wvt, dec, c0_all, l, bt):
    n = proj.shape[0]
    return pl.pallas_call(
        functools.partial(_sample_state_kernel, bt=bt),
        grid=(n // bt,),
        in_specs=[
            pl.BlockSpec((bt, N_IN), lambda i: (i, 0)),
            pl.BlockSpec((MWP, n), lambda i: (0, 0)),
            pl.BlockSpec((bt, MWP), lambda i: (i, 0)),
            pl.BlockSpec((None, bt, NH, HD, HD), lambda i: (l, i, 0, 0, 0)),
        ],
        out_specs=[
            pl.BlockSpec((bt, NH, HD, HD), lambda i: (i, 0, 0, 0)),
            pl.BlockSpec((MWP, n), lambda i: (0, 0)),
        ],
        out_shape=[
            jax.ShapeDtypeStruct((n, NH, HD, HD), F32),
            jax.ShapeDtypeStruct((MWP, n), F32),
        ],
        compiler_params=pltpu.CompilerParams(
            dimension_semantics=("arbitrary",), vmem_limit_bytes=VMEM_LIMIT),
        name="sample_matrix_memory",
    )(proj, wvt, dec, c0_all)


def _sample_mix_kernel(x_ref, gt_ref, proj_ref, cqt_ref, h0_ref, conv_ref, n0_ref, m0_ref, pool_ref,
                       convw_ref, convb_ref, wgate_ref, bgate_ref, lam_ref, gbias_ref, ng_ref,
                       poolw_ref, pscale_ref, wout_ref,
                       xo_ref, ho_ref, convo_ref, no_ref, mo_ref, poolo_ref):
    proj = proj_ref[...]

    xr = proj[:, C_XR:C_XR + LRU_W]
    gr = proj[:, C_GR:C_GR + LRU_W]
    cs = conv_ref[...]
    cw = convw_ref[...]
    xc = (convb_ref[...] + cw[3:4] * xr + cw[2:3] * cs[:, 2 * LRU_W:3 * LRU_W]
          + cw[1:2] * cs[:, LRU_W:2 * LRU_W] + cw[0:1] * cs[:, 0:LRU_W])
    convo_ref[...] = jnp.concatenate([cs[:, LRU_W:], xr], axis=1)
    a, mult, gate_i = _lru_coeffs(xc, wgate_ref[...], bgate_ref[...], lam_ref[...])
    h_new = a * h0_ref[...] + mult * gate_i * xc
    ho_ref[...] = h_new
    y_a = h_new * _gelu_tanh(gr)

    m_t, inter, p = _sample_gate_terms(proj[:, C_G:C_G + HP] + gbias_ref[...], m0_ref[...])
    mo_ref[...] = m_t
    e_floor = jnp.exp(-m_t)
    cq = cqt_ref[...].T
    n0 = n0_ref[...]
    ng = ng_ref[...]
    y_b_heads, n_heads = [], []
    for h in range(NH):
        sl = slice(h * HP, (h + 1) * HP)
        q = proj[:, C_Q + h * HP:C_Q + (h + 1) * HP]
        ks = proj[:, C_K + h * HP:C_K + (h + 1) * HP] * (1.0 / math.sqrt(HD))
        v = proj[:, C_V + h * HP:C_V + (h + 1) * HP]
        o = proj[:, C_O + h * HP:C_O + (h + 1) * HP]
        p_h, inter_h, e_h = p[:, h:h + 1], inter[:, h:h + 1], e_floor[:, h:h + 1]
        s = jnp.sum(q * ks, axis=1, keepdims=True) * p_h
        num = s * v + inter_h * cq[:, sl]
        den = s + inter_h * jnp.sum(n0[:, sl] * q, axis=1, keepdims=True)
        hout = num / jnp.maximum(jnp.abs(den), e_h)
        n_heads.append(inter_h * n0[:, sl] + p_h * ks)
        y_b_heads.append(_head_norm_gate(hout, o, ng[:, sl]))
    no_ref[...] = jnp.concatenate(n_heads, axis=1)

    u = proj[:, C_U:C_U + PW]
    hist = pool_ref[...]
    back = lambda jdx: hist[:, (POOL_BUF - jdx) * PW:(POOL_BUF - jdx + 1) * PW]
    sums, acc, nxt = {}, u, 1
    for win in POOL_WINDOWS:
        while nxt < win:
            acc = acc + back(nxt)
            nxt += 1
        sums[win] = acc
    wsum = _pool_select(sums[2], sums[4], sums[8], sums[16])
    cnt = jnp.minimum(PAST_LEN + 1, _pool_window_lanes(u.shape)).astype(F32)
    diff = wsum / cnt - u
    y_c = _dot(diff.astype(BF16), poolw_ref[...]) * pscale_ref[...]
    poolo_ref[...] = jnp.concatenate([hist[:, PW:], u], axis=1)

    mix = jnp.concatenate([y_a] + y_b_heads + [y_c], axis=1).astype(BF16)
    xo_ref[...] = x_ref[...] + gt_ref[...] * _dot(mix, wout_ref[...])


def _sample_mix(xs, mods, proj, cqt, l, st, wts):
    n = xs.shape[0]
    lsel3 = lambda i: (l, 0, 0)
    full = lambda a: pl.BlockSpec(a.shape, lambda i: (0, 0))
    lspec = lambda a: pl.BlockSpec((None,) + a.shape[1:], lsel3)
    snames = ("lru_h", "conv", "n", "m", "pool")
    wnames = ("conv_w", "conv_b", "w_gate", "b_gate", "lam", "gbias", "ng", "pool_w", "pool_scale", "w_out")
    out_shapes = [
        jax.ShapeDtypeStruct((n, D_MODEL), F32),
        jax.ShapeDtypeStruct((n, LRU_W), F32),
        jax.ShapeDtypeStruct((n, 3 * LRU_W), F32),
        jax.ShapeDtypeStruct((n, MWP), F32),
        jax.ShapeDtypeStruct((n, HP), F32),
        jax.ShapeDtypeStruct((n, POOL_BUF * PW), F32),
    ]
    return pl.pallas_call(
        _sample_mix_kernel,
        grid=(1,),
        in_specs=[
            full(xs),
            pl.BlockSpec((None, n, D_MODEL), lambda i: (l, 0, 2)),
            full(proj), full(cqt),
        ] + [lspec(st[k]) for k in snames] + [lspec(wts[k]) for k in wnames],
        out_specs=[pl.BlockSpec(s.shape, lambda i: (0, 0)) for s in out_shapes],
        out_shape=out_shapes,
        compiler_params=pltpu.CompilerParams(
            dimension_semantics=("arbitrary",), vmem_limit_bytes=VMEM_LIMIT),
        name="sample_mixer",
    )(xs, mods, proj, cqt, *[st[k] for k in snames], *[wts[k] for k in wnames])


def _pad_heads(w):
    lead = w.shape[:-1]
    w = w.reshape(lead + (NH, HD))
    w = jnp.pad(w, [(0, 0)] * len(lead) + [(0, 0), (0, HP - HD)])
    return w.reshape(lead + (MWP,))


def _block_diag(w):
    depth, g, n, _ = w.shape
    eye = jnp.eye(g, dtype=w.dtype)
    return (w[:, :, :, None, :] * eye[None, :, None, :, None]).reshape(depth, g * n, g * n)


def _prepare_weights(norm1_g, norm2_g, w_in, conv_w, conv_b, lru_wa, lru_ba, lru_wx, lru_bx, lru_lam,
                     mlstm_bi, mlstm_bf, mlstm_norm_g, pool_w, pool_scale, w_out, w_ff1, w_ff2):
    depth = w_in.shape[0]
    o0 = 2 * LRU_W
    seg = lambda k: w_in[:, :, o0 + k * MW:o0 + (k + 1) * MW]
    g0 = o0 + 4 * MW
    gates = jnp.pad(w_in[:, :, g0:g0 + 2 * NH], ((0, 0), (0, 0), (0, HP - 2 * NH)))
    w_in_p = jnp.concatenate(
        [w_in[:, :, 0:o0]] + [_pad_heads(seg(k)) for k in range(4)]
        + [w_in[:, :, g0 + 2 * NH:], gates], axis=-1).astype(BF16)
    wo_b = w_out[:, LRU_W:LRU_W + MW].reshape(depth, NH, HD, D_MODEL)
    wo_b = jnp.pad(wo_b, ((0, 0), (0, 0), (0, HP - HD), (0, 0))).reshape(depth, MWP, D_MODEL)
    w_out_p = jnp.concatenate([w_out[:, 0:LRU_W], wo_b, w_out[:, LRU_W + MW:]], axis=1).astype(BF16)
    row = lambda v: v.reshape(depth, 1, -1)
    return {
        "g1": row(norm1_g), "g2": row(norm2_g),
        "w_in": w_in_p, "w_out": w_out_p,
        "conv_w": conv_w, "conv_b": row(conv_b),
        "w_gate": jnp.concatenate([_block_diag(lru_wa), _block_diag(lru_wx)], axis=-1).astype(BF16),
        "b_gate": jnp.concatenate([row(lru_ba), row(lru_bx)], axis=-1),
        "lam": row(lru_lam),
        "gbias": jnp.pad(jnp.concatenate([mlstm_bi, mlstm_bf], axis=-1), ((0, 0), (0, HP - 2 * NH)))[:, None, :],
        "ng": row(_pad_heads(mlstm_norm_g.reshape(depth, MW))),
        "pool_w": _block_diag(pool_w).astype(BF16),
        "pool_scale": row(pool_scale),
        "w_ff1": w_ff1.astype(BF16), "w_ff2": w_ff2.astype(BF16),
    }


def kernel(x_prompt, x_sample, state_lru_h, state_lru_conv, state_mlstm_C, state_mlstm_n, state_mlstm_m,
           state_pool, c_prompt, c_sample, norm1_g, norm2_g, w_ada, b_ada, w_in, conv_w, conv_b, lru_wa,
           lru_ba, lru_wx, lru_bx, lru_lam, mlstm_bi, mlstm_bf, mlstm_norm_g, pool_w, pool_scale, w_out,
           w_ff1, w_ff2, final_g):
    nb, seq, _ = x_prompt.shape
    ns = x_sample.shape[0]
    depth = w_in.shape[0]
    wts = _prepare_weights(norm1_g, norm2_g, w_in, conv_w, conv_b, lru_wa, lru_ba, lru_wx, lru_bx, lru_lam,
                           mlstm_bi, mlstm_bf, mlstm_norm_g, pool_w, pool_scale, w_out, w_ff1, w_ff2)
    final_row = final_g.reshape(1, D_MODEL)
    modp, mods = _modulation(jnp.concatenate([c_prompt, c_sample], axis=0), w_ada,
                             b_ada.reshape(depth, 1, -1), nb)

    sample_state = {
        "lru_h": state_lru_h,
        "conv": state_lru_conv.reshape(depth, ns, 3 * LRU_W),
        "n": _pad_heads(state_mlstm_n.reshape(depth, ns, MW)),
        "m": jnp.pad(state_mlstm_m, ((0, 0), (0, 0), (0, HP - NH))),
        "pool": state_pool.reshape(depth, ns, POOL_BUF * PW),
    }

    xp = x_prompt
    xs = x_sample.reshape(ns, D_MODEL)
    p_out = [[] for _ in range(6)]
    s_out = [[] for _ in range(6)]
    for l in range(depth):
        last = l == depth - 1
        xp, h_l, conv_l, c_l, n_l, m_l, pool_l = _prompt_mixer(xp, modp, l, wts, TM_MIX)
        mod_spec = pl.BlockSpec((None, None, 1, 3 * D_MODEL),
                                lambda i, l=l: (l, i // (seq // TM_FFN), 0, 1))
        xp = _ffn(xp.reshape(nb * seq, D_MODEL), modp, mod_spec, l, wts, final_row, TM_FFN, last)
        xp = xp.reshape(nb, seq, D_MODEL)
        for lst, val in zip(p_out, (h_l.reshape(nb, LRU_W), conv_l[:, 8 - 3:], c_l, n_l[:, :NH, :HD],
                                    m_l[:, :NH, 0], pool_l[:, 16 - POOL_BUF:])):
            lst.append(val)
        proj, wvt, dec = _sample_in(xs, mods, l, wts, sample_state["m"])
        c_new, cqt = _sample_state(proj, wvt, dec, state_mlstm_C, l, BT_STATE)
        xs, h_s, conv_s, n_s, m_s, pool_s = _sample_mix(xs, mods, proj, cqt, l, sample_state, wts)
        mod_spec = pl.BlockSpec((None, ns, 3 * D_MODEL), lambda i, l=l: (l, 0, 1))
        xs = _ffn(xs, mods, mod_spec, l, wts, final_row, ns, last)
        for lst, val in zip(s_out, (h_s, conv_s.reshape(ns, 3, LRU_W), c_new,
                                    n_s.reshape(ns, NH, HP)[:, :, :HD], m_s[:, :NH],
                                    pool_s.reshape(ns, POOL_BUF, PW))):
            lst.append(val)

    stack = lambda lst: jnp.stack(lst, axis=0)
    return (xp, xs.reshape(ns, 1, D_MODEL),
            *[stack(v) for v in p_out], *[stack(v) for v in s_out])
```

```python
import functools
import math

import jax
import jax.numpy as jnp
from jax import lax
from jax.experimental import pallas as pl
from jax.experimental.pallas import tpu as pltpu

F32 = jnp.float32
BF16 = jnp.bfloat16

D_MODEL = 1024
LRU_W = 384
LRU_HEADS = 6
LRU_C = 8.0
NH = 4
HD = 96
HP = 128
MW = NH * HD
MWP = NH * HP
CHUNK = 128
PW = 256
POOL_WINDOWS = (2, 4, 8, 16)
POOL_GROUP = 64
POOL_BUF = 15
D_FF = 4096
EPS = 1e-6
PAST_LEN = 16384

C_XR, C_GR, C_Q, C_K, C_V, C_O, C_U, C_G = 0, 384, 768, 1280, 1792, 2304, 2816, 3072
N_IN = 3200
K_OUT = LRU_W + MWP + PW

VMEM_LIMIT = 56 * 1024 * 1024
TM_MIX = 256
PROJ_PIECE = 256
TM_FFN = 512
BT_STATE = 8


def _log_sigmoid(x):
    return jnp.minimum(x, 0.0) - jnp.log1p(jnp.exp(-jnp.abs(x)))


def _sigmoid(x):
    return 1.0 / (1.0 + jnp.exp(-x))


def _gelu_tanh(x):
    c = math.sqrt(2.0 / math.pi)
    return 0.5 * x * (1.0 + jnp.tanh(c * (x + 0.044715 * (x * x * x))))


def _rms(x, g):
    ms = jnp.mean(x * x, axis=-1, keepdims=True)
    return x * lax.rsqrt(ms + EPS) * g


def _dot(a, b):
    return jnp.dot(a, b, preferred_element_type=F32)


def _dot_nt(a, b):
    return lax.dot_general(a, b, (((1,), (1,)), ((), ())), preferred_element_type=F32)


def _lru_coeffs(xc, wgate, bgate, lam):
    gates = _dot(xc.astype(BF16), wgate) + bgate
    r = _sigmoid(gates[:, :LRU_W])
    i = _sigmoid(gates[:, LRU_W:])
    log_a = LRU_C * r * _log_sigmoid(lam)
    a = jnp.exp(log_a)
    th = jnp.tanh(log_a)
    mult = jnp.sqrt(-2.0 * th / (1.0 - th))
    return a, mult, i


def _scan_rows(a, b, h_prev):
    n, w = a.shape
    groups = n // 8
    a3 = a.reshape(groups, 8, w)
    b3 = b.reshape(groups, 8, w)
    row = lax.broadcasted_iota(jnp.int32, a3.shape, 1)
    for s in (1, 2, 4):
        keep = row >= s
        b3 = a3 * jnp.where(keep, pltpu.roll(b3, s, 1), 0.0) + b3
        a3 = a3 * jnp.where(keep, pltpu.roll(a3, s, 1), 1.0)
    carry = h_prev
    outs = []
    for g in range(groups):
        outs.append(a3[g] * carry + b3[g])
        carry = a3[g, 7:8] * carry + b3[g, 7:8]
    return jnp.concatenate(outs, axis=0), carry


def _head_norm_gate(hout, o, ng):
    ms = jnp.sum(hout * hout, axis=-1, keepdims=True) * (1.0 / HD)
    return _sigmoid(o) * (hout * lax.rsqrt(ms + EPS) * ng)


def _pool_select(s2, s4, s8, s16):
    lane = lax.broadcasted_iota(jnp.int32, s2.shape, 1)
    return jnp.where(lane < 64, s2, jnp.where(lane < 128, s4, jnp.where(lane < 192, s8, s16)))


def _pool_window_lanes(shape):
    lane = lax.broadcasted_iota(jnp.int32, shape, 1)
    return jnp.where(lane < 64, 2, jnp.where(lane < 128, 4, jnp.where(lane < 192, 8, 16)))


def _mod_kernel(c_ref, w_ref, b_ref, modp_ref, mods_ref, *, n_prompt):
    c = c_ref[...]
    act = (c * _sigmoid(c)).astype(BF16)
    out = _dot(act, w_ref[...].astype(BF16)) + b_ref[...]
    for r in range(n_prompt):
        modp_ref[r] = out[r:r + 1]
    mods_ref[...] = out[n_prompt:]


def _modulation(c_all, w_ada, b_ada, n_prompt):
    depth = w_ada.shape[0]
    n_all = c_all.shape[0]
    n_mod = w_ada.shape[2]
    tn = 1536
    return pl.pallas_call(
        functools.partial(_mod_kernel, n_prompt=n_prompt),
        grid=(depth, n_mod // tn),
        in_specs=[
            pl.BlockSpec((n_all, D_MODEL), lambda l, j: (0, 0)),
            pl.BlockSpec((None, D_MODEL, tn), lambda l, j: (l, 0, j)),
            pl.BlockSpec((None, 1, tn), lambda l, j: (l, 0, j)),
        ],
        out_specs=[
            pl.BlockSpec((None, n_prompt, 1, tn), lambda l, j: (l, 0, 0, j)),
            pl.BlockSpec((None, n_all - n_prompt, tn), lambda l, j: (l, 0, j)),
        ],
        out_shape=[
            jax.ShapeDtypeStruct((depth, n_prompt, 1, n_mod), F32),
            jax.ShapeDtypeStruct((depth, n_all - n_prompt, n_mod), F32),
        ],
        compiler_params=pltpu.CompilerParams(
            dimension_semantics=("arbitrary", "arbitrary"), vmem_limit_bytes=VMEM_LIMIT),
        name="adaln_modulation",
    )(c_all, w_ada, b_ada)


def _mlstm_chunk(q, k, v, ib_col, lf_col, ib_row, lf_row, c_prev, n_prev, m_prev):
    t_idx = lax.broadcasted_iota(jnp.int32, (CHUNK, CHUNK), 0)
    s_idx = lax.broadcasted_iota(jnp.int32, (CHUNK, CHUNK), 1)
    causal = s_idx <= t_idx
    bcum_col = jnp.sum(jnp.where(causal, lf_row, 0.0), axis=1, keepdims=True)
    bcum_row = jnp.sum(jnp.where(t_idx <= s_idx, lf_col, 0.0), axis=0, keepdims=True)
    dmat = jnp.where(causal, bcum_col - bcum_row + ib_row, -jnp.inf)
    g_col = bcum_col + m_prev
    m_t = jnp.maximum(g_col, jnp.max(dmat, axis=1, keepdims=True))
    inter = jnp.exp(g_col - m_t)
    qb = q.astype(BF16)
    ks = k * (1.0 / math.sqrt(HD))
    kb = ks.astype(BF16)
    s_mat = _dot_nt(qb, kb) * jnp.exp(dmat - m_t)
    num = _dot(s_mat.astype(BF16), v.astype(BF16)) + inter * _dot_nt(qb, c_prev.astype(BF16))
    den = jnp.sum(s_mat, axis=1, keepdims=True) + inter * jnp.sum(q * n_prev, axis=1, keepdims=True)
    hout = num / jnp.maximum(jnp.abs(den), jnp.exp(-m_t))
    b_last = bcum_col[CHUNK - 1:CHUNK, :]
    m_new = m_t[CHUNK - 1:CHUNK, :]
    decay = jnp.exp(b_last + m_prev - m_new)
    w_col = jnp.exp(b_last - bcum_col + ib_col - m_new)
    wv_t = (w_col * v).T.astype(BF16)
    c_new = decay * c_prev + _dot(wv_t, kb)
    n_new = decay * n_prev + jnp.sum(w_col * ks, axis=0, keepdims=True)
    return hout, c_new, n_new, m_new


def _project(x, mod, g1, w_in):
    hn = _rms(x, g1) * (1.0 + mod[:, D_MODEL:2 * D_MODEL]) + mod[:, 0:D_MODEL]
    return _dot(hn.astype(BF16), w_in)


def _mix_tile(proj, x, gt1, j, tm, convw_ref, convb_ref, wgate_ref, bgate_ref,
              lam_ref, gbias_ref, ng_ref, poolw_ref, pscale_ref, wout_ref,
              xo_ref, hlast_ref, convo_ref, co_ref, no_ref, mo_ref, poolo_ref,
              h_sc, xbuf, ubuf, c_sc, n_sc, m_sc, pump=lambda n: None):
    row = lax.broadcasted_iota(jnp.int32, (tm, 1), 0)

    xr = proj[:, C_XR:C_XR + LRU_W]
    gr = proj[:, C_GR:C_GR + LRU_W]
    xbuf[8:8 + tm, :] = xr
    cw = convw_ref[...]
    xc = (convb_ref[...] + cw[3:4] * xr + cw[2:3] * xbuf[7:7 + tm, :]
          + cw[1:2] * xbuf[6:6 + tm, :] + cw[0:1] * xbuf[5:5 + tm, :])
    tail = xbuf[tm:tm + 8, :]
    xbuf[0:8, :] = tail
    convo_ref[0] = tail
    pump(1)
    a, mult, gate_i = _lru_coeffs(xc, wgate_ref[...], bgate_ref[...], lam_ref[...])
    pump(2)
    first = jnp.logical_and(row == 0, j == 0)
    mult = jnp.where(first, 1.0, mult)
    hseq, h_last = _scan_rows(a, mult * gate_i * xc, h_sc[...])
    h_sc[...] = h_last
    hlast_ref[0] = h_last
    pump(2)
    y_a = hseq * _gelu_tanh(gr)
    pump(1)

    gcol = proj[:, C_G:C_G + HP] + gbias_ref[...]
    lane = lax.broadcasted_iota(jnp.int32, gcol.shape, 1)
    gl = jnp.where(jnp.logical_and(lane >= NH, lane < 2 * NH), _log_sigmoid(gcol), gcol)
    gl_t = gl.T
    ng = ng_ref[...]
    state = [(c_sc[h], n_sc[h:h + 1, :], m_sc[h:h + 1, 0:1]) for h in range(NH)]
    outs = [[] for _ in range(NH)]
    for c in range(tm // CHUNK):
        r0 = c * CHUNK
        for h in range(NH):
            q = proj[r0:r0 + CHUNK, C_Q + h * HP:C_Q + (h + 1) * HP]
            k = proj[r0:r0 + CHUNK, C_K + h * HP:C_K + (h + 1) * HP]
            v = proj[r0:r0 + CHUNK, C_V + h * HP:C_V + (h + 1) * HP]
            hout, c_h, n_h, m_h = _mlstm_chunk(
                q, k, v,
                gl[r0:r0 + CHUNK, h:h + 1], gl[r0:r0 + CHUNK, NH + h:NH + h + 1],
                gl_t[h:h + 1, r0:r0 + CHUNK], gl_t[NH + h:NH + h + 1, r0:r0 + CHUNK],
                *state[h])
            state[h] = (c_h, n_h, m_h)
            outs[h].append(hout)
            pump(h % 2)
    y_b_heads = []
    for h in range(NH):
        c_h, n_h, m_h = state[h]
        c_sc[h] = c_h
        n_sc[h:h + 1, :] = n_h
        m_sc[h:h + 1, :] = jnp.broadcast_to(m_h, (1, HP))
        co_ref[0, h] = c_h[0:HD, 0:HD]
        hout_all = jnp.concatenate(outs[h], axis=0) if len(outs[h]) > 1 else outs[h][0]
        o = proj[:, C_O + h * HP:C_O + (h + 1) * HP]
        y_b_heads.append(_head_norm_gate(hout_all, o, ng[:, h * HP:(h + 1) * HP]))
        pump(h % 2)
    no_ref[0] = n_sc[...]
    mo_ref[0] = m_sc[...]

    u = proj[:, C_U:C_U + PW]
    ubuf[16:16 + tm, :] = u
    ext = ubuf[...]
    s2 = ext + pltpu.roll(ext, 1, 0)
    s4 = s2 + pltpu.roll(s2, 2, 0)
    s8 = s4 + pltpu.roll(s4, 4, 0)
    s16 = s8 + pltpu.roll(s8, 8, 0)
    wsum = _pool_select(s2[16:], s4[16:], s8[16:], s16[16:])
    pos = row + j * tm
    cnt = jnp.minimum(pos + 1, _pool_window_lanes((tm, PW))).astype(F32)
    diff = wsum / cnt - u
    y_c = _dot(diff.astype(BF16), poolw_ref[...]) * pscale_ref[...]
    ptail = ubuf[tm:tm + 16, :]
    ubuf[0:16, :] = ptail
    poolo_ref[0] = ptail
    pump(N_IN)

    mix = jnp.concatenate([y_a] + y_b_heads + [y_c], axis=1).astype(BF16)
    xo_ref[0] = x + gt1 * _dot(mix, wout_ref[...])


def _prompt_mixer_kernel(x_ref, xn_ref, mod_ref, modn_ref, g1_ref, win_ref, *rest, tm):
    weights_outs = rest[:17]
    h_sc, xbuf, ubuf, c_sc, n_sc, m_sc, proj_a, proj_b = rest[17:]
    j = pl.program_id(1)
    t = pl.program_id(0) * pl.num_programs(1) + j

    @pl.when(j == 0)
    def _():
        h_sc[...] = jnp.zeros_like(h_sc)
        xbuf[0:8, :] = jnp.zeros((8, LRU_W), F32)
        ubuf[0:16, :] = jnp.zeros((16, PW), F32)
        c_sc[...] = jnp.zeros_like(c_sc)
        n_sc[...] = jnp.zeros_like(n_sc)
        m_sc[...] = jnp.zeros_like(m_sc)

    @pl.when(t == 0)
    def _():
        proj_a[...] = _project(x_ref[0], mod_ref[...], g1_ref[...], win_ref[...])

    def step(proj_cur, proj_nxt):
        modn = modn_ref[...]
        hn = (_rms(xn_ref[0], g1_ref[...]) * (1.0 + modn[:, D_MODEL:2 * D_MODEL])
              + modn[:, 0:D_MODEL]).astype(BF16)
        done = [0]

        def pump(n):
            for _ in range(n):
                c0 = done[0]
                if c0 >= N_IN:
                    return
                c1 = min(c0 + PROJ_PIECE, N_IN)
                proj_nxt[:, c0:c1] = _dot(hn, win_ref[:, c0:c1])
                done[0] = c1

        gt1 = mod_ref[:, 2 * D_MODEL:3 * D_MODEL]
        _mix_tile(proj_cur, x_ref[0], gt1, j, tm, *weights_outs, h_sc, xbuf, ubuf, c_sc, n_sc, m_sc,
                  pump=pump)

    @pl.when(t % 2 == 0)
    def _():
        step(proj_a, proj_b)

    @pl.when(t % 2 == 1)
    def _():
        step(proj_b, proj_a)


def _prompt_mixer(x, modp, l, wts, tm):
    nb, seq, _ = x.shape
    n_t = seq // tm

    def next_tile(b, j):
        t1 = jnp.minimum(b * n_t + j + 1, nb * n_t - 1)
        return t1 // n_t, t1 % n_t

    lsel3 = lambda b, j: (l, 0, 0)
    wspec = lambda a: pl.BlockSpec((None,) + a.shape[1:], lsel3)
    names = ("g1", "w_in", "conv_w", "conv_b", "w_gate", "b_gate", "lam", "gbias", "ng", "pool_w",
             "pool_scale", "w_out")
    out_shapes = [
        jax.ShapeDtypeStruct((nb, seq, D_MODEL), F32),
        jax.ShapeDtypeStruct((nb, 1, LRU_W), F32),
        jax.ShapeDtypeStruct((nb, 8, LRU_W), F32),
        jax.ShapeDtypeStruct((nb, NH, HD, HD), F32),
        jax.ShapeDtypeStruct((nb, 8, HP), F32),
        jax.ShapeDtypeStruct((nb, 8, HP), F32),
        jax.ShapeDtypeStruct((nb, 16, PW), F32),
    ]
    out_specs = [
        pl.BlockSpec((1, tm, D_MODEL), lambda b, j: (b, j, 0)),
        pl.BlockSpec((1, 1, LRU_W), lambda b, j: (b, 0, 0)),
        pl.BlockSpec((1, 8, LRU_W), lambda b, j: (b, 0, 0)),
        pl.BlockSpec((1, NH, HD, HD), lambda b, j: (b, 0, 0, 0)),
        pl.BlockSpec((1, 8, HP), lambda b, j: (b, 0, 0)),
        pl.BlockSpec((1, 8, HP), lambda b, j: (b, 0, 0)),
        pl.BlockSpec((1, 16, PW), lambda b, j: (b, 0, 0)),
    ]
    return pl.pallas_call(
        functools.partial(_prompt_mixer_kernel, tm=tm),
        grid=(nb, seq // tm),
        in_specs=[
            pl.BlockSpec((1, tm, D_MODEL), lambda b, j: (b, j, 0)),
            pl.BlockSpec((1, tm, D_MODEL), lambda b, j: next_tile(b, j) + (0,)),
            pl.BlockSpec((None, None, 1, 3 * D_MODEL), lambda b, j: (l, b, 0, 0)),
            pl.BlockSpec((None, None, 1, 3 * D_MODEL), lambda b, j: (l, next_tile(b, j)[0], 0, 0)),
        ] + [wspec(wts[n]) for n in names],
        out_specs=out_specs,
        out_shape=out_shapes,
        scratch_shapes=[
            pltpu.VMEM((1, LRU_W), F32),
            pltpu.VMEM((tm + 8, LRU_W), F32),
            pltpu.VMEM((tm + 16, PW), F32),
            pltpu.VMEM((NH, HP, HP), F32),
            pltpu.VMEM((8, HP), F32),
            pltpu.VMEM((8, HP), F32),
            pltpu.VMEM((tm, N_IN), F32),
            pltpu.VMEM((tm, N_IN), F32),
        ],
        compiler_params=pltpu.CompilerParams(
            dimension_semantics=("arbitrary", "arbitrary"), vmem_limit_bytes=VMEM_LIMIT),
        name="prompt_mixer",
    )(x, x, modp, modp, *[wts[n] for n in names])


def _ffn_kernel(x_ref, mod_ref, g2_ref, w1_ref, w2_ref, gf_ref, o_ref, *, final):
    x = x_ref[...]
    mod = mod_ref[...]
    sh2 = mod[:, 0:D_MODEL]
    sc2 = mod[:, D_MODEL:2 * D_MODEL]
    gt2 = mod[:, 2 * D_MODEL:3 * D_MODEL]
    h2 = (_rms(x, g2_ref[...]) * (1.0 + sc2) + sh2).astype(BF16)
    acc = jnp.zeros(x.shape, F32)
    step = 1024
    for c in range(D_FF // step):
        hid = _dot(h2, w1_ref[:, c * step:(c + 1) * step])
        hid = jnp.square(jnp.maximum(hid, 0.0)).astype(BF16)
        acc = acc + _dot(hid, w2_ref[c * step:(c + 1) * step, :])
    y = x + gt2 * acc
    if final:
        y = _rms(y, gf_ref[...])
    o_ref[...] = y


def _ffn(x2d, mod, mod_spec, l, wts, final_g, tm, final):
    n_tok = x2d.shape[0]
    lsel3 = lambda i: (l, 0, 0)
    wspec = lambda a: pl.BlockSpec((None,) + a.shape[1:], lsel3, pipeline_mode=pl.Buffered(1))
    return pl.pallas_call(
        functools.partial(_ffn_kernel, final=final),
        grid=(n_tok // tm,),
        in_specs=[
            pl.BlockSpec((tm, D_MODEL), lambda i: (i, 0)),
            mod_spec,
            pl.BlockSpec((None, 1, D_MODEL), lsel3),
            wspec(wts["w_ff1"]),
            wspec(wts["w_ff2"]),
            pl.BlockSpec((1, D_MODEL), lambda i: (0, 0)),
        ],
        out_specs=pl.BlockSpec((tm, D_MODEL), lambda i: (i, 0)),
        out_shape=jax.ShapeDtypeStruct((n_tok, D_MODEL), F32),
        compiler_params=pltpu.CompilerParams(
            dimension_semantics=("arbitrary",), vmem_limit_bytes=VMEM_LIMIT),
        name="ffn",
    )(x2d, mod, wts["g2"], wts["w_ff1"], wts["w_ff2"], final_g)


def _sample_gate_terms(gcol, m0):
    ib = gcol
    lf = pltpu.roll(_log_sigmoid(gcol), HP - NH, 1)
    g = lf + m0
    m_t = jnp.maximum(g, ib)
    inter = jnp.exp(g - m_t)
    p = jnp.exp(ib - m_t)
    return m_t, inter, p


def _expand_heads(t):
    rows = t.shape[0]
    return jnp.concatenate([jnp.broadcast_to(t[:, h:h + 1], (rows, HP)) for h in range(NH)], axis=1)


def _sample_in_kernel(x_ref, mod_ref, g1_ref, win_ref, gbias_ref, m0_ref, proj_ref, wvt_ref, dec_ref):
    x = x_ref[...]
    mod = mod_ref[...]
    hn = _rms(x, g1_ref[...]) * (1.0 + mod[:, D_MODEL:2 * D_MODEL]) + mod[:, 0:D_MODEL]
    proj = _dot(hn.astype(BF16), win_ref[...])
    proj_ref[...] = proj
    _, inter, p = _sample_gate_terms(proj[:, C_G:C_G + HP] + gbias_ref[...], m0_ref[...])
    dec_ref[...] = _expand_heads(inter)
    wv = _expand_heads(p) * proj[:, C_V:C_V + MWP]
    wvt_ref[...] = wv.T


def _sample_in(xs, mods, l, wts, m0p):
    n = xs.shape[0]
    lsel3 = lambda i: (l, 0, 0)
    return pl.pallas_call(
        _sample_in_kernel,
        grid=(1,),
        in_specs=[
            pl.BlockSpec((n, D_MODEL), lambda i: (0, 0)),
            pl.BlockSpec((None, n, 2 * D_MODEL), lsel3),
            pl.BlockSpec((None, 1, D_MODEL), lsel3),
            pl.BlockSpec((None, D_MODEL, N_IN), lsel3),
            pl.BlockSpec((None, 1, HP), lsel3),
            pl.BlockSpec((None, n, HP), lsel3),
        ],
        out_specs=[
            pl.BlockSpec((n, N_IN), lambda i: (0, 0)),
            pl.BlockSpec((MWP, n), lambda i: (0, 0)),
            pl.BlockSpec((n, MWP), lambda i: (0, 0)),
        ],
        out_shape=[
            jax.ShapeDtypeStruct((n, N_IN), F32),
            jax.ShapeDtypeStruct((MWP, n), F32),
            jax.ShapeDtypeStruct((n, MWP), F32),
        ],
        compiler_params=pltpu.CompilerParams(
            dimension_semantics=("arbitrary",), vmem_limit_bytes=VMEM_LIMIT),
        name="sample_in_proj",
    )(xs, mods, wts["g1"], wts["w_in"], wts["gbias"], m0p)


def _sample_state_kernel(proj_ref, wvt_ref, dec_ref, c0_ref, cnew_ref, cqt_ref, *, bt):
    i = pl.program_id(0)

    @pl.when(i == 0)
    def _():
        cqt_ref[...] = jnp.zeros_like(cqt_ref)

    n_tok = wvt_ref.shape[1]
    lane = lax.broadcasted_iota(jnp.int32, (HD, n_tok), 1)
    scale = 1.0 / math.sqrt(HD)

    for h in range(NH):
        q_rows = proj_ref[:, C_Q + h * HP:C_Q + h * HP + HD]
        k_rows = proj_ref[:, C_K + h * HP:C_K + h * HP + HD] * scale
        d_rows = dec_ref[:, h * HP:h * HP + HD]
        wvt_h = wvt_ref[h * HP:h * HP + HD, :]
        cq_acc = cqt_ref[h * HP:h * HP + HD, :]
        for jj in range(bt):
            sel = lane == i * bt + jj
            c_old = c0_ref[jj, h]
            cq = jnp.sum(c_old * q_rows[jj:jj + 1], axis=1, keepdims=True)
            wv_col = jnp.sum(jnp.where(sel, wvt_h, 0.0), axis=1, keepdims=True)
            cnew_ref[jj, h] = d_rows[jj:jj + 1] * c_old + wv_col * k_rows[jj:jj + 1]
            cq_acc = jnp.where(sel, cq, cq_acc)
        cqt_ref[h * HP:h * HP + HD, :] = cq_acc


def _sample_state(proj, wvt, dec, c0_all, l, bt):
    n = proj.shape[0]
    return pl.pallas_call(
        functools.partial(_sample_state_kernel, bt=bt),
        grid=(n // bt,),
        in_specs=[
            pl.BlockSpec((bt, N_IN), lambda i: (i, 0)),
            pl.BlockSpec((MWP, n), lambda i: (0, 0)),
            pl.BlockSpec((bt, MWP), lambda i: (i, 0)),
            pl.BlockSpec((None, bt, NH, HD, HD), lambda i: (l, i, 0, 0, 0)),
        ],
        out_specs=[
            pl.BlockSpec((bt, NH, HD, HD), lambda i: (i, 0, 0, 0)),
            pl.BlockSpec((MWP, n), lambda i: (0, 0)),
        ],
        out_shape=[
            jax.ShapeDtypeStruct((n, NH, HD, HD), F32),
            jax.ShapeDtypeStruct((MWP, n), F32),
        ],
        compiler_params=pltpu.CompilerParams(
            dimension_semantics=("arbitrary",), vmem_limit_bytes=VMEM_LIMIT),
        name="sample_matrix_memory",
    )(proj, wvt, dec, c0_all)


def _sample_mix_kernel(x_ref, gt_ref, proj_ref, cqt_ref, h0_ref, conv_ref, n0_ref, m0_ref, pool_ref,
                       convw_ref, convb_ref, wgate_ref, bgate_ref, lam_ref, gbias_ref, ng_ref,
                       poolw_ref, pscale_ref, wout_ref,
                       xo_ref, ho_ref, convo_ref, no_ref, mo_ref, poolo_ref):
    proj = proj_ref[...]

    xr = proj[:, C_XR:C_XR + LRU_W]
    gr = proj[:, C_GR:C_GR + LRU_W]
    cs = conv_ref[...]
    cw = convw_ref[...]
    xc = (convb_ref[...] + cw[3:4] * xr + cw[2:3] * cs[:, 2 * LRU_W:3 * LRU_W]
          + cw[1:2] * cs[:, LRU_W:2 * LRU_W] + cw[0:1] * cs[:, 0:LRU_W])
    convo_ref[...] = jnp.concatenate([cs[:, LRU_W:], xr], axis=1)
    a, mult, gate_i = _lru_coeffs(xc, wgate_ref[...], bgate_ref[...], lam_ref[...])
    h_new = a * h0_ref[...] + mult * gate_i * xc
    ho_ref[...] = h_new
    y_a = h_new * _gelu_tanh(gr)

    m_t, inter, p = _sample_gate_terms(proj[:, C_G:C_G + HP] + gbias_ref[...], m0_ref[...])
    mo_ref[...] = m_t
    e_floor = jnp.exp(-m_t)
    cq = cqt_ref[...].T
    n0 = n0_ref[...]
    ng = ng_ref[...]
    y_b_heads, n_heads = [], []
    for h in range(NH):
        sl = slice(h * HP, (h + 1) * HP)
        q = proj[:, C_Q + h * HP:C_Q + (h + 1) * HP]
        ks = proj[:, C_K + h * HP:C_K + (h + 1) * HP] * (1.0 / math.sqrt(HD))
        v = proj[:, C_V + h * HP:C_V + (h + 1) * HP]
        o = proj[:, C_O + h * HP:C_O + (h + 1) * HP]
        p_h, inter_h, e_h = p[:, h:h + 1], inter[:, h:h + 1], e_floor[:, h:h + 1]
        s = jnp.sum(q * ks, axis=1, keepdims=True) * p_h
        num = s * v + inter_h * cq[:, sl]
        den = s + inter_h * jnp.sum(n0[:, sl] * q, axis=1, keepdims=True)
        hout = num / jnp.maximum(jnp.abs(den), e_h)
        n_heads.append(inter_h * n0[:, sl] + p_h * ks)
        y_b_heads.append(_head_norm_gate(hout, o, ng[:, sl]))
    no_ref[...] = jnp.concatenate(n_heads, axis=1)

    u = proj[:, C_U:C_U + PW]
    hist = pool_ref[...]
    back = lambda jdx: hist[:, (POOL_BUF - jdx) * PW:(POOL_BUF - jdx + 1) * PW]
    sums, acc, nxt = {}, u, 1
    for win in POOL_WINDOWS:
        while nxt < win:
            acc = acc + back(nxt)
            nxt += 1
        sums[win] = acc
    wsum = _pool_select(sums[2], sums[4], sums[8], sums[16])
    cnt = jnp.minimum(PAST_LEN + 1, _pool_window_lanes(u.shape)).astype(F32)
    diff = wsum / cnt - u
    y_c = _dot(diff.astype(BF16), poolw_ref[...]) * pscale_ref[...]
    poolo_ref[...] = jnp.concatenate([hist[:, PW:], u], axis=1)

    mix = jnp.concatenate([y_a] + y_b_heads + [y_c], axis=1).astype(BF16)
    xo_ref[...] = x_ref[...] + gt_ref[...] * _dot(mix, wout_ref[...])


def _sample_mix(xs, mods, proj, cqt, l, st, wts):
    n = xs.shape[0]
    lsel3 = lambda i: (l, 0, 0)
    full = lambda a: pl.BlockSpec(a.shape, lambda i: (0, 0))
    lspec = lambda a: pl.BlockSpec((None,) + a.shape[1:], lsel3)
    snames = ("lru_h", "conv", "n", "m", "pool")
    wnames = ("conv_w", "conv_b", "w_gate", "b_gate", "lam", "gbias", "ng", "pool_w", "pool_scale", "w_out")
    out_shapes = [
        jax.ShapeDtypeStruct((n, D_MODEL), F32),
        jax.ShapeDtypeStruct((n, LRU_W), F32),
        jax.ShapeDtypeStruct((n, 3 * LRU_W), F32),
        jax.ShapeDtypeStruct((n, MWP), F32),
        jax.ShapeDtypeStruct((n, HP), F32),
        jax.ShapeDtypeStruct((n, POOL_BUF * PW), F32),
    ]
    return pl.pallas_call(
        _sample_mix_kernel,
        grid=(1,),
        in_specs=[
            full(xs),
            pl.BlockSpec((None, n, D_MODEL), lambda i: (l, 0, 2)),
            full(proj), full(cqt),
        ] + [lspec(st[k]) for k in snames] + [lspec(wts[k]) for k in wnames],
        out_specs=[pl.BlockSpec(s.shape, lambda i: (0, 0)) for s in out_shapes],
        out_shape=out_shapes,
        compiler_params=pltpu.CompilerParams(
            dimension_semantics=("arbitrary",), vmem_limit_bytes=VMEM_LIMIT),
        name="sample_mixer",
    )(xs, mods, proj, cqt, *[st[k] for k in snames], *[wts[k] for k in wnames])


def _pad_heads(w):
    lead = w.shape[:-1]
    w = w.reshape(lead + (NH, HD))
    w = jnp.pad(w, [(0, 0)] * len(lead) + [(0, 0), (0, HP - HD)])
    return w.reshape(lead + (MWP,))


def _block_diag(w):
    depth, g, n, _ = w.shape
    eye = jnp.eye(g, dtype=w.dtype)
    return (w[:, :, :, None, :] * eye[None, :, None, :, None]).reshape(depth, g * n, g * n)


def _prepare_weights(norm1_g, norm2_g, w_in, conv_w, conv_b, lru_wa, lru_ba, lru_wx, lru_bx, lru_lam,
                     mlstm_bi, mlstm_bf, mlstm_norm_g, pool_w, pool_scale, w_out, w_ff1, w_ff2):
    depth = w_in.shape[0]
    o0 = 2 * LRU_W
    seg = lambda k: w_in[:, :, o0 + k * MW:o0 + (k + 1) * MW]
    g0 = o0 + 4 * MW
    gates = jnp.pad(w_in[:, :, g0:g0 + 2 * NH], ((0, 0), (0, 0), (0, HP - 2 * NH)))
    w_in_p = jnp.concatenate(
        [w_in[:, :, 0:o0]] + [_pad_heads(seg(k)) for k in range(4)]
        + [w_in[:, :, g0 + 2 * NH:], gates], axis=-1).astype(BF16)
    wo_b = w_out[:, LRU_W:LRU_W + MW].reshape(depth, NH, HD, D_MODEL)
    wo_b = jnp.pad(wo_b, ((0, 0), (0, 0), (0, HP - HD), (0, 0))).reshape(depth, MWP, D_MODEL)
    w_out_p = jnp.concatenate([w_out[:, 0:LRU_W], wo_b, w_out[:, LRU_W + MW:]], axis=1).astype(BF16)
    row = lambda v: v.reshape(depth, 1, -1)
    return {
        "g1": row(norm1_g), "g2": row(norm2_g),
        "w_in": w_in_p, "w_out": w_out_p,
        "conv_w": conv_w, "conv_b": row(conv_b),
        "w_gate": jnp.concatenate([_block_diag(lru_wa), _block_diag(lru_wx)], axis=-1).astype(BF16),
        "b_gate": jnp.concatenate([row(lru_ba), row(lru_bx)], axis=-1),
        "lam": row(lru_lam),
        "gbias": jnp.pad(jnp.concatenate([mlstm_bi, mlstm_bf], axis=-1), ((0, 0), (0, HP - 2 * NH)))[:, None, :],
        "ng": row(_pad_heads(mlstm_norm_g.reshape(depth, MW))),
        "pool_w": _block_diag(pool_w).astype(BF16),
        "pool_scale": row(pool_scale),
        "w_ff1": w_ff1.astype(BF16), "w_ff2": w_ff2.astype(BF16),
    }


def kernel(x_prompt, x_sample, state_lru_h, state_lru_conv, state_mlstm_C, state_mlstm_n, state_mlstm_m,
           state_pool, c_prompt, c_sample, norm1_g, norm2_g, w_ada, b_ada, w_in, conv_w, conv_b, lru_wa,
           lru_ba, lru_wx, lru_bx, lru_lam, mlstm_bi, mlstm_bf, mlstm_norm_g, pool_w, pool_scale, w_out,
           w_ff1, w_ff2, final_g):
    nb, seq, _ = x_prompt.shape
    ns = x_sample.shape[0]
    depth = w_in.shape[0]
    wts = _prepare_weights(norm1_g, norm2_g, w_in, conv_w, conv_b, lru_wa, lru_ba, lru_wx, lru_bx, lru_lam,
                           mlstm_bi, mlstm_bf, mlstm_norm_g, pool_w, pool_scale, w_out, w_ff1, w_ff2)
    final_row = final_g.reshape(1, D_MODEL)
    modp, mods = _modulation(jnp.concatenate([c_prompt, c_sample], axis=0), w_ada,
                             b_ada.reshape(depth, 1, -1), nb)

    sample_state = {
        "lru_h": state_lru_h,
        "conv": state_lru_conv.reshape(depth, ns, 3 * LRU_W),
        "n": _pad_heads(state_mlstm_n.reshape(depth, ns, MW)),
        "m": jnp.pad(state_mlstm_m, ((0, 0), (0, 0), (0, HP - NH))),
        "pool": state_pool.reshape(depth, ns, POOL_BUF * PW),
    }

    xp = x_prompt
    xs = x_sample.reshape(ns, D_MODEL)
    p_out = [[] for _ in range(6)]
    s_out = [[] for _ in range(6)]
    for l in range(depth):
        last = l == depth - 1
        xp, h_l, conv_l, c_l, n_l, m_l, pool_l = _prompt_mixer(xp, modp, l, wts, TM_MIX)
        mod_spec = pl.BlockSpec((None, None, 1, 3 * D_MODEL),
                                lambda i, l=l: (l, i // (seq // TM_FFN), 0, 1))
        xp = _ffn(xp.reshape(nb * seq, D_MODEL), modp, mod_spec, l, wts, final_row, TM_FFN, last)
        xp = xp.reshape(nb, seq, D_MODEL)
        for lst, val in zip(p_out, (h_l.reshape(nb, LRU_W), conv_l[:, 8 - 3:], c_l, n_l[:, :NH, :HD],
                                    m_l[:, :NH, 0], pool_l[:, 16 - POOL_BUF:])):
            lst.append(val)
        proj, wvt, dec = _sample_in(xs, mods, l, wts, sample_state["m"])
        c_new, cqt = _sample_state(proj, wvt, dec, state_mlstm_C, l, BT_STATE)
        xs, h_s, conv_s, n_s, m_s, pool_s = _sample_mix(xs, mods, proj, cqt, l, sample_state, wts)
        mod_spec = pl.BlockSpec((None, ns, 3 * D_MODEL), lambda i, l=l: (l, 0, 1))
        xs = _ffn(xs, mods, mod_spec, l, wts, final_row, ns, last)
        for lst, val in zip(s_out, (h_s, conv_s.reshape(ns, 3, LRU_W), c_new,
                                    n_s.reshape(ns, NH, HP)[:, :, :HD], m_s[:, :NH],
                                    pool_s.reshape(ns, POOL_BUF, PW))):
            lst.append(val)

    stack = lambda lst: jnp.stack(lst, axis=0)
    return (xp, xs.reshape(ns, 1, D_MODEL),
            *[stack(v) for v in p_out], *[stack(v) for v in s_out])
```

```python
import functools
import math

import jax
import jax.numpy as jnp
from jax import lax
from jax.experimental import pallas as pl
from jax.experimental.pallas import tpu as pltpu

F32 = jnp.float32
BF16 = jnp.bfloat16

D_MODEL = 1024
LRU_W = 384
LRU_HEADS = 6
LRU_C = 8.0
NH = 4
HD = 96
HP = 128
MW = NH * HD
MWP = NH * HP
CHUNK = 128
PW = 256
POOL_WINDOWS = (2, 4, 8, 16)
POOL_GROUP = 64
POOL_BUF = 15
D_FF = 4096
EPS = 1e-6
PAST_LEN = 16384

C_XR, C_GR, C_Q, C_K, C_V, C_O, C_U, C_G = 0, 384, 768, 1280, 1792, 2304, 2816, 3072
N_IN = 3200
K_OUT = LRU_W + MWP + PW

VMEM_LIMIT = 56 * 1024 * 1024
TM_MIX = 256
PROJ_PIECE = 256
TM_FFN = 512
VB_STATE = 32


def _log_sigmoid(x):
    return jnp.minimum(x, 0.0) - jnp.log1p(jnp.exp(-jnp.abs(x)))


def _sigmoid(x):
    return 1.0 / (1.0 + jnp.exp(-x))


def _gelu_tanh(x):
    c = math.sqrt(2.0 / math.pi)
    return 0.5 * x * (1.0 + jnp.tanh(c * (x + 0.044715 * (x * x * x))))


def _rms(x, g):
    ms = jnp.mean(x * x, axis=-1, keepdims=True)
    return x * lax.rsqrt(ms + EPS) * g


def _ada_norm(x, g, mod):
    ms = jnp.mean(x * x, axis=-1, keepdims=True)
    gain = g * (1.0 + mod[:, D_MODEL:2 * D_MODEL])
    return x * lax.rsqrt(ms + EPS) * gain + mod[:, 0:D_MODEL]


def _dot(a, b):
    return jnp.dot(a, b, preferred_element_type=F32)


def _dot_nt(a, b):
    return lax.dot_general(a, b, (((1,), (1,)), ((), ())), preferred_element_type=F32)


def _lru_coeffs(gates, log_lam):
    r = _sigmoid(gates[:, :LRU_W])
    i = _sigmoid(gates[:, LRU_W:])
    log_a = r * log_lam
    a = jnp.exp(log_a)
    th = jnp.tanh(log_a)
    mult = jnp.sqrt(-2.0 * th / (1.0 - th))
    return a, mult, i


def _scan_rows(a, b, h_prev):
    n, w = a.shape
    groups = n // 8
    a3 = a.reshape(groups, 8, w)
    b3 = b.reshape(groups, 8, w)
    row = lax.broadcasted_iota(jnp.int32, a3.shape, 1)
    for s in (1, 2, 4):
        keep = row >= s
        b3 = a3 * jnp.where(keep, pltpu.roll(b3, s, 1), 0.0) + b3
        a3 = a3 * jnp.where(keep, pltpu.roll(a3, s, 1), 1.0)
    carry = h_prev
    outs = []
    for g in range(groups):
        outs.append(a3[g] * carry + b3[g])
        carry = a3[g, 7:8] * carry + b3[g, 7:8]
    return jnp.concatenate(outs, axis=0), carry


def _head_norm_gate(hout, o, ng):
    ms = jnp.sum(hout * hout, axis=-1, keepdims=True) * (1.0 / HD)
    return _sigmoid(o) * (hout * lax.rsqrt(ms + EPS) * ng)


def _pool_select(s2, s4, s8, s16):
    lane = lax.broadcasted_iota(jnp.int32, s2.shape, 1)
    return jnp.where(lane < 64, s2, jnp.where(lane < 128, s4, jnp.where(lane < 192, s8, s16)))


def _pool_window_lanes(shape):
    lane = lax.broadcasted_iota(jnp.int32, shape, 1)
    return jnp.where(lane < 64, 2, jnp.where(lane < 128, 4, jnp.where(lane < 192, 8, 16)))


def _mod_kernel(c_ref, w_ref, b_ref, modp_ref, mods_ref, *, n_prompt):
    c = c_ref[...]
    act = (c * _sigmoid(c)).astype(BF16)
    out = _dot(act, w_ref[...].astype(BF16)) + b_ref[...]
    for r in range(n_prompt):
        modp_ref[r] = out[r:r + 1]
    mods_ref[...] = out[n_prompt:]


def _modulation(c_all, w_ada, b_ada, n_prompt):
    depth = w_ada.shape[0]
    n_all = c_all.shape[0]
    n_mod = w_ada.shape[2]
    tn = 1536
    return pl.pallas_call(
        functools.partial(_mod_kernel, n_prompt=n_prompt),
        grid=(depth, n_mod // tn),
        in_specs=[
            pl.BlockSpec((n_all, D_MODEL), lambda l, j: (0, 0)),
            pl.BlockSpec((None, D_MODEL, tn), lambda l, j: (l, 0, j)),
            pl.BlockSpec((None, 1, tn), lambda l, j: (l, 0, j)),
        ],
        out_specs=[
            pl.BlockSpec((None, n_prompt, 1, tn), lambda l, j: (l, 0, 0, j)),
            pl.BlockSpec((None, n_all - n_prompt, tn), lambda l, j: (l, 0, j)),
        ],
        out_shape=[
            jax.ShapeDtypeStruct((depth, n_prompt, 1, n_mod), F32),
            jax.ShapeDtypeStruct((depth, n_all - n_prompt, n_mod), F32),
        ],
        compiler_params=pltpu.CompilerParams(
            dimension_semantics=("arbitrary", "arbitrary"), vmem_limit_bytes=VMEM_LIMIT),
        name="adaln_modulation",
    )(c_all, w_ada, b_ada)


def _mlstm_tile(proj, gl, gl_t, state, n_chunks, pump):
    pairs = [(c, h) for c in range(n_chunks) for h in range(NH)]
    t_idx = lax.broadcasted_iota(jnp.int32, (CHUNK, CHUNK), 0)
    s_idx = lax.broadcasted_iota(jnp.int32, (CHUNK, CHUNK), 1)
    causal = s_idx <= t_idx
    upper = t_idx <= s_idx
    rows = lambda c: slice(c * CHUNK, (c + 1) * CHUNK)
    head = lambda base, h: slice(base + h * HP, base + (h + 1) * HP)

    bcum_col, dmat, rowmax, ib_col = {}, {}, {}, {}
    for c, h in pairs:
        lf_row = gl_t[NH + h:NH + h + 1, rows(c)]
        lf_col = gl[rows(c), NH + h:NH + h + 1]
        ib_col[c, h] = gl[rows(c), h:h + 1]
        bcum_col[c, h] = jnp.sum(jnp.where(causal, lf_row, 0.0), axis=1, keepdims=True)
        bcum_row = jnp.sum(jnp.where(upper, lf_col, 0.0), axis=0, keepdims=True)
        dmat[c, h] = jnp.where(causal, bcum_col[c, h] - bcum_row + gl_t[h:h + 1, rows(c)], -jnp.inf)
        rowmax[c, h] = jnp.max(dmat[c, h], axis=1, keepdims=True)
    pump(1)

    m_start, m_t = {}, {}
    m_run = [state[h][2] for h in range(NH)]
    for c, h in pairs:
        m_start[c, h] = m_run[h]
        m_t[c, h] = jnp.maximum(bcum_col[c, h] + m_run[h], rowmax[c, h])
        m_run[h] = m_t[c, h][CHUNK - 1:CHUNK, :]
    inter, pmat, floor, decay, w_col = {}, {}, {}, {}, {}
    for c, h in pairs:
        b_last = bcum_col[c, h][CHUNK - 1:CHUNK, :]
        inter[c, h] = jnp.exp(bcum_col[c, h] + m_start[c, h] - m_t[c, h])
        pmat[c, h] = jnp.exp(dmat[c, h] - m_t[c, h])
        floor[c, h] = jnp.exp(-m_t[c, h])
        m_new = m_t[c, h][CHUNK - 1:CHUNK, :]
        decay[c, h] = jnp.exp(b_last + m_start[c, h] - m_new)
        w_col[c, h] = jnp.exp(b_last - bcum_col[c, h] + ib_col[c, h] - m_new)
    pump(1)

    qb, kb, s_mat, sv, wvk, wk = {}, {}, {}, {}, {}, {}
    for c, h in pairs:
        qb[c, h] = proj[rows(c), head(C_Q, h)].astype(BF16)
        ks = proj[rows(c), head(C_K, h)] * (1.0 / math.sqrt(HD))
        kb[c, h] = ks.astype(BF16)
        wk[c, h] = jnp.sum(w_col[c, h] * ks, axis=0, keepdims=True)
        s_mat[c, h] = _dot_nt(qb[c, h], kb[c, h]) * pmat[c, h]
    pump(1)
    for c, h in pairs:
        v = proj[rows(c), head(C_V, h)]
        sv[c, h] = _dot(s_mat[c, h].astype(BF16), v.astype(BF16))
        wvk[c, h] = _dot((w_col[c, h] * v).T.astype(BF16), kb[c, h])
    pump(1)

    outs = [[] for _ in range(NH)]
    c_run = [state[h][0] for h in range(NH)]
    n_run = [state[h][1] for h in range(NH)]
    for c, h in pairs:
        q = proj[rows(c), head(C_Q, h)]
        qc = _dot_nt(qb[c, h], c_run[h].astype(BF16))
        qn = jnp.sum(q * n_run[h], axis=1, keepdims=True)
        num = sv[c, h] + inter[c, h] * qc
        den = jnp.sum(s_mat[c, h], axis=1, keepdims=True) + inter[c, h] * qn
        outs[h].append(num / jnp.maximum(jnp.abs(den), floor[c, h]))
        c_run[h] = decay[c, h] * c_run[h] + wvk[c, h]
        n_run[h] = decay[c, h] * n_run[h] + wk[c, h]
    return outs, [(c_run[h], n_run[h], m_run[h]) for h in range(NH)]


def _project(x, mod, g1, w_in):
    return _dot(_ada_norm(x, g1, mod).astype(BF16), w_in)


def _mix_tile(proj, x, gt1, j, tm, convw_ref, convb_ref, wgate_ref, bgate_ref,
              lam_ref, gbias_ref, ng_ref, poolw_ref, pscale_ref, wout_ref,
              xo_ref, hlast_ref, convo_ref, co_ref, no_ref, mo_ref, poolo_ref,
              h_sc, xbuf, ubuf, c_sc, n_sc, m_sc, pump=lambda n: None):
    row = lax.broadcasted_iota(jnp.int32, (tm, 1), 0)

    xr = proj[:, C_XR:C_XR + LRU_W]
    gr = proj[:, C_GR:C_GR + LRU_W]
    xbuf[8:8 + tm, :] = xr
    cw = convw_ref[...]
    xc = (convb_ref[...] + cw[3:4] * xr + cw[2:3] * xbuf[7:7 + tm, :]
          + cw[1:2] * xbuf[6:6 + tm, :] + cw[0:1] * xbuf[5:5 + tm, :])
    tail = xbuf[tm:tm + 8, :]
    xbuf[0:8, :] = tail
    convo_ref[0] = tail
    gates = _dot(xc.astype(BF16), wgate_ref[...]) + bgate_ref[...]
    pump(0)
    log_lam = LRU_C * _log_sigmoid(lam_ref[...])
    h_run = h_sc[...]
    y_a_parts = []
    n_split = 2
    for r in range(n_split):
        rs = slice(r * tm // n_split, (r + 1) * tm // n_split)
        a, mult, gate_i = _lru_coeffs(gates[rs], log_lam)
        if r == 0:
            mult = jnp.where(jnp.logical_and(row[rs] == 0, j == 0), 1.0, mult)
        pump(2)
        hseq, h_run = _scan_rows(a, mult * gate_i * xc[rs], h_run)
        pump(1)
        y_a_parts.append(hseq * _gelu_tanh(gr[rs]))
    h_sc[...] = h_run
    hlast_ref[0] = h_run
    y_a = jnp.concatenate(y_a_parts, axis=0)
    out_acc = _dot(y_a.astype(BF16), wout_ref[0:LRU_W, :])

    gcol = proj[:, C_G:C_G + HP] + gbias_ref[...]
    lane = lax.broadcasted_iota(jnp.int32, gcol.shape, 1)
    gl = jnp.where(jnp.logical_and(lane >= NH, lane < 2 * NH), _log_sigmoid(gcol), gcol)
    gl_t = gl.T
    ng = ng_ref[...]
    state = [(c_sc[h], n_sc[h:h + 1, :], m_sc[h:h + 1, 0:1]) for h in range(NH)]
    outs, state = _mlstm_tile(proj, gl, gl_t, state, tm // CHUNK, pump)
    y_b_heads = []
    for h in range(NH):
        c_h, n_h, m_h = state[h]
        c_sc[h] = c_h
        n_sc[h:h + 1, :] = n_h
        m_sc[h:h + 1, :] = jnp.broadcast_to(m_h, (1, HP))
        co_ref[0, h] = c_h[0:HD, 0:HD]
        hout_all = jnp.concatenate(outs[h], axis=0) if len(outs[h]) > 1 else outs[h][0]
        o = proj[:, C_O + h * HP:C_O + (h + 1) * HP]
        y_b_heads.append(_head_norm_gate(hout_all, o, ng[:, h * HP:(h + 1) * HP]))
        pump(h % 2)
    no_ref[0] = n_sc[...]
    mo_ref[0] = m_sc[...]
    y_b = jnp.concatenate(y_b_heads, axis=1).astype(BF16)
    out_acc = out_acc + _dot(y_b, wout_ref[LRU_W:LRU_W + MWP, :])

    u = proj[:, C_U:C_U + PW]
    ubuf[16:16 + tm, :] = u
    ext = ubuf[...]
    s2 = ext + pltpu.roll(ext, 1, 0)
    s4 = s2 + pltpu.roll(s2, 2, 0)
    s8 = s4 + pltpu.roll(s4, 4, 0)
    s16 = s8 + pltpu.roll(s8, 8, 0)
    wsum = _pool_select(s2[16:], s4[16:], s8[16:], s16[16:])
    pos = row + j * tm
    cnt = jnp.minimum(pos + 1, _pool_window_lanes((tm, PW))).astype(F32)
    diff = wsum / cnt - u
    y_c = _dot(diff.astype(BF16), poolw_ref[...]) * pscale_ref[...]
    ptail = ubuf[tm:tm + 16, :]
    ubuf[0:16, :] = ptail
    poolo_ref[0] = ptail
    pump(N_IN)

    out_acc = out_acc + _dot(y_c.astype(BF16), wout_ref[LRU_W + MWP:K_OUT, :])
    xo_ref[0] = x + gt1 * out_acc


def _prompt_mixer_kernel(x_ref, xn_ref, mod_ref, modn_ref, g1_ref, win_ref, *rest, tm):
    weights_outs = rest[:17]
    h_sc, xbuf, ubuf, c_sc, n_sc, m_sc, proj_a, proj_b = rest[17:]
    j = pl.program_id(1)
    t = pl.program_id(0) * pl.num_programs(1) + j

    @pl.when(j == 0)
    def _():
        h_sc[...] = jnp.zeros_like(h_sc)
        xbuf[0:8, :] = jnp.zeros((8, LRU_W), F32)
        ubuf[0:16, :] = jnp.zeros((16, PW), F32)
        c_sc[...] = jnp.zeros_like(c_sc)
        n_sc[...] = jnp.zeros_like(n_sc)
        m_sc[...] = jnp.zeros_like(m_sc)

    @pl.when(t == 0)
    def _():
        proj_a[...] = _project(x_ref[0], mod_ref[...], g1_ref[...], win_ref[...])

    def step(proj_cur, proj_nxt):
        done = [0]
        lhs = []

        def pump(n):
            if not lhs:
                lhs.append(_ada_norm(xn_ref[0], g1_ref[...], modn_ref[...]).astype(BF16))
            for _ in range(n):
                c0 = done[0]
                if c0 >= N_IN:
                    return
                c1 = min(c0 + PROJ_PIECE, N_IN)
                proj_nxt[:, c0:c1] = _dot(lhs[0], win_ref[:, c0:c1])
                done[0] = c1

        gt1 = mod_ref[:, 2 * D_MODEL:3 * D_MODEL]
        _mix_tile(proj_cur, x_ref[0], gt1, j, tm, *weights_outs, h_sc, xbuf, ubuf, c_sc, n_sc, m_sc,
                  pump=pump)

    @pl.when(t % 2 == 0)
    def _():
        step(proj_a, proj_b)

    @pl.when(t % 2 == 1)
    def _():
        step(proj_b, proj_a)


def _prompt_mixer(x, modp, l, wts, tm):
    nb, seq, _ = x.shape
    n_t = seq // tm

    def next_tile(b, j):
        t1 = jnp.minimum(b * n_t + j + 1, nb * n_t - 1)
        return t1 // n_t, t1 % n_t

    lsel3 = lambda b, j: (l, 0, 0)
    wspec = lambda a: pl.BlockSpec((None,) + a.shape[1:], lsel3)
    names = ("g1", "w_in", "conv_w", "conv_b", "w_gate", "b_gate", "lam", "gbias", "ng", "pool_w",
             "pool_scale", "w_out")
    out_shapes = [
        jax.ShapeDtypeStruct((nb, seq, D_MODEL), F32),
        jax.ShapeDtypeStruct((nb, 1, LRU_W), F32),
        jax.ShapeDtypeStruct((nb, 8, LRU_W), F32),
        jax.ShapeDtypeStruct((nb, NH, HD, HD), F32),
        jax.ShapeDtypeStruct((nb, 8, HP), F32),
        jax.ShapeDtypeStruct((nb, 8, HP), F32),
        jax.ShapeDtypeStruct((nb, 16, PW), F32),
    ]
    out_specs = [
        pl.BlockSpec((1, tm, D_MODEL), lambda b, j: (b, j, 0)),
        pl.BlockSpec((1, 1, LRU_W), lambda b, j: (b, 0, 0)),
        pl.BlockSpec((1, 8, LRU_W), lambda b, j: (b, 0, 0)),
        pl.BlockSpec((1, NH, HD, HD), lambda b, j: (b, 0, 0, 0)),
        pl.BlockSpec((1, 8, HP), lambda b, j: (b, 0, 0)),
        pl.BlockSpec((1, 8, HP), lambda b, j: (b, 0, 0)),
        pl.BlockSpec((1, 16, PW), lambda b, j: (b, 0, 0)),
    ]
    return pl.pallas_call(
        functools.partial(_prompt_mixer_kernel, tm=tm),
        grid=(nb, seq // tm),
        in_specs=[
            pl.BlockSpec((1, tm, D_MODEL), lambda b, j: (b, j, 0)),
            pl.BlockSpec((1, tm, D_MODEL), lambda b, j: next_tile(b, j) + (0,)),
            pl.BlockSpec((None, None, 1, 3 * D_MODEL), lambda b, j: (l, b, 0, 0)),
            pl.BlockSpec((None, None, 1, 3 * D_MODEL), lambda b, j: (l, next_tile(b, j)[0], 0, 0)),
        ] + [wspec(wts[n]) for n in names],
        out_specs=out_specs,
        out_shape=out_shapes,
        scratch_shapes=[
            pltpu.VMEM((1, LRU_W), F32),
            pltpu.VMEM((tm + 8, LRU_W), F32),
            pltpu.VMEM((tm + 16, PW), F32),
            pltpu.VMEM((NH, HP, HP), F32),
            pltpu.VMEM((8, HP), F32),
            pltpu.VMEM((8, HP), F32),
            pltpu.VMEM((tm, N_IN), F32),
            pltpu.VMEM((tm, N_IN), F32),
        ],
        compiler_params=pltpu.CompilerParams(
            dimension_semantics=("arbitrary", "arbitrary"), vmem_limit_bytes=VMEM_LIMIT),
        name="prompt_mixer",
    )(x, x, modp, modp, *[wts[n] for n in names])


def _ffn_kernel(x_ref, mod_ref, g2_ref, w1_ref, w2_ref, gf_ref, o_ref, *, final):
    x = x_ref[...]
    mod = mod_ref[...]
    gt2 = mod[:, 2 * D_MODEL:3 * D_MODEL]
    h2 = _ada_norm(x, g2_ref[...], mod).astype(BF16)
    acc = jnp.zeros(x.shape, F32)
    step = 1024
    for c in range(D_FF // step):
        hid = _dot(h2, w1_ref[:, c * step:(c + 1) * step])
        hid = jnp.square(jnp.maximum(hid, 0.0)).astype(BF16)
        acc = acc + _dot(hid, w2_ref[c * step:(c + 1) * step, :])
    y = x + gt2 * acc
    if final:
        y = _rms(y, gf_ref[...])
    o_ref[...] = y


def _ffn(x2d, mod, mod_spec, l, wts, final_g, tm, final):
    n_tok = x2d.shape[0]
    lsel3 = lambda i: (l, 0, 0)
    wspec = lambda a: pl.BlockSpec((None,) + a.shape[1:], lsel3, pipeline_mode=pl.Buffered(1))
    return pl.pallas_call(
        functools.partial(_ffn_kernel, final=final),
        grid=(n_tok // tm,),
        in_specs=[
            pl.BlockSpec((tm, D_MODEL), lambda i: (i, 0)),
            mod_spec,
            pl.BlockSpec((None, 1, D_MODEL), lsel3),
            wspec(wts["w_ff1"]),
            wspec(wts["w_ff2"]),
            pl.BlockSpec((1, D_MODEL), lambda i: (0, 0)),
        ],
        out_specs=pl.BlockSpec((tm, D_MODEL), lambda i: (i, 0)),
        out_shape=jax.ShapeDtypeStruct((n_tok, D_MODEL), F32),
        compiler_params=pltpu.CompilerParams(
            dimension_semantics=("arbitrary",), vmem_limit_bytes=VMEM_LIMIT),
        name="ffn",
    )(x2d, mod, wts["g2"], wts["w_ff1"], wts["w_ff2"], final_g)


def _sample_in_kernel(x_ref, mod_ref, g1_ref, win_ref, gbias_ref, m0_ref,
                      proj_ref, qt_ref, kst_ref, vt_ref, wvt_ref, gs_ref, dec_ref, mo_ref):
    proj = _project(x_ref[...], mod_ref[...], g1_ref[...], win_ref[...])
    proj_ref[...] = proj
    g_t = (proj[:, C_G:C_G + HP] + gbias_ref[...]).T
    ib = g_t[0:NH]
    g = _log_sigmoid(g_t[NH:2 * NH]) + m0_ref[...]
    m_t = jnp.maximum(g, ib)
    inter = jnp.exp(g - m_t)
    p = jnp.exp(ib - m_t)
    mo_ref[...] = m_t
    gs_ref[0] = inter
    gs_ref[1] = p
    gs_ref[2] = jnp.exp(-m_t)
    q_t = proj[:, C_Q:C_Q + MWP].T
    ks_t = (proj[:, C_K:C_K + MWP] * (1.0 / math.sqrt(HD))).T
    v_t = proj[:, C_V:C_V + MWP].T
    for h in range(NH):
        hs = slice(h * HP, (h + 1) * HP)
        qt_ref[h] = q_t[hs]
        kst_ref[h] = ks_t[hs]
        vt_ref[h] = v_t[hs]
        wvt_ref[h] = p[h:h + 1] * v_t[hs]
        dec_ref[h] = inter[h:h + 1]


def _sample_in(xs, mods, l, wts, m0_t):
    n = xs.shape[0]
    lsel3 = lambda i: (l, 0, 0)
    head_t = jax.ShapeDtypeStruct((NH, HP, n), F32)
    out_shapes = [
        jax.ShapeDtypeStruct((n, N_IN), F32),
        head_t, head_t, head_t, head_t,
        jax.ShapeDtypeStruct((3, NH, n), F32),
        jax.ShapeDtypeStruct((NH, 1, n), F32),
        jax.ShapeDtypeStruct((NH, n), F32),
    ]
    return pl.pallas_call(
        _sample_in_kernel,
        grid=(1,),
        in_specs=[
            pl.BlockSpec((n, D_MODEL), lambda i: (0, 0)),
            pl.BlockSpec((None, n, 2 * D_MODEL), lsel3),
            pl.BlockSpec((None, 1, D_MODEL), lsel3),
            pl.BlockSpec((None, D_MODEL, N_IN), lsel3),
            pl.BlockSpec((None, 1, HP), lsel3),
            pl.BlockSpec((None, NH, n), lsel3),
        ],
        out_specs=[pl.BlockSpec(s.shape, lambda i, nd=len(s.shape): (0,) * nd) for s in out_shapes],
        out_shape=out_shapes,
        compiler_params=pltpu.CompilerParams(
            dimension_semantics=("arbitrary",), vmem_limit_bytes=VMEM_LIMIT),
        name="sample_in_proj",
    )(xs, mods, wts["g1"], wts["w_in"], wts["gbias"], m0_t)


def _sample_state_kernel(c0_ref, qt_ref, kst_ref, wvt_ref, dec_ref, *rest, vb):
    cnew_ref, cq_ref = rest[-2:]
    q = qt_ref[0:HD, :]
    ks = kst_ref[0:HD, :]
    dec = dec_ref[...]
    for vi in range(vb):
        c_old = c0_ref[vi]
        cq_ref[vi:vi + 1, :] = jnp.sum(c_old * q, axis=0, keepdims=True)
        cnew_ref[vi] = dec * c_old + wvt_ref[vi:vi + 1, :] * ks


def _sample_state(c0_t, c_out_prev, qt, kst, wvt, dec, l, vb):
    n = c0_t.shape[-1]
    head_spec = pl.BlockSpec((None, HP, n), lambda h, i: (h, 0, 0))
    in_specs = [
        pl.BlockSpec((None, None, vb, HD, n), lambda h, i: (l, h, i, 0, 0)),
        head_spec, head_spec,
        pl.BlockSpec((None, vb, n), lambda h, i: (h, i, 0)),
        pl.BlockSpec((None, 1, n), lambda h, i: (h, 0, 0)),
    ]
    args = [c0_t, qt, kst, wvt, dec]
    aliases = {}
    if c_out_prev is not None:
        in_specs.append(pl.BlockSpec(memory_space=pl.ANY))
        args.append(c_out_prev)
        aliases = {len(args) - 1: 0}
    return pl.pallas_call(
        functools.partial(_sample_state_kernel, vb=vb),
        grid=(NH, HD // vb),
        in_specs=in_specs,
        out_specs=[
            pl.BlockSpec((None, None, vb, HD, n), lambda h, i: (l, h, i, 0, 0)),
            pl.BlockSpec((None, vb, n), lambda h, i: (h, i, 0)),
        ],
        out_shape=[
            jax.ShapeDtypeStruct(c0_t.shape, F32),
            jax.ShapeDtypeStruct((NH, HD, n), F32),
        ],
        input_output_aliases=aliases,
        compiler_params=pltpu.CompilerParams(
            dimension_semantics=("arbitrary", "arbitrary"), vmem_limit_bytes=VMEM_LIMIT),
        name="sample_matrix_memory",
    )(*args)


def _sample_mix_kernel(x_ref, gt_ref, proj_ref, qt_ref, kst_ref, vt_ref, cq_ref, gs_ref,
                       h0_ref, conv_ref, n0_ref, pool_ref,
                       convw_ref, convb_ref, wgate_ref, bgate_ref, lam_ref, ng_ref,
                       poolw_ref, pscale_ref, wout_ref,
                       xo_ref, ho_ref, convo_ref, no_ref, poolo_ref):
    proj = proj_ref[...]
    n_tok = proj.shape[0]

    xr = proj[:, C_XR:C_XR + LRU_W]
    gr = proj[:, C_GR:C_GR + LRU_W]
    cw = convw_ref[...]
    xc = (convb_ref[...] + cw[3:4] * xr + cw[2:3] * conv_ref[2]
          + cw[1:2] * conv_ref[1] + cw[0:1] * conv_ref[0])
    convo_ref[0] = conv_ref[1]
    convo_ref[1] = conv_ref[2]
    convo_ref[2] = xr
    gates = _dot(xc.astype(BF16), wgate_ref[...]) + bgate_ref[...]
    a, mult, gate_i = _lru_coeffs(gates, LRU_C * _log_sigmoid(lam_ref[...]))
    h_new = a * h0_ref[...] + mult * gate_i * xc
    ho_ref[...] = h_new
    y_a = h_new * _gelu_tanh(gr)

    normed = []
    for h in range(NH):
        q = qt_ref[h, 0:HD, :]
        ks = kst_ref[h, 0:HD, :]
        v = vt_ref[h, 0:HD, :]
        n0 = n0_ref[h]
        inter, p, floor = gs_ref[0, h:h + 1, :], gs_ref[1, h:h + 1, :], gs_ref[2, h:h + 1, :]
        s = jnp.sum(q * ks, axis=0, keepdims=True) * p
        num = s * v + inter * cq_ref[h]
        den = s + inter * jnp.sum(n0 * q, axis=0, keepdims=True)
        hout = num / jnp.maximum(jnp.abs(den), floor)
        no_ref[h] = inter * n0 + p * ks
        ms = jnp.sum(hout * hout, axis=0, keepdims=True) * (1.0 / HD)
        normed.append(hout * lax.rsqrt(ms + EPS))
        normed.append(jnp.zeros((HP - HD, n_tok), F32))
    hm = jnp.concatenate(normed, axis=0).T
    y_b = _sigmoid(proj[:, C_O:C_O + MWP]) * (hm * ng_ref[...])

    u = proj[:, C_U:C_U + PW]
    sums, acc, nxt = {}, u, 1
    for win in POOL_WINDOWS:
        while nxt < win:
            acc = acc + pool_ref[POOL_BUF - nxt]
            nxt += 1
        sums[win] = acc
    wsum = _pool_select(sums[2], sums[4], sums[8], sums[16])
    cnt = jnp.minimum(PAST_LEN + 1, _pool_window_lanes(u.shape)).astype(F32)
    diff = wsum / cnt - u
    y_c = _dot(diff.astype(BF16), poolw_ref[...]) * pscale_ref[...]
    for r in range(POOL_BUF - 1):
        poolo_ref[r] = pool_ref[r + 1]
    poolo_ref[POOL_BUF - 1] = u

    mix = jnp.concatenate([y_a, y_b, y_c], axis=1).astype(BF16)
    xo_ref[...] = x_ref[...] + gt_ref[...] * _dot(mix, wout_ref[...])


def _sample_mix(xs, mods, proj, qt, kst, vt, cq, gs, l, st, wts):
    n = xs.shape[0]
    zeros = lambda nd: (lambda i: (0,) * nd)
    full = lambda a: pl.BlockSpec(a.shape, zeros(a.ndim))
    lspec = lambda a: pl.BlockSpec((None,) + a.shape[1:], lambda i, nd=a.ndim: (l,) + (0,) * (nd - 1))
    snames = ("lru_h", "conv", "n", "pool")
    wnames = ("conv_w", "conv_b", "w_gate", "b_gate", "lam", "ng", "pool_w", "pool_scale", "w_out")
    out_shapes = [
        jax.ShapeDtypeStruct((n, D_MODEL), F32),
        jax.ShapeDtypeStruct((n, LRU_W), F32),
        jax.ShapeDtypeStruct((3, n, LRU_W), F32),
        jax.ShapeDtypeStruct((NH, HD, n), F32),
        jax.ShapeDtypeStruct((POOL_BUF, n, PW), F32),
    ]
    return pl.pallas_call(
        _sample_mix_kernel,
        grid=(1,),
        in_specs=[
            full(xs),
            pl.BlockSpec((None, n, D_MODEL), lambda i: (l, 0, 2)),
            full(proj), full(qt), full(kst), full(vt), full(cq), full(gs),
        ] + [lspec(st[k]) for k in snames] + [lspec(wts[k]) for k in wnames],
        out_specs=[pl.BlockSpec(s.shape, zeros(len(s.shape))) for s in out_shapes],
        out_shape=out_shapes,
        compiler_params=pltpu.CompilerParams(
            dimension_semantics=("arbitrary",), vmem_limit_bytes=VMEM_LIMIT),
        name="sample_mixer",
    )(xs, mods, proj, qt, kst, vt, cq, gs, *[st[k] for k in snames], *[wts[k] for k in wnames])


def _pad_heads(w):
    lead = w.shape[:-1]
    w = w.reshape(lead + (NH, HD))
    w = jnp.pad(w, [(0, 0)] * len(lead) + [(0, 0), (0, HP - HD)])
    return w.reshape(lead + (MWP,))


def _block_diag(w):
    depth, g, n, _ = w.shape
    eye = jnp.eye(g, dtype=w.dtype)
    return (w[:, :, :, None, :] * eye[None, :, None, :, None]).reshape(depth, g * n, g * n)


def _prepare_weights(norm1_g, norm2_g, w_in, conv_w, conv_b, lru_wa, lru_ba, lru_wx, lru_bx, lru_lam,
                     mlstm_bi, mlstm_bf, mlstm_norm_g, pool_w, pool_scale, w_out, w_ff1, w_ff2):
    depth = w_in.shape[0]
    o0 = 2 * LRU_W
    seg = lambda k: w_in[:, :, o0 + k * MW:o0 + (k + 1) * MW]
    g0 = o0 + 4 * MW
    gates = jnp.pad(w_in[:, :, g0:g0 + 2 * NH], ((0, 0), (0, 0), (0, HP - 2 * NH)))
    w_in_p = jnp.concatenate(
        [w_in[:, :, 0:o0]] + [_pad_heads(seg(k)) for k in range(4)]
        + [w_in[:, :, g0 + 2 * NH:], gates], axis=-1).astype(BF16)
    wo_b = w_out[:, LRU_W:LRU_W + MW].reshape(depth, NH, HD, D_MODEL)
    wo_b = jnp.pad(wo_b, ((0, 0), (0, 0), (0, HP - HD), (0, 0))).reshape(depth, MWP, D_MODEL)
    w_out_p = jnp.concatenate([w_out[:, 0:LRU_W], wo_b, w_out[:, LRU_W + MW:]], axis=1).astype(BF16)
    row = lambda v: v.reshape(depth, 1, -1)
    return {
        "g1": row(norm1_g), "g2": row(norm2_g),
        "w_in": w_in_p, "w_out": w_out_p,
        "conv_w": conv_w, "conv_b": row(conv_b),
        "w_gate": jnp.concatenate([_block_diag(lru_wa), _block_diag(lru_wx)], axis=-1).astype(BF16),
        "b_gate": jnp.concatenate([row(lru_ba), row(lru_bx)], axis=-1),
        "lam": row(lru_lam),
        "gbias": jnp.pad(jnp.concatenate([mlstm_bi, mlstm_bf], axis=-1), ((0, 0), (0, HP - 2 * NH)))[:, None, :],
        "ng": row(_pad_heads(mlstm_norm_g.reshape(depth, MW))),
        "pool_w": _block_diag(pool_w).astype(BF16),
        "pool_scale": row(pool_scale),
        "w_ff1": w_ff1.astype(BF16), "w_ff2": w_ff2.astype(BF16),
    }


def kernel(x_prompt, x_sample, state_lru_h, state_lru_conv, state_mlstm_C, state_mlstm_n, state_mlstm_m,
           state_pool, c_prompt, c_sample, norm1_g, norm2_g, w_ada, b_ada, w_in, conv_w, conv_b, lru_wa,
           lru_ba, lru_wx, lru_bx, lru_lam, mlstm_bi, mlstm_bf, mlstm_norm_g, pool_w, pool_scale, w_out,
           w_ff1, w_ff2, final_g):
    nb, seq, _ = x_prompt.shape
    ns = x_sample.shape[0]
    depth = w_in.shape[0]
    wts = _prepare_weights(norm1_g, norm2_g, w_in, conv_w, conv_b, lru_wa, lru_ba, lru_wx, lru_bx, lru_lam,
                           mlstm_bi, mlstm_bf, mlstm_norm_g, pool_w, pool_scale, w_out, w_ff1, w_ff2)
    final_row = final_g.reshape(1, D_MODEL)
    modp, mods = _modulation(jnp.concatenate([c_prompt, c_sample], axis=0), w_ada,
                             b_ada.reshape(depth, 1, -1), nb)

    sample_state = {
        "lru_h": state_lru_h,
        "conv": jnp.transpose(state_lru_conv, (0, 2, 1, 3)),
        "n": jnp.transpose(state_mlstm_n, (0, 2, 3, 1)),
        "pool": jnp.transpose(state_pool, (0, 2, 1, 3)),
    }
    m0_t = jnp.transpose(state_mlstm_m, (0, 2, 1))
    c0_t = jnp.transpose(state_mlstm_C, (0, 2, 3, 4, 1))
    c_new_t = None

    xp = x_prompt
    xs = x_sample.reshape(ns, D_MODEL)
    p_out = [[] for _ in range(6)]
    s_out = [[] for _ in range(5)]
    for l in range(depth):
        last = l == depth - 1
        xp, h_l, conv_l, c_l, n_l, m_l, pool_l = _prompt_mixer(xp, modp, l, wts, TM_MIX)
        mod_spec = pl.BlockSpec((None, None, 1, 3 * D_MODEL),
                                lambda i, l=l: (l, i // (seq // TM_FFN), 0, 1))
        xp = _ffn(xp.reshape(nb * seq, D_MODEL), modp, mod_spec, l, wts, final_row, TM_FFN, last)
        xp = xp.reshape(nb, seq, D_MODEL)
        for lst, val in zip(p_out, (h_l.reshape(nb, LRU_W), conv_l[:, 8 - 3:], c_l, n_l[:, :NH, :HD],
                                    m_l[:, :NH, 0], pool_l[:, 16 - POOL_BUF:])):
            lst.append(val)
        proj, qt, kst, vt, wvt, gs, dec, m_s = _sample_in(xs, mods, l, wts, m0_t)
        c_new_t, cq = _sample_state(c0_t, c_new_t, qt, kst, wvt, dec, l, VB_STATE)
        xs, h_s, conv_s, n_s, pool_s = _sample_mix(xs, mods, proj, qt, kst, vt, cq, gs, l, sample_state, wts)
        mod_spec = pl.BlockSpec((None, ns, 3 * D_MODEL), lambda i, l=l: (l, 0, 1))
        xs = _ffn(xs, mods, mod_spec, l, wts, final_row, ns, last)
        for lst, val in zip(s_out, (h_s, conv_s, n_s, m_s, pool_s)):
            lst.append(val)

    stack = lambda lst: jnp.stack(lst, axis=0)
    h_s, conv_s, n_s, m_s, pool_s = [stack(v) for v in s_out]
    return (xp, xs.reshape(ns, 1, D_MODEL),
            *[stack(v) for v in p_out],
            h_s,
            jnp.transpose(conv_s, (0, 2, 1, 3)),
            jnp.transpose(c_new_t, (0, 4, 1, 2, 3)),
            jnp.transpose(n_s, (0, 3, 1, 2)),
            jnp.transpose(m_s, (0, 2, 1)),
            jnp.transpose(pool_s, (0, 2, 1, 3)))
```

```python
import functools
import math

import jax
import jax.numpy as jnp
from jax import lax
from jax.experimental import pallas as pl
from jax.experimental.pallas import tpu as pltpu

F32 = jnp.float32
BF16 = jnp.bfloat16

D_MODEL = 1024
LRU_W = 384
LRU_HEADS = 6
LRU_C = 8.0
NH = 4
HD = 96
HP = 128
MW = NH * HD
MWP = NH * HP
CHUNK = 128
PW = 256
POOL_WINDOWS = (2, 4, 8, 16)
POOL_GROUP = 64
POOL_BUF = 15
D_FF = 4096
EPS = 1e-6
PAST_LEN = 16384

C_XR, C_GR, C_Q, C_K, C_V, C_O, C_U = 0, 384, 768, 1280, 1792, 2304, 2816
N_IN = 3072
G_OFF = HD
K_OUT = LRU_W + MWP + PW

VMEM_LIMIT = 56 * 1024 * 1024
TM_MIX = 512
PROJ_PIECE = 256
TM_FFN = 1024
VB_STATE = 32


def _log_sigmoid(x):
    return jnp.minimum(x, 0.0) - jnp.log1p(jnp.exp(-jnp.abs(x)))


def _sigmoid(x):
    return 1.0 / (1.0 + jnp.exp(-x))


def _gelu_tanh(x):
    c = math.sqrt(2.0 / math.pi)
    return 0.5 * x * (1.0 + jnp.tanh(c * (x + 0.044715 * (x * x * x))))


def _rms(x, g):
    ms = jnp.mean(x * x, axis=-1, keepdims=True)
    return x * lax.rsqrt(ms + EPS) * g


def _ada_norm(x, g, mod):
    ms = jnp.mean(x * x, axis=-1, keepdims=True)
    gain = g * (1.0 + mod[:, D_MODEL:2 * D_MODEL])
    return x * lax.rsqrt(ms + EPS) * gain + mod[:, 0:D_MODEL]


def _dot(a, b):
    return jnp.dot(a, b, preferred_element_type=F32)


def _dot_nt(a, b):
    return lax.dot_general(a, b, (((1,), (1,)), ((), ())), preferred_element_type=F32)


def _lru_coeffs(gates, log_lam):
    r = _sigmoid(gates[:, :LRU_W])
    i = _sigmoid(gates[:, LRU_W:])
    log_a = r * log_lam
    a = jnp.exp(log_a)
    th = jnp.tanh(log_a)
    mult = jnp.sqrt(-2.0 * th / (1.0 - th))
    return a, mult, i


def _scan_rows(a, b, h_prev):
    n, w = a.shape
    groups = n // 8
    a3 = a.reshape(groups, 8, w)
    b3 = b.reshape(groups, 8, w)
    row = lax.broadcasted_iota(jnp.int32, a3.shape, 1)
    for s in (1, 2, 4):
        keep = row >= s
        b3 = a3 * jnp.where(keep, pltpu.roll(b3, s, 1), 0.0) + b3
        a3 = a3 * jnp.where(keep, pltpu.roll(a3, s, 1), 1.0)
    carry = h_prev
    outs = []
    for g in range(groups):
        outs.append(a3[g] * carry + b3[g])
        carry = a3[g, 7:8] * carry + b3[g, 7:8]
    return jnp.concatenate(outs, axis=0), carry


def _head_norm_gate(hout, o, ng):
    ms = jnp.sum(hout * hout, axis=-1, keepdims=True) * (1.0 / HD)
    return _sigmoid(o) * (hout * lax.rsqrt(ms + EPS) * ng)


def _pool_select(s2, s4, s8, s16):
    lane = lax.broadcasted_iota(jnp.int32, s2.shape, 1)
    return jnp.where(lane < 64, s2, jnp.where(lane < 128, s4, jnp.where(lane < 192, s8, s16)))


def _pool_window_lanes(shape):
    lane = lax.broadcasted_iota(jnp.int32, shape, 1)
    return jnp.where(lane < 64, 2, jnp.where(lane < 128, 4, jnp.where(lane < 192, 8, 16)))


def _mod_kernel(c_ref, w_ref, b_ref, modp_ref, mods_ref, *, n_prompt):
    c = c_ref[...]
    act = (c * _sigmoid(c)).astype(BF16)
    out = _dot(act, w_ref[...].astype(BF16)) + b_ref[...]
    for r in range(n_prompt):
        modp_ref[r] = out[r:r + 1]
    mods_ref[...] = out[n_prompt:]


def _modulation(c_all, w_ada, b_ada, n_prompt):
    depth = w_ada.shape[0]
    n_all = c_all.shape[0]
    n_mod = w_ada.shape[2]
    tn = 1536
    return pl.pallas_call(
        functools.partial(_mod_kernel, n_prompt=n_prompt),
        grid=(depth, n_mod // tn),
        in_specs=[
            pl.BlockSpec((n_all, D_MODEL), lambda l, j: (0, 0)),
            pl.BlockSpec((None, D_MODEL, tn), lambda l, j: (l, 0, j)),
            pl.BlockSpec((None, 1, tn), lambda l, j: (l, 0, j)),
        ],
        out_specs=[
            pl.BlockSpec((None, n_prompt, 1, tn), lambda l, j: (l, 0, 0, j)),
            pl.BlockSpec((None, n_all - n_prompt, tn), lambda l, j: (l, 0, j)),
        ],
        out_shape=[
            jax.ShapeDtypeStruct((depth, n_prompt, 1, n_mod), F32),
            jax.ShapeDtypeStruct((depth, n_all - n_prompt, n_mod), F32),
        ],
        compiler_params=pltpu.CompilerParams(
            dimension_semantics=("arbitrary", "arbitrary"), vmem_limit_bytes=VMEM_LIMIT),
        name="adaln_modulation",
    )(c_all, w_ada, b_ada)


def _mlstm_tile(proj, gl, gl_t, state, n_chunks, pump):
    pairs = [(c, h) for c in range(n_chunks) for h in range(NH)]
    t_idx = lax.broadcasted_iota(jnp.int32, (CHUNK, CHUNK), 0)
    s_idx = lax.broadcasted_iota(jnp.int32, (CHUNK, CHUNK), 1)
    causal = s_idx <= t_idx
    upper = t_idx <= s_idx
    rows = lambda c: slice(c * CHUNK, (c + 1) * CHUNK)
    head = lambda base, h: slice(base + h * HP, base + (h + 1) * HP)

    bcum_col, dmat, rowmax, ib_col = {}, {}, {}, {}
    for c, h in pairs:
        lf_row = gl_t[NH + h:NH + h + 1, rows(c)]
        lf_col = gl[rows(c), NH + h:NH + h + 1]
        ib_col[c, h] = gl[rows(c), h:h + 1]
        bcum_col[c, h] = jnp.sum(jnp.where(causal, lf_row, 0.0), axis=1, keepdims=True)
        bcum_row = jnp.sum(jnp.where(upper, lf_col, 0.0), axis=0, keepdims=True)
        dmat[c, h] = jnp.where(causal, bcum_col[c, h] - bcum_row + gl_t[h:h + 1, rows(c)], -jnp.inf)
        rowmax[c, h] = jnp.max(dmat[c, h], axis=1, keepdims=True)
    pump(1)

    m_start, m_t = {}, {}
    m_run = [state[h][2] for h in range(NH)]
    for c, h in pairs:
        m_start[c, h] = m_run[h]
        m_t[c, h] = jnp.maximum(bcum_col[c, h] + m_run[h], rowmax[c, h])
        m_run[h] = m_t[c, h][CHUNK - 1:CHUNK, :]
    inter, pmat, floor, decay, w_col = {}, {}, {}, {}, {}
    for c, h in pairs:
        b_last = bcum_col[c, h][CHUNK - 1:CHUNK, :]
        inter[c, h] = jnp.exp(bcum_col[c, h] + m_start[c, h] - m_t[c, h])
        pmat[c, h] = jnp.exp(dmat[c, h] - m_t[c, h])
        floor[c, h] = jnp.exp(-m_t[c, h])
        m_new = m_t[c, h][CHUNK - 1:CHUNK, :]
        decay[c, h] = jnp.exp(b_last + m_start[c, h] - m_new)
        w_col[c, h] = jnp.exp(b_last - bcum_col[c, h] + ib_col[c, h] - m_new)
    pump(1)

    qb, kb, s_mat, sv, wvk, wk = {}, {}, {}, {}, {}, {}
    for c, h in pairs:
        qb[c, h] = proj[rows(c), head(C_Q, h)].astype(BF16)
        ks = proj[rows(c), head(C_K, h)] * (1.0 / math.sqrt(HD))
        kb[c, h] = ks.astype(BF16)
        wk[c, h] = jnp.sum(w_col[c, h] * ks, axis=0, keepdims=True)
        s_mat[c, h] = _dot_nt(qb[c, h], kb[c, h]) * pmat[c, h]
    pump(1)
    for c, h in pairs:
        v = proj[rows(c), head(C_V, h)]
        sv[c, h] = _dot(s_mat[c, h].astype(BF16), v.astype(BF16))
        wvk[c, h] = _dot((w_col[c, h] * v).T.astype(BF16), kb[c, h])
    pump(1)

    outs = [[] for _ in range(NH)]
    c_run = [state[h][0] for h in range(NH)]
    n_run = [state[h][1] for h in range(NH)]
    for c, h in pairs:
        q = proj[rows(c), head(C_Q, h)]
        qc = _dot_nt(qb[c, h], c_run[h].astype(BF16))
        qn = jnp.sum(q * n_run[h], axis=1, keepdims=True)
        num = sv[c, h] + inter[c, h] * qc
        den = jnp.sum(s_mat[c, h], axis=1, keepdims=True) + inter[c, h] * qn
        outs[h].append(num / jnp.maximum(jnp.abs(den), floor[c, h]))
        c_run[h] = decay[c, h] * c_run[h] + wvk[c, h]
        n_run[h] = decay[c, h] * n_run[h] + wk[c, h]
    return outs, [(c_run[h], n_run[h], m_run[h]) for h in range(NH)]


def _project(x, mod, g1, w_in):
    return _dot(_ada_norm(x, g1, mod).astype(BF16), w_in)


def _mix_tile(proj, x, gt1, j, tm, convw_ref, convb_ref, wgate_ref, bgate_ref,
              lam_ref, gbias_ref, ng_ref, poolw_ref, pscale_ref, wout_ref,
              xo_ref, hlast_ref, convo_ref, co_ref, no_ref, mo_ref, poolo_ref,
              h_sc, xbuf, ubuf, c_sc, n_sc, m_sc, pump=lambda n: None):
    row = lax.broadcasted_iota(jnp.int32, (tm, 1), 0)

    xr = proj[:, C_XR:C_XR + LRU_W]
    gr = proj[:, C_GR:C_GR + LRU_W]
    xbuf[8:8 + tm, :] = xr
    cw = convw_ref[...]
    xc = (convb_ref[...] + cw[3:4] * xr + cw[2:3] * xbuf[7:7 + tm, :]
          + cw[1:2] * xbuf[6:6 + tm, :] + cw[0:1] * xbuf[5:5 + tm, :])
    tail = xbuf[tm:tm + 8, :]
    xbuf[0:8, :] = tail
    convo_ref[0] = tail
    gates = _dot(xc.astype(BF16), wgate_ref[...]) + bgate_ref[...]
    pump(0)
    log_lam = LRU_C * _log_sigmoid(lam_ref[...])
    h_run = h_sc[...]
    y_a_parts = []
    n_split = 2
    for r in range(n_split):
        rs = slice(r * tm // n_split, (r + 1) * tm // n_split)
        a, mult, gate_i = _lru_coeffs(gates[rs], log_lam)
        if r == 0:
            mult = jnp.where(jnp.logical_and(row[rs] == 0, j == 0), 1.0, mult)
        pump(2)
        hseq, h_run = _scan_rows(a, mult * gate_i * xc[rs], h_run)
        pump(1)
        y_a_parts.append(hseq * _gelu_tanh(gr[rs]))
    h_sc[...] = h_run
    hlast_ref[0] = h_run
    y_a = jnp.concatenate(y_a_parts, axis=0)
    out_acc = _dot(y_a.astype(BF16), wout_ref[0:LRU_W, :])

    g_rows = (proj[:, C_Q:C_Q + HP] + gbias_ref[...]).T[G_OFF:G_OFF + 2 * NH]
    row8 = lax.broadcasted_iota(jnp.int32, g_rows.shape, 0)
    gl_t = jnp.where(row8 >= NH, _log_sigmoid(g_rows), g_rows)
    gl = jnp.concatenate([gl_t, jnp.zeros((HP - 2 * NH, tm), F32)], axis=0).T
    ng = ng_ref[...]
    state = [(c_sc[h], n_sc[h:h + 1, :], m_sc[h:h + 1, 0:1]) for h in range(NH)]
    outs, state = _mlstm_tile(proj, gl, gl_t, state, tm // CHUNK, pump)
    y_b_heads = []
    for h in range(NH):
        c_h, n_h, m_h = state[h]
        c_sc[h] = c_h
        n_sc[h:h + 1, :] = n_h
        m_sc[h:h + 1, :] = jnp.broadcast_to(m_h, (1, HP))
        co_ref[0, h] = c_h[0:HD, 0:HD]
        hout_all = jnp.concatenate(outs[h], axis=0) if len(outs[h]) > 1 else outs[h][0]
        o = proj[:, C_O + h * HP:C_O + (h + 1) * HP]
        y_b_heads.append(_head_norm_gate(hout_all, o, ng[:, h * HP:(h + 1) * HP]))
        pump(h % 2)
    no_ref[0] = n_sc[...]
    mo_ref[0] = m_sc[...]
    y_b = jnp.concatenate(y_b_heads, axis=1).astype(BF16)
    out_acc = out_acc + _dot(y_b, wout_ref[LRU_W:LRU_W + MWP, :])

    u = proj[:, C_U:C_U + PW]
    ubuf[16:16 + tm, :] = u
    ext = ubuf[...]
    s2 = ext + pltpu.roll(ext, 1, 0)
    s4 = s2 + pltpu.roll(s2, 2, 0)
    s8 = s4 + pltpu.roll(s4, 4, 0)
    s16 = s8 + pltpu.roll(s8, 8, 0)
    wsum = _pool_select(s2[16:], s4[16:], s8[16:], s16[16:])
    pos = row + j * tm
    cnt = jnp.minimum(pos + 1, _pool_window_lanes((tm, PW))).astype(F32)
    diff = wsum / cnt - u
    y_c = _dot(diff.astype(BF16), poolw_ref[...]) * pscale_ref[...]
    ptail = ubuf[tm:tm + 16, :]
    ubuf[0:16, :] = ptail
    poolo_ref[0] = ptail
    pump(N_IN)

    out_acc = out_acc + _dot(y_c.astype(BF16), wout_ref[LRU_W + MWP:K_OUT, :])
    xo_ref[0] = x + gt1 * out_acc


def _prompt_mixer_kernel(x_ref, xn_ref, mod_ref, modn_ref, g1_ref, win_ref, *rest, tm):
    weights_outs = rest[:17]
    h_sc, xbuf, ubuf, c_sc, n_sc, m_sc, proj_a, proj_b = rest[17:]
    j = pl.program_id(1)
    t = pl.program_id(0) * pl.num_programs(1) + j

    @pl.when(j == 0)
    def _():
        h_sc[...] = jnp.zeros_like(h_sc)
        xbuf[0:8, :] = jnp.zeros((8, LRU_W), F32)
        ubuf[0:16, :] = jnp.zeros((16, PW), F32)
        c_sc[...] = jnp.zeros_like(c_sc)
        n_sc[...] = jnp.zeros_like(n_sc)
        m_sc[...] = jnp.zeros_like(m_sc)

    @pl.when(t == 0)
    def _():
        proj_a[...] = _project(x_ref[0], mod_ref[...], g1_ref[...], win_ref[...])

    def step(proj_cur, proj_nxt):
        done = [0]
        lhs = []

        def pump(n):
            if not lhs:
                lhs.append(_ada_norm(xn_ref[0], g1_ref[...], modn_ref[...]).astype(BF16))
            for _ in range(n):
                c0 = done[0]
                if c0 >= N_IN:
                    return
                c1 = min(c0 + PROJ_PIECE, N_IN)
                proj_nxt[:, c0:c1] = _dot(lhs[0], win_ref[:, c0:c1])
                done[0] = c1

        gt1 = mod_ref[:, 2 * D_MODEL:3 * D_MODEL]
        _mix_tile(proj_cur, x_ref[0], gt1, j, tm, *weights_outs, h_sc, xbuf, ubuf, c_sc, n_sc, m_sc,
                  pump=pump)

    @pl.when(t % 2 == 0)
    def _():
        step(proj_a, proj_b)

    @pl.when(t % 2 == 1)
    def _():
        step(proj_b, proj_a)


def _prompt_mixer(x, modp, l, wts, tm):
    nb, seq, _ = x.shape
    n_t = seq // tm

    def next_tile(b, j):
        t1 = jnp.minimum(b * n_t + j + 1, nb * n_t - 1)
        return t1 // n_t, t1 % n_t

    lsel3 = lambda b, j: (l, 0, 0)
    wspec = lambda a: pl.BlockSpec((None,) + a.shape[1:], lsel3)
    names = ("g1", "w_in", "conv_w", "conv_b", "w_gate", "b_gate", "lam", "gbias", "ng", "pool_w",
             "pool_scale", "w_out")
    out_shapes = [
        jax.ShapeDtypeStruct((nb, seq, D_MODEL), F32),
        jax.ShapeDtypeStruct((nb, 1, LRU_W), F32),
        jax.ShapeDtypeStruct((nb, 8, LRU_W), F32),
        jax.ShapeDtypeStruct((nb, NH, HD, HD), F32),
        jax.ShapeDtypeStruct((nb, 8, HP), F32),
        jax.ShapeDtypeStruct((nb, 8, HP), F32),
        jax.ShapeDtypeStruct((nb, 16, PW), F32),
    ]
    out_specs = [
        pl.BlockSpec((1, tm, D_MODEL), lambda b, j: (b, j, 0)),
        pl.BlockSpec((1, 1, LRU_W), lambda b, j: (b, 0, 0)),
        pl.BlockSpec((1, 8, LRU_W), lambda b, j: (b, 0, 0)),
        pl.BlockSpec((1, NH, HD, HD), lambda b, j: (b, 0, 0, 0)),
        pl.BlockSpec((1, 8, HP), lambda b, j: (b, 0, 0)),
        pl.BlockSpec((1, 8, HP), lambda b, j: (b, 0, 0)),
        pl.BlockSpec((1, 16, PW), lambda b, j: (b, 0, 0)),
    ]
    return pl.pallas_call(
        functools.partial(_prompt_mixer_kernel, tm=tm),
        grid=(nb, seq // tm),
        in_specs=[
            pl.BlockSpec((1, tm, D_MODEL), lambda b, j: (b, j, 0)),
            pl.BlockSpec((1, tm, D_MODEL), lambda b, j: next_tile(b, j) + (0,)),
            pl.BlockSpec((None, None, 1, 3 * D_MODEL), lambda b, j: (l, b, 0, 0)),
            pl.BlockSpec((None, None, 1, 3 * D_MODEL), lambda b, j: (l, next_tile(b, j)[0], 0, 0)),
        ] + [wspec(wts[n]) for n in names],
        out_specs=out_specs,
        out_shape=out_shapes,
        scratch_shapes=[
            pltpu.VMEM((1, LRU_W), F32),
            pltpu.VMEM((tm + 8, LRU_W), F32),
            pltpu.VMEM((tm + 16, PW), F32),
            pltpu.VMEM((NH, HP, HP), F32),
            pltpu.VMEM((8, HP), F32),
            pltpu.VMEM((8, HP), F32),
            pltpu.VMEM((tm, N_IN), F32),
            pltpu.VMEM((tm, N_IN), F32),
        ],
        compiler_params=pltpu.CompilerParams(
            dimension_semantics=("arbitrary", "arbitrary"), vmem_limit_bytes=VMEM_LIMIT),
        name="prompt_mixer",
    )(x, x, modp, modp, *[wts[n] for n in names])


def _ffn_kernel(xp_ref, xs_ref, modp_ref, mods_ref, g2_ref, w1_ref, w2_ref, gf_ref, op_ref, os_ref,
                *, n_prompt_steps, final):
    def ffn(x_ref, mod_ref, o_ref):
        x = x_ref[...]
        mod = mod_ref[...]
        gt2 = mod[:, 2 * D_MODEL:3 * D_MODEL]
        h2 = _ada_norm(x, g2_ref[...], mod).astype(BF16)
        acc = jnp.zeros(x.shape, F32)
        step = 1024
        for c in range(D_FF // step):
            hid = _dot(h2, w1_ref[:, c * step:(c + 1) * step])
            hid = jnp.square(jnp.maximum(hid, 0.0)).astype(BF16)
            acc = acc + _dot(hid, w2_ref[c * step:(c + 1) * step, :])
        y = x + gt2 * acc
        if final:
            y = _rms(y, gf_ref[...])
        o_ref[...] = y

    i = pl.program_id(0)

    @pl.when(i < n_prompt_steps)
    def _():
        ffn(xp_ref, modp_ref, op_ref)

    @pl.when(i == n_prompt_steps)
    def _():
        ffn(xs_ref, mods_ref, os_ref)


def _ffn(xp2d, xs, modp, mods, l, wts, final_g, tm, seq, final):
    n_tok = xp2d.shape[0]
    ns = xs.shape[0]
    n_steps = n_tok // tm
    lsel3 = lambda i: (l, 0, 0)
    wspec = lambda a: pl.BlockSpec((None,) + a.shape[1:], lsel3, pipeline_mode=pl.Buffered(1))
    tile = lambda i: jnp.minimum(i, n_steps - 1)
    return pl.pallas_call(
        functools.partial(_ffn_kernel, n_prompt_steps=n_steps, final=final),
        grid=(n_steps + 1,),
        in_specs=[
            pl.BlockSpec((tm, D_MODEL), lambda i: (tile(i), 0)),
            pl.BlockSpec((ns, D_MODEL), lambda i: (0, 0)),
            pl.BlockSpec((None, None, 1, 3 * D_MODEL), lambda i: (l, tile(i) // (seq // tm), 0, 1)),
            pl.BlockSpec((None, ns, 3 * D_MODEL), lambda i: (l, 0, 1)),
            pl.BlockSpec((None, 1, D_MODEL), lsel3),
            wspec(wts["w_ff1"]),
            wspec(wts["w_ff2"]),
            pl.BlockSpec((1, D_MODEL), lambda i: (0, 0)),
        ],
        out_specs=[
            pl.BlockSpec((tm, D_MODEL), lambda i: (tile(i), 0)),
            pl.BlockSpec((ns, D_MODEL), lambda i: (0, 0)),
        ],
        out_shape=[
            jax.ShapeDtypeStruct((n_tok, D_MODEL), F32),
            jax.ShapeDtypeStruct((ns, D_MODEL), F32),
        ],
        compiler_params=pltpu.CompilerParams(
            dimension_semantics=("arbitrary",), vmem_limit_bytes=VMEM_LIMIT),
        name="ffn",
    )(xp2d, xs, modp, mods, wts["g2"], wts["w_ff1"], wts["w_ff2"], final_g)


def _sample_in_kernel(x_ref, mod_ref, g1_ref, win_ref, gbias_ref, m0_ref,
                      proj_ref, qt_ref, kst_ref, vt_ref, wvt_ref, gs_ref, dec_ref, mo_ref):
    proj = _project(x_ref[...], mod_ref[...], g1_ref[...], win_ref[...])
    proj_ref[...] = proj
    g_t = (proj[:, C_Q:C_Q + HP] + gbias_ref[...]).T
    ib = g_t[G_OFF:G_OFF + NH]
    g = _log_sigmoid(g_t[G_OFF + NH:G_OFF + 2 * NH]) + m0_ref[...]
    m_t = jnp.maximum(g, ib)
    inter = jnp.exp(g - m_t)
    p = jnp.exp(ib - m_t)
    mo_ref[...] = m_t
    gs_ref[0] = inter
    gs_ref[1] = p
    gs_ref[2] = jnp.exp(-m_t)
    q_t = proj[:, C_Q:C_Q + MWP].T
    ks_t = (proj[:, C_K:C_K + MWP] * (1.0 / math.sqrt(HD))).T
    v_t = proj[:, C_V:C_V + MWP].T
    for h in range(NH):
        hs = slice(h * HP, (h + 1) * HP)
        qt_ref[h] = q_t[hs]
        kst_ref[h] = ks_t[hs]
        vt_ref[h] = v_t[hs]
        wvt_ref[h] = p[h:h + 1] * v_t[hs]
        dec_ref[h] = inter[h:h + 1]


def _sample_in(xs, mods, l, wts, m0_t):
    n = xs.shape[0]
    lsel3 = lambda i: (l, 0, 0)
    head_t = jax.ShapeDtypeStruct((NH, HP, n), F32)
    out_shapes = [
        jax.ShapeDtypeStruct((n, N_IN), F32),
        head_t, head_t, head_t, head_t,
        jax.ShapeDtypeStruct((3, NH, n), F32),
        jax.ShapeDtypeStruct((NH, 1, n), F32),
        jax.ShapeDtypeStruct((NH, n), F32),
    ]
    return pl.pallas_call(
        _sample_in_kernel,
        grid=(1,),
        in_specs=[
            pl.BlockSpec((n, D_MODEL), lambda i: (0, 0)),
            pl.BlockSpec((None, n, 2 * D_MODEL), lsel3),
            pl.BlockSpec((None, 1, D_MODEL), lsel3),
            pl.BlockSpec((None, D_MODEL, N_IN), lsel3),
            pl.BlockSpec((None, 1, HP), lsel3),
            pl.BlockSpec((None, NH, n), lsel3),
        ],
        out_specs=[pl.BlockSpec(s.shape, lambda i, nd=len(s.shape): (0,) * nd) for s in out_shapes],
        out_shape=out_shapes,
        compiler_params=pltpu.CompilerParams(
            dimension_semantics=("arbitrary",), vmem_limit_bytes=VMEM_LIMIT),
        name="sample_in_proj",
    )(xs, mods, wts["g1"], wts["w_in"], wts["gbias"], m0_t)


def _sample_state_kernel(c0_ref, qt_ref, kst_ref, wvt_ref, dec_ref, *rest, vb):
    cnew_ref, cq_ref = rest[-2:]
    q = qt_ref[0:HD, :]
    ks = kst_ref[0:HD, :]
    dec = dec_ref[...]
    for vi in range(vb):
        c_old = c0_ref[vi]
        cq_ref[vi:vi + 1, :] = jnp.sum(c_old * q, axis=0, keepdims=True)
        cnew_ref[vi] = dec * c_old + wvt_ref[vi:vi + 1, :] * ks


def _sample_state(c0_t, c_out_prev, qt, kst, wvt, dec, l, vb):
    n = c0_t.shape[-1]
    head_spec = pl.BlockSpec((None, HP, n), lambda h, i: (h, 0, 0))
    in_specs = [
        pl.BlockSpec((None, None, vb, HD, n), lambda h, i: (l, h, i, 0, 0)),
        head_spec, head_spec,
        pl.BlockSpec((None, vb, n), lambda h, i: (h, i, 0)),
        pl.BlockSpec((None, 1, n), lambda h, i: (h, 0, 0)),
    ]
    args = [c0_t, qt, kst, wvt, dec]
    aliases = {}
    if c_out_prev is not None:
        in_specs.append(pl.BlockSpec(memory_space=pl.ANY))
        args.append(c_out_prev)
        aliases = {len(args) - 1: 0}
    return pl.pallas_call(
        functools.partial(_sample_state_kernel, vb=vb),
        grid=(NH, HD // vb),
        in_specs=in_specs,
        out_specs=[
            pl.BlockSpec((None, None, vb, HD, n), lambda h, i: (l, h, i, 0, 0)),
            pl.BlockSpec((None, vb, n), lambda h, i: (h, i, 0)),
        ],
        out_shape=[
            jax.ShapeDtypeStruct(c0_t.shape, F32),
            jax.ShapeDtypeStruct((NH, HD, n), F32),
        ],
        input_output_aliases=aliases,
        compiler_params=pltpu.CompilerParams(
            dimension_semantics=("arbitrary", "arbitrary"), vmem_limit_bytes=VMEM_LIMIT),
        name="sample_matrix_memory",
    )(*args)


def _sample_mix_kernel(x_ref, gt_ref, proj_ref, qt_ref, kst_ref, vt_ref, cq_ref, gs_ref,
                       h0_ref, conv_ref, n0_ref, pool_ref,
                       convw_ref, convb_ref, wgate_ref, bgate_ref, lam_ref, ng_ref,
                       poolw_ref, pscale_ref, wout_ref,
                       xo_ref, ho_ref, convo_ref, no_ref, poolo_ref):
    proj = proj_ref[...]
    n_tok = proj.shape[0]

    xr = proj[:, C_XR:C_XR + LRU_W]
    gr = proj[:, C_GR:C_GR + LRU_W]
    cw = convw_ref[...]
    xc = (convb_ref[...] + cw[3:4] * xr + cw[2:3] * conv_ref[2]
          + cw[1:2] * conv_ref[1] + cw[0:1] * conv_ref[0])
    convo_ref[0] = conv_ref[1]
    convo_ref[1] = conv_ref[2]
    convo_ref[2] = xr
    gates = _dot(xc.astype(BF16), wgate_ref[...]) + bgate_ref[...]
    a, mult, gate_i = _lru_coeffs(gates, LRU_C * _log_sigmoid(lam_ref[...]))
    h_new = a * h0_ref[...] + mult * gate_i * xc
    ho_ref[...] = h_new
    y_a = h_new * _gelu_tanh(gr)

    normed = []
    for h in range(NH):
        q = qt_ref[h, 0:HD, :]
        ks = kst_ref[h, 0:HD, :]
        v = vt_ref[h, 0:HD, :]
        n0 = n0_ref[h]
        inter, p, floor = gs_ref[0, h:h + 1, :], gs_ref[1, h:h + 1, :], gs_ref[2, h:h + 1, :]
        s = jnp.sum(q * ks, axis=0, keepdims=True) * p
        num = s * v + inter * cq_ref[h]
        den = s + inter * jnp.sum(n0 * q, axis=0, keepdims=True)
        hout = num / jnp.maximum(jnp.abs(den), floor)
        no_ref[h] = inter * n0 + p * ks
        ms = jnp.sum(hout * hout, axis=0, keepdims=True) * (1.0 / HD)
        normed.append(hout * lax.rsqrt(ms + EPS))
        normed.append(jnp.zeros((HP - HD, n_tok), F32))
    hm = jnp.concatenate(normed, axis=0).T
    y_b = _sigmoid(proj[:, C_O:C_O + MWP]) * (hm * ng_ref[...])

    u = proj[:, C_U:C_U + PW]
    sums, acc, nxt = {}, u, 1
    for win in POOL_WINDOWS:
        while nxt < win:
            acc = acc + pool_ref[POOL_BUF - nxt]
            nxt += 1
        sums[win] = acc
    wsum = _pool_select(sums[2], sums[4], sums[8], sums[16])
    cnt = jnp.minimum(PAST_LEN + 1, _pool_window_lanes(u.shape)).astype(F32)
    diff = wsum / cnt - u
    y_c = _dot(diff.astype(BF16), poolw_ref[...]) * pscale_ref[...]
    for r in range(POOL_BUF - 1):
        poolo_ref[r] = pool_ref[r + 1]
    poolo_ref[POOL_BUF - 1] = u

    mix = jnp.concatenate([y_a, y_b, y_c], axis=1).astype(BF16)
    xo_ref[...] = x_ref[...] + gt_ref[...] * _dot(mix, wout_ref[...])


def _sample_mix(xs, mods, proj, qt, kst, vt, cq, gs, l, st, wts):
    n = xs.shape[0]
    zeros = lambda nd: (lambda i: (0,) * nd)
    full = lambda a: pl.BlockSpec(a.shape, zeros(a.ndim))
    lspec = lambda a: pl.BlockSpec((None,) + a.shape[1:], lambda i, nd=a.ndim: (l,) + (0,) * (nd - 1))
    snames = ("lru_h", "conv", "n", "pool")
    wnames = ("conv_w", "conv_b", "w_gate", "b_gate", "lam", "ng", "pool_w", "pool_scale", "w_out")
    out_shapes = [
        jax.ShapeDtypeStruct((n, D_MODEL), F32),
        jax.ShapeDtypeStruct((n, LRU_W), F32),
        jax.ShapeDtypeStruct((3, n, LRU_W), F32),
        jax.ShapeDtypeStruct((NH, HD, n), F32),
        jax.ShapeDtypeStruct((POOL_BUF, n, PW), F32),
    ]
    return pl.pallas_call(
        _sample_mix_kernel,
        grid=(1,),
        in_specs=[
            full(xs),
            pl.BlockSpec((None, n, D_MODEL), lambda i: (l, 0, 2)),
            full(proj), full(qt), full(kst), full(vt), full(cq), full(gs),
        ] + [lspec(st[k]) for k in snames] + [lspec(wts[k]) for k in wnames],
        out_specs=[pl.BlockSpec(s.shape, zeros(len(s.shape))) for s in out_shapes],
        out_shape=out_shapes,
        compiler_params=pltpu.CompilerParams(
            dimension_semantics=("arbitrary",), vmem_limit_bytes=VMEM_LIMIT),
        name="sample_mixer",
    )(xs, mods, proj, qt, kst, vt, cq, gs, *[st[k] for k in snames], *[wts[k] for k in wnames])


def _pad_heads(w):
    lead = w.shape[:-1]
    w = w.reshape(lead + (NH, HD))
    w = jnp.pad(w, [(0, 0)] * len(lead) + [(0, 0), (0, HP - HD)])
    return w.reshape(lead + (MWP,))


def _block_diag(w):
    depth, g, n, _ = w.shape
    eye = jnp.eye(g, dtype=w.dtype)
    return (w[:, :, :, None, :] * eye[None, :, None, :, None]).reshape(depth, g * n, g * n)


def _prepare_weights(norm1_g, norm2_g, w_in, conv_w, conv_b, lru_wa, lru_ba, lru_wx, lru_bx, lru_lam,
                     mlstm_bi, mlstm_bf, mlstm_norm_g, pool_w, pool_scale, w_out, w_ff1, w_ff2):
    depth = w_in.shape[0]
    o0 = 2 * LRU_W
    seg = lambda k: w_in[:, :, o0 + k * MW:o0 + (k + 1) * MW]
    g0 = o0 + 4 * MW
    q_cols = _pad_heads(seg(0)).at[:, :, G_OFF:G_OFF + 2 * NH].set(w_in[:, :, g0:g0 + 2 * NH])
    w_in_p = jnp.concatenate(
        [w_in[:, :, 0:o0], q_cols] + [_pad_heads(seg(k)) for k in range(1, 4)]
        + [w_in[:, :, g0 + 2 * NH:]], axis=-1).astype(BF16)
    wo_b = w_out[:, LRU_W:LRU_W + MW].reshape(depth, NH, HD, D_MODEL)
    wo_b = jnp.pad(wo_b, ((0, 0), (0, 0), (0, HP - HD), (0, 0))).reshape(depth, MWP, D_MODEL)
    w_out_p = jnp.concatenate([w_out[:, 0:LRU_W], wo_b, w_out[:, LRU_W + MW:]], axis=1).astype(BF16)
    row = lambda v: v.reshape(depth, 1, -1)
    return {
        "g1": row(norm1_g), "g2": row(norm2_g),
        "w_in": w_in_p, "w_out": w_out_p,
        "conv_w": conv_w, "conv_b": row(conv_b),
        "w_gate": jnp.concatenate([_block_diag(lru_wa), _block_diag(lru_wx)], axis=-1).astype(BF16),
        "b_gate": jnp.concatenate([row(lru_ba), row(lru_bx)], axis=-1),
        "lam": row(lru_lam),
        "gbias": jnp.pad(jnp.concatenate([mlstm_bi, mlstm_bf], axis=-1),
                         ((0, 0), (G_OFF, HP - G_OFF - 2 * NH)))[:, None, :],
        "ng": row(_pad_heads(mlstm_norm_g.reshape(depth, MW))),
        "pool_w": _block_diag(pool_w).astype(BF16),
        "pool_scale": row(pool_scale),
        "w_ff1": w_ff1.astype(BF16), "w_ff2": w_ff2.astype(BF16),
    }


def kernel(x_prompt, x_sample, state_lru_h, state_lru_conv, state_mlstm_C, state_mlstm_n, state_mlstm_m,
           state_pool, c_prompt, c_sample, norm1_g, norm2_g, w_ada, b_ada, w_in, conv_w, conv_b, lru_wa,
           lru_ba, lru_wx, lru_bx, lru_lam, mlstm_bi, mlstm_bf, mlstm_norm_g, pool_w, pool_scale, w_out,
           w_ff1, w_ff2, final_g):
    nb, seq, _ = x_prompt.shape
    ns = x_sample.shape[0]
    depth = w_in.shape[0]
    wts = _prepare_weights(norm1_g, norm2_g, w_in, conv_w, conv_b, lru_wa, lru_ba, lru_wx, lru_bx, lru_lam,
                           mlstm_bi, mlstm_bf, mlstm_norm_g, pool_w, pool_scale, w_out, w_ff1, w_ff2)
    final_row = final_g.reshape(1, D_MODEL)
    modp, mods = _modulation(jnp.concatenate([c_prompt, c_sample], axis=0), w_ada,
                             b_ada.reshape(depth, 1, -1), nb)

    sample_state = {
        "lru_h": state_lru_h,
        "conv": jnp.transpose(state_lru_conv, (0, 2, 1, 3)),
        "n": jnp.transpose(state_mlstm_n, (0, 2, 3, 1)),
        "pool": jnp.transpose(state_pool, (0, 2, 1, 3)),
    }
    m0_t = jnp.transpose(state_mlstm_m, (0, 2, 1))
    c0_t = jnp.transpose(state_mlstm_C, (0, 2, 3, 4, 1))
    c_new_t = None

    xp = x_prompt
    xs = x_sample.reshape(ns, D_MODEL)
    p_out = [[] for _ in range(6)]
    s_out = [[] for _ in range(5)]
    for l in range(depth):
        last = l == depth - 1
        xp, h_l, conv_l, c_l, n_l, m_l, pool_l = _prompt_mixer(xp, modp, l, wts, TM_MIX)
        for lst, val in zip(p_out, (h_l.reshape(nb, LRU_W), conv_l[:, 8 - 3:], c_l, n_l[:, :NH, :HD],
                                    m_l[:, :NH, 0], pool_l[:, 16 - POOL_BUF:])):
            lst.append(val)
        proj, qt, kst, vt, wvt, gs, dec, m_s = _sample_in(xs, mods, l, wts, m0_t)
        c_new_t, cq = _sample_state(c0_t, c_new_t, qt, kst, wvt, dec, l, VB_STATE)
        xs, h_s, conv_s, n_s, pool_s = _sample_mix(xs, mods, proj, qt, kst, vt, cq, gs, l, sample_state, wts)
        for lst, val in zip(s_out, (h_s, conv_s, n_s, m_s, pool_s)):
            lst.append(val)
        xp, xs = _ffn(xp.reshape(nb * seq, D_MODEL), xs, modp, mods, l, wts, final_row, TM_FFN, seq, last)
        xp = xp.reshape(nb, seq, D_MODEL)

    stack = lambda lst: jnp.stack(lst, axis=0)
    h_s, conv_s, n_s, m_s, pool_s = [stack(v) for v in s_out]
    return (xp, xs.reshape(ns, 1, D_MODEL),
            *[stack(v) for v in p_out],
            h_s,
            jnp.transpose(conv_s, (0, 2, 1, 3)),
            jnp.transpose(c_new_t, (0, 4, 1, 2, 3)),
            jnp.transpose(n_s, (0, 3, 1, 2)),
            jnp.transpose(m_s, (0, 2, 1)),
            jnp.transpose(pool_s, (0, 2, 1, 3)))
```

```python
import functools
import math

import jax
import jax.numpy as jnp
from jax import lax
from jax.experimental import pallas as pl
from jax.experimental.pallas import tpu as pltpu

F32 = jnp.float32
BF16 = jnp.bfloat16

D_MODEL = 1024
LRU_W = 384
LRU_HEADS = 6
LRU_C = 8.0
NH = 4
HD = 96
HP = 128
MW = NH * HD
MWP = NH * HP
CHUNK = 128
PW = 256
POOL_WINDOWS = (2, 4, 8, 16)
POOL_GROUP = 64
POOL_BUF = 15
D_FF = 4096
EPS = 1e-6
PAST_LEN = 16384

C_XR, C_GR, C_Q, C_K, C_V, C_O, C_U = 0, 384, 768, 1280, 1792, 2304, 2816
N_IN = 3072
G_OFF = HD
C_LATE = C_O
K_OUT = LRU_W + MWP + PW

VMEM_LIMIT = 56 * 1024 * 1024
TM_MIX = 512
PROJ_PIECE = 256
TM_FFN = 1024
VB_STATE = 96


def _log_sigmoid(x):
    return jnp.minimum(x, 0.0) - jnp.log1p(jnp.exp(-jnp.abs(x)))


def _sigmoid(x):
    return 1.0 / (1.0 + jnp.exp(-x))


def _gelu_tanh(x):
    c = math.sqrt(2.0 / math.pi)
    return 0.5 * x * (1.0 + jnp.tanh(c * (x + 0.044715 * (x * x * x))))


def _rms(x, g):
    ms = jnp.mean(x * x, axis=-1, keepdims=True)
    return x * lax.rsqrt(ms + EPS) * g


def _ada_norm(x, g, mod):
    ms = jnp.mean(x * x, axis=-1, keepdims=True)
    gain = g * (1.0 + mod[:, D_MODEL:2 * D_MODEL])
    return x * lax.rsqrt(ms + EPS) * gain + mod[:, 0:D_MODEL]


def _dot(a, b):
    return jnp.dot(a, b, preferred_element_type=F32)


def _dot_nt(a, b):
    return lax.dot_general(a, b, (((1,), (1,)), ((), ())), preferred_element_type=F32)


def _lru_coeffs(gates, log_lam):
    r = _sigmoid(gates[:, :LRU_W])
    i = _sigmoid(gates[:, LRU_W:])
    log_a = r * log_lam
    a = jnp.exp(log_a)
    th = jnp.tanh(log_a)
    mult = jnp.sqrt(-2.0 * th / (1.0 - th))
    return a, mult, i


def _scan_rows(a, b, h_prev):
    n, w = a.shape
    groups = n // 8
    a3 = a.reshape(groups, 8, w)
    b3 = b.reshape(groups, 8, w)
    row = lax.broadcasted_iota(jnp.int32, a3.shape, 1)
    for s in (1, 2, 4):
        keep = row >= s
        b3 = a3 * jnp.where(keep, pltpu.roll(b3, s, 1), 0.0) + b3
        a3 = a3 * jnp.where(keep, pltpu.roll(a3, s, 1), 1.0)
    carry = h_prev
    outs = []
    for g in range(groups):
        outs.append(a3[g] * carry + b3[g])
        carry = a3[g, 7:8] * carry + b3[g, 7:8]
    return jnp.concatenate(outs, axis=0), carry


def _head_norm_gate(hout, o, ng):
    ms = jnp.sum(hout * hout, axis=-1, keepdims=True) * (1.0 / HD)
    return _sigmoid(o) * (hout * lax.rsqrt(ms + EPS) * ng)


def _pool_select(s2, s4, s8, s16):
    lane = lax.broadcasted_iota(jnp.int32, s2.shape, 1)
    return jnp.where(lane < 64, s2, jnp.where(lane < 128, s4, jnp.where(lane < 192, s8, s16)))


def _pool_window_lanes(shape):
    lane = lax.broadcasted_iota(jnp.int32, shape, 1)
    return jnp.where(lane < 64, 2, jnp.where(lane < 128, 4, jnp.where(lane < 192, 8, 16)))


def _mod_kernel(c_ref, w_ref, b_ref, modp_ref, mods_ref, *, n_prompt):
    c = c_ref[...]
    act = (c * _sigmoid(c)).astype(BF16)
    out = _dot(act, w_ref[...].astype(BF16)) + b_ref[...]
    for r in range(n_prompt):
        modp_ref[r] = out[r:r + 1]
    mods_ref[...] = out[n_prompt:]


def _modulation(c_all, w_ada, b_ada, n_prompt):
    depth = w_ada.shape[0]
    n_all = c_all.shape[0]
    n_mod = w_ada.shape[2]
    tn = 1536
    return pl.pallas_call(
        functools.partial(_mod_kernel, n_prompt=n_prompt),
        grid=(depth, n_mod // tn),
        in_specs=[
            pl.BlockSpec((n_all, D_MODEL), lambda l, j: (0, 0)),
            pl.BlockSpec((None, D_MODEL, tn), lambda l, j: (l, 0, j)),
            pl.BlockSpec((None, 1, tn), lambda l, j: (l, 0, j)),
        ],
        out_specs=[
            pl.BlockSpec((None, n_prompt, 1, tn), lambda l, j: (l, 0, 0, j)),
            pl.BlockSpec((None, n_all - n_prompt, tn), lambda l, j: (l, 0, j)),
        ],
        out_shape=[
            jax.ShapeDtypeStruct((depth, n_prompt, 1, n_mod), F32),
            jax.ShapeDtypeStruct((depth, n_all - n_prompt, n_mod), F32),
        ],
        compiler_params=pltpu.CompilerParams(
            dimension_semantics=("arbitrary", "arbitrary"), vmem_limit_bytes=VMEM_LIMIT),
        name="adaln_modulation",
    )(c_all, w_ada, b_ada)


def _mlstm_tile(proj, gl, gl_t, state, n_chunks, pump):
    pairs = [(c, h) for c in range(n_chunks) for h in range(NH)]
    t_idx = lax.broadcasted_iota(jnp.int32, (CHUNK, CHUNK), 0)
    s_idx = lax.broadcasted_iota(jnp.int32, (CHUNK, CHUNK), 1)
    causal = s_idx <= t_idx
    upper = t_idx <= s_idx
    rows = lambda c: slice(c * CHUNK, (c + 1) * CHUNK)
    head = lambda base, h: slice(base + h * HP, base + (h + 1) * HP)

    bcum_col, dmat, rowmax, ib_col = {}, {}, {}, {}
    for c, h in pairs:
        lf_row = gl_t[NH + h:NH + h + 1, rows(c)]
        lf_col = gl[rows(c), NH + h:NH + h + 1]
        ib_col[c, h] = gl[rows(c), h:h + 1]
        bcum_col[c, h] = jnp.sum(jnp.where(causal, lf_row, 0.0), axis=1, keepdims=True)
        bcum_row = jnp.sum(jnp.where(upper, lf_col, 0.0), axis=0, keepdims=True)
        dmat[c, h] = jnp.where(causal, bcum_col[c, h] - bcum_row + gl_t[h:h + 1, rows(c)], -jnp.inf)
        rowmax[c, h] = jnp.max(dmat[c, h], axis=1, keepdims=True)
    pump(1)

    m_start, m_t = {}, {}
    m_run = [state[h][2] for h in range(NH)]
    for c, h in pairs:
        m_start[c, h] = m_run[h]
        m_t[c, h] = jnp.maximum(bcum_col[c, h] + m_run[h], rowmax[c, h])
        m_run[h] = m_t[c, h][CHUNK - 1:CHUNK, :]
    inter, pmat, floor, decay, w_col = {}, {}, {}, {}, {}
    for c, h in pairs:
        b_last = bcum_col[c, h][CHUNK - 1:CHUNK, :]
        inter[c, h] = jnp.exp(bcum_col[c, h] + m_start[c, h] - m_t[c, h])
        pmat[c, h] = jnp.exp(dmat[c, h] - m_t[c, h])
        floor[c, h] = jnp.exp(-m_t[c, h])
        m_new = m_t[c, h][CHUNK - 1:CHUNK, :]
        decay[c, h] = jnp.exp(b_last + m_start[c, h] - m_new)
        w_col[c, h] = jnp.exp(b_last - bcum_col[c, h] + ib_col[c, h] - m_new)
    pump(1)

    qb, kb, s_mat, sv, wvk, wk = {}, {}, {}, {}, {}, {}
    for c, h in pairs:
        qb[c, h] = proj[rows(c), head(C_Q, h)].astype(BF16)
        ks = proj[rows(c), head(C_K, h)] * (1.0 / math.sqrt(HD))
        kb[c, h] = ks.astype(BF16)
        wk[c, h] = jnp.sum(w_col[c, h] * ks, axis=0, keepdims=True)
        s_mat[c, h] = _dot_nt(qb[c, h], kb[c, h]) * pmat[c, h]
    pump(1)
    for c, h in pairs:
        v = proj[rows(c), head(C_V, h)]
        sv[c, h] = _dot(s_mat[c, h].astype(BF16), v.astype(BF16))
        wvk[c, h] = _dot((w_col[c, h] * v).T.astype(BF16), kb[c, h])
    pump(1)

    outs = [[] for _ in range(NH)]
    c_run = [state[h][0] for h in range(NH)]
    n_run = [state[h][1] for h in range(NH)]
    for c, h in pairs:
        q = proj[rows(c), head(C_Q, h)]
        qc = _dot_nt(qb[c, h], c_run[h].astype(BF16))
        qn = jnp.sum(q * n_run[h], axis=1, keepdims=True)
        num = sv[c, h] + inter[c, h] * qc
        den = jnp.sum(s_mat[c, h], axis=1, keepdims=True) + inter[c, h] * qn
        outs[h].append(num / jnp.maximum(jnp.abs(den), floor[c, h]))
        c_run[h] = decay[c, h] * c_run[h] + wvk[c, h]
        n_run[h] = decay[c, h] * n_run[h] + wk[c, h]
    return outs, [(c_run[h], n_run[h], m_run[h]) for h in range(NH)]


def _project(x, mod, g1, w_in):
    return _dot(_ada_norm(x, g1, mod).astype(BF16), w_in)


def _mix_tile(proj, x, gt1, j, tm, convw_ref, convb_ref, wgate_ref, bgate_ref,
              lam_ref, gbias_ref, ng_ref, poolw_ref, pscale_ref, wout_ref,
              xo_ref, hlast_ref, convo_ref, co_ref, no_ref, mo_ref, poolo_ref,
              h_sc, xbuf, ubuf, c_sc, n_sc, m_sc, pump=lambda n: None):
    row = lax.broadcasted_iota(jnp.int32, (tm, 1), 0)

    xr = proj[:, C_XR:C_XR + LRU_W]
    gr = proj[:, C_GR:C_GR + LRU_W]
    xbuf[8:8 + tm, :] = xr
    cw = convw_ref[...]
    xc = (convb_ref[...] + cw[3:4] * xr + cw[2:3] * xbuf[7:7 + tm, :]
          + cw[1:2] * xbuf[6:6 + tm, :] + cw[0:1] * xbuf[5:5 + tm, :])
    tail = xbuf[tm:tm + 8, :]
    xbuf[0:8, :] = tail
    convo_ref[0] = tail
    gates = _dot(xc.astype(BF16), wgate_ref[...]) + bgate_ref[...]
    pump(0)
    log_lam = LRU_C * _log_sigmoid(lam_ref[...])
    h_run = h_sc[...]
    y_a_parts = []
    n_split = 2
    for r in range(n_split):
        rs = slice(r * tm // n_split, (r + 1) * tm // n_split)
        a, mult, gate_i = _lru_coeffs(gates[rs], log_lam)
        if r == 0:
            mult = jnp.where(jnp.logical_and(row[rs] == 0, j == 0), 1.0, mult)
        pump(2)
        hseq, h_run = _scan_rows(a, mult * gate_i * xc[rs], h_run)
        pump(1)
        y_a_parts.append(hseq * _gelu_tanh(gr[rs]))
    h_sc[...] = h_run
    hlast_ref[0] = h_run
    y_a = jnp.concatenate(y_a_parts, axis=0)
    out_acc = _dot(y_a.astype(BF16), wout_ref[0:LRU_W, :])

    g_rows = (proj[:, C_Q:C_Q + HP] + gbias_ref[...]).T[G_OFF:G_OFF + 2 * NH]
    row8 = lax.broadcasted_iota(jnp.int32, g_rows.shape, 0)
    gl_t = jnp.where(row8 >= NH, _log_sigmoid(g_rows), g_rows)
    gl = jnp.concatenate([gl_t, jnp.zeros((HP - 2 * NH, tm), F32)], axis=0).T
    ng = ng_ref[...]
    state = [(c_sc[h], n_sc[h:h + 1, :], m_sc[h:h + 1, 0:1]) for h in range(NH)]
    outs, state = _mlstm_tile(proj, gl, gl_t, state, tm // CHUNK, pump)
    y_b_heads = []
    for h in range(NH):
        c_h, n_h, m_h = state[h]
        c_sc[h] = c_h
        n_sc[h:h + 1, :] = n_h
        m_sc[h:h + 1, :] = jnp.broadcast_to(m_h, (1, HP))
        co_ref[0, h] = c_h[0:HD, 0:HD]
        hout_all = jnp.concatenate(outs[h], axis=0) if len(outs[h]) > 1 else outs[h][0]
        o = proj[:, C_O + h * HP:C_O + (h + 1) * HP]
        y_b_heads.append(_head_norm_gate(hout_all, o, ng[:, h * HP:(h + 1) * HP]))
        pump(h % 2)
    no_ref[0] = n_sc[...]
    mo_ref[0] = m_sc[...]
    y_b = jnp.concatenate(y_b_heads, axis=1).astype(BF16)
    out_acc = out_acc + _dot(y_b, wout_ref[LRU_W:LRU_W + MWP, :])

    u = proj[:, C_U:C_U + PW]
    ubuf[16:16 + tm, :] = u
    ext = ubuf[...]
    s2 = ext + pltpu.roll(ext, 1, 0)
    s4 = s2 + pltpu.roll(s2, 2, 0)
    s8 = s4 + pltpu.roll(s4, 4, 0)
    s16 = s8 + pltpu.roll(s8, 8, 0)
    wsum = _pool_select(s2[16:], s4[16:], s8[16:], s16[16:])
    pos = row + j * tm
    cnt = jnp.minimum(pos + 1, _pool_window_lanes((tm, PW))).astype(F32)
    diff = wsum / cnt - u
    y_c = _dot(diff.astype(BF16), poolw_ref[...]) * pscale_ref[...]
    ptail = ubuf[tm:tm + 16, :]
    ubuf[0:16, :] = ptail
    poolo_ref[0] = ptail
    pump(N_IN)

    out_acc = out_acc + _dot(y_c.astype(BF16), wout_ref[LRU_W + MWP:K_OUT, :])
    xo_ref[0] = x + gt1 * out_acc


def _prompt_mixer_kernel(x_ref, xn_ref, mod_ref, modn_ref, g1_ref, win_ref, *rest, tm):
    weights_outs = rest[:17]
    h_sc, xbuf, ubuf, c_sc, n_sc, m_sc, proj_a, proj_b, lhs_sc = rest[17:]
    j = pl.program_id(1)
    t = pl.program_id(0) * pl.num_programs(1) + j

    @pl.when(j == 0)
    def _():
        h_sc[...] = jnp.zeros_like(h_sc)
        xbuf[0:8, :] = jnp.zeros((8, LRU_W), F32)
        ubuf[0:16, :] = jnp.zeros((16, PW), F32)
        c_sc[...] = jnp.zeros_like(c_sc)
        n_sc[...] = jnp.zeros_like(n_sc)
        m_sc[...] = jnp.zeros_like(m_sc)

    @pl.when(t == 0)
    def _():
        lhs0 = _ada_norm(x_ref[0], g1_ref[...], mod_ref[...]).astype(BF16)
        lhs_sc[...] = lhs0
        proj_a[:, 0:C_LATE] = _dot(lhs0, win_ref[:, 0:C_LATE])

    def step(proj_cur, proj_nxt):
        lhs_cur = lhs_sc[...]
        for c0 in range(C_LATE, N_IN, PROJ_PIECE):
            c1 = min(c0 + PROJ_PIECE, N_IN)
            proj_cur[:, c0:c1] = _dot(lhs_cur, win_ref[:, c0:c1])
        done = [0]
        lhs = []

        def pump(n):
            if not lhs:
                lhs.append(_ada_norm(xn_ref[0], g1_ref[...], modn_ref[...]).astype(BF16))
            for _ in range(n):
                c0 = done[0]
                if c0 >= C_LATE:
                    return
                c1 = min(c0 + PROJ_PIECE, C_LATE)
                proj_nxt[:, c0:c1] = _dot(lhs[0], win_ref[:, c0:c1])
                done[0] = c1

        gt1 = mod_ref[:, 2 * D_MODEL:3 * D_MODEL]
        _mix_tile(proj_cur, x_ref[0], gt1, j, tm, *weights_outs, h_sc, xbuf, ubuf, c_sc, n_sc, m_sc,
                  pump=pump)
        lhs_sc[...] = lhs[0]

    @pl.when(t % 2 == 0)
    def _():
        step(proj_a, proj_b)

    @pl.when(t % 2 == 1)
    def _():
        step(proj_b, proj_a)


def _prompt_mixer(x, modp, l, wts, tm):
    nb, seq, _ = x.shape
    n_t = seq // tm

    def next_tile(b, j):
        t1 = jnp.minimum(b * n_t + j + 1, nb * n_t - 1)
        return t1 // n_t, t1 % n_t

    lsel3 = lambda b, j: (l, 0, 0)
    wspec = lambda a: pl.BlockSpec((None,) + a.shape[1:], lsel3)
    names = ("g1", "w_in", "conv_w", "conv_b", "w_gate", "b_gate", "lam", "gbias", "ng", "pool_w",
             "pool_scale", "w_out")
    out_shapes = [
        jax.ShapeDtypeStruct((nb, seq, D_MODEL), F32),
        jax.ShapeDtypeStruct((nb, 1, LRU_W), F32),
        jax.ShapeDtypeStruct((nb, 8, LRU_W), F32),
        jax.ShapeDtypeStruct((nb, NH, HD, HD), F32),
        jax.ShapeDtypeStruct((nb, 8, HP), F32),
        jax.ShapeDtypeStruct((nb, 8, HP), F32),
        jax.ShapeDtypeStruct((nb, 16, PW), F32),
    ]
    out_specs = [
        pl.BlockSpec((1, tm, D_MODEL), lambda b, j: (b, j, 0)),
        pl.BlockSpec((1, 1, LRU_W), lambda b, j: (b, 0, 0)),
        pl.BlockSpec((1, 8, LRU_W), lambda b, j: (b, 0, 0)),
        pl.BlockSpec((1, NH, HD, HD), lambda b, j: (b, 0, 0, 0)),
        pl.BlockSpec((1, 8, HP), lambda b, j: (b, 0, 0)),
        pl.BlockSpec((1, 8, HP), lambda b, j: (b, 0, 0)),
        pl.BlockSpec((1, 16, PW), lambda b, j: (b, 0, 0)),
    ]
    return pl.pallas_call(
        functools.partial(_prompt_mixer_kernel, tm=tm),
        grid=(nb, seq // tm),
        in_specs=[
            pl.BlockSpec((1, tm, D_MODEL), lambda b, j: (b, j, 0)),
            pl.BlockSpec((1, tm, D_MODEL), lambda b, j: next_tile(b, j) + (0,)),
            pl.BlockSpec((None, None, 1, 3 * D_MODEL), lambda b, j: (l, b, 0, 0)),
            pl.BlockSpec((None, None, 1, 3 * D_MODEL), lambda b, j: (l, next_tile(b, j)[0], 0, 0)),
        ] + [wspec(wts[n]) for n in names],
        out_specs=out_specs,
        out_shape=out_shapes,
        scratch_shapes=[
            pltpu.VMEM((1, LRU_W), F32),
            pltpu.VMEM((tm + 8, LRU_W), F32),
            pltpu.VMEM((tm + 16, PW), F32),
            pltpu.VMEM((NH, HP, HP), F32),
            pltpu.VMEM((8, HP), F32),
            pltpu.VMEM((8, HP), F32),
            pltpu.VMEM((tm, N_IN), F32),
            pltpu.VMEM((tm, N_IN), F32),
            pltpu.VMEM((tm, D_MODEL), BF16),
        ],
        compiler_params=pltpu.CompilerParams(
            dimension_semantics=("arbitrary", "arbitrary"), vmem_limit_bytes=VMEM_LIMIT),
        name="prompt_mixer",
    )(x, x, modp, modp, *[wts[n] for n in names])


def _ffn_kernel(xp_ref, xs_ref, modp_ref, mods_ref, g2_ref, w1_ref, w2_ref, gf_ref, op_ref, os_ref,
                *, n_prompt_steps, final):
    def ffn(x_ref, mod_ref, o_ref):
        x = x_ref[...]
        mod = mod_ref[...]
        gt2 = mod[:, 2 * D_MODEL:3 * D_MODEL]
        h2 = _ada_norm(x, g2_ref[...], mod).astype(BF16)
        acc = jnp.zeros(x.shape, F32)
        step = 1024
        for c in range(D_FF // step):
            hid = _dot(h2, w1_ref[:, c * step:(c + 1) * step])
            hid = jnp.square(jnp.maximum(hid, 0.0)).astype(BF16)
            acc = acc + _dot(hid, w2_ref[c * step:(c + 1) * step, :])
        y = x + gt2 * acc
        if final:
            y = _rms(y, gf_ref[...])
        o_ref[...] = y

    i = pl.program_id(0)

    @pl.when(i < n_prompt_steps)
    def _():
        ffn(xp_ref, modp_ref, op_ref)

    @pl.when(i == n_prompt_steps)
    def _():
        ffn(xs_ref, mods_ref, os_ref)


def _ffn(xp2d, xs, modp, mods, l, wts, final_g, tm, seq, final):
    n_tok = xp2d.shape[0]
    ns = xs.shape[0]
    n_steps = n_tok // tm
    lsel3 = lambda i: (l, 0, 0)
    wspec = lambda a: pl.BlockSpec((None,) + a.shape[1:], lsel3, pipeline_mode=pl.Buffered(1))
    tile = lambda i: jnp.minimum(i, n_steps - 1)
    return pl.pallas_call(
        functools.partial(_ffn_kernel, n_prompt_steps=n_steps, final=final),
        grid=(n_steps + 1,),
        in_specs=[
            pl.BlockSpec((tm, D_MODEL), lambda i: (tile(i), 0)),
            pl.BlockSpec((ns, D_MODEL), lambda i: (0, 0)),
            pl.BlockSpec((None, None, 1, 3 * D_MODEL), lambda i: (l, tile(i) // (seq // tm), 0, 1)),
            pl.BlockSpec((None, ns, 3 * D_MODEL), lambda i: (l, 0, 1)),
            pl.BlockSpec((None, 1, D_MODEL), lsel3),
            wspec(wts["w_ff1"]),
            wspec(wts["w_ff2"]),
            pl.BlockSpec((1, D_MODEL), lambda i: (0, 0)),
        ],
        out_specs=[
            pl.BlockSpec((tm, D_MODEL), lambda i: (tile(i), 0)),
            pl.BlockSpec((ns, D_MODEL), lambda i: (0, 0)),
        ],
        out_shape=[
            jax.ShapeDtypeStruct((n_tok, D_MODEL), F32),
            jax.ShapeDtypeStruct((ns, D_MODEL), F32),
        ],
        compiler_params=pltpu.CompilerParams(
            dimension_semantics=("arbitrary",), vmem_limit_bytes=VMEM_LIMIT),
        name="ffn",
    )(xp2d, xs, modp, mods, wts["g2"], wts["w_ff1"], wts["w_ff2"], final_g)


def _sample_in_kernel(x_ref, mod_ref, g1_ref, win_ref, gbias_ref, m0_ref,
                      proj_ref, qt_ref, kst_ref, vt_ref, wvt_ref, gs_ref, dec_ref, mo_ref):
    proj = _project(x_ref[...], mod_ref[...], g1_ref[...], win_ref[...])
    proj_ref[...] = proj
    g_t = (proj[:, C_Q:C_Q + HP] + gbias_ref[...]).T
    ib = g_t[G_OFF:G_OFF + NH]
    g = _log_sigmoid(g_t[G_OFF + NH:G_OFF + 2 * NH]) + m0_ref[...]
    m_t = jnp.maximum(g, ib)
    inter = jnp.exp(g - m_t)
    p = jnp.exp(ib - m_t)
    mo_ref[...] = m_t
    gs_ref[0] = inter
    gs_ref[1] = p
    gs_ref[2] = jnp.exp(-m_t)
    q_t = proj[:, C_Q:C_Q + MWP].T
    ks_t = (proj[:, C_K:C_K + MWP] * (1.0 / math.sqrt(HD))).T
    v_t = proj[:, C_V:C_V + MWP].T
    for h in range(NH):
        hs = slice(h * HP, (h + 1) * HP)
        qt_ref[h] = q_t[hs]
        kst_ref[h] = ks_t[hs]
        vt_ref[h] = v_t[hs]
        wvt_ref[h] = p[h:h + 1] * v_t[hs]
        dec_ref[h] = inter[h:h + 1]


def _sample_in(xs, mods, l, wts, m0_t):
    n = xs.shape[0]
    lsel3 = lambda i: (l, 0, 0)
    head_t = jax.ShapeDtypeStruct((NH, HP, n), F32)
    out_shapes = [
        jax.ShapeDtypeStruct((n, N_IN), F32),
        head_t, head_t, head_t, head_t,
        jax.ShapeDtypeStruct((3, NH, n), F32),
        jax.ShapeDtypeStruct((NH, 1, n), F32),
        jax.ShapeDtypeStruct((NH, n), F32),
    ]
    return pl.pallas_call(
        _sample_in_kernel,
        grid=(1,),
        in_specs=[
            pl.BlockSpec((n, D_MODEL), lambda i: (0, 0)),
            pl.BlockSpec((None, n, 2 * D_MODEL), lsel3),
            pl.BlockSpec((None, 1, D_MODEL), lsel3),
            pl.BlockSpec((None, D_MODEL, N_IN), lsel3),
            pl.BlockSpec((None, 1, HP), lsel3),
            pl.BlockSpec((None, NH, n), lsel3),
        ],
        out_specs=[pl.BlockSpec(s.shape, lambda i, nd=len(s.shape): (0,) * nd) for s in out_shapes],
        out_shape=out_shapes,
        compiler_params=pltpu.CompilerParams(
            dimension_semantics=("arbitrary",), vmem_limit_bytes=VMEM_LIMIT),
        name="sample_in_proj",
    )(xs, mods, wts["g1"], wts["w_in"], wts["gbias"], m0_t)


def _sample_state_kernel(c0_ref, qt_ref, kst_ref, wvt_ref, dec_ref, *rest, vb):
    cnew_ref, cq_ref = rest[-2:]
    q = qt_ref[0:HD, :]
    ks = kst_ref[0:HD, :]
    dec = dec_ref[...]
    for vi in range(vb):
        c_old = c0_ref[vi]
        cq_ref[vi:vi + 1, :] = jnp.sum(c_old * q, axis=0, keepdims=True)
        cnew_ref[vi] = dec * c_old + wvt_ref[vi:vi + 1, :] * ks


def _sample_state(c0_t, c_out_prev, qt, kst, wvt, dec, l, vb):
    n = c0_t.shape[-1]
    head_spec = pl.BlockSpec((None, HP, n), lambda h, i: (h, 0, 0))
    in_specs = [
        pl.BlockSpec((None, None, vb, HD, n), lambda h, i: (l, h, i, 0, 0)),
        head_spec, head_spec,
        pl.BlockSpec((None, vb, n), lambda h, i: (h, i, 0)),
        pl.BlockSpec((None, 1, n), lambda h, i: (h, 0, 0)),
    ]
    args = [c0_t, qt, kst, wvt, dec]
    aliases = {}
    if c_out_prev is not None:
        in_specs.append(pl.BlockSpec(memory_space=pl.ANY))
        args.append(c_out_prev)
        aliases = {len(args) - 1: 0}
    return pl.pallas_call(
        functools.partial(_sample_state_kernel, vb=vb),
        grid=(NH, HD // vb),
        in_specs=in_specs,
        out_specs=[
            pl.BlockSpec((None, None, vb, HD, n), lambda h, i: (l, h, i, 0, 0)),
            pl.BlockSpec((None, vb, n), lambda h, i: (h, i, 0)),
        ],
        out_shape=[
            jax.ShapeDtypeStruct(c0_t.shape, F32),
            jax.ShapeDtypeStruct((NH, HD, n), F32),
        ],
        input_output_aliases=aliases,
        compiler_params=pltpu.CompilerParams(
            dimension_semantics=("arbitrary", "arbitrary"), vmem_limit_bytes=VMEM_LIMIT),
        name="sample_matrix_memory",
    )(*args)


def _sample_mix_kernel(x_ref, gt_ref, proj_ref, qt_ref, kst_ref, vt_ref, cq_ref, gs_ref,
                       h0_ref, conv_ref, n0_ref, pool_ref,
                       convw_ref, convb_ref, wgate_ref, bgate_ref, lam_ref, ng_ref,
                       poolw_ref, pscale_ref, wout_ref,
                       xo_ref, ho_ref, convo_ref, no_ref, poolo_ref):
    proj = proj_ref[...]
    n_tok = proj.shape[0]

    xr = proj[:, C_XR:C_XR + LRU_W]
    gr = proj[:, C_GR:C_GR + LRU_W]
    cw = convw_ref[...]
    xc = (convb_ref[...] + cw[3:4] * xr + cw[2:3] * conv_ref[2]
          + cw[1:2] * conv_ref[1] + cw[0:1] * conv_ref[0])
    convo_ref[0] = conv_ref[1]
    convo_ref[1] = conv_ref[2]
    convo_ref[2] = xr
    gates = _dot(xc.astype(BF16), wgate_ref[...]) + bgate_ref[...]
    a, mult, gate_i = _lru_coeffs(gates, LRU_C * _log_sigmoid(lam_ref[...]))
    h_new = a * h0_ref[...] + mult * gate_i * xc
    ho_ref[...] = h_new
    y_a = h_new * _gelu_tanh(gr)

    normed = []
    for h in range(NH):
        q = qt_ref[h, 0:HD, :]
        ks = kst_ref[h, 0:HD, :]
        v = vt_ref[h, 0:HD, :]
        n0 = n0_ref[h]
        inter, p, floor = gs_ref[0, h:h + 1, :], gs_ref[1, h:h + 1, :], gs_ref[2, h:h + 1, :]
        s = jnp.sum(q * ks, axis=0, keepdims=True) * p
        num = s * v + inter * cq_ref[h]
        den = s + inter * jnp.sum(n0 * q, axis=0, keepdims=True)
        hout = num / jnp.maximum(jnp.abs(den), floor)
        no_ref[h] = inter * n0 + p * ks
        ms = jnp.sum(hout * hout, axis=0, keepdims=True) * (1.0 / HD)
        normed.append(hout * lax.rsqrt(ms + EPS))
        normed.append(jnp.zeros((HP - HD, n_tok), F32))
    hm = jnp.concatenate(normed, axis=0).T
    y_b = _sigmoid(proj[:, C_O:C_O + MWP]) * (hm * ng_ref[...])

    u = proj[:, C_U:C_U + PW]
    sums, acc, nxt = {}, u, 1
    for win in POOL_WINDOWS:
        while nxt < win:
            acc = acc + pool_ref[POOL_BUF - nxt]
            nxt += 1
        sums[win] = acc
    wsum = _pool_select(sums[2], sums[4], sums[8], sums[16])
    cnt = jnp.minimum(PAST_LEN + 1, _pool_window_lanes(u.shape)).astype(F32)
    diff = wsum / cnt - u
    y_c = _dot(diff.astype(BF16), poolw_ref[...]) * pscale_ref[...]
    for r in range(POOL_BUF - 1):
        poolo_ref[r] = pool_ref[r + 1]
    poolo_ref[POOL_BUF - 1] = u

    mix = jnp.concatenate([y_a, y_b, y_c], axis=1).astype(BF16)
    xo_ref[...] = x_ref[...] + gt_ref[...] * _dot(mix, wout_ref[...])


def _sample_mix(xs, mods, proj, qt, kst, vt, cq, gs, l, st, wts):
    n = xs.shape[0]
    zeros = lambda nd: (lambda i: (0,) * nd)
    full = lambda a: pl.BlockSpec(a.shape, zeros(a.ndim))
    lspec = lambda a: pl.BlockSpec((None,) + a.shape[1:], lambda i, nd=a.ndim: (l,) + (0,) * (nd - 1))
    snames = ("lru_h", "conv", "n", "pool")
    wnames = ("conv_w", "conv_b", "w_gate", "b_gate", "lam", "ng", "pool_w", "pool_scale", "w_out")
    out_shapes = [
        jax.ShapeDtypeStruct((n, D_MODEL), F32),
        jax.ShapeDtypeStruct((n, LRU_W), F32),
        jax.ShapeDtypeStruct((3, n, LRU_W), F32),
        jax.ShapeDtypeStruct((NH, HD, n), F32),
        jax.ShapeDtypeStruct((POOL_BUF, n, PW), F32),
    ]
    return pl.pallas_call(
        _sample_mix_kernel,
        grid=(1,),
        in_specs=[
            full(xs),
            pl.BlockSpec((None, n, D_MODEL), lambda i: (l, 0, 2)),
            full(proj), full(qt), full(kst), full(vt), full(cq), full(gs),
        ] + [lspec(st[k]) for k in snames] + [lspec(wts[k]) for k in wnames],
        out_specs=[pl.BlockSpec(s.shape, zeros(len(s.shape))) for s in out_shapes],
        out_shape=out_shapes,
        compiler_params=pltpu.CompilerParams(
            dimension_semantics=("arbitrary",), vmem_limit_bytes=VMEM_LIMIT),
        name="sample_mixer",
    )(xs, mods, proj, qt, kst, vt, cq, gs, *[st[k] for k in snames], *[wts[k] for k in wnames])


def _pad_heads(w):
    lead = w.shape[:-1]
    w = w.reshape(lead + (NH, HD))
    w = jnp.pad(w, [(0, 0)] * len(lead) + [(0, 0), (0, HP - HD)])
    return w.reshape(lead + (MWP,))


def _block_diag(w):
    depth, g, n, _ = w.shape
    eye = jnp.eye(g, dtype=w.dtype)
    return (w[:, :, :, None, :] * eye[None, :, None, :, None]).reshape(depth, g * n, g * n)


def _permute_kernel(wint_ref, wout_ref, winp_ref, woutp_ref):
    o0 = 2 * LRU_W
    g0 = o0 + 4 * MW

    def put(dst_col, parts):
        blocks = [wint_ref[r0:r0 + n, :] for r0, n in parts]
        n_rows = sum(n for _, n in parts)
        if n_rows < HP:
            blocks.append(jnp.zeros((HP - n_rows, D_MODEL), F32))
        blk = jnp.concatenate(blocks, axis=0) if len(blocks) > 1 else blocks[0]
        winp_ref[:, dst_col:dst_col + HP] = blk.T.astype(BF16)

    for b in range(o0 // HP):
        put(b * HP, [(b * HP, HP)])
    for k, dst in enumerate((C_Q, C_K, C_V, C_O)):
        for h in range(NH):
            parts = [(o0 + k * MW + h * HD, HD)]
            if k == 0 and h == 0:
                parts.append((g0, 2 * NH))
            put(dst + h * HP, parts)
    for b in range(PW // HP):
        put(C_U + b * HP, [(g0 + 2 * NH + b * HP, HP)])

    woutp_ref[0:LRU_W, :] = wout_ref[0:LRU_W, :].astype(BF16)
    for h in range(NH):
        r0 = LRU_W + h * HP
        woutp_ref[r0:r0 + HD, :] = wout_ref[LRU_W + h * HD:LRU_W + (h + 1) * HD, :].astype(BF16)
        woutp_ref[r0 + HD:r0 + HP, :] = jnp.zeros((HP - HD, D_MODEL), BF16)
    woutp_ref[LRU_W + MWP:K_OUT, :] = wout_ref[LRU_W + MW:D_MODEL, :].astype(BF16)


def _permute_projections(w_in, w_out):
    depth, _, in_cols = w_in.shape
    lsel3 = lambda l: (l, 0, 0)
    return pl.pallas_call(
        _permute_kernel,
        grid=(depth,),
        in_specs=[
            pl.BlockSpec((None, in_cols, D_MODEL), lsel3),
            pl.BlockSpec((None, D_MODEL, D_MODEL), lsel3),
        ],
        out_specs=[
            pl.BlockSpec((None, D_MODEL, N_IN), lsel3),
            pl.BlockSpec((None, K_OUT, D_MODEL), lsel3),
        ],
        out_shape=[
            jax.ShapeDtypeStruct((depth, D_MODEL, N_IN), BF16),
            jax.ShapeDtypeStruct((depth, K_OUT, D_MODEL), BF16),
        ],
        compiler_params=pltpu.CompilerParams(
            dimension_semantics=("arbitrary",), vmem_limit_bytes=VMEM_LIMIT),
        name="permute_projections",
    )(jnp.transpose(w_in, (0, 2, 1)), w_out)


def _prepare_weights(norm1_g, norm2_g, w_in, conv_w, conv_b, lru_wa, lru_ba, lru_wx, lru_bx, lru_lam,
                     mlstm_bi, mlstm_bf, mlstm_norm_g, pool_w, pool_scale, w_out, w_ff1, w_ff2):
    depth = w_in.shape[0]
    w_in_p, w_out_p = _permute_projections(w_in, w_out)
    row = lambda v: v.reshape(depth, 1, -1)
    return {
        "g1": row(norm1_g), "g2": row(norm2_g),
        "w_in": w_in_p, "w_out": w_out_p,
        "conv_w": conv_w, "conv_b": row(conv_b),
        "w_gate": jnp.concatenate([_block_diag(lru_wa), _block_diag(lru_wx)], axis=-1).astype(BF16),
        "b_gate": jnp.concatenate([row(lru_ba), row(lru_bx)], axis=-1),
        "lam": row(lru_lam),
        "gbias": jnp.pad(jnp.concatenate([mlstm_bi, mlstm_bf], axis=-1),
                         ((0, 0), (G_OFF, HP - G_OFF - 2 * NH)))[:, None, :],
        "ng": row(_pad_heads(mlstm_norm_g.reshape(depth, MW))),
        "pool_w": _block_diag(pool_w).astype(BF16),
        "pool_scale": row(pool_scale),
        "w_ff1": w_ff1.astype(BF16), "w_ff2": w_ff2.astype(BF16),
    }


def kernel(x_prompt, x_sample, state_lru_h, state_lru_conv, state_mlstm_C, state_mlstm_n, state_mlstm_m,
           state_pool, c_prompt, c_sample, norm1_g, norm2_g, w_ada, b_ada, w_in, conv_w, conv_b, lru_wa,
           lru_ba, lru_wx, lru_bx, lru_lam, mlstm_bi, mlstm_bf, mlstm_norm_g, pool_w, pool_scale, w_out,
           w_ff1, w_ff2, final_g):
    nb, seq, _ = x_prompt.shape
    ns = x_sample.shape[0]
    depth = w_in.shape[0]
    wts = _prepare_weights(norm1_g, norm2_g, w_in, conv_w, conv_b, lru_wa, lru_ba, lru_wx, lru_bx, lru_lam,
                           mlstm_bi, mlstm_bf, mlstm_norm_g, pool_w, pool_scale, w_out, w_ff1, w_ff2)
    final_row = final_g.reshape(1, D_MODEL)
    modp, mods = _modulation(jnp.concatenate([c_prompt, c_sample], axis=0), w_ada,
                             b_ada.reshape(depth, 1, -1), nb)

    sample_state = {
        "lru_h": state_lru_h,
        "conv": jnp.transpose(state_lru_conv, (0, 2, 1, 3)),
        "n": jnp.transpose(state_mlstm_n, (0, 2, 3, 1)),
        "pool": jnp.transpose(state_pool, (0, 2, 1, 3)),
    }
    m0_t = jnp.transpose(state_mlstm_m, (0, 2, 1))
    c0_t = jnp.transpose(state_mlstm_C, (0, 2, 3, 4, 1))
    c_new_t = None

    xp = x_prompt
    xs = x_sample.reshape(ns, D_MODEL)
    p_out = [[] for _ in range(6)]
    s_out = [[] for _ in range(5)]
    for l in range(depth):
        last = l == depth - 1
        xp, h_l, conv_l, c_l, n_l, m_l, pool_l = _prompt_mixer(xp, modp, l, wts, TM_MIX)
        for lst, val in zip(p_out, (h_l.reshape(nb, LRU_W), conv_l[:, 8 - 3:], c_l, n_l[:, :NH, :HD],
                                    m_l[:, :NH, 0], pool_l[:, 16 - POOL_BUF:])):
            lst.append(val)
        proj, qt, kst, vt, wvt, gs, dec, m_s = _sample_in(xs, mods, l, wts, m0_t)
        c_new_t, cq = _sample_state(c0_t, c_new_t, qt, kst, wvt, dec, l, VB_STATE)
        xs, h_s, conv_s, n_s, pool_s = _sample_mix(xs, mods, proj, qt, kst, vt, cq, gs, l, sample_state, wts)
        for lst, val in zip(s_out, (h_s, conv_s, n_s, m_s, pool_s)):
            lst.append(val)
        xp, xs = _ffn(xp.reshape(nb * seq, D_MODEL), xs, modp, mods, l, wts, final_row, TM_FFN, seq, last)
        xp = xp.reshape(nb, seq, D_MODEL)

    stack = lambda lst: jnp.stack(lst, axis=0)
    h_s, conv_s, n_s, m_s, pool_s = [stack(v) for v in s_out]
    return (xp, xs.reshape(ns, 1, D_MODEL),
            *[stack(v) for v in p_out],
            h_s,
            jnp.transpose(conv_s, (0, 2, 1, 3)),
            jnp.transpose(c_new_t, (0, 4, 1, 2, 3)),
            jnp.transpose(n_s, (0, 3, 1, 2)),
            jnp.transpose(m_s, (0, 2, 1)),
            jnp.transpose(pool_s, (0, 2, 1, 3)))
```

```python
import functools
import math

import jax
import jax.numpy as jnp
from jax import lax
from jax.experimental import pallas as pl
from jax.experimental.pallas import tpu as pltpu

F32 = jnp.float32
BF16 = jnp.bfloat16

D_MODEL = 1024
LRU_W = 384
LRU_HEADS = 6
LRU_C = 8.0
NH = 4
HD = 96
HP = 128
MW = NH * HD
MWP = NH * HP
CHUNK = 128
PW = 256
POOL_WINDOWS = (2, 4, 8, 16)
POOL_GROUP = 64
POOL_BUF = 15
D_FF = 4096
EPS = 1e-6
PAST_LEN = 16384

C_XR, C_GR, C_Q, C_K, C_V, C_O, C_U = 0, 384, 768, 1280, 1792, 2304, 2816
N_IN = 3072
G_OFF = HD
K_OUT = LRU_W + MWP + PW

VMEM_LIMIT = 56 * 1024 * 1024
TM_MIX = 512
PROJ_PIECE = 256
TM_FFN = 1024
VB_STATE = 96


def _log_sigmoid(x):
    return jnp.minimum(x, 0.0) - jnp.log1p(jnp.exp(-jnp.abs(x)))


def _sigmoid(x):
    return 1.0 / (1.0 + jnp.exp(-x))


def _gelu_tanh(x):
    c = math.sqrt(2.0 / math.pi)
    return 0.5 * x * (1.0 + jnp.tanh(c * (x + 0.044715 * (x * x * x))))


def _rms(x, g):
    ms = jnp.mean(x * x, axis=-1, keepdims=True)
    return x * lax.rsqrt(ms + EPS) * g


def _ada_norm(x, g, mod):
    ms = jnp.mean(x * x, axis=-1, keepdims=True)
    gain = g * (1.0 + mod[:, D_MODEL:2 * D_MODEL])
    return x * lax.rsqrt(ms + EPS) * gain + mod[:, 0:D_MODEL]


def _dot(a, b):
    return jnp.dot(a, b, preferred_element_type=F32)


def _dot_nt(a, b):
    return lax.dot_general(a, b, (((1,), (1,)), ((), ())), preferred_element_type=F32)


def _lru_coeffs(gates, log_lam):
    r = _sigmoid(gates[:, :LRU_W])
    i = _sigmoid(gates[:, LRU_W:])
    log_a = r * log_lam
    a = jnp.exp(log_a)
    th = jnp.tanh(log_a)
    mult = jnp.sqrt(-2.0 * th / (1.0 - th))
    return a, mult, i


def _scan_rows(a, b, h_prev):
    n, w = a.shape
    groups = n // 8
    a3 = a.reshape(groups, 8, w)
    b3 = b.reshape(groups, 8, w)
    row = lax.broadcasted_iota(jnp.int32, a3.shape, 1)
    for s in (1, 2, 4):
        keep = row >= s
        b3 = a3 * jnp.where(keep, pltpu.roll(b3, s, 1), 0.0) + b3
        a3 = a3 * jnp.where(keep, pltpu.roll(a3, s, 1), 1.0)
    carry = h_prev
    outs = []
    for g in range(groups):
        outs.append(a3[g] * carry + b3[g])
        carry = a3[g, 7:8] * carry + b3[g, 7:8]
    return jnp.concatenate(outs, axis=0), carry


def _head_norm_gate(hout, o, ng):
    ms = jnp.sum(hout * hout, axis=-1, keepdims=True) * (1.0 / HD)
    return _sigmoid(o) * (hout * lax.rsqrt(ms + EPS) * ng)


def _pool_select(s2, s4, s8, s16):
    lane = lax.broadcasted_iota(jnp.int32, s2.shape, 1)
    return jnp.where(lane < 64, s2, jnp.where(lane < 128, s4, jnp.where(lane < 192, s8, s16)))


def _pool_window_lanes(shape):
    lane = lax.broadcasted_iota(jnp.int32, shape, 1)
    return jnp.where(lane < 64, 2, jnp.where(lane < 128, 4, jnp.where(lane < 192, 8, 16)))


def _mod_kernel(c_ref, w_ref, b_ref, modp_ref, mods_ref, *, n_prompt):
    c = c_ref[...]
    act = (c * _sigmoid(c)).astype(BF16)
    out = _dot(act, w_ref[...].astype(BF16)) + b_ref[...]
    for r in range(n_prompt):
        modp_ref[r] = out[r:r + 1]
    mods_ref[...] = out[n_prompt:]


def _modulation(c_all, w_ada, b_ada, n_prompt):
    depth = w_ada.shape[0]
    n_all = c_all.shape[0]
    n_mod = w_ada.shape[2]
    tn = 1536
    return pl.pallas_call(
        functools.partial(_mod_kernel, n_prompt=n_prompt),
        grid=(depth, n_mod // tn),
        in_specs=[
            pl.BlockSpec((n_all, D_MODEL), lambda l, j: (0, 0)),
            pl.BlockSpec((None, D_MODEL, tn), lambda l, j: (l, 0, j)),
            pl.BlockSpec((None, 1, tn), lambda l, j: (l, 0, j)),
        ],
        out_specs=[
            pl.BlockSpec((None, n_prompt, 1, tn), lambda l, j: (l, 0, 0, j)),
            pl.BlockSpec((None, n_all - n_prompt, tn), lambda l, j: (l, 0, j)),
        ],
        out_shape=[
            jax.ShapeDtypeStruct((depth, n_prompt, 1, n_mod), F32),
            jax.ShapeDtypeStruct((depth, n_all - n_prompt, n_mod), F32),
        ],
        compiler_params=pltpu.CompilerParams(
            dimension_semantics=("arbitrary", "arbitrary"), vmem_limit_bytes=VMEM_LIMIT),
        name="adaln_modulation",
    )(c_all, w_ada, b_ada)


def _mlstm_tile(proj, gl, gl_t, state, n_chunks, pump):
    pairs = [(c, h) for c in range(n_chunks) for h in range(NH)]
    t_idx = lax.broadcasted_iota(jnp.int32, (CHUNK, CHUNK), 0)
    s_idx = lax.broadcasted_iota(jnp.int32, (CHUNK, CHUNK), 1)
    causal = s_idx <= t_idx
    upper = t_idx <= s_idx
    rows = lambda c: slice(c * CHUNK, (c + 1) * CHUNK)
    head = lambda base, h: slice(base + h * HP, base + (h + 1) * HP)

    bcum_col, dmat, rowmax, ib_col = {}, {}, {}, {}
    for c, h in pairs:
        lf_row = gl_t[NH + h:NH + h + 1, rows(c)]
        lf_col = gl[rows(c), NH + h:NH + h + 1]
        ib_col[c, h] = gl[rows(c), h:h + 1]
        bcum_col[c, h] = jnp.sum(jnp.where(causal, lf_row, 0.0), axis=1, keepdims=True)
        bcum_row = jnp.sum(jnp.where(upper, lf_col, 0.0), axis=0, keepdims=True)
        dmat[c, h] = jnp.where(causal, bcum_col[c, h] - bcum_row + gl_t[h:h + 1, rows(c)], -jnp.inf)
        rowmax[c, h] = jnp.max(dmat[c, h], axis=1, keepdims=True)
    pump(1)

    m_start, m_t = {}, {}
    m_run = [state[h][2] for h in range(NH)]
    for c, h in pairs:
        m_start[c, h] = m_run[h]
        m_t[c, h] = jnp.maximum(bcum_col[c, h] + m_run[h], rowmax[c, h])
        m_run[h] = m_t[c, h][CHUNK - 1:CHUNK, :]
    inter, pmat, floor, decay, w_col = {}, {}, {}, {}, {}
    for c, h in pairs:
        b_last = bcum_col[c, h][CHUNK - 1:CHUNK, :]
        inter[c, h] = jnp.exp(bcum_col[c, h] + m_start[c, h] - m_t[c, h])
        pmat[c, h] = jnp.exp(dmat[c, h] - m_t[c, h])
        floor[c, h] = jnp.exp(-m_t[c, h])
        m_new = m_t[c, h][CHUNK - 1:CHUNK, :]
        decay[c, h] = jnp.exp(b_last + m_start[c, h] - m_new)
        w_col[c, h] = jnp.exp(b_last - bcum_col[c, h] + ib_col[c, h] - m_new)
    pump(1)

    qb, kb, s_mat, sv, wvk, wk = {}, {}, {}, {}, {}, {}
    for c, h in pairs:
        qb[c, h] = proj[rows(c), head(C_Q, h)].astype(BF16)
        ks = proj[rows(c), head(C_K, h)] * (1.0 / math.sqrt(HD))
        kb[c, h] = ks.astype(BF16)
        wk[c, h] = jnp.sum(w_col[c, h] * ks, axis=0, keepdims=True)
        s_mat[c, h] = _dot_nt(qb[c, h], kb[c, h]) * pmat[c, h]
    pump(1)
    for c, h in pairs:
        v = proj[rows(c), head(C_V, h)]
        sv[c, h] = _dot(s_mat[c, h].astype(BF16), v.astype(BF16))
        wvk[c, h] = _dot((w_col[c, h] * v).T.astype(BF16), kb[c, h])
    pump(1)

    outs = [[] for _ in range(NH)]
    c_run = [state[h][0] for h in range(NH)]
    n_run = [state[h][1] for h in range(NH)]
    for c, h in pairs:
        q = proj[rows(c), head(C_Q, h)]
        qc = _dot_nt(qb[c, h], c_run[h].astype(BF16))
        qn = jnp.sum(q * n_run[h], axis=1, keepdims=True)
        num = sv[c, h] + inter[c, h] * qc
        den = jnp.sum(s_mat[c, h], axis=1, keepdims=True) + inter[c, h] * qn
        outs[h].append(num / jnp.maximum(jnp.abs(den), floor[c, h]))
        c_run[h] = decay[c, h] * c_run[h] + wvk[c, h]
        n_run[h] = decay[c, h] * n_run[h] + wk[c, h]
    return outs, [(c_run[h], n_run[h], m_run[h]) for h in range(NH)]


def _project(x, mod, g1, w_in):
    return _dot(_ada_norm(x, g1, mod).astype(BF16), w_in)


def _mix_tile(proj, x, gt1, j, tm, convw_ref, convb_ref, wgate_ref, bgate_ref,
              lam_ref, gbias_ref, ng_ref, poolw_ref, pscale_ref, wout_ref,
              xo_ref, hlast_ref, convo_ref, co_ref, no_ref, mo_ref, poolo_ref,
              h_sc, xbuf, ubuf, c_sc, n_sc, m_sc, pump=lambda n: None):
    row = lax.broadcasted_iota(jnp.int32, (tm, 1), 0)

    xr = proj[:, C_XR:C_XR + LRU_W]
    gr = proj[:, C_GR:C_GR + LRU_W]
    xbuf[8:8 + tm, :] = xr
    cw = convw_ref[...]
    xc = (convb_ref[...] + cw[3:4] * xr + cw[2:3] * xbuf[7:7 + tm, :]
          + cw[1:2] * xbuf[6:6 + tm, :] + cw[0:1] * xbuf[5:5 + tm, :])
    tail = xbuf[tm:tm + 8, :]
    xbuf[0:8, :] = tail
    convo_ref[0] = tail
    gates = _dot(xc.astype(BF16), wgate_ref[...]) + bgate_ref[...]
    pump(0)
    log_lam = LRU_C * _log_sigmoid(lam_ref[...])
    h_run = h_sc[...]
    y_a_parts = []
    n_split = 2
    for r in range(n_split):
        rs = slice(r * tm // n_split, (r + 1) * tm // n_split)
        a, mult, gate_i = _lru_coeffs(gates[rs], log_lam)
        if r == 0:
            mult = jnp.where(jnp.logical_and(row[rs] == 0, j == 0), 1.0, mult)
        pump(2)
        hseq, h_run = _scan_rows(a, mult * gate_i * xc[rs], h_run)
        pump(1)
        y_a_parts.append(hseq * _gelu_tanh(gr[rs]))
    h_sc[...] = h_run
    hlast_ref[0] = h_run
    y_a = jnp.concatenate(y_a_parts, axis=0)
    out_acc = _dot(y_a.astype(BF16), wout_ref[0:LRU_W, :])

    g_rows = (proj[:, C_Q:C_Q + HP] + gbias_ref[...]).T[G_OFF:G_OFF + 2 * NH]
    row8 = lax.broadcasted_iota(jnp.int32, g_rows.shape, 0)
    gl_t = jnp.where(row8 >= NH, _log_sigmoid(g_rows), g_rows)
    gl = jnp.concatenate([gl_t, jnp.zeros((HP - 2 * NH, tm), F32)], axis=0).T
    ng = ng_ref[...]
    state = [(c_sc[h], n_sc[h:h + 1, :], m_sc[h:h + 1, 0:1]) for h in range(NH)]
    outs, state = _mlstm_tile(proj, gl, gl_t, state, tm // CHUNK, pump)
    y_b_heads = []
    for h in range(NH):
        c_h, n_h, m_h = state[h]
        c_sc[h] = c_h
        n_sc[h:h + 1, :] = n_h
        m_sc[h:h + 1, :] = jnp.broadcast_to(m_h, (1, HP))
        co_ref[0, h] = c_h[0:HD, 0:HD]
        hout_all = jnp.concatenate(outs[h], axis=0) if len(outs[h]) > 1 else outs[h][0]
        o = proj[:, C_O + h * HP:C_O + (h + 1) * HP]
        y_b_heads.append(_head_norm_gate(hout_all, o, ng[:, h * HP:(h + 1) * HP]))
        pump(h % 2)
    no_ref[0] = n_sc[...]
    mo_ref[0] = m_sc[...]
    y_b = jnp.concatenate(y_b_heads, axis=1).astype(BF16)
    out_acc = out_acc + _dot(y_b, wout_ref[LRU_W:LRU_W + MWP, :])

    u = proj[:, C_U:C_U + PW]
    ubuf[16:16 + tm, :] = u
    ext = ubuf[...]
    s2 = ext + pltpu.roll(ext, 1, 0)
    s4 = s2 + pltpu.roll(s2, 2, 0)
    s8 = s4 + pltpu.roll(s4, 4, 0)
    s16 = s8 + pltpu.roll(s8, 8, 0)
    wsum = _pool_select(s2[16:], s4[16:], s8[16:], s16[16:])
    pos = row + j * tm
    cnt = jnp.minimum(pos + 1, _pool_window_lanes((tm, PW))).astype(F32)
    diff = wsum / cnt - u
    y_c = _dot(diff.astype(BF16), poolw_ref[...]) * pscale_ref[...]
    ptail = ubuf[tm:tm + 16, :]
    ubuf[0:16, :] = ptail
    poolo_ref[0] = ptail
    pump(N_IN)

    out_acc = out_acc + _dot(y_c.astype(BF16), wout_ref[LRU_W + MWP:K_OUT, :])
    xo_ref[0] = x + gt1 * out_acc


def _prompt_mixer_kernel(x_ref, xn_ref, mod_ref, modn_ref, g1_ref, win_ref, *rest, tm):
    weights_outs = rest[:17]
    h_sc, xbuf, ubuf, c_sc, n_sc, m_sc, proj_a, proj_b = rest[17:]
    j = pl.program_id(1)
    t = pl.program_id(0) * pl.num_programs(1) + j

    @pl.when(j == 0)
    def _():
        h_sc[...] = jnp.zeros_like(h_sc)
        xbuf[0:8, :] = jnp.zeros((8, LRU_W), F32)
        ubuf[0:16, :] = jnp.zeros((16, PW), F32)
        c_sc[...] = jnp.zeros_like(c_sc)
        n_sc[...] = jnp.zeros_like(n_sc)
        m_sc[...] = jnp.zeros_like(m_sc)

    @pl.when(t == 0)
    def _():
        proj_a[...] = _project(x_ref[0], mod_ref[...], g1_ref[...], win_ref[...])

    def step(proj_cur, proj_nxt):
        done = [0]
        lhs = []

        def pump(n):
            if not lhs:
                lhs.append(_ada_norm(xn_ref[0], g1_ref[...], modn_ref[...]).astype(BF16))
            for _ in range(n):
                c0 = done[0]
                if c0 >= N_IN:
                    return
                c1 = min(c0 + PROJ_PIECE, N_IN)
                proj_nxt[:, c0:c1] = _dot(lhs[0], win_ref[:, c0:c1])
                done[0] = c1

        gt1 = mod_ref[:, 2 * D_MODEL:3 * D_MODEL]
        _mix_tile(proj_cur, x_ref[0], gt1, j, tm, *weights_outs, h_sc, xbuf, ubuf, c_sc, n_sc, m_sc,
                  pump=pump)

    @pl.when(t % 2 == 0)
    def _():
        step(proj_a, proj_b)

    @pl.when(t % 2 == 1)
    def _():
        step(proj_b, proj_a)


def _prompt_mixer(x, modp, l, wts, tm):
    nb, seq, _ = x.shape
    n_t = seq // tm

    def next_tile(b, j):
        t1 = jnp.minimum(b * n_t + j + 1, nb * n_t - 1)
        return t1 // n_t, t1 % n_t

    lsel3 = lambda b, j: (l, 0, 0)
    wspec = lambda a: pl.BlockSpec((None,) + a.shape[1:], lsel3)
    names = ("g1", "w_in", "conv_w", "conv_b", "w_gate", "b_gate", "lam", "gbias", "ng", "pool_w",
             "pool_scale", "w_out")
    out_shapes = [
        jax.ShapeDtypeStruct((nb, seq, D_MODEL), F32),
        jax.ShapeDtypeStruct((nb, 1, LRU_W), F32),
        jax.ShapeDtypeStruct((nb, 8, LRU_W), F32),
        jax.ShapeDtypeStruct((nb, NH, HD, HD), F32),
        jax.ShapeDtypeStruct((nb, 8, HP), F32),
        jax.ShapeDtypeStruct((nb, 8, HP), F32),
        jax.ShapeDtypeStruct((nb, 16, PW), F32),
    ]
    out_specs = [
        pl.BlockSpec((1, tm, D_MODEL), lambda b, j: (b, j, 0)),
        pl.BlockSpec((1, 1, LRU_W), lambda b, j: (b, 0, 0)),
        pl.BlockSpec((1, 8, LRU_W), lambda b, j: (b, 0, 0)),
        pl.BlockSpec((1, NH, HD, HD), lambda b, j: (b, 0, 0, 0)),
        pl.BlockSpec((1, 8, HP), lambda b, j: (b, 0, 0)),
        pl.BlockSpec((1, 8, HP), lambda b, j: (b, 0, 0)),
        pl.BlockSpec((1, 16, PW), lambda b, j: (b, 0, 0)),
    ]
    return pl.pallas_call(
        functools.partial(_prompt_mixer_kernel, tm=tm),
        grid=(nb, seq // tm),
        in_specs=[
            pl.BlockSpec((1, tm, D_MODEL), lambda b, j: (b, j, 0)),
            pl.BlockSpec((1, tm, D_MODEL), lambda b, j: next_tile(b, j) + (0,)),
            pl.BlockSpec((None, None, 1, 3 * D_MODEL), lambda b, j: (l, b, 0, 0)),
            pl.BlockSpec((None, None, 1, 3 * D_MODEL), lambda b, j: (l, next_tile(b, j)[0], 0, 0)),
        ] + [wspec(wts[n]) for n in names],
        out_specs=out_specs,
        out_shape=out_shapes,
        scratch_shapes=[
            pltpu.VMEM((1, LRU_W), F32),
            pltpu.VMEM((tm + 8, LRU_W), F32),
            pltpu.VMEM((tm + 16, PW), F32),
            pltpu.VMEM((NH, HP, HP), F32),
            pltpu.VMEM((8, HP), F32),
            pltpu.VMEM((8, HP), F32),
            pltpu.VMEM((tm, N_IN), F32),
            pltpu.VMEM((tm, N_IN), F32),
        ],
        compiler_params=pltpu.CompilerParams(
            dimension_semantics=("arbitrary", "arbitrary"), vmem_limit_bytes=VMEM_LIMIT),
        name="prompt_mixer",
    )(x, x, modp, modp, *[wts[n] for n in names])


def _ffn_kernel(xp_ref, xs_ref, modp_ref, mods_ref, g2_ref, w1_ref, w2_ref, gf_ref, *rest,
                n_prompt_steps, final):
    if len(rest) == 6:
        w1n_ref, w2n_ref, op_ref, os_ref, w1o_ref, w2o_ref = rest
    else:
        (op_ref, os_ref), w1n_ref = rest, None

    def ffn(x_ref, mod_ref, o_ref):
        x = x_ref[...]
        mod = mod_ref[...]
        gt2 = mod[:, 2 * D_MODEL:3 * D_MODEL]
        h2 = _ada_norm(x, g2_ref[...], mod).astype(BF16)
        acc = jnp.zeros(x.shape, F32)
        step = 1024
        for c in range(D_FF // step):
            hid = _dot(h2, w1_ref[:, c * step:(c + 1) * step])
            hid = jnp.square(jnp.maximum(hid, 0.0)).astype(BF16)
            acc = acc + _dot(hid, w2_ref[c * step:(c + 1) * step, :])
        y = x + gt2 * acc
        if final:
            y = _rms(y, gf_ref[...])
        o_ref[...] = y

    i = pl.program_id(0)

    @pl.when(i < n_prompt_steps)
    def _():
        ffn(xp_ref, modp_ref, op_ref)
        if w1n_ref is not None:
            w1o_ref[...] = w1n_ref[...].astype(BF16)
            w2o_ref[...] = w2n_ref[...].astype(BF16)

    @pl.when(i == n_prompt_steps)
    def _():
        ffn(xs_ref, mods_ref, os_ref)


def _ffn(xp2d, xs, modp, mods, l, g2, w1, w2, next_w, final_g, tm, seq, final):
    n_tok = xp2d.shape[0]
    ns = xs.shape[0]
    n_steps = n_tok // tm
    lsel3 = lambda i: (l, 0, 0)
    wspec = lambda a: pl.BlockSpec(a.shape, lambda i: (0, 0), pipeline_mode=pl.Buffered(1))
    tile = lambda i: jnp.minimum(i, n_steps - 1)
    in_specs = [
        pl.BlockSpec((tm, D_MODEL), lambda i: (tile(i), 0)),
        pl.BlockSpec((ns, D_MODEL), lambda i: (0, 0)),
        pl.BlockSpec((None, None, 1, 3 * D_MODEL), lambda i: (l, tile(i) // (seq // tm), 0, 1)),
        pl.BlockSpec((None, ns, 3 * D_MODEL), lambda i: (l, 0, 1)),
        pl.BlockSpec((None, 1, D_MODEL), lsel3),
        wspec(w1),
        wspec(w2),
        pl.BlockSpec((1, D_MODEL), lambda i: (0, 0)),
    ]
    out_specs = [
        pl.BlockSpec((tm, D_MODEL), lambda i: (tile(i), 0)),
        pl.BlockSpec((ns, D_MODEL), lambda i: (0, 0)),
    ]
    out_shape = [
        jax.ShapeDtypeStruct((n_tok, D_MODEL), F32),
        jax.ShapeDtypeStruct((ns, D_MODEL), F32),
    ]
    args = [xp2d, xs, modp, mods, g2, w1, w2, final_g]
    if next_w is not None:
        slab = D_FF // n_steps
        in_specs += [
            pl.BlockSpec((None, D_MODEL, slab), lambda i: (l + 1, 0, tile(i))),
            pl.BlockSpec((None, slab, D_MODEL), lambda i: (l + 1, tile(i), 0)),
        ]
        out_specs += [
            pl.BlockSpec((D_MODEL, slab), lambda i: (0, tile(i))),
            pl.BlockSpec((slab, D_MODEL), lambda i: (tile(i), 0)),
        ]
        out_shape += [
            jax.ShapeDtypeStruct((D_MODEL, D_FF), BF16),
            jax.ShapeDtypeStruct((D_FF, D_MODEL), BF16),
        ]
        args += list(next_w)
    return pl.pallas_call(
        functools.partial(_ffn_kernel, n_prompt_steps=n_steps, final=final),
        grid=(n_steps + 1,),
        in_specs=in_specs,
        out_specs=out_specs,
        out_shape=out_shape,
        compiler_params=pltpu.CompilerParams(
            dimension_semantics=("arbitrary",), vmem_limit_bytes=VMEM_LIMIT),
        name="ffn",
    )(*args)


def _sample_in_kernel(x_ref, mod_ref, g1_ref, win_ref, gbias_ref, m0_ref,
                      proj_ref, qt_ref, kst_ref, vt_ref, wvt_ref, gs_ref, dec_ref, mo_ref):
    proj = _project(x_ref[...], mod_ref[...], g1_ref[...], win_ref[...])
    proj_ref[...] = proj
    g_t = (proj[:, C_Q:C_Q + HP] + gbias_ref[...]).T
    ib = g_t[G_OFF:G_OFF + NH]
    g = _log_sigmoid(g_t[G_OFF + NH:G_OFF + 2 * NH]) + m0_ref[...]
    m_t = jnp.maximum(g, ib)
    inter = jnp.exp(g - m_t)
    p = jnp.exp(ib - m_t)
    mo_ref[...] = m_t
    gs_ref[0] = inter
    gs_ref[1] = p
    gs_ref[2] = jnp.exp(-m_t)
    q_t = proj[:, C_Q:C_Q + MWP].T
    ks_t = (proj[:, C_K:C_K + MWP] * (1.0 / math.sqrt(HD))).T
    v_t = proj[:, C_V:C_V + MWP].T
    for h in range(NH):
        hs = slice(h * HP, (h + 1) * HP)
        qt_ref[h] = q_t[hs]
        kst_ref[h] = ks_t[hs]
        vt_ref[h] = v_t[hs]
        wvt_ref[h] = p[h:h + 1] * v_t[hs]
        dec_ref[h] = inter[h:h + 1]


def _sample_in(xs, mods, l, wts, m0_t):
    n = xs.shape[0]
    lsel3 = lambda i: (l, 0, 0)
    head_t = jax.ShapeDtypeStruct((NH, HP, n), F32)
    out_shapes = [
        jax.ShapeDtypeStruct((n, N_IN), F32),
        head_t, head_t, head_t, head_t,
        jax.ShapeDtypeStruct((3, NH, n), F32),
        jax.ShapeDtypeStruct((NH, 1, n), F32),
        jax.ShapeDtypeStruct((NH, n), F32),
    ]
    return pl.pallas_call(
        _sample_in_kernel,
        grid=(1,),
        in_specs=[
            pl.BlockSpec((n, D_MODEL), lambda i: (0, 0)),
            pl.BlockSpec((None, n, 2 * D_MODEL), lsel3),
            pl.BlockSpec((None, 1, D_MODEL), lsel3),
            pl.BlockSpec((None, D_MODEL, N_IN), lsel3),
            pl.BlockSpec((None, 1, HP), lsel3),
            pl.BlockSpec((None, NH, n), lsel3),
        ],
        out_specs=[pl.BlockSpec(s.shape, lambda i, nd=len(s.shape): (0,) * nd) for s in out_shapes],
        out_shape=out_shapes,
        compiler_params=pltpu.CompilerParams(
            dimension_semantics=("arbitrary",), vmem_limit_bytes=VMEM_LIMIT),
        name="sample_in_proj",
    )(xs, mods, wts["g1"], wts["w_in"], wts["gbias"], m0_t)


def _sample_state_kernel(c0_ref, qt_ref, kst_ref, wvt_ref, dec_ref, *rest, vb):
    cnew_ref, cq_ref = rest[-2:]
    q = qt_ref[0:HD, :]
    ks = kst_ref[0:HD, :]
    dec = dec_ref[...]
    for vi in range(vb):
        c_old = c0_ref[vi]
        cq_ref[vi:vi + 1, :] = jnp.sum(c_old * q, axis=0, keepdims=True)
        cnew_ref[vi] = dec * c_old + wvt_ref[vi:vi + 1, :] * ks


def _sample_state(c0_t, c_out_prev, qt, kst, wvt, dec, l, vb):
    n = c0_t.shape[-1]
    head_spec = pl.BlockSpec((None, HP, n), lambda h, i: (h, 0, 0))
    in_specs = [
        pl.BlockSpec((None, None, vb, HD, n), lambda h, i: (l, h, i, 0, 0)),
        head_spec, head_spec,
        pl.BlockSpec((None, vb, n), lambda h, i: (h, i, 0)),
        pl.BlockSpec((None, 1, n), lambda h, i: (h, 0, 0)),
    ]
    args = [c0_t, qt, kst, wvt, dec]
    aliases = {}
    if c_out_prev is not None:
        in_specs.append(pl.BlockSpec(memory_space=pl.ANY))
        args.append(c_out_prev)
        aliases = {len(args) - 1: 0}
    return pl.pallas_call(
        functools.partial(_sample_state_kernel, vb=vb),
        grid=(NH, HD // vb),
        in_specs=in_specs,
        out_specs=[
            pl.BlockSpec((None, None, vb, HD, n), lambda h, i: (l, h, i, 0, 0)),
            pl.BlockSpec((None, vb, n), lambda h, i: (h, i, 0)),
        ],
        out_shape=[
            jax.ShapeDtypeStruct(c0_t.shape, F32),
            jax.ShapeDtypeStruct((NH, HD, n), F32),
        ],
        input_output_aliases=aliases,
        compiler_params=pltpu.CompilerParams(
            dimension_semantics=("arbitrary", "arbitrary"), vmem_limit_bytes=VMEM_LIMIT),
        name="sample_matrix_memory",
    )(*args)


def _sample_mix_kernel(x_ref, gt_ref, proj_ref, qt_ref, kst_ref, vt_ref, cq_ref, gs_ref,
                       h0_ref, conv_ref, n0_ref, pool_ref,
                       convw_ref, convb_ref, wgate_ref, bgate_ref, lam_ref, ng_ref,
                       poolw_ref, pscale_ref, wout_ref,
                       xo_ref, ho_ref, convo_ref, no_ref, poolo_ref):
    proj = proj_ref[...]
    n_tok = proj.shape[0]

    xr = proj[:, C_XR:C_XR + LRU_W]
    gr = proj[:, C_GR:C_GR + LRU_W]
    cw = convw_ref[...]
    xc = (convb_ref[...] + cw[3:4] * xr + cw[2:3] * conv_ref[2]
          + cw[1:2] * conv_ref[1] + cw[0:1] * conv_ref[0])
    convo_ref[0] = conv_ref[1]
    convo_ref[1] = conv_ref[2]
    convo_ref[2] = xr
    gates = _dot(xc.astype(BF16), wgate_ref[...]) + bgate_ref[...]
    a, mult, gate_i = _lru_coeffs(gates, LRU_C * _log_sigmoid(lam_ref[...]))
    h_new = a * h0_ref[...] + mult * gate_i * xc
    ho_ref[...] = h_new
    y_a = h_new * _gelu_tanh(gr)

    normed = []
    for h in range(NH):
        q = qt_ref[h, 0:HD, :]
        ks = kst_ref[h, 0:HD, :]
        v = vt_ref[h, 0:HD, :]
        n0 = n0_ref[h]
        inter, p, floor = gs_ref[0, h:h + 1, :], gs_ref[1, h:h + 1, :], gs_ref[2, h:h + 1, :]
        s = jnp.sum(q * ks, axis=0, keepdims=True) * p
        num = s * v + inter * cq_ref[h]
        den = s + inter * jnp.sum(n0 * q, axis=0, keepdims=True)
        hout = num / jnp.maximum(jnp.abs(den), floor)
        no_ref[h] = inter * n0 + p * ks
        ms = jnp.sum(hout * hout, axis=0, keepdims=True) * (1.0 / HD)
        normed.append(hout * lax.rsqrt(ms + EPS))
        normed.append(jnp.zeros((HP - HD, n_tok), F32))
    hm = jnp.concatenate(normed, axis=0).T
    y_b = _sigmoid(proj[:, C_O:C_O + MWP]) * (hm * ng_ref[...])

    u = proj[:, C_U:C_U + PW]
    sums, acc, nxt = {}, u, 1
    for win in POOL_WINDOWS:
        while nxt < win:
            acc = acc + pool_ref[POOL_BUF - nxt]
            nxt += 1
        sums[win] = acc
    wsum = _pool_select(sums[2], sums[4], sums[8], sums[16])
    cnt = jnp.minimum(PAST_LEN + 1, _pool_window_lanes(u.shape)).astype(F32)
    diff = wsum / cnt - u
    y_c = _dot(diff.astype(BF16), poolw_ref[...]) * pscale_ref[...]
    for r in range(POOL_BUF - 1):
        poolo_ref[r] = pool_ref[r + 1]
    poolo_ref[POOL_BUF - 1] = u

    mix = jnp.concatenate([y_a, y_b, y_c], axis=1).astype(BF16)
    xo_ref[...] = x_ref[...] + gt_ref[...] * _dot(mix, wout_ref[...])


def _sample_mix(xs, mods, proj, qt, kst, vt, cq, gs, l, st, wts):
    n = xs.shape[0]
    zeros = lambda nd: (lambda i: (0,) * nd)
    full = lambda a: pl.BlockSpec(a.shape, zeros(a.ndim))
    lspec = lambda a: pl.BlockSpec((None,) + a.shape[1:], lambda i, nd=a.ndim: (l,) + (0,) * (nd - 1))
    snames = ("lru_h", "conv", "n", "pool")
    wnames = ("conv_w", "conv_b", "w_gate", "b_gate", "lam", "ng", "pool_w", "pool_scale", "w_out")
    out_shapes = [
        jax.ShapeDtypeStruct((n, D_MODEL), F32),
        jax.ShapeDtypeStruct((n, LRU_W), F32),
        jax.ShapeDtypeStruct((3, n, LRU_W), F32),
        jax.ShapeDtypeStruct((NH, HD, n), F32),
        jax.ShapeDtypeStruct((POOL_BUF, n, PW), F32),
    ]
    return pl.pallas_call(
        _sample_mix_kernel,
        grid=(1,),
        in_specs=[
            full(xs),
            pl.BlockSpec((None, n, D_MODEL), lambda i: (l, 0, 2)),
            full(proj), full(qt), full(kst), full(vt), full(cq), full(gs),
        ] + [lspec(st[k]) for k in snames] + [lspec(wts[k]) for k in wnames],
        out_specs=[pl.BlockSpec(s.shape, zeros(len(s.shape))) for s in out_shapes],
        out_shape=out_shapes,
        compiler_params=pltpu.CompilerParams(
            dimension_semantics=("arbitrary",), vmem_limit_bytes=VMEM_LIMIT),
        name="sample_mixer",
    )(xs, mods, proj, qt, kst, vt, cq, gs, *[st[k] for k in snames], *[wts[k] for k in wnames])


def _pad_heads(w):
    lead = w.shape[:-1]
    w = w.reshape(lead + (NH, HD))
    w = jnp.pad(w, [(0, 0)] * len(lead) + [(0, 0), (0, HP - HD)])
    return w.reshape(lead + (MWP,))


def _block_diag(w):
    depth, g, n, _ = w.shape
    eye = jnp.eye(g, dtype=w.dtype)
    return (w[:, :, :, None, :] * eye[None, :, None, :, None]).reshape(depth, g * n, g * n)


def _permute_kernel(wint_ref, wout_ref, winp_ref, woutp_ref):
    o0 = 2 * LRU_W
    g0 = o0 + 4 * MW

    def put(dst_col, parts):
        blocks = [wint_ref[r0:r0 + n, :] for r0, n in parts]
        n_rows = sum(n for _, n in parts)
        if n_rows < HP:
            blocks.append(jnp.zeros((HP - n_rows, D_MODEL), F32))
        blk = jnp.concatenate(blocks, axis=0) if len(blocks) > 1 else blocks[0]
        winp_ref[:, dst_col:dst_col + HP] = blk.T.astype(BF16)

    for b in range(o0 // HP):
        put(b * HP, [(b * HP, HP)])
    for k, dst in enumerate((C_Q, C_K, C_V, C_O)):
        for h in range(NH):
            parts = [(o0 + k * MW + h * HD, HD)]
            if k == 0 and h == 0:
                parts.append((g0, 2 * NH))
            put(dst + h * HP, parts)
    for b in range(PW // HP):
        put(C_U + b * HP, [(g0 + 2 * NH + b * HP, HP)])

    woutp_ref[0:LRU_W, :] = wout_ref[0:LRU_W, :].astype(BF16)
    for h in range(NH):
        r0 = LRU_W + h * HP
        woutp_ref[r0:r0 + HD, :] = wout_ref[LRU_W + h * HD:LRU_W + (h + 1) * HD, :].astype(BF16)
        woutp_ref[r0 + HD:r0 + HP, :] = jnp.zeros((HP - HD, D_MODEL), BF16)
    woutp_ref[LRU_W + MWP:K_OUT, :] = wout_ref[LRU_W + MW:D_MODEL, :].astype(BF16)


def _permute_projections(w_in, w_out):
    depth, _, in_cols = w_in.shape
    lsel3 = lambda l: (l, 0, 0)
    return pl.pallas_call(
        _permute_kernel,
        grid=(depth,),
        in_specs=[
            pl.BlockSpec((None, in_cols, D_MODEL), lsel3),
            pl.BlockSpec((None, D_MODEL, D_MODEL), lsel3),
        ],
        out_specs=[
            pl.BlockSpec((None, D_MODEL, N_IN), lsel3),
            pl.BlockSpec((None, K_OUT, D_MODEL), lsel3),
        ],
        out_shape=[
            jax.ShapeDtypeStruct((depth, D_MODEL, N_IN), BF16),
            jax.ShapeDtypeStruct((depth, K_OUT, D_MODEL), BF16),
        ],
        compiler_params=pltpu.CompilerParams(
            dimension_semantics=("arbitrary",), vmem_limit_bytes=VMEM_LIMIT),
        name="permute_projections",
    )(jnp.transpose(w_in, (0, 2, 1)), w_out)


def _prepare_weights(norm1_g, norm2_g, w_in, conv_w, conv_b, lru_wa, lru_ba, lru_wx, lru_bx, lru_lam,
                     mlstm_bi, mlstm_bf, mlstm_norm_g, pool_w, pool_scale, w_out):
    depth = w_in.shape[0]
    w_in_p, w_out_p = _permute_projections(w_in, w_out)
    row = lambda v: v.reshape(depth, 1, -1)
    return {
        "g1": row(norm1_g), "g2": row(norm2_g),
        "w_in": w_in_p, "w_out": w_out_p,
        "conv_w": conv_w, "conv_b": row(conv_b),
        "w_gate": jnp.concatenate([_block_diag(lru_wa), _block_diag(lru_wx)], axis=-1).astype(BF16),
        "b_gate": jnp.concatenate([row(lru_ba), row(lru_bx)], axis=-1),
        "lam": row(lru_lam),
        "gbias": jnp.pad(jnp.concatenate([mlstm_bi, mlstm_bf], axis=-1),
                         ((0, 0), (G_OFF, HP - G_OFF - 2 * NH)))[:, None, :],
        "ng": row(_pad_heads(mlstm_norm_g.reshape(depth, MW))),
        "pool_w": _block_diag(pool_w).astype(BF16),
        "pool_scale": row(pool_scale),
    }


def kernel(x_prompt, x_sample, state_lru_h, state_lru_conv, state_mlstm_C, state_mlstm_n, state_mlstm_m,
           state_pool, c_prompt, c_sample, norm1_g, norm2_g, w_ada, b_ada, w_in, conv_w, conv_b, lru_wa,
           lru_ba, lru_wx, lru_bx, lru_lam, mlstm_bi, mlstm_bf, mlstm_norm_g, pool_w, pool_scale, w_out,
           w_ff1, w_ff2, final_g):
    nb, seq, _ = x_prompt.shape
    ns = x_sample.shape[0]
    depth = w_in.shape[0]
    wts = _prepare_weights(norm1_g, norm2_g, w_in, conv_w, conv_b, lru_wa, lru_ba, lru_wx, lru_bx, lru_lam,
                           mlstm_bi, mlstm_bf, mlstm_norm_g, pool_w, pool_scale, w_out)
    w1_bf, w2_bf = w_ff1[0].astype(BF16), w_ff2[0].astype(BF16)
    final_row = final_g.reshape(1, D_MODEL)
    modp, mods = _modulation(jnp.concatenate([c_prompt, c_sample], axis=0), w_ada,
                             b_ada.reshape(depth, 1, -1), nb)

    sample_state = {
        "lru_h": state_lru_h,
        "conv": jnp.transpose(state_lru_conv, (0, 2, 1, 3)),
        "n": jnp.transpose(state_mlstm_n, (0, 2, 3, 1)),
        "pool": jnp.transpose(state_pool, (0, 2, 1, 3)),
    }
    m0_t = jnp.transpose(state_mlstm_m, (0, 2, 1))
    c0_t = jnp.transpose(state_mlstm_C, (0, 2, 3, 4, 1))
    c_new_t = None

    xp = x_prompt
    xs = x_sample.reshape(ns, D_MODEL)
    p_out = [[] for _ in range(6)]
    s_out = [[] for _ in range(5)]
    for l in range(depth):
        last = l == depth - 1
        xp, h_l, conv_l, c_l, n_l, m_l, pool_l = _prompt_mixer(xp, modp, l, wts, TM_MIX)
        for lst, val in zip(p_out, (h_l.reshape(nb, LRU_W), conv_l[:, 8 - 3:], c_l, n_l[:, :NH, :HD],
                                    m_l[:, :NH, 0], pool_l[:, 16 - POOL_BUF:])):
            lst.append(val)
        proj, qt, kst, vt, wvt, gs, dec, m_s = _sample_in(xs, mods, l, wts, m0_t)
        c_new_t, cq = _sample_state(c0_t, c_new_t, qt, kst, wvt, dec, l, VB_STATE)
        xs, h_s, conv_s, n_s, pool_s = _sample_mix(xs, mods, proj, qt, kst, vt, cq, gs, l, sample_state, wts)
        for lst, val in zip(s_out, (h_s, conv_s, n_s, m_s, pool_s)):
            lst.append(val)
        res = _ffn(xp.reshape(nb * seq, D_MODEL), xs, modp, mods, l, wts["g2"], w1_bf, w2_bf,
                   None if last else (w_ff1, w_ff2), final_row, TM_FFN, seq, last)
        xp, xs = res[0].reshape(nb, seq, D_MODEL), res[1]
        if not last:
            w1_bf, w2_bf = res[2], res[3]

    stack = lambda lst: jnp.stack(lst, axis=0)
    h_s, conv_s, n_s, m_s, pool_s = [stack(v) for v in s_out]
    return (xp, xs.reshape(ns, 1, D_MODEL),
            *[stack(v) for v in p_out],
            h_s,
            jnp.transpose(conv_s, (0, 2, 1, 3)),
            jnp.transpose(c_new_t, (0, 4, 1, 2, 3)),
            jnp.transpose(n_s, (0, 3, 1, 2)),
            jnp.transpose(m_s, (0, 2, 1)),
            jnp.transpose(pool_s, (0, 2, 1, 3)))
```

```python
import functools
import math

import jax
import jax.numpy as jnp
from jax import lax
from jax.experimental import pallas as pl
from jax.experimental.pallas import tpu as pltpu

F32 = jnp.float32
BF16 = jnp.bfloat16

D_MODEL = 1024
LRU_W = 384
LRU_HEADS = 6
LRU_C = 8.0
NH = 4
HD = 96
HP = 128
MW = NH * HD
MWP = NH * HP
CHUNK = 128
PW = 256
POOL_WINDOWS = (2, 4, 8, 16)
POOL_GROUP = 64
POOL_BUF = 15
D_FF = 4096
EPS = 1e-6
PAST_LEN = 16384

C_XR, C_GR, C_Q, C_K, C_V, C_O, C_U = 0, 384, 768, 1280, 1792, 2304, 2816
N_IN = 3072
G_OFF = HD
K_OUT = LRU_W + MWP + PW

VMEM_LIMIT = 56 * 1024 * 1024
TM_MIX = 512
PROJ_PIECE = 256
TM_FFN = 1024
VB_STATE = 96


def _log_sigmoid(x):
    return jnp.minimum(x, 0.0) - jnp.log1p(jnp.exp(-jnp.abs(x)))


def _sigmoid(x):
    return 1.0 / (1.0 + jnp.exp(-x))


def _gelu_tanh(x):
    c = math.sqrt(2.0 / math.pi)
    return 0.5 * x * (1.0 + jnp.tanh(c * (x + 0.044715 * (x * x * x))))


def _rms(x, g):
    ms = jnp.mean(x * x, axis=-1, keepdims=True)
    return x * lax.rsqrt(ms + EPS) * g


def _ada_norm(x, g, mod):
    ms = jnp.mean(x * x, axis=-1, keepdims=True)
    gain = g * (1.0 + mod[:, D_MODEL:2 * D_MODEL])
    return x * lax.rsqrt(ms + EPS) * gain + mod[:, 0:D_MODEL]


def _dot(a, b):
    return jnp.dot(a, b, preferred_element_type=F32)


def _dot_nt(a, b):
    return lax.dot_general(a, b, (((1,), (1,)), ((), ())), preferred_element_type=F32)


def _lru_coeffs(gates, log_lam):
    r = _sigmoid(gates[:, :LRU_W])
    i = _sigmoid(gates[:, LRU_W:])
    log_a = r * log_lam
    a = jnp.exp(log_a)
    th = jnp.tanh(log_a)
    mult = jnp.sqrt(-2.0 * th / (1.0 - th))
    return a, mult, i


def _scan_rows(a, b, h_prev):
    n, w = a.shape
    groups = n // 8
    a3 = a.reshape(groups, 8, w)
    b3 = b.reshape(groups, 8, w)
    row = lax.broadcasted_iota(jnp.int32, a3.shape, 1)
    for s in (1, 2, 4):
        keep = row >= s
        b3 = a3 * jnp.where(keep, pltpu.roll(b3, s, 1), 0.0) + b3
        a3 = a3 * jnp.where(keep, pltpu.roll(a3, s, 1), 1.0)
    carry = h_prev
    outs = []
    for g in range(groups):
        outs.append(a3[g] * carry + b3[g])
        carry = a3[g, 7:8] * carry + b3[g, 7:8]
    return jnp.concatenate(outs, axis=0), carry


def _head_norm_gate(hout, o, ng):
    ms = jnp.sum(hout * hout, axis=-1, keepdims=True) * (1.0 / HD)
    return _sigmoid(o) * (hout * lax.rsqrt(ms + EPS) * ng)


def _pool_select(s2, s4, s8, s16):
    lane = lax.broadcasted_iota(jnp.int32, s2.shape, 1)
    return jnp.where(lane < 64, s2, jnp.where(lane < 128, s4, jnp.where(lane < 192, s8, s16)))


def _pool_window_lanes(shape):
    lane = lax.broadcasted_iota(jnp.int32, shape, 1)
    return jnp.where(lane < 64, 2, jnp.where(lane < 128, 4, jnp.where(lane < 192, 8, 16)))


def _mod_kernel(c_ref, w_ref, b_ref, modp_ref, mods_ref, *, n_prompt):
    c = c_ref[...]
    act = (c * _sigmoid(c)).astype(BF16)
    out = _dot(act, w_ref[...].astype(BF16)) + b_ref[...]
    for r in range(n_prompt):
        modp_ref[r] = out[r:r + 1]
    mods_ref[...] = out[n_prompt:]


def _modulation(c_all, w_ada, b_ada, n_prompt):
    depth = w_ada.shape[0]
    n_all = c_all.shape[0]
    n_mod = w_ada.shape[2]
    tn = 1536
    return pl.pallas_call(
        functools.partial(_mod_kernel, n_prompt=n_prompt),
        grid=(depth, n_mod // tn),
        in_specs=[
            pl.BlockSpec((n_all, D_MODEL), lambda l, j: (0, 0)),
            pl.BlockSpec((None, D_MODEL, tn), lambda l, j: (l, 0, j)),
            pl.BlockSpec((None, 1, tn), lambda l, j: (l, 0, j)),
        ],
        out_specs=[
            pl.BlockSpec((None, n_prompt, 1, tn), lambda l, j: (l, 0, 0, j)),
            pl.BlockSpec((None, n_all - n_prompt, tn), lambda l, j: (l, 0, j)),
        ],
        out_shape=[
            jax.ShapeDtypeStruct((depth, n_prompt, 1, n_mod), F32),
            jax.ShapeDtypeStruct((depth, n_all - n_prompt, n_mod), F32),
        ],
        compiler_params=pltpu.CompilerParams(
            dimension_semantics=("arbitrary", "arbitrary"), vmem_limit_bytes=VMEM_LIMIT),
        name="adaln_modulation",
    )(c_all, w_ada, b_ada)


def _mlstm_tile(proj, gl, gl_t, state, n_chunks, pump):
    pairs = [(c, h) for c in range(n_chunks) for h in range(NH)]
    t_idx = lax.broadcasted_iota(jnp.int32, (CHUNK, CHUNK), 0)
    s_idx = lax.broadcasted_iota(jnp.int32, (CHUNK, CHUNK), 1)
    causal = s_idx <= t_idx
    upper = t_idx <= s_idx
    rows = lambda c: slice(c * CHUNK, (c + 1) * CHUNK)
    head = lambda base, h: slice(base + h * HP, base + (h + 1) * HP)

    bcum_col, dmat, rowmax, ib_col = {}, {}, {}, {}
    for c, h in pairs:
        lf_row = gl_t[NH + h:NH + h + 1, rows(c)]
        lf_col = gl[rows(c), NH + h:NH + h + 1]
        ib_col[c, h] = gl[rows(c), h:h + 1]
        bcum_col[c, h] = jnp.sum(jnp.where(causal, lf_row, 0.0), axis=1, keepdims=True)
        bcum_row = jnp.sum(jnp.where(upper, lf_col, 0.0), axis=0, keepdims=True)
        dmat[c, h] = jnp.where(causal, bcum_col[c, h] - bcum_row + gl_t[h:h + 1, rows(c)], -jnp.inf)
        rowmax[c, h] = jnp.max(dmat[c, h], axis=1, keepdims=True)
    pump(1)

    m_start, m_t = {}, {}
    m_run = [state[h][2] for h in range(NH)]
    for c, h in pairs:
        m_start[c, h] = m_run[h]
        m_t[c, h] = jnp.maximum(bcum_col[c, h] + m_run[h], rowmax[c, h])
        m_run[h] = m_t[c, h][CHUNK - 1:CHUNK, :]
    inter, pmat, floor, decay, w_col = {}, {}, {}, {}, {}
    for c, h in pairs:
        b_last = bcum_col[c, h][CHUNK - 1:CHUNK, :]
        inter[c, h] = jnp.exp(bcum_col[c, h] + m_start[c, h] - m_t[c, h])
        pmat[c, h] = jnp.exp(dmat[c, h] - m_t[c, h])
        floor[c, h] = jnp.exp(-m_t[c, h])
        m_new = m_t[c, h][CHUNK - 1:CHUNK, :]
        decay[c, h] = jnp.exp(b_last + m_start[c, h] - m_new)
        w_col[c, h] = jnp.exp(b_last - bcum_col[c, h] + ib_col[c, h] - m_new)
    pump(1)

    qb, kb, s_mat, sv, wvk, wk = {}, {}, {}, {}, {}, {}
    for c, h in pairs:
        qb[c, h] = proj[rows(c), head(C_Q, h)].astype(BF16)
        ks = proj[rows(c), head(C_K, h)] * (1.0 / math.sqrt(HD))
        kb[c, h] = ks.astype(BF16)
        wk[c, h] = jnp.sum(w_col[c, h] * ks, axis=0, keepdims=True)
        s_mat[c, h] = _dot_nt(qb[c, h], kb[c, h]) * pmat[c, h]
    pump(1)
    for c, h in pairs:
        v = proj[rows(c), head(C_V, h)]
        sv[c, h] = _dot(s_mat[c, h].astype(BF16), v.astype(BF16))
        wvk[c, h] = _dot((w_col[c, h] * v).T.astype(BF16), kb[c, h])
    pump(1)

    outs = [[] for _ in range(NH)]
    c_run = [state[h][0] for h in range(NH)]
    n_run = [state[h][1] for h in range(NH)]
    for c, h in pairs:
        q = proj[rows(c), head(C_Q, h)]
        qc = _dot_nt(qb[c, h], c_run[h].astype(BF16))
        qn = jnp.sum(q * n_run[h], axis=1, keepdims=True)
        num = sv[c, h] + inter[c, h] * qc
        den = jnp.sum(s_mat[c, h], axis=1, keepdims=True) + inter[c, h] * qn
        outs[h].append(num / jnp.maximum(jnp.abs(den), floor[c, h]))
        c_run[h] = decay[c, h] * c_run[h] + wvk[c, h]
        n_run[h] = decay[c, h] * n_run[h] + wk[c, h]
    return outs, [(c_run[h], n_run[h], m_run[h]) for h in range(NH)]


def _project(x, mod, g1, w_in):
    return _dot(_ada_norm(x, g1, mod).astype(BF16), w_in)


def _mix_tile(proj, x, gt1, j, tm, convw_ref, convb_ref, wgate_ref, bgate_ref,
              lam_ref, gbias_ref, ng_ref, poolw_ref, pscale_ref, wout_ref,
              xo_ref, hlast_ref, convo_ref, co_ref, no_ref, mo_ref, poolo_ref,
              h_sc, xbuf, ubuf, c_sc, n_sc, m_sc, pump=lambda n: None):
    row = lax.broadcasted_iota(jnp.int32, (tm, 1), 0)

    xr = proj[:, C_XR:C_XR + LRU_W]
    gr = proj[:, C_GR:C_GR + LRU_W]
    xbuf[8:8 + tm, :] = xr
    cw = convw_ref[...]
    xc = (convb_ref[...] + cw[3:4] * xr + cw[2:3] * xbuf[7:7 + tm, :]
          + cw[1:2] * xbuf[6:6 + tm, :] + cw[0:1] * xbuf[5:5 + tm, :])
    tail = xbuf[tm:tm + 8, :]
    xbuf[0:8, :] = tail
    convo_ref[0] = tail
    gates = _dot(xc.astype(BF16), wgate_ref[...]) + bgate_ref[...]
    pump(0)
    log_lam = LRU_C * _log_sigmoid(lam_ref[...])
    h_run = h_sc[...]
    y_a_parts = []
    n_split = 2
    for r in range(n_split):
        rs = slice(r * tm // n_split, (r + 1) * tm // n_split)
        a, mult, gate_i = _lru_coeffs(gates[rs], log_lam)
        if r == 0:
            mult = jnp.where(jnp.logical_and(row[rs] == 0, j == 0), 1.0, mult)
        pump(2)
        hseq, h_run = _scan_rows(a, mult * gate_i * xc[rs], h_run)
        pump(1)
        y_a_parts.append(hseq * _gelu_tanh(gr[rs]))
    h_sc[...] = h_run
    hlast_ref[0] = h_run
    y_a = jnp.concatenate(y_a_parts, axis=0)
    out_acc = _dot(y_a.astype(BF16), wout_ref[0:LRU_W, :])

    g_rows = (proj[:, C_Q:C_Q + HP] + gbias_ref[...]).T[G_OFF:G_OFF + 2 * NH]
    row8 = lax.broadcasted_iota(jnp.int32, g_rows.shape, 0)
    gl_t = jnp.where(row8 >= NH, _log_sigmoid(g_rows), g_rows)
    gl = jnp.concatenate([gl_t, jnp.zeros((HP - 2 * NH, tm), F32)], axis=0).T
    ng = ng_ref[...]
    state = [(c_sc[h], n_sc[h:h + 1, :], m_sc[h:h + 1, 0:1]) for h in range(NH)]
    outs, state = _mlstm_tile(proj, gl, gl_t, state, tm // CHUNK, pump)
    y_b_heads = []
    for h in range(NH):
        c_h, n_h, m_h = state[h]
        c_sc[h] = c_h
        n_sc[h:h + 1, :] = n_h
        m_sc[h:h + 1, :] = jnp.broadcast_to(m_h, (1, HP))
        co_ref[0, h] = c_h[0:HD, 0:HD]
        hout_all = jnp.concatenate(outs[h], axis=0) if len(outs[h]) > 1 else outs[h][0]
        o = proj[:, C_O + h * HP:C_O + (h + 1) * HP]
        y_b_heads.append(_head_norm_gate(hout_all, o, ng[:, h * HP:(h + 1) * HP]))
        pump(h % 2)
    no_ref[0] = n_sc[...]
    mo_ref[0] = m_sc[...]
    y_b = jnp.concatenate(y_b_heads, axis=1).astype(BF16)
    out_acc = out_acc + _dot(y_b, wout_ref[LRU_W:LRU_W + MWP, :])

    u = proj[:, C_U:C_U + PW]
    ubuf[16:16 + tm, :] = u
    ext = ubuf[...]
    s2 = ext + pltpu.roll(ext, 1, 0)
    s4 = s2 + pltpu.roll(s2, 2, 0)
    s8 = s4 + pltpu.roll(s4, 4, 0)
    s16 = s8 + pltpu.roll(s8, 8, 0)
    wsum = _pool_select(s2[16:], s4[16:], s8[16:], s16[16:])
    pos = row + j * tm
    cnt = jnp.minimum(pos + 1, _pool_window_lanes((tm, PW))).astype(F32)
    diff = wsum / cnt - u
    y_c = _dot(diff.astype(BF16), poolw_ref[...]) * pscale_ref[...]
    ptail = ubuf[tm:tm + 16, :]
    ubuf[0:16, :] = ptail
    poolo_ref[0] = ptail
    pump(N_IN)

    out_acc = out_acc + _dot(y_c.astype(BF16), wout_ref[LRU_W + MWP:K_OUT, :])
    xo_ref[0] = x + gt1 * out_acc


def _prompt_mixer_kernel(x_ref, xn_ref, mod_ref, modn_ref, g1_ref, win_ref, *rest, tm):
    weights_outs = rest[:17]
    h_sc, xbuf, ubuf, c_sc, n_sc, m_sc, proj_a, proj_b = rest[17:]
    j = pl.program_id(1)
    t = pl.program_id(0) * pl.num_programs(1) + j

    @pl.when(j == 0)
    def _():
        h_sc[...] = jnp.zeros_like(h_sc)
        xbuf[0:8, :] = jnp.zeros((8, LRU_W), F32)
        ubuf[0:16, :] = jnp.zeros((16, PW), F32)
        c_sc[...] = jnp.zeros_like(c_sc)
        n_sc[...] = jnp.zeros_like(n_sc)
        m_sc[...] = jnp.zeros_like(m_sc)

    @pl.when(t == 0)
    def _():
        proj_a[...] = _project(x_ref[0], mod_ref[...], g1_ref[...], win_ref[...])

    def step(proj_cur, proj_nxt):
        done = [0]
        lhs = []

        def pump(n):
            if not lhs:
                lhs.append(_ada_norm(xn_ref[0], g1_ref[...], modn_ref[...]).astype(BF16))
            for _ in range(n):
                c0 = done[0]
                if c0 >= N_IN:
                    return
                c1 = min(c0 + PROJ_PIECE, N_IN)
                proj_nxt[:, c0:c1] = _dot(lhs[0], win_ref[:, c0:c1])
                done[0] = c1

        gt1 = mod_ref[:, 2 * D_MODEL:3 * D_MODEL]
        _mix_tile(proj_cur, x_ref[0], gt1, j, tm, *weights_outs, h_sc, xbuf, ubuf, c_sc, n_sc, m_sc,
                  pump=pump)

    @pl.when(t % 2 == 0)
    def _():
        step(proj_a, proj_b)

    @pl.when(t % 2 == 1)
    def _():
        step(proj_b, proj_a)


def _prompt_mixer(x, modp, l, wts, tm):
    nb, seq, _ = x.shape
    n_t = seq // tm

    def next_tile(b, j):
        t1 = jnp.minimum(b * n_t + j + 1, nb * n_t - 1)
        return t1 // n_t, t1 % n_t

    lsel3 = lambda b, j: (l, 0, 0)
    wspec = lambda a: pl.BlockSpec((None,) + a.shape[1:], lsel3)
    names = ("g1", "w_in", "conv_w", "conv_b", "w_gate", "b_gate", "lam", "gbias", "ng", "pool_w",
             "pool_scale", "w_out")
    out_shapes = [
        jax.ShapeDtypeStruct((nb, seq, D_MODEL), F32),
        jax.ShapeDtypeStruct((nb, 1, LRU_W), F32),
        jax.ShapeDtypeStruct((nb, 8, LRU_W), F32),
        jax.ShapeDtypeStruct((nb, NH, HD, HD), F32),
        jax.ShapeDtypeStruct((nb, 8, HP), F32),
        jax.ShapeDtypeStruct((nb, 8, HP), F32),
        jax.ShapeDtypeStruct((nb, 16, PW), F32),
    ]
    out_specs = [
        pl.BlockSpec((1, tm, D_MODEL), lambda b, j: (b, j, 0)),
        pl.BlockSpec((1, 1, LRU_W), lambda b, j: (b, 0, 0)),
        pl.BlockSpec((1, 8, LRU_W), lambda b, j: (b, 0, 0)),
        pl.BlockSpec((1, NH, HD, HD), lambda b, j: (b, 0, 0, 0)),
        pl.BlockSpec((1, 8, HP), lambda b, j: (b, 0, 0)),
        pl.BlockSpec((1, 8, HP), lambda b, j: (b, 0, 0)),
        pl.BlockSpec((1, 16, PW), lambda b, j: (b, 0, 0)),
    ]
    return pl.pallas_call(
        functools.partial(_prompt_mixer_kernel, tm=tm),
        grid=(nb, seq // tm),
        in_specs=[
            pl.BlockSpec((1, tm, D_MODEL), lambda b, j: (b, j, 0)),
            pl.BlockSpec((1, tm, D_MODEL), lambda b, j: next_tile(b, j) + (0,)),
            pl.BlockSpec((None, None, 1, 3 * D_MODEL), lambda b, j: (l, b, 0, 0)),
            pl.BlockSpec((None, None, 1, 3 * D_MODEL), lambda b, j: (l, next_tile(b, j)[0], 0, 0)),
        ] + [wspec(wts[n]) for n in names],
        out_specs=out_specs,
        out_shape=out_shapes,
        scratch_shapes=[
            pltpu.VMEM((1, LRU_W), F32),
            pltpu.VMEM((tm + 8, LRU_W), F32),
            pltpu.VMEM((tm + 16, PW), F32),
            pltpu.VMEM((NH, HP, HP), F32),
            pltpu.VMEM((8, HP), F32),
            pltpu.VMEM((8, HP), F32),
            pltpu.VMEM((tm, N_IN), F32),
            pltpu.VMEM((tm, N_IN), F32),
        ],
        compiler_params=pltpu.CompilerParams(
            dimension_semantics=("arbitrary", "arbitrary"), vmem_limit_bytes=VMEM_LIMIT),
        name="prompt_mixer",
    )(x, x, modp, modp, *[wts[n] for n in names])


def _ffn_kernel(xp_ref, xs_ref, modp_ref, mods_ref, g2_ref, w1_ref, w2_ref, gf_ref, *rest,
                n_prompt_steps, final):
    if len(rest) == 6:
        w1n_ref, w2n_ref, op_ref, os_ref, w1o_ref, w2o_ref = rest
    else:
        (op_ref, os_ref), w1n_ref = rest, None

    def ffn(x_ref, mod_ref, o_ref):
        x = x_ref[...]
        mod = mod_ref[...]
        gt2 = mod[:, 2 * D_MODEL:3 * D_MODEL]
        h2 = _ada_norm(x, g2_ref[...], mod).astype(BF16)
        acc = jnp.zeros(x.shape, F32)
        step = 1024
        for c in range(D_FF // step):
            hid = _dot(h2, w1_ref[:, c * step:(c + 1) * step])
            hid = jnp.square(jnp.maximum(hid, 0.0)).astype(BF16)
            acc = acc + _dot(hid, w2_ref[c * step:(c + 1) * step, :])
        y = x + gt2 * acc
        if final:
            y = _rms(y, gf_ref[...])
        o_ref[...] = y

    i = pl.program_id(0)

    @pl.when(i < n_prompt_steps)
    def _():
        ffn(xp_ref, modp_ref, op_ref)
        if w1n_ref is not None:
            w1o_ref[...] = w1n_ref[...].astype(BF16)
            w2o_ref[...] = w2n_ref[...].astype(BF16)

    @pl.when(i == n_prompt_steps)
    def _():
        ffn(xs_ref, mods_ref, os_ref)


def _ffn(xp2d, xs, modp, mods, l, g2, w1, w2, next_w, final_g, tm, seq, final):
    n_tok = xp2d.shape[0]
    ns = xs.shape[0]
    n_steps = n_tok // tm
    lsel3 = lambda i: (l, 0, 0)
    wspec = lambda a: pl.BlockSpec(a.shape, lambda i: (0, 0), pipeline_mode=pl.Buffered(1))
    tile = lambda i: jnp.minimum(i, n_steps - 1)
    in_specs = [
        pl.BlockSpec((tm, D_MODEL), lambda i: (tile(i), 0)),
        pl.BlockSpec((ns, D_MODEL), lambda i: (0, 0)),
        pl.BlockSpec((None, None, 1, 3 * D_MODEL), lambda i: (l, tile(i) // (seq // tm), 0, 1)),
        pl.BlockSpec((None, ns, 3 * D_MODEL), lambda i: (l, 0, 1)),
        pl.BlockSpec((None, 1, D_MODEL), lsel3),
        wspec(w1),
        wspec(w2),
        pl.BlockSpec((1, D_MODEL), lambda i: (0, 0)),
    ]
    out_specs = [
        pl.BlockSpec((tm, D_MODEL), lambda i: (tile(i), 0)),
        pl.BlockSpec((ns, D_MODEL), lambda i: (0, 0)),
    ]
    out_shape = [
        jax.ShapeDtypeStruct((n_tok, D_MODEL), F32),
        jax.ShapeDtypeStruct((ns, D_MODEL), F32),
    ]
    args = [xp2d, xs, modp, mods, g2, w1, w2, final_g]
    if next_w is not None:
        slab = D_FF // n_steps
        in_specs += [
            pl.BlockSpec((None, D_MODEL, slab), lambda i: (l + 1, 0, tile(i))),
            pl.BlockSpec((None, slab, D_MODEL), lambda i: (l + 1, tile(i), 0)),
        ]
        out_specs += [
            pl.BlockSpec((D_MODEL, slab), lambda i: (0, tile(i))),
            pl.BlockSpec((slab, D_MODEL), lambda i: (tile(i), 0)),
        ]
        out_shape += [
            jax.ShapeDtypeStruct((D_MODEL, D_FF), BF16),
            jax.ShapeDtypeStruct((D_FF, D_MODEL), BF16),
        ]
        args += list(next_w)
    return pl.pallas_call(
        functools.partial(_ffn_kernel, n_prompt_steps=n_steps, final=final),
        grid=(n_steps + 1,),
        in_specs=in_specs,
        out_specs=out_specs,
        out_shape=out_shape,
        compiler_params=pltpu.CompilerParams(
            dimension_semantics=("arbitrary",), vmem_limit_bytes=VMEM_LIMIT),
        name="ffn",
    )(*args)


def _sample_in_kernel(x_ref, mod_ref, g1_ref, win_ref, gbias_ref, m0_ref,
                      proj_ref, qt_ref, kst_ref, vt_ref, wvt_ref, gs_ref, dec_ref, mo_ref):
    proj = _project(x_ref[...], mod_ref[...], g1_ref[...], win_ref[...])
    proj_ref[...] = proj
    g_t = (proj[:, C_Q:C_Q + HP] + gbias_ref[...]).T
    ib = g_t[G_OFF:G_OFF + NH]
    g = _log_sigmoid(g_t[G_OFF + NH:G_OFF + 2 * NH]) + m0_ref[...]
    m_t = jnp.maximum(g, ib)
    inter = jnp.exp(g - m_t)
    p = jnp.exp(ib - m_t)
    mo_ref[...] = m_t
    gs_ref[0] = inter
    gs_ref[1] = p
    gs_ref[2] = jnp.exp(-m_t)
    q_t = proj[:, C_Q:C_Q + MWP].T
    ks_t = (proj[:, C_K:C_K + MWP] * (1.0 / math.sqrt(HD))).T
    v_t = proj[:, C_V:C_V + MWP].T
    for h in range(NH):
        hs = slice(h * HP, (h + 1) * HP)
        qt_ref[h] = q_t[hs]
        kst_ref[h] = ks_t[hs]
        vt_ref[h] = v_t[hs]
        wvt_ref[h] = p[h:h + 1] * v_t[hs]
        dec_ref[h] = inter[h:h + 1]


def _sample_in(xs, mods, l, wts, m0_t):
    n = xs.shape[0]
    lsel3 = lambda i: (l, 0, 0)
    head_t = jax.ShapeDtypeStruct((NH, HP, n), F32)
    out_shapes = [
        jax.ShapeDtypeStruct((n, N_IN), F32),
        head_t, head_t, head_t, head_t,
        jax.ShapeDtypeStruct((3, NH, n), F32),
        jax.ShapeDtypeStruct((NH, 1, n), F32),
        jax.ShapeDtypeStruct((NH, n), F32),
    ]
    return pl.pallas_call(
        _sample_in_kernel,
        grid=(1,),
        in_specs=[
            pl.BlockSpec((n, D_MODEL), lambda i: (0, 0)),
            pl.BlockSpec((None, n, 2 * D_MODEL), lsel3),
            pl.BlockSpec((None, 1, D_MODEL), lsel3),
            pl.BlockSpec((None, D_MODEL, N_IN), lsel3),
            pl.BlockSpec((None, 1, HP), lsel3),
            pl.BlockSpec((None, NH, n), lsel3),
        ],
        out_specs=[pl.BlockSpec(s.shape, lambda i, nd=len(s.shape): (0,) * nd) for s in out_shapes],
        out_shape=out_shapes,
        compiler_params=pltpu.CompilerParams(
            dimension_semantics=("arbitrary",), vmem_limit_bytes=VMEM_LIMIT),
        name="sample_in_proj",
    )(xs, mods, wts["g1"], wts["w_in"], wts["gbias"], m0_t)


def _sample_state_kernel(c0_ref, qt_ref, kst_ref, wvt_ref, dec_ref, *rest, vb, n_fill):
    cnew_ref, cq_ref = rest[-2:]
    g = pl.program_id(0)

    @pl.when(g < n_fill)
    def _():
        cnew_ref[...] = jnp.zeros_like(cnew_ref)

    @pl.when(g == n_fill)
    def _():
        q = qt_ref[0:HD, :]
        ks = kst_ref[0:HD, :]
        dec = dec_ref[...]
        for vi in range(vb):
            c_old = c0_ref[vi]
            cq_ref[vi:vi + 1, :] = jnp.sum(c_old * q, axis=0, keepdims=True)
            cnew_ref[vi] = dec * c_old + wvt_ref[vi:vi + 1, :] * ks


def _sample_state(c0_t, c_out_prev, qt, kst, wvt, dec, l, vb):
    depth = c0_t.shape[0]
    n = c0_t.shape[-1]
    if c_out_prev is None:
        assert l == 0
        n_fill = depth - 1
    else:
        n_fill = 0
    real = lambda g: g // max(n_fill, 1) if n_fill else 1
    layer = lambda g: (g + 1) % depth if n_fill else l
    head_spec = pl.BlockSpec((None, HP, n), lambda g, h, i: (h, 0, 0))
    in_specs = [
        pl.BlockSpec((None, None, vb, HD, n), lambda g, h, i: (l, h * real(g), i * real(g), 0, 0)),
        head_spec, head_spec,
        pl.BlockSpec((None, vb, n), lambda g, h, i: (h, i, 0)),
        pl.BlockSpec((None, 1, n), lambda g, h, i: (h, 0, 0)),
    ]
    args = [c0_t, qt, kst, wvt, dec]
    aliases = {}
    if c_out_prev is not None:
        in_specs.append(pl.BlockSpec(memory_space=pl.ANY))
        args.append(c_out_prev)
        aliases = {len(args) - 1: 0}
    return pl.pallas_call(
        functools.partial(_sample_state_kernel, vb=vb, n_fill=n_fill),
        grid=(n_fill + 1, NH, HD // vb),
        in_specs=in_specs,
        out_specs=[
            pl.BlockSpec((None, None, vb, HD, n), lambda g, h, i: (layer(g), h, i, 0, 0)),
            pl.BlockSpec((None, vb, n), lambda g, h, i: (h * real(g), i * real(g), 0)),
        ],
        out_shape=[
            jax.ShapeDtypeStruct(c0_t.shape, F32),
            jax.ShapeDtypeStruct((NH, HD, n), F32),
        ],
        input_output_aliases=aliases,
        compiler_params=pltpu.CompilerParams(
            dimension_semantics=("arbitrary", "arbitrary", "arbitrary"), vmem_limit_bytes=VMEM_LIMIT),
        name="sample_matrix_memory",
    )(*args)


def _sample_mix_kernel(x_ref, gt_ref, proj_ref, qt_ref, kst_ref, vt_ref, cq_ref, gs_ref,
                       h0_ref, conv_ref, n0_ref, pool_ref,
                       convw_ref, convb_ref, wgate_ref, bgate_ref, lam_ref, ng_ref,
                       poolw_ref, pscale_ref, wout_ref,
                       xo_ref, ho_ref, convo_ref, no_ref, poolo_ref):
    proj = proj_ref[...]
    n_tok = proj.shape[0]

    xr = proj[:, C_XR:C_XR + LRU_W]
    gr = proj[:, C_GR:C_GR + LRU_W]
    cw = convw_ref[...]
    xc = (convb_ref[...] + cw[3:4] * xr + cw[2:3] * conv_ref[2]
          + cw[1:2] * conv_ref[1] + cw[0:1] * conv_ref[0])
    convo_ref[0] = conv_ref[1]
    convo_ref[1] = conv_ref[2]
    convo_ref[2] = xr
    gates = _dot(xc.astype(BF16), wgate_ref[...]) + bgate_ref[...]
    a, mult, gate_i = _lru_coeffs(gates, LRU_C * _log_sigmoid(lam_ref[...]))
    h_new = a * h0_ref[...] + mult * gate_i * xc
    ho_ref[...] = h_new
    y_a = h_new * _gelu_tanh(gr)

    normed = []
    for h in range(NH):
        q = qt_ref[h, 0:HD, :]
        ks = kst_ref[h, 0:HD, :]
        v = vt_ref[h, 0:HD, :]
        n0 = n0_ref[h]
        inter, p, floor = gs_ref[0, h:h + 1, :], gs_ref[1, h:h + 1, :], gs_ref[2, h:h + 1, :]
        s = jnp.sum(q * ks, axis=0, keepdims=True) * p
        num = s * v + inter * cq_ref[h]
        den = s + inter * jnp.sum(n0 * q, axis=0, keepdims=True)
        hout = num / jnp.maximum(jnp.abs(den), floor)
        no_ref[h] = inter * n0 + p * ks
        ms = jnp.sum(hout * hout, axis=0, keepdims=True) * (1.0 / HD)
        normed.append(hout * lax.rsqrt(ms + EPS))
        normed.append(jnp.zeros((HP - HD, n_tok), F32))
    hm = jnp.concatenate(normed, axis=0).T
    y_b = _sigmoid(proj[:, C_O:C_O + MWP]) * (hm * ng_ref[...])

    u = proj[:, C_U:C_U + PW]
    sums, acc, nxt = {}, u, 1
    for win in POOL_WINDOWS:
        while nxt < win:
            acc = acc + pool_ref[POOL_BUF - nxt]
            nxt += 1
        sums[win] = acc
    wsum = _pool_select(sums[2], sums[4], sums[8], sums[16])
    cnt = jnp.minimum(PAST_LEN + 1, _pool_window_lanes(u.shape)).astype(F32)
    diff = wsum / cnt - u
    y_c = _dot(diff.astype(BF16), poolw_ref[...]) * pscale_ref[...]
    for r in range(POOL_BUF - 1):
        poolo_ref[r] = pool_ref[r + 1]
    poolo_ref[POOL_BUF - 1] = u

    mix = jnp.concatenate([y_a, y_b, y_c], axis=1).astype(BF16)
    xo_ref[...] = x_ref[...] + gt_ref[...] * _dot(mix, wout_ref[...])


def _sample_mix(xs, mods, proj, qt, kst, vt, cq, gs, l, st, wts):
    n = xs.shape[0]
    zeros = lambda nd: (lambda i: (0,) * nd)
    full = lambda a: pl.BlockSpec(a.shape, zeros(a.ndim))
    lspec = lambda a: pl.BlockSpec((None,) + a.shape[1:], lambda i, nd=a.ndim: (l,) + (0,) * (nd - 1))
    snames = ("lru_h", "conv", "n", "pool")
    wnames = ("conv_w", "conv_b", "w_gate", "b_gate", "lam", "ng", "pool_w", "pool_scale", "w_out")
    out_shapes = [
        jax.ShapeDtypeStruct((n, D_MODEL), F32),
        jax.ShapeDtypeStruct((n, LRU_W), F32),
        jax.ShapeDtypeStruct((3, n, LRU_W), F32),
        jax.ShapeDtypeStruct((NH, HD, n), F32),
        jax.ShapeDtypeStruct((POOL_BUF, n, PW), F32),
    ]
    return pl.pallas_call(
        _sample_mix_kernel,
        grid=(1,),
        in_specs=[
            full(xs),
            pl.BlockSpec((None, n, D_MODEL), lambda i: (l, 0, 2)),
            full(proj), full(qt), full(kst), full(vt), full(cq), full(gs),
        ] + [lspec(st[k]) for k in snames] + [lspec(wts[k]) for k in wnames],
        out_specs=[pl.BlockSpec(s.shape, zeros(len(s.shape))) for s in out_shapes],
        out_shape=out_shapes,
        compiler_params=pltpu.CompilerParams(
            dimension_semantics=("arbitrary",), vmem_limit_bytes=VMEM_LIMIT),
        name="sample_mixer",
    )(xs, mods, proj, qt, kst, vt, cq, gs, *[st[k] for k in snames], *[wts[k] for k in wnames])


def _pad_heads(w):
    lead = w.shape[:-1]
    w = w.reshape(lead + (NH, HD))
    w = jnp.pad(w, [(0, 0)] * len(lead) + [(0, 0), (0, HP - HD)])
    return w.reshape(lead + (MWP,))


def _block_diag(w):
    depth, g, n, _ = w.shape
    eye = jnp.eye(g, dtype=w.dtype)
    return (w[:, :, :, None, :] * eye[None, :, None, :, None]).reshape(depth, g * n, g * n)


def _permute_kernel(wint_ref, wout_ref, winp_ref, woutp_ref):
    o0 = 2 * LRU_W
    g0 = o0 + 4 * MW

    def put(dst_col, parts):
        blocks = [wint_ref[r0:r0 + n, :] for r0, n in parts]
        n_rows = sum(n for _, n in parts)
        if n_rows < HP:
            blocks.append(jnp.zeros((HP - n_rows, D_MODEL), F32))
        blk = jnp.concatenate(blocks, axis=0) if len(blocks) > 1 else blocks[0]
        winp_ref[:, dst_col:dst_col + HP] = blk.T.astype(BF16)

    for b in range(o0 // HP):
        put(b * HP, [(b * HP, HP)])
    for k, dst in enumerate((C_Q, C_K, C_V, C_O)):
        for h in range(NH):
            parts = [(o0 + k * MW + h * HD, HD)]
            if k == 0 and h == 0:
                parts.append((g0, 2 * NH))
            put(dst + h * HP, parts)
    for b in range(PW // HP):
        put(C_U + b * HP, [(g0 + 2 * NH + b * HP, HP)])

    woutp_ref[0:LRU_W, :] = wout_ref[0:LRU_W, :].astype(BF16)
    for h in range(NH):
        r0 = LRU_W + h * HP
        woutp_ref[r0:r0 + HD, :] = wout_ref[LRU_W + h * HD:LRU_W + (h + 1) * HD, :].astype(BF16)
        woutp_ref[r0 + HD:r0 + HP, :] = jnp.zeros((HP - HD, D_MODEL), BF16)
    woutp_ref[LRU_W + MWP:K_OUT, :] = wout_ref[LRU_W + MW:D_MODEL, :].astype(BF16)


def _permute_projections(w_in, w_out):
    depth, _, in_cols = w_in.shape
    lsel3 = lambda l: (l, 0, 0)
    return pl.pallas_call(
        _permute_kernel,
        grid=(depth,),
        in_specs=[
            pl.BlockSpec((None, in_cols, D_MODEL), lsel3),
            pl.BlockSpec((None, D_MODEL, D_MODEL), lsel3),
        ],
        out_specs=[
            pl.BlockSpec((None, D_MODEL, N_IN), lsel3),
            pl.BlockSpec((None, K_OUT, D_MODEL), lsel3),
        ],
        out_shape=[
            jax.ShapeDtypeStruct((depth, D_MODEL, N_IN), BF16),
            jax.ShapeDtypeStruct((depth, K_OUT, D_MODEL), BF16),
        ],
        compiler_params=pltpu.CompilerParams(
            dimension_semantics=("arbitrary",), vmem_limit_bytes=VMEM_LIMIT),
        name="permute_projections",
    )(jnp.transpose(w_in, (0, 2, 1)), w_out)


def _prepare_weights(norm1_g, norm2_g, w_in, conv_w, conv_b, lru_wa, lru_ba, lru_wx, lru_bx, lru_lam,
                     mlstm_bi, mlstm_bf, mlstm_norm_g, pool_w, pool_scale, w_out):
    depth = w_in.shape[0]
    w_in_p, w_out_p = _permute_projections(w_in, w_out)
    row = lambda v: v.reshape(depth, 1, -1)
    return {
        "g1": row(norm1_g), "g2": row(norm2_g),
        "w_in": w_in_p, "w_out": w_out_p,
        "conv_w": conv_w, "conv_b": row(conv_b),
        "w_gate": jnp.concatenate([_block_diag(lru_wa), _block_diag(lru_wx)], axis=-1).astype(BF16),
        "b_gate": jnp.concatenate([row(lru_ba), row(lru_bx)], axis=-1),
        "lam": row(lru_lam),
        "gbias": jnp.pad(jnp.concatenate([mlstm_bi, mlstm_bf], axis=-1),
                         ((0, 0), (G_OFF, HP - G_OFF - 2 * NH)))[:, None, :],
        "ng": row(_pad_heads(mlstm_norm_g.reshape(depth, MW))),
        "pool_w": _block_diag(pool_w).astype(BF16),
        "pool_scale": row(pool_scale),
    }


def kernel(x_prompt, x_sample, state_lru_h, state_lru_conv, state_mlstm_C, state_mlstm_n, state_mlstm_m,
           state_pool, c_prompt, c_sample, norm1_g, norm2_g, w_ada, b_ada, w_in, conv_w, conv_b, lru_wa,
           lru_ba, lru_wx, lru_bx, lru_lam, mlstm_bi, mlstm_bf, mlstm_norm_g, pool_w, pool_scale, w_out,
           w_ff1, w_ff2, final_g):
    nb, seq, _ = x_prompt.shape
    ns = x_sample.shape[0]
    depth = w_in.shape[0]
    wts = _prepare_weights(norm1_g, norm2_g, w_in, conv_w, conv_b, lru_wa, lru_ba, lru_wx, lru_bx, lru_lam,
                           mlstm_bi, mlstm_bf, mlstm_norm_g, pool_w, pool_scale, w_out)
    w1_bf, w2_bf = w_ff1[0].astype(BF16), w_ff2[0].astype(BF16)
    final_row = final_g.reshape(1, D_MODEL)
    modp, mods = _modulation(jnp.concatenate([c_prompt, c_sample], axis=0), w_ada,
                             b_ada.reshape(depth, 1, -1), nb)

    sample_state = {
        "lru_h": state_lru_h,
        "conv": jnp.transpose(state_lru_conv, (0, 2, 1, 3)),
        "n": jnp.transpose(state_mlstm_n, (0, 2, 3, 1)),
        "pool": jnp.transpose(state_pool, (0, 2, 1, 3)),
    }
    m0_t = jnp.transpose(state_mlstm_m, (0, 2, 1))
    c0_t = jnp.transpose(state_mlstm_C, (0, 2, 3, 4, 1))
    c_new_t = None

    xp = x_prompt
    xs = x_sample.reshape(ns, D_MODEL)
    p_out = [[] for _ in range(6)]
    s_out = [[] for _ in range(5)]
    for l in range(depth):
        last = l == depth - 1
        xp, h_l, conv_l, c_l, n_l, m_l, pool_l = _prompt_mixer(xp, modp, l, wts, TM_MIX)
        for lst, val in zip(p_out, (h_l.reshape(nb, LRU_W), conv_l[:, 8 - 3:], c_l, n_l[:, :NH, :HD],
                                    m_l[:, :NH, 0], pool_l[:, 16 - POOL_BUF:])):
            lst.append(val)
        proj, qt, kst, vt, wvt, gs, dec, m_s = _sample_in(xs, mods, l, wts, m0_t)
        c_new_t, cq = _sample_state(c0_t, c_new_t, qt, kst, wvt, dec, l, VB_STATE)
        xs, h_s, conv_s, n_s, pool_s = _sample_mix(xs, mods, proj, qt, kst, vt, cq, gs, l, sample_state, wts)
        for lst, val in zip(s_out, (h_s, conv_s, n_s, m_s, pool_s)):
            lst.append(val)
        res = _ffn(xp.reshape(nb * seq, D_MODEL), xs, modp, mods, l, wts["g2"], w1_bf, w2_bf,
                   None if last else (w_ff1, w_ff2), final_row, TM_FFN, seq, last)
        xp, xs = res[0].reshape(nb, seq, D_MODEL), res[1]
        if not last:
            w1_bf, w2_bf = res[2], res[3]

    stack = lambda lst: jnp.stack(lst, axis=0)
    h_s, conv_s, n_s, m_s, pool_s = [stack(v) for v in s_out]
    return (xp, xs.reshape(ns, 1, D_MODEL),
            *[stack(v) for v in p_out],
            h_s,
            jnp.transpose(conv_s, (0, 2, 1, 3)),
            jnp.transpose(c_new_t, (0, 4, 1, 2, 3)),
            jnp.transpose(n_s, (0, 3, 1, 2)),
            jnp.transpose(m_s, (0, 2, 1)),
            jnp.transpose(pool_s, (0, 2, 1, 3)))
```

```python
import functools
import math

import jax
import jax.numpy as jnp
from jax import lax
from jax.experimental import pallas as pl
from jax.experimental.pallas import tpu as pltpu

F32 = jnp.float32
BF16 = jnp.bfloat16

LANES = 128
SUBLANES = 8
MXU_TILE = 256
VMEM_BYTES = 64 * 1024 * 1024

D_MODEL = 1024
LRU_W = 384
LRU_C = 8.0
NH = 4
HD = 96
HP = LANES
MW = NH * HD
MWP = NH * HP
CHUNK = 128
PW = 256
POOL_WINDOWS = (2, 4, 8, 16)
POOL_GROUP = PW // len(POOL_WINDOWS)
POOL_BUF = 15
CONV_TAPS = 4
CONV_HIST = SUBLANES
POOL_HIST = 2 * SUBLANES
D_FF = 4096
EPS = 1e-6
PAST_LEN = 16384

C_XR, C_GR, C_Q, C_K, C_V, C_O, C_U = 0, 384, 768, 1280, 1792, 2304, 2816
N_IN = 3072
G_OFF = HD
K_OUT = LRU_W + MWP + PW

VMEM_LIMIT = VMEM_BYTES - 8 * 1024 * 1024
TM_MIX = 512
PROJ_PIECE = MXU_TILE
TM_FFN = 1024
FFN_CHUNK = 1024
MOD_TN = 1536
VB_STATE = HD


def _log_sigmoid(x):
    return jnp.minimum(x, 0.0) - jnp.log1p(jnp.exp(-jnp.abs(x)))


def _sigmoid(x):
    return 1.0 / (1.0 + jnp.exp(-x))


def _gelu_tanh(x):
    c = math.sqrt(2.0 / math.pi)
    return 0.5 * x * (1.0 + jnp.tanh(c * (x + 0.044715 * (x * x * x))))


def _rms(x, g):
    ms = jnp.mean(x * x, axis=-1, keepdims=True)
    return x * lax.rsqrt(ms + EPS) * g


def _ada_norm(x, g, mod):
    ms = jnp.mean(x * x, axis=-1, keepdims=True)
    gain = g * (1.0 + mod[:, D_MODEL:2 * D_MODEL])
    return x * lax.rsqrt(ms + EPS) * gain + mod[:, 0:D_MODEL]


def _dot(a, b):
    return jnp.dot(a, b, preferred_element_type=F32)


def _dot_nt(a, b):
    return lax.dot_general(a, b, (((1,), (1,)), ((), ())), preferred_element_type=F32)


def _lru_coeffs(gates, log_lam):
    r = _sigmoid(gates[:, :LRU_W])
    i = _sigmoid(gates[:, LRU_W:])
    log_a = r * log_lam
    a = jnp.exp(log_a)
    th = jnp.tanh(log_a)
    mult = jnp.sqrt(-2.0 * th / (1.0 - th))
    return a, mult, i


def _scan_rows(a, b, h_prev):
    n, w = a.shape
    groups = n // SUBLANES
    a3 = a.reshape(groups, SUBLANES, w)
    b3 = b.reshape(groups, SUBLANES, w)
    row = lax.broadcasted_iota(jnp.int32, a3.shape, 1)
    s = 1
    while s < SUBLANES:
        keep = row >= s
        b3 = a3 * jnp.where(keep, pltpu.roll(b3, s, 1), 0.0) + b3
        a3 = a3 * jnp.where(keep, pltpu.roll(a3, s, 1), 1.0)
        s *= 2
    carry = h_prev
    outs = []
    last = SUBLANES - 1
    for g in range(groups):
        outs.append(a3[g] * carry + b3[g])
        carry = a3[g, last:last + 1] * carry + b3[g, last:last + 1]
    return jnp.concatenate(outs, axis=0), carry


def _head_norm_gate(hout, o, ng):
    ms = jnp.sum(hout * hout, axis=-1, keepdims=True) * (1.0 / HD)
    return _sigmoid(o) * (hout * lax.rsqrt(ms + EPS) * ng)


def _pool_by_group(shape, values):
    lane = lax.broadcasted_iota(jnp.int32, shape, 1)
    out = values[-1]
    for g in range(len(values) - 2, -1, -1):
        out = jnp.where(lane < (g + 1) * POOL_GROUP, values[g], out)
    return out


def _pool_select(sums):
    vals = [sums[w] for w in POOL_WINDOWS]
    return _pool_by_group(vals[0].shape, vals)


def _pool_window_lanes(shape):
    return _pool_by_group(shape, list(POOL_WINDOWS))


def _mod_kernel(c_ref, w_ref, b_ref, modp_ref, mods_ref, *, n_prompt):
    c = c_ref[...]
    act = (c * _sigmoid(c)).astype(BF16)
    out = _dot(act, w_ref[...].astype(BF16)) + b_ref[...]
    for r in range(n_prompt):
        modp_ref[r] = out[r:r + 1]
    mods_ref[...] = out[n_prompt:]


def _modulation(c_all, w_ada, b_ada, n_prompt):
    depth = w_ada.shape[0]
    n_all = c_all.shape[0]
    n_mod = w_ada.shape[2]
    tn = MOD_TN
    return pl.pallas_call(
        functools.partial(_mod_kernel, n_prompt=n_prompt),
        grid=(depth, n_mod // tn),
        in_specs=[
            pl.BlockSpec((n_all, D_MODEL), lambda l, j: (0, 0)),
            pl.BlockSpec((None, D_MODEL, tn), lambda l, j: (l, 0, j)),
            pl.BlockSpec((None, 1, tn), lambda l, j: (l, 0, j)),
        ],
        out_specs=[
            pl.BlockSpec((None, n_prompt, 1, tn), lambda l, j: (l, 0, 0, j)),
            pl.BlockSpec((None, n_all - n_prompt, tn), lambda l, j: (l, 0, j)),
        ],
        out_shape=[
            jax.ShapeDtypeStruct((depth, n_prompt, 1, n_mod), F32),
            jax.ShapeDtypeStruct((depth, n_all - n_prompt, n_mod), F32),
        ],
        compiler_params=pltpu.CompilerParams(
            dimension_semantics=("arbitrary", "arbitrary"), vmem_limit_bytes=VMEM_LIMIT),
        name="adaln_modulation",
    )(c_all, w_ada, b_ada)


def _mlstm_tile(proj, gl, gl_t, state, n_chunks, pump):
    pairs = [(c, h) for c in range(n_chunks) for h in range(NH)]
    t_idx = lax.broadcasted_iota(jnp.int32, (CHUNK, CHUNK), 0)
    s_idx = lax.broadcasted_iota(jnp.int32, (CHUNK, CHUNK), 1)
    causal = s_idx <= t_idx
    upper = t_idx <= s_idx
    rows = lambda c: slice(c * CHUNK, (c + 1) * CHUNK)
    head = lambda base, h: slice(base + h * HP, base + (h + 1) * HP)

    bcum_col, dmat, rowmax, ib_col = {}, {}, {}, {}
    for c, h in pairs:
        lf_row = gl_t[NH + h:NH + h + 1, rows(c)]
        lf_col = gl[rows(c), NH + h:NH + h + 1]
        ib_col[c, h] = gl[rows(c), h:h + 1]
        bcum_col[c, h] = jnp.sum(jnp.where(causal, lf_row, 0.0), axis=1, keepdims=True)
        bcum_row = jnp.sum(jnp.where(upper, lf_col, 0.0), axis=0, keepdims=True)
        dmat[c, h] = jnp.where(causal, bcum_col[c, h] - bcum_row + gl_t[h:h + 1, rows(c)], -jnp.inf)
        rowmax[c, h] = jnp.max(dmat[c, h], axis=1, keepdims=True)
    pump(1)

    m_start, m_t = {}, {}
    m_run = [state[h][2] for h in range(NH)]
    for c, h in pairs:
        m_start[c, h] = m_run[h]
        m_t[c, h] = jnp.maximum(bcum_col[c, h] + m_run[h], rowmax[c, h])
        m_run[h] = m_t[c, h][CHUNK - 1:CHUNK, :]
    inter, pmat, floor, decay, w_col = {}, {}, {}, {}, {}
    for c, h in pairs:
        b_last = bcum_col[c, h][CHUNK - 1:CHUNK, :]
        inter[c, h] = jnp.exp(bcum_col[c, h] + m_start[c, h] - m_t[c, h])
        pmat[c, h] = jnp.exp(dmat[c, h] - m_t[c, h])
        floor[c, h] = jnp.exp(-m_t[c, h])
        m_new = m_t[c, h][CHUNK - 1:CHUNK, :]
        decay[c, h] = jnp.exp(b_last + m_start[c, h] - m_new)
        w_col[c, h] = jnp.exp(b_last - bcum_col[c, h] + ib_col[c, h] - m_new)
    pump(1)

    qb, kb, s_mat, sv, wvk, wk = {}, {}, {}, {}, {}, {}
    for c, h in pairs:
        qb[c, h] = proj[rows(c), head(C_Q, h)].astype(BF16)
        ks = proj[rows(c), head(C_K, h)] * (1.0 / math.sqrt(HD))
        kb[c, h] = ks.astype(BF16)
        wk[c, h] = jnp.sum(w_col[c, h] * ks, axis=0, keepdims=True)
        s_mat[c, h] = _dot_nt(qb[c, h], kb[c, h]) * pmat[c, h]
    pump(1)
    for c, h in pairs:
        v = proj[rows(c), head(C_V, h)]
        sv[c, h] = _dot(s_mat[c, h].astype(BF16), v.astype(BF16))
        wvk[c, h] = _dot((w_col[c, h] * v).T.astype(BF16), kb[c, h])
    pump(1)

    outs = [[] for _ in range(NH)]
    c_run = [state[h][0] for h in range(NH)]
    n_run = [state[h][1] for h in range(NH)]
    for c, h in pairs:
        q = proj[rows(c), head(C_Q, h)]
        qc = _dot_nt(qb[c, h], c_run[h].astype(BF16))
        qn = jnp.sum(q * n_run[h], axis=1, keepdims=True)
        num = sv[c, h] + inter[c, h] * qc
        den = jnp.sum(s_mat[c, h], axis=1, keepdims=True) + inter[c, h] * qn
        outs[h].append(num / jnp.maximum(jnp.abs(den), floor[c, h]))
        c_run[h] = decay[c, h] * c_run[h] + wvk[c, h]
        n_run[h] = decay[c, h] * n_run[h] + wk[c, h]
    return outs, [(c_run[h], n_run[h], m_run[h]) for h in range(NH)]


def _project(x, mod, g1, w_in):
    return _dot(_ada_norm(x, g1, mod).astype(BF16), w_in)


def _mix_tile(proj, x, gt1, j, tm, convw_ref, convb_ref, wgate_ref, bgate_ref,
              lam_ref, gbias_ref, ng_ref, poolw_ref, pscale_ref, wout_ref,
              xo_ref, hlast_ref, convo_ref, co_ref, no_ref, mo_ref, poolo_ref,
              h_sc, xbuf, ubuf, c_sc, n_sc, m_sc, pump=lambda n: None):
    row = lax.broadcasted_iota(jnp.int32, (tm, 1), 0)

    xr = proj[:, C_XR:C_XR + LRU_W]
    gr = proj[:, C_GR:C_GR + LRU_W]
    xbuf[CONV_HIST:CONV_HIST + tm, :] = xr
    cw = convw_ref[...]
    xc = convb_ref[...] + cw[CONV_TAPS - 1:CONV_TAPS] * xr
    for back in range(1, CONV_TAPS):
        xc = xc + cw[CONV_TAPS - 1 - back:CONV_TAPS - back] * xbuf[CONV_HIST - back:CONV_HIST - back + tm, :]
    tail = xbuf[tm:tm + CONV_HIST, :]
    xbuf[0:CONV_HIST, :] = tail
    convo_ref[0] = tail
    gates = _dot(xc.astype(BF16), wgate_ref[...]) + bgate_ref[...]
    pump(0)
    log_lam = LRU_C * _log_sigmoid(lam_ref[...])
    h_run = h_sc[...]
    y_a_parts = []
    n_split = 2
    for r in range(n_split):
        rs = slice(r * tm // n_split, (r + 1) * tm // n_split)
        a, mult, gate_i = _lru_coeffs(gates[rs], log_lam)
        if r == 0:
            mult = jnp.where(jnp.logical_and(row[rs] == 0, j == 0), 1.0, mult)
        pump(2)
        hseq, h_run = _scan_rows(a, mult * gate_i * xc[rs], h_run)
        pump(1)
        y_a_parts.append(hseq * _gelu_tanh(gr[rs]))
    h_sc[...] = h_run
    hlast_ref[0] = h_run
    y_a = jnp.concatenate(y_a_parts, axis=0)
    out_acc = _dot(y_a.astype(BF16), wout_ref[0:LRU_W, :])

    g_rows = (proj[:, C_Q:C_Q + HP] + gbias_ref[...]).T[G_OFF:G_OFF + 2 * NH]
    row8 = lax.broadcasted_iota(jnp.int32, g_rows.shape, 0)
    gl_t = jnp.where(row8 >= NH, _log_sigmoid(g_rows), g_rows)
    gl = jnp.concatenate([gl_t, jnp.zeros((HP - 2 * NH, tm), F32)], axis=0).T
    ng = ng_ref[...]
    state = [(c_sc[h], n_sc[h:h + 1, :], m_sc[h:h + 1, 0:1]) for h in range(NH)]
    outs, state = _mlstm_tile(proj, gl, gl_t, state, tm // CHUNK, pump)
    y_b_heads = []
    for h in range(NH):
        c_h, n_h, m_h = state[h]
        c_sc[h] = c_h
        n_sc[h:h + 1, :] = n_h
        m_sc[h:h + 1, :] = jnp.broadcast_to(m_h, (1, HP))
        co_ref[0, h] = c_h[0:HD, 0:HD]
        hout_all = jnp.concatenate(outs[h], axis=0) if len(outs[h]) > 1 else outs[h][0]
        o = proj[:, C_O + h * HP:C_O + (h + 1) * HP]
        y_b_heads.append(_head_norm_gate(hout_all, o, ng[:, h * HP:(h + 1) * HP]))
        pump(h % 2)
    no_ref[0] = n_sc[...]
    mo_ref[0] = m_sc[...]
    y_b = jnp.concatenate(y_b_heads, axis=1).astype(BF16)
    out_acc = out_acc + _dot(y_b, wout_ref[LRU_W:LRU_W + MWP, :])

    u = proj[:, C_U:C_U + PW]
    ubuf[POOL_HIST:POOL_HIST + tm, :] = u
    sums, acc, win = {}, ubuf[...], 1
    while win < POOL_WINDOWS[-1]:
        acc = acc + pltpu.roll(acc, win, 0)
        win *= 2
        sums[win] = acc[POOL_HIST:]
    wsum = _pool_select(sums)
    pos = row + j * tm
    cnt = jnp.minimum(pos + 1, _pool_window_lanes((tm, PW))).astype(F32)
    diff = wsum / cnt - u
    y_c = _dot(diff.astype(BF16), poolw_ref[...]) * pscale_ref[...]
    ptail = ubuf[tm:tm + POOL_HIST, :]
    ubuf[0:POOL_HIST, :] = ptail
    poolo_ref[0] = ptail
    pump(N_IN)

    out_acc = out_acc + _dot(y_c.astype(BF16), wout_ref[LRU_W + MWP:K_OUT, :])
    xo_ref[0] = x + gt1 * out_acc


def _prompt_mixer_kernel(x_ref, xn_ref, mod_ref, modn_ref, g1_ref, win_ref, *rest, tm):
    n_scratch = 8
    weights_outs = rest[:-n_scratch]
    h_sc, xbuf, ubuf, c_sc, n_sc, m_sc, proj_a, proj_b = rest[-n_scratch:]
    j = pl.program_id(1)
    t = pl.program_id(0) * pl.num_programs(1) + j

    @pl.when(j == 0)
    def _():
        h_sc[...] = jnp.zeros_like(h_sc)
        xbuf[0:CONV_HIST, :] = jnp.zeros((CONV_HIST, LRU_W), F32)
        ubuf[0:POOL_HIST, :] = jnp.zeros((POOL_HIST, PW), F32)
        c_sc[...] = jnp.zeros_like(c_sc)
        n_sc[...] = jnp.zeros_like(n_sc)
        m_sc[...] = jnp.zeros_like(m_sc)

    @pl.when(t == 0)
    def _():
        proj_a[...] = _project(x_ref[0], mod_ref[...], g1_ref[...], win_ref[...])

    def step(proj_cur, proj_nxt):
        done = [0]
        lhs = []

        def pump(n):
            if not lhs:
                lhs.append(_ada_norm(xn_ref[0], g1_ref[...], modn_ref[...]).astype(BF16))
            for _ in range(n):
                c0 = done[0]
                if c0 >= N_IN:
                    return
                c1 = min(c0 + PROJ_PIECE, N_IN)
                proj_nxt[:, c0:c1] = _dot(lhs[0], win_ref[:, c0:c1])
                done[0] = c1

        gt1 = mod_ref[:, 2 * D_MODEL:3 * D_MODEL]
        _mix_tile(proj_cur, x_ref[0], gt1, j, tm, *weights_outs, h_sc, xbuf, ubuf, c_sc, n_sc, m_sc,
                  pump=pump)

    @pl.when(t % 2 == 0)
    def _():
        step(proj_a, proj_b)

    @pl.when(t % 2 == 1)
    def _():
        step(proj_b, proj_a)


def _prompt_mixer(x, modp, l, wts, tm):
    nb, seq, _ = x.shape
    n_t = seq // tm

    def next_tile(b, j):
        t1 = jnp.minimum(b * n_t + j + 1, nb * n_t - 1)
        return t1 // n_t, t1 % n_t

    lsel3 = lambda b, j: (l, 0, 0)
    wspec = lambda a: pl.BlockSpec((None,) + a.shape[1:], lsel3)
    names = ("g1", "w_in", "conv_w", "conv_b", "w_gate", "b_gate", "lam", "gbias", "ng", "pool_w",
             "pool_scale", "w_out")
    out_shapes = [
        jax.ShapeDtypeStruct((nb, seq, D_MODEL), F32),
        jax.ShapeDtypeStruct((nb, 1, LRU_W), F32),
        jax.ShapeDtypeStruct((nb, CONV_HIST, LRU_W), F32),
        jax.ShapeDtypeStruct((nb, NH, HD, HD), F32),
        jax.ShapeDtypeStruct((nb, SUBLANES, HP), F32),
        jax.ShapeDtypeStruct((nb, SUBLANES, HP), F32),
        jax.ShapeDtypeStruct((nb, POOL_HIST, PW), F32),
    ]
    out_specs = [
        pl.BlockSpec((1, tm, D_MODEL), lambda b, j: (b, j, 0)),
        pl.BlockSpec((1, 1, LRU_W), lambda b, j: (b, 0, 0)),
        pl.BlockSpec((1, CONV_HIST, LRU_W), lambda b, j: (b, 0, 0)),
        pl.BlockSpec((1, NH, HD, HD), lambda b, j: (b, 0, 0, 0)),
        pl.BlockSpec((1, SUBLANES, HP), lambda b, j: (b, 0, 0)),
        pl.BlockSpec((1, SUBLANES, HP), lambda b, j: (b, 0, 0)),
        pl.BlockSpec((1, POOL_HIST, PW), lambda b, j: (b, 0, 0)),
    ]
    return pl.pallas_call(
        functools.partial(_prompt_mixer_kernel, tm=tm),
        grid=(nb, seq // tm),
        in_specs=[
            pl.BlockSpec((1, tm, D_MODEL), lambda b, j: (b, j, 0)),
            pl.BlockSpec((1, tm, D_MODEL), lambda b, j: next_tile(b, j) + (0,)),
            pl.BlockSpec((None, None, 1, 3 * D_MODEL), lambda b, j: (l, b, 0, 0)),
            pl.BlockSpec((None, None, 1, 3 * D_MODEL), lambda b, j: (l, next_tile(b, j)[0], 0, 0)),
        ] + [wspec(wts[n]) for n in names],
        out_specs=out_specs,
        out_shape=out_shapes,
        scratch_shapes=[
            pltpu.VMEM((1, LRU_W), F32),
            pltpu.VMEM((CONV_HIST + tm, LRU_W), F32),
            pltpu.VMEM((POOL_HIST + tm, PW), F32),
            pltpu.VMEM((NH, HP, HP), F32),
            pltpu.VMEM((SUBLANES, HP), F32),
            pltpu.VMEM((SUBLANES, HP), F32),
            pltpu.VMEM((tm, N_IN), F32),
            pltpu.VMEM((tm, N_IN), F32),
        ],
        compiler_params=pltpu.CompilerParams(
            dimension_semantics=("arbitrary", "arbitrary"), vmem_limit_bytes=VMEM_LIMIT),
        name="prompt_mixer",
    )(x, x, modp, modp, *[wts[n] for n in names])


def _ffn_kernel(xp_ref, xs_ref, modp_ref, mods_ref, g2_ref, w1_ref, w2_ref, gf_ref, *rest,
                n_prompt_steps, final):
    if len(rest) == 6:
        w1n_ref, w2n_ref, op_ref, os_ref, w1o_ref, w2o_ref = rest
    else:
        (op_ref, os_ref), w1n_ref = rest, None

    def ffn(x_ref, mod_ref, o_ref):
        x = x_ref[...]
        mod = mod_ref[...]
        gt2 = mod[:, 2 * D_MODEL:3 * D_MODEL]
        h2 = _ada_norm(x, g2_ref[...], mod).astype(BF16)
        acc = jnp.zeros(x.shape, F32)
        step = FFN_CHUNK
        for c in range(D_FF // step):
            hid = _dot(h2, w1_ref[:, c * step:(c + 1) * step])
            hid = jnp.square(jnp.maximum(hid, 0.0)).astype(BF16)
            acc = acc + _dot(hid, w2_ref[c * step:(c + 1) * step, :])
        y = x + gt2 * acc
        if final:
            y = _rms(y, gf_ref[...])
        o_ref[...] = y

    i = pl.program_id(0)

    @pl.when(i < n_prompt_steps)
    def _():
        ffn(xp_ref, modp_ref, op_ref)
        if w1n_ref is not None:
            w1o_ref[...] = w1n_ref[...].astype(BF16)
            w2o_ref[...] = w2n_ref[...].astype(BF16)

    @pl.when(i == n_prompt_steps)
    def _():
        ffn(xs_ref, mods_ref, os_ref)


def _ffn(xp2d, xs, modp, mods, l, g2, w1, w2, next_w, final_g, tm, seq, final):
    n_tok = xp2d.shape[0]
    ns = xs.shape[0]
    n_steps = n_tok // tm
    lsel3 = lambda i: (l, 0, 0)
    wspec = lambda a: pl.BlockSpec(a.shape, lambda i: (0, 0), pipeline_mode=pl.Buffered(1))
    tile = lambda i: jnp.minimum(i, n_steps - 1)
    in_specs = [
        pl.BlockSpec((tm, D_MODEL), lambda i: (tile(i), 0)),
        pl.BlockSpec((ns, D_MODEL), lambda i: (0, 0)),
        pl.BlockSpec((None, None, 1, 3 * D_MODEL), lambda i: (l, tile(i) // (seq // tm), 0, 1)),
        pl.BlockSpec((None, ns, 3 * D_MODEL), lambda i: (l, 0, 1)),
        pl.BlockSpec((None, 1, D_MODEL), lsel3),
        wspec(w1),
        wspec(w2),
        pl.BlockSpec((1, D_MODEL), lambda i: (0, 0)),
    ]
    out_specs = [
        pl.BlockSpec((tm, D_MODEL), lambda i: (tile(i), 0)),
        pl.BlockSpec((ns, D_MODEL), lambda i: (0, 0)),
    ]
    out_shape = [
        jax.ShapeDtypeStruct((n_tok, D_MODEL), F32),
        jax.ShapeDtypeStruct((ns, D_MODEL), F32),
    ]
    args = [xp2d, xs, modp, mods, g2, w1, w2, final_g]
    if next_w is not None:
        slab = D_FF // n_steps
        in_specs += [
            pl.BlockSpec((None, D_MODEL, slab), lambda i: (l + 1, 0, tile(i))),
            pl.BlockSpec((None, slab, D_MODEL), lambda i: (l + 1, tile(i), 0)),
        ]
        out_specs += [
            pl.BlockSpec((D_MODEL, slab), lambda i: (0, tile(i))),
            pl.BlockSpec((slab, D_MODEL), lambda i: (tile(i), 0)),
        ]
        out_shape += [
            jax.ShapeDtypeStruct((D_MODEL, D_FF), BF16),
            jax.ShapeDtypeStruct((D_FF, D_MODEL), BF16),
        ]
        args += list(next_w)
    return pl.pallas_call(
        functools.partial(_ffn_kernel, n_prompt_steps=n_steps, final=final),
        grid=(n_steps + 1,),
        in_specs=in_specs,
        out_specs=out_specs,
        out_shape=out_shape,
        compiler_params=pltpu.CompilerParams(
            dimension_semantics=("arbitrary",), vmem_limit_bytes=VMEM_LIMIT),
        name="ffn",
    )(*args)


def _sample_in_kernel(x_ref, mod_ref, g1_ref, win_ref, gbias_ref, m0_ref,
                      proj_ref, qt_ref, kst_ref, vt_ref, wvt_ref, gs_ref, dec_ref, mo_ref):
    proj = _project(x_ref[...], mod_ref[...], g1_ref[...], win_ref[...])
    proj_ref[...] = proj
    g_t = (proj[:, C_Q:C_Q + HP] + gbias_ref[...]).T
    ib = g_t[G_OFF:G_OFF + NH]
    g = _log_sigmoid(g_t[G_OFF + NH:G_OFF + 2 * NH]) + m0_ref[...]
    m_t = jnp.maximum(g, ib)
    inter = jnp.exp(g - m_t)
    p = jnp.exp(ib - m_t)
    mo_ref[...] = m_t
    gs_ref[0] = inter
    gs_ref[1] = p
    gs_ref[2] = jnp.exp(-m_t)
    q_t = proj[:, C_Q:C_Q + MWP].T
    ks_t = (proj[:, C_K:C_K + MWP] * (1.0 / math.sqrt(HD))).T
    v_t = proj[:, C_V:C_V + MWP].T
    for h in range(NH):
        hs = slice(h * HP, (h + 1) * HP)
        qt_ref[h] = q_t[hs]
        kst_ref[h] = ks_t[hs]
        vt_ref[h] = v_t[hs]
        wvt_ref[h] = p[h:h + 1] * v_t[hs]
        dec_ref[h] = inter[h:h + 1]


def _sample_in(xs, mods, l, wts, m0_t):
    n = xs.shape[0]
    lsel3 = lambda i: (l, 0, 0)
    head_t = jax.ShapeDtypeStruct((NH, HP, n), F32)
    out_shapes = [
        jax.ShapeDtypeStruct((n, N_IN), F32),
        head_t, head_t, head_t, head_t,
        jax.ShapeDtypeStruct((3, NH, n), F32),
        jax.ShapeDtypeStruct((NH, 1, n), F32),
        jax.ShapeDtypeStruct((NH, n), F32),
    ]
    return pl.pallas_call(
        _sample_in_kernel,
        grid=(1,),
        in_specs=[
            pl.BlockSpec((n, D_MODEL), lambda i: (0, 0)),
            pl.BlockSpec((None, n, 2 * D_MODEL), lsel3),
            pl.BlockSpec((None, 1, D_MODEL), lsel3),
            pl.BlockSpec((None, D_MODEL, N_IN), lsel3),
            pl.BlockSpec((None, 1, HP), lsel3),
            pl.BlockSpec((None, NH, n), lsel3),
        ],
        out_specs=[pl.BlockSpec(s.shape, lambda i, nd=len(s.shape): (0,) * nd) for s in out_shapes],
        out_shape=out_shapes,
        compiler_params=pltpu.CompilerParams(
            dimension_semantics=("arbitrary",), vmem_limit_bytes=VMEM_LIMIT),
        name="sample_in_proj",
    )(xs, mods, wts["g1"], wts["w_in"], wts["gbias"], m0_t)


def _sample_state_kernel(c0_ref, qt_ref, kst_ref, wvt_ref, dec_ref, *rest, vb, n_fill):
    cnew_ref, cq_ref = rest[-2:]
    g = pl.program_id(0)

    @pl.when(g < n_fill)
    def _():
        cnew_ref[...] = jnp.zeros_like(cnew_ref)

    @pl.when(g == n_fill)
    def _():
        q = qt_ref[0:HD, :]
        ks = kst_ref[0:HD, :]
        dec = dec_ref[...]
        for vi in range(vb):
            c_old = c0_ref[vi]
            cq_ref[vi:vi + 1, :] = jnp.sum(c_old * q, axis=0, keepdims=True)
            cnew_ref[vi] = dec * c_old + wvt_ref[vi:vi + 1, :] * ks


def _sample_state(c0_t, c_out_prev, qt, kst, wvt, dec, l, vb):
    depth = c0_t.shape[0]
    n = c0_t.shape[-1]
    if c_out_prev is None:
        assert l == 0
        n_fill = depth - 1
    else:
        n_fill = 0
    real = lambda g: g // max(n_fill, 1) if n_fill else 1
    layer = lambda g: (g + 1) % depth if n_fill else l
    head_spec = pl.BlockSpec((None, HP, n), lambda g, h, i: (h, 0, 0))
    in_specs = [
        pl.BlockSpec((None, None, vb, HD, n), lambda g, h, i: (l, h * real(g), i * real(g), 0, 0)),
        head_spec, head_spec,
        pl.BlockSpec((None, vb, n), lambda g, h, i: (h, i, 0)),
        pl.BlockSpec((None, 1, n), lambda g, h, i: (h, 0, 0)),
    ]
    args = [c0_t, qt, kst, wvt, dec]
    aliases = {}
    if c_out_prev is not None:
        in_specs.append(pl.BlockSpec(memory_space=pl.ANY))
        args.append(c_out_prev)
        aliases = {len(args) - 1: 0}
    return pl.pallas_call(
        functools.partial(_sample_state_kernel, vb=vb, n_fill=n_fill),
        grid=(n_fill + 1, NH, HD // vb),
        in_specs=in_specs,
        out_specs=[
            pl.BlockSpec((None, None, vb, HD, n), lambda g, h, i: (layer(g), h, i, 0, 0)),
            pl.BlockSpec((None, vb, n), lambda g, h, i: (h * real(g), i * real(g), 0)),
        ],
        out_shape=[
            jax.ShapeDtypeStruct(c0_t.shape, F32),
            jax.ShapeDtypeStruct((NH, HD, n), F32),
        ],
        input_output_aliases=aliases,
        compiler_params=pltpu.CompilerParams(
            dimension_semantics=("arbitrary", "arbitrary", "arbitrary"), vmem_limit_bytes=VMEM_LIMIT),
        name="sample_matrix_memory",
    )(*args)


def _sample_mix_kernel(x_ref, gt_ref, proj_ref, qt_ref, kst_ref, vt_ref, cq_ref, gs_ref,
                       h0_ref, conv_ref, n0_ref, pool_ref,
                       convw_ref, convb_ref, wgate_ref, bgate_ref, lam_ref, ng_ref,
                       poolw_ref, pscale_ref, wout_ref,
                       xo_ref, ho_ref, convo_ref, no_ref, poolo_ref):
    proj = proj_ref[...]
    n_tok = proj.shape[0]

    xr = proj[:, C_XR:C_XR + LRU_W]
    gr = proj[:, C_GR:C_GR + LRU_W]
    cw = convw_ref[...]
    xc = (convb_ref[...] + cw[3:4] * xr + cw[2:3] * conv_ref[2]
          + cw[1:2] * conv_ref[1] + cw[0:1] * conv_ref[0])
    convo_ref[0] = conv_ref[1]
    convo_ref[1] = conv_ref[2]
    convo_ref[2] = xr
    gates = _dot(xc.astype(BF16), wgate_ref[...]) + bgate_ref[...]
    a, mult, gate_i = _lru_coeffs(gates, LRU_C * _log_sigmoid(lam_ref[...]))
    h_new = a * h0_ref[...] + mult * gate_i * xc
    ho_ref[...] = h_new
    y_a = h_new * _gelu_tanh(gr)

    normed = []
    for h in range(NH):
        q = qt_ref[h, 0:HD, :]
        ks = kst_ref[h, 0:HD, :]
        v = vt_ref[h, 0:HD, :]
        n0 = n0_ref[h]
        inter, p, floor = gs_ref[0, h:h + 1, :], gs_ref[1, h:h + 1, :], gs_ref[2, h:h + 1, :]
        s = jnp.sum(q * ks, axis=0, keepdims=True) * p
        num = s * v + inter * cq_ref[h]
        den = s + inter * jnp.sum(n0 * q, axis=0, keepdims=True)
        hout = num / jnp.maximum(jnp.abs(den), floor)
        no_ref[h] = inter * n0 + p * ks
        ms = jnp.sum(hout * hout, axis=0, keepdims=True) * (1.0 / HD)
        normed.append(hout * lax.rsqrt(ms + EPS))
        normed.append(jnp.zeros((HP - HD, n_tok), F32))
    hm = jnp.concatenate(normed, axis=0).T
    y_b = _sigmoid(proj[:, C_O:C_O + MWP]) * (hm * ng_ref[...])

    u = proj[:, C_U:C_U + PW]
    sums, acc, nxt = {}, u, 1
    for win in POOL_WINDOWS:
        while nxt < win:
            acc = acc + pool_ref[POOL_BUF - nxt]
            nxt += 1
        sums[win] = acc
    wsum = _pool_select(sums)
    cnt = jnp.minimum(PAST_LEN + 1, _pool_window_lanes(u.shape)).astype(F32)
    diff = wsum / cnt - u
    y_c = _dot(diff.astype(BF16), poolw_ref[...]) * pscale_ref[...]
    for r in range(POOL_BUF - 1):
        poolo_ref[r] = pool_ref[r + 1]
    poolo_ref[POOL_BUF - 1] = u

    mix = jnp.concatenate([y_a, y_b, y_c], axis=1).astype(BF16)
    xo_ref[...] = x_ref[...] + gt_ref[...] * _dot(mix, wout_ref[...])


def _sample_mix(xs, mods, proj, qt, kst, vt, cq, gs, l, st, wts):
    n = xs.shape[0]
    zeros = lambda nd: (lambda i: (0,) * nd)
    full = lambda a: pl.BlockSpec(a.shape, zeros(a.ndim))
    lspec = lambda a: pl.BlockSpec((None,) + a.shape[1:], lambda i, nd=a.ndim: (l,) + (0,) * (nd - 1))
    snames = ("lru_h", "conv", "n", "pool")
    wnames = ("conv_w", "conv_b", "w_gate", "b_gate", "lam", "ng", "pool_w", "pool_scale", "w_out")
    out_shapes = [
        jax.ShapeDtypeStruct((n, D_MODEL), F32),
        jax.ShapeDtypeStruct((n, LRU_W), F32),
        jax.ShapeDtypeStruct((3, n, LRU_W), F32),
        jax.ShapeDtypeStruct((NH, HD, n), F32),
        jax.ShapeDtypeStruct((POOL_BUF, n, PW), F32),
    ]
    return pl.pallas_call(
        _sample_mix_kernel,
        grid=(1,),
        in_specs=[
            full(xs),
            pl.BlockSpec((None, n, D_MODEL), lambda i: (l, 0, 2)),
            full(proj), full(qt), full(kst), full(vt), full(cq), full(gs),
        ] + [lspec(st[k]) for k in snames] + [lspec(wts[k]) for k in wnames],
        out_specs=[pl.BlockSpec(s.shape, zeros(len(s.shape))) for s in out_shapes],
        out_shape=out_shapes,
        compiler_params=pltpu.CompilerParams(
            dimension_semantics=("arbitrary",), vmem_limit_bytes=VMEM_LIMIT),
        name="sample_mixer",
    )(xs, mods, proj, qt, kst, vt, cq, gs, *[st[k] for k in snames], *[wts[k] for k in wnames])


def _pad_heads(w):
    lead = w.shape[:-1]
    w = w.reshape(lead + (NH, HD))
    w = jnp.pad(w, [(0, 0)] * len(lead) + [(0, 0), (0, HP - HD)])
    return w.reshape(lead + (MWP,))


def _block_diag(w):
    depth, g, n, _ = w.shape
    eye = jnp.eye(g, dtype=w.dtype)
    return (w[:, :, :, None, :] * eye[None, :, None, :, None]).reshape(depth, g * n, g * n)


def _permute_kernel(wint_ref, wout_ref, winp_ref, woutp_ref):
    o0 = 2 * LRU_W
    g0 = o0 + 4 * MW

    def put(dst_col, parts):
        blocks = [wint_ref[r0:r0 + n, :] for r0, n in parts]
        n_rows = sum(n for _, n in parts)
        if n_rows < HP:
            blocks.append(jnp.zeros((HP - n_rows, D_MODEL), F32))
        blk = jnp.concatenate(blocks, axis=0) if len(blocks) > 1 else blocks[0]
        winp_ref[:, dst_col:dst_col + HP] = blk.T.astype(BF16)

    for b in range(o0 // HP):
        put(b * HP, [(b * HP, HP)])
    for k, dst in enumerate((C_Q, C_K, C_V, C_O)):
        for h in range(NH):
            parts = [(o0 + k * MW + h * HD, HD)]
            if k == 0 and h == 0:
                parts.append((g0, 2 * NH))
            put(dst + h * HP, parts)
    for b in range(PW // HP):
        put(C_U + b * HP, [(g0 + 2 * NH + b * HP, HP)])

    woutp_ref[0:LRU_W, :] = wout_ref[0:LRU_W, :].astype(BF16)
    for h in range(NH):
        r0 = LRU_W + h * HP
        woutp_ref[r0:r0 + HD, :] = wout_ref[LRU_W + h * HD:LRU_W + (h + 1) * HD, :].astype(BF16)
        woutp_ref[r0 + HD:r0 + HP, :] = jnp.zeros((HP - HD, D_MODEL), BF16)
    woutp_ref[LRU_W + MWP:K_OUT, :] = wout_ref[LRU_W + MW:D_MODEL, :].astype(BF16)


def _permute_projections(w_in, w_out):
    depth, _, in_cols = w_in.shape
    lsel3 = lambda l: (l, 0, 0)
    return pl.pallas_call(
        _permute_kernel,
        grid=(depth,),
        in_specs=[
            pl.BlockSpec((None, in_cols, D_MODEL), lsel3),
            pl.BlockSpec((None, D_MODEL, D_MODEL), lsel3),
        ],
        out_specs=[
            pl.BlockSpec((None, D_MODEL, N_IN), lsel3),
            pl.BlockSpec((None, K_OUT, D_MODEL), lsel3),
        ],
        out_shape=[
            jax.ShapeDtypeStruct((depth, D_MODEL, N_IN), BF16),
            jax.ShapeDtypeStruct((depth, K_OUT, D_MODEL), BF16),
        ],
        compiler_params=pltpu.CompilerParams(
            dimension_semantics=("arbitrary",), vmem_limit_bytes=VMEM_LIMIT),
        name="permute_projections",
    )(jnp.transpose(w_in, (0, 2, 1)), w_out)


def _prepare_weights(norm1_g, norm2_g, w_in, conv_w, conv_b, lru_wa, lru_ba, lru_wx, lru_bx, lru_lam,
                     mlstm_bi, mlstm_bf, mlstm_norm_g, pool_w, pool_scale, w_out):
    depth = w_in.shape[0]
    w_in_p, w_out_p = _permute_projections(w_in, w_out)
    row = lambda v: v.reshape(depth, 1, -1)
    return {
        "g1": row(norm1_g), "g2": row(norm2_g),
        "w_in": w_in_p, "w_out": w_out_p,
        "conv_w": conv_w, "conv_b": row(conv_b),
        "w_gate": jnp.concatenate([_block_diag(lru_wa), _block_diag(lru_wx)], axis=-1).astype(BF16),
        "b_gate": jnp.concatenate([row(lru_ba), row(lru_bx)], axis=-1),
        "lam": row(lru_lam),
        "gbias": jnp.pad(jnp.concatenate([mlstm_bi, mlstm_bf], axis=-1),
                         ((0, 0), (G_OFF, HP - G_OFF - 2 * NH)))[:, None, :],
        "ng": row(_pad_heads(mlstm_norm_g.reshape(depth, MW))),
        "pool_w": _block_diag(pool_w).astype(BF16),
        "pool_scale": row(pool_scale),
    }


def kernel(x_prompt, x_sample, state_lru_h, state_lru_conv, state_mlstm_C, state_mlstm_n, state_mlstm_m,
           state_pool, c_prompt, c_sample, norm1_g, norm2_g, w_ada, b_ada, w_in, conv_w, conv_b, lru_wa,
           lru_ba, lru_wx, lru_bx, lru_lam, mlstm_bi, mlstm_bf, mlstm_norm_g, pool_w, pool_scale, w_out,
           w_ff1, w_ff2, final_g):
    nb, seq, _ = x_prompt.shape
    ns = x_sample.shape[0]
    depth = w_in.shape[0]
    wts = _prepare_weights(norm1_g, norm2_g, w_in, conv_w, conv_b, lru_wa, lru_ba, lru_wx, lru_bx, lru_lam,
                           mlstm_bi, mlstm_bf, mlstm_norm_g, pool_w, pool_scale, w_out)
    w1_bf, w2_bf = w_ff1[0].astype(BF16), w_ff2[0].astype(BF16)
    final_row = final_g.reshape(1, D_MODEL)
    modp, mods = _modulation(jnp.concatenate([c_prompt, c_sample], axis=0), w_ada,
                             b_ada.reshape(depth, 1, -1), nb)

    sample_state = {
        "lru_h": state_lru_h,
        "conv": jnp.transpose(state_lru_conv, (0, 2, 1, 3)),
        "n": jnp.transpose(state_mlstm_n, (0, 2, 3, 1)),
        "pool": jnp.transpose(state_pool, (0, 2, 1, 3)),
    }
    m0_t = jnp.transpose(state_mlstm_m, (0, 2, 1))
    c0_t = jnp.transpose(state_mlstm_C, (0, 2, 3, 4, 1))
    c_new_t = None

    xp = x_prompt
    xs = x_sample.reshape(ns, D_MODEL)
    p_out = [[] for _ in range(6)]
    s_out = [[] for _ in range(5)]
    for l in range(depth):
        last = l == depth - 1
        xp, h_l, conv_l, c_l, n_l, m_l, pool_l = _prompt_mixer(xp, modp, l, wts, TM_MIX)
        for lst, val in zip(p_out, (h_l.reshape(nb, LRU_W), conv_l[:, CONV_HIST - (CONV_TAPS - 1):], c_l, n_l[:, :NH, :HD],
                                    m_l[:, :NH, 0], pool_l[:, POOL_HIST - POOL_BUF:])):
            lst.append(val)
        proj, qt, kst, vt, wvt, gs, dec, m_s = _sample_in(xs, mods, l, wts, m0_t)
        c_new_t, cq = _sample_state(c0_t, c_new_t, qt, kst, wvt, dec, l, VB_STATE)
        xs, h_s, conv_s, n_s, pool_s = _sample_mix(xs, mods, proj, qt, kst, vt, cq, gs, l, sample_state, wts)
        for lst, val in zip(s_out, (h_s, conv_s, n_s, m_s, pool_s)):
            lst.append(val)
        res = _ffn(xp.reshape(nb * seq, D_MODEL), xs, modp, mods, l, wts["g2"], w1_bf, w2_bf,
                   None if last else (w_ff1, w_ff2), final_row, TM_FFN, seq, last)
        xp, xs = res[0].reshape(nb, seq, D_MODEL), res[1]
        if not last:
            w1_bf, w2_bf = res[2], res[3]

    stack = lambda lst: jnp.stack(lst, axis=0)
    h_s, conv_s, n_s, m_s, pool_s = [stack(v) for v in s_out]
    return (xp, xs.reshape(ns, 1, D_MODEL),
            *[stack(v) for v in p_out],
            h_s,
            jnp.transpose(conv_s, (0, 2, 1, 3)),
            jnp.transpose(c_new_t, (0, 4, 1, 2, 3)),
            jnp.transpose(n_s, (0, 3, 1, 2)),
            jnp.transpose(m_s, (0, 2, 1)),
            jnp.transpose(pool_s, (0, 2, 1, 3)))
```

```python
import functools
import math

import jax
import jax.numpy as jnp
from jax import lax
from jax.experimental import pallas as pl
from jax.experimental.pallas import tpu as pltpu

F32 = jnp.float32
BF16 = jnp.bfloat16

LANES = 128
SUBLANES = 8
MXU_TILE = 256
VMEM_BYTES = 64 * 1024 * 1024

D_MODEL = 1024
LRU_W = 384
LRU_C = 8.0
NH = 4
HD = 96
HP = LANES
MW = NH * HD
MWP = NH * HP
CHUNK = 128
PW = 256
POOL_WINDOWS = (2, 4, 8, 16)
POOL_GROUP = PW // len(POOL_WINDOWS)
POOL_BUF = 15
CONV_TAPS = 4
CONV_HIST = SUBLANES
POOL_HIST = 2 * SUBLANES
D_FF = 4096
EPS = 1e-6
PAST_LEN = 16384

C_XR, C_GR, C_Q, C_K, C_V, C_O, C_U = 0, 384, 768, 1280, 1792, 2304, 2816
N_IN = 3072
G_OFF = HD
K_OUT = LRU_W + MWP + PW

VMEM_LIMIT = VMEM_BYTES - 4 * 1024 * 1024
TM_MIX = 512
PROJ_PIECE = MXU_TILE
TM_FFN = 1024
FFN_CHUNK = 1024
MOD_TN = 1536
VB_STATE = HD


def _log_sigmoid(x):
    return jnp.minimum(x, 0.0) - jnp.log1p(jnp.exp(-jnp.abs(x)))


def _sigmoid(x):
    return 1.0 / (1.0 + jnp.exp(-x))


def _gelu_tanh(x):
    c = math.sqrt(2.0 / math.pi)
    return 0.5 * x * (1.0 + jnp.tanh(c * (x + 0.044715 * (x * x * x))))


def _rms(x, g):
    ms = jnp.mean(x * x, axis=-1, keepdims=True)
    return x * lax.rsqrt(ms + EPS) * g


def _ada_norm(x, g, mod):
    ms = jnp.mean(x * x, axis=-1, keepdims=True)
    gain = g * (1.0 + mod[:, D_MODEL:2 * D_MODEL])
    return x * lax.rsqrt(ms + EPS) * gain + mod[:, 0:D_MODEL]


def _dot(a, b):
    return jnp.dot(a, b, preferred_element_type=F32)


def _dot_nt(a, b):
    return lax.dot_general(a, b, (((1,), (1,)), ((), ())), preferred_element_type=F32)


def _lru_coeffs(gates, log_lam):
    r = _sigmoid(gates[:, :LRU_W])
    i = _sigmoid(gates[:, LRU_W:])
    log_a = r * log_lam
    a = jnp.exp(log_a)
    th = jnp.tanh(log_a)
    mult = jnp.sqrt(-2.0 * th / (1.0 - th))
    return a, mult, i


def _scan_rows(a, b, h_prev):
    n, w = a.shape
    groups = n // SUBLANES
    a3 = a.reshape(groups, SUBLANES, w)
    b3 = b.reshape(groups, SUBLANES, w)
    row = lax.broadcasted_iota(jnp.int32, a3.shape, 1)
    s = 1
    while s < SUBLANES:
        keep = row >= s
        b3 = a3 * jnp.where(keep, pltpu.roll(b3, s, 1), 0.0) + b3
        a3 = a3 * jnp.where(keep, pltpu.roll(a3, s, 1), 1.0)
        s *= 2
    carry = h_prev
    outs = []
    last = SUBLANES - 1
    for g in range(groups):
        outs.append(a3[g] * carry + b3[g])
        carry = a3[g, last:last + 1] * carry + b3[g, last:last + 1]
    return jnp.concatenate(outs, axis=0), carry


def _head_norm_gate(hout, o, ng):
    ms = jnp.sum(hout * hout, axis=-1, keepdims=True) * (1.0 / HD)
    return _sigmoid(o) * (hout * lax.rsqrt(ms + EPS) * ng)


def _pool_by_group(shape, values):
    lane = lax.broadcasted_iota(jnp.int32, shape, 1)
    out = values[-1]
    for g in range(len(values) - 2, -1, -1):
        out = jnp.where(lane < (g + 1) * POOL_GROUP, values[g], out)
    return out


def _pool_select(sums):
    vals = [sums[w] for w in POOL_WINDOWS]
    return _pool_by_group(vals[0].shape, vals)


def _pool_window_lanes(shape):
    return _pool_by_group(shape, list(POOL_WINDOWS))


def _mod_kernel(c_ref, w_ref, b_ref, modp_ref, mods_ref, *, n_prompt):
    c = c_ref[...]
    act = (c * _sigmoid(c)).astype(BF16)
    out = _dot(act, w_ref[...].astype(BF16)) + b_ref[...]
    for r in range(n_prompt):
        modp_ref[r] = out[r:r + 1]
    mods_ref[...] = out[n_prompt:]


def _modulation(c_all, w_ada, b_ada, n_prompt):
    depth = w_ada.shape[0]
    n_all = c_all.shape[0]
    n_mod = w_ada.shape[2]
    tn = MOD_TN
    return pl.pallas_call(
        functools.partial(_mod_kernel, n_prompt=n_prompt),
        grid=(depth, n_mod // tn),
        in_specs=[
            pl.BlockSpec((n_all, D_MODEL), lambda l, j: (0, 0)),
            pl.BlockSpec((None, D_MODEL, tn), lambda l, j: (l, 0, j)),
            pl.BlockSpec((None, 1, tn), lambda l, j: (l, 0, j)),
        ],
        out_specs=[
            pl.BlockSpec((None, n_prompt, 1, tn), lambda l, j: (l, 0, 0, j)),
            pl.BlockSpec((None, n_all - n_prompt, tn), lambda l, j: (l, 0, j)),
        ],
        out_shape=[
            jax.ShapeDtypeStruct((depth, n_prompt, 1, n_mod), F32),
            jax.ShapeDtypeStruct((depth, n_all - n_prompt, n_mod), F32),
        ],
        compiler_params=pltpu.CompilerParams(
            dimension_semantics=("arbitrary", "arbitrary"), vmem_limit_bytes=VMEM_LIMIT),
        name="adaln_modulation",
    )(c_all, w_ada, b_ada)


def _mlstm_tile(proj, gl, gl_t, state, n_chunks, pump):
    pairs = [(c, h) for c in range(n_chunks) for h in range(NH)]
    t_idx = lax.broadcasted_iota(jnp.int32, (CHUNK, CHUNK), 0)
    s_idx = lax.broadcasted_iota(jnp.int32, (CHUNK, CHUNK), 1)
    causal = s_idx <= t_idx
    upper = t_idx <= s_idx
    rows = lambda c: slice(c * CHUNK, (c + 1) * CHUNK)
    head = lambda base, h: slice(base + h * HP, base + (h + 1) * HP)

    bcum_col, dmat, rowmax, ib_col = {}, {}, {}, {}
    for c, h in pairs:
        lf_row = gl_t[NH + h:NH + h + 1, rows(c)]
        lf_col = gl[rows(c), NH + h:NH + h + 1]
        ib_col[c, h] = gl[rows(c), h:h + 1]
        bcum_col[c, h] = jnp.sum(jnp.where(causal, lf_row, 0.0), axis=1, keepdims=True)
        bcum_row = jnp.sum(jnp.where(upper, lf_col, 0.0), axis=0, keepdims=True)
        dmat[c, h] = jnp.where(causal, bcum_col[c, h] - bcum_row + gl_t[h:h + 1, rows(c)], -jnp.inf)
        rowmax[c, h] = jnp.max(dmat[c, h], axis=1, keepdims=True)
    pump(1)

    m_start, m_t = {}, {}
    m_run = [state[h][2] for h in range(NH)]
    for c, h in pairs:
        m_start[c, h] = m_run[h]
        m_t[c, h] = jnp.maximum(bcum_col[c, h] + m_run[h], rowmax[c, h])
        m_run[h] = m_t[c, h][CHUNK - 1:CHUNK, :]
    inter, pmat, floor, decay, w_col = {}, {}, {}, {}, {}
    for c, h in pairs:
        b_last = bcum_col[c, h][CHUNK - 1:CHUNK, :]
        inter[c, h] = jnp.exp(bcum_col[c, h] + m_start[c, h] - m_t[c, h])
        pmat[c, h] = jnp.exp(dmat[c, h] - m_t[c, h])
        floor[c, h] = jnp.exp(-m_t[c, h])
        m_new = m_t[c, h][CHUNK - 1:CHUNK, :]
        decay[c, h] = jnp.exp(b_last + m_start[c, h] - m_new)
        w_col[c, h] = jnp.exp(b_last - bcum_col[c, h] + ib_col[c, h] - m_new)
    pump(1)

    qb, kb, s_mat, sv, wvk, wk = {}, {}, {}, {}, {}, {}
    for c, h in pairs:
        qb[c, h] = proj[rows(c), head(C_Q, h)].astype(BF16)
        ks = proj[rows(c), head(C_K, h)] * (1.0 / math.sqrt(HD))
        kb[c, h] = ks.astype(BF16)
        wk[c, h] = jnp.sum(w_col[c, h] * ks, axis=0, keepdims=True)
        s_mat[c, h] = _dot_nt(qb[c, h], kb[c, h]) * pmat[c, h]
    pump(1)
    for c, h in pairs:
        v = proj[rows(c), head(C_V, h)]
        sv[c, h] = _dot(s_mat[c, h].astype(BF16), v.astype(BF16))
        wvk[c, h] = _dot((w_col[c, h] * v).T.astype(BF16), kb[c, h])
    pump(1)

    outs = [[] for _ in range(NH)]
    c_run = [state[h][0] for h in range(NH)]
    n_run = [state[h][1] for h in range(NH)]
    for c, h in pairs:
        q = proj[rows(c), head(C_Q, h)]
        qc = _dot_nt(qb[c, h], c_run[h].astype(BF16))
        qn = jnp.sum(q * n_run[h], axis=1, keepdims=True)
        num = sv[c, h] + inter[c, h] * qc
        den = jnp.sum(s_mat[c, h], axis=1, keepdims=True) + inter[c, h] * qn
        outs[h].append(num / jnp.maximum(jnp.abs(den), floor[c, h]))
        c_run[h] = decay[c, h] * c_run[h] + wvk[c, h]
        n_run[h] = decay[c, h] * n_run[h] + wk[c, h]
    return outs, [(c_run[h], n_run[h], m_run[h]) for h in range(NH)]


def _project(x, mod, g1, w_in):
    return _dot(_ada_norm(x, g1, mod).astype(BF16), w_in)


def _mix_tile(proj, x, gt1, j, tm, convw_ref, convb_ref, wgate_ref, bgate_ref,
              lam_ref, gbias_ref, ng_ref, poolw_ref, pscale_ref, wout_ref,
              xo_ref, hlast_ref, convo_ref, co_ref, no_ref, mo_ref, poolo_ref,
              h_sc, xbuf, ubuf, c_sc, n_sc, m_sc, pump=lambda n: None):
    row = lax.broadcasted_iota(jnp.int32, (tm, 1), 0)

    xr = proj[:, C_XR:C_XR + LRU_W]
    gr = proj[:, C_GR:C_GR + LRU_W]
    xbuf[CONV_HIST:CONV_HIST + tm, :] = xr
    cw = convw_ref[...]
    xc = convb_ref[...] + cw[CONV_TAPS - 1:CONV_TAPS] * xr
    for back in range(1, CONV_TAPS):
        xc = xc + cw[CONV_TAPS - 1 - back:CONV_TAPS - back] * xbuf[CONV_HIST - back:CONV_HIST - back + tm, :]
    tail = xbuf[tm:tm + CONV_HIST, :]
    xbuf[0:CONV_HIST, :] = tail
    convo_ref[0] = tail
    gates = _dot(xc.astype(BF16), wgate_ref[...]) + bgate_ref[...]
    pump(0)
    log_lam = LRU_C * _log_sigmoid(lam_ref[...])
    h_run = h_sc[...]
    y_a_parts = []
    n_split = 2
    for r in range(n_split):
        rs = slice(r * tm // n_split, (r + 1) * tm // n_split)
        a, mult, gate_i = _lru_coeffs(gates[rs], log_lam)
        if r == 0:
            mult = jnp.where(jnp.logical_and(row[rs] == 0, j == 0), 1.0, mult)
        pump(2)
        hseq, h_run = _scan_rows(a, mult * gate_i * xc[rs], h_run)
        pump(1)
        y_a_parts.append(hseq * _gelu_tanh(gr[rs]))
    h_sc[...] = h_run
    hlast_ref[0] = h_run
    y_a = jnp.concatenate(y_a_parts, axis=0)
    out_acc = _dot(y_a.astype(BF16), wout_ref[0:LRU_W, :])

    g_rows = (proj[:, C_Q:C_Q + HP] + gbias_ref[...]).T[G_OFF:G_OFF + 2 * NH]
    row8 = lax.broadcasted_iota(jnp.int32, g_rows.shape, 0)
    gl_t = jnp.where(row8 >= NH, _log_sigmoid(g_rows), g_rows)
    gl = jnp.concatenate([gl_t, jnp.zeros((HP - 2 * NH, tm), F32)], axis=0).T
    ng = ng_ref[...]
    state = [(c_sc[h], n_sc[h:h + 1, :], m_sc[h:h + 1, 0:1]) for h in range(NH)]
    outs, state = _mlstm_tile(proj, gl, gl_t, state, tm // CHUNK, pump)
    y_b_heads = []
    for h in range(NH):
        c_h, n_h, m_h = state[h]
        c_sc[h] = c_h
        n_sc[h:h + 1, :] = n_h
        m_sc[h:h + 1, :] = jnp.broadcast_to(m_h, (1, HP))
        co_ref[0, h] = c_h[0:HD, 0:HD]
        hout_all = jnp.concatenate(outs[h], axis=0) if len(outs[h]) > 1 else outs[h][0]
        o = proj[:, C_O + h * HP:C_O + (h + 1) * HP]
        y_b_heads.append(_head_norm_gate(hout_all, o, ng[:, h * HP:(h + 1) * HP]))
        pump(h % 2)
    no_ref[0] = n_sc[...]
    mo_ref[0] = m_sc[...]
    y_b = jnp.concatenate(y_b_heads, axis=1).astype(BF16)
    out_acc = out_acc + _dot(y_b, wout_ref[LRU_W:LRU_W + MWP, :])

    u = proj[:, C_U:C_U + PW]
    ubuf[POOL_HIST:POOL_HIST + tm, :] = u
    sums, acc, win = {}, ubuf[...], 1
    while win < POOL_WINDOWS[-1]:
        acc = acc + pltpu.roll(acc, win, 0)
        win *= 2
        sums[win] = acc[POOL_HIST:]
    wsum = _pool_select(sums)
    pos = row + j * tm
    cnt = jnp.minimum(pos + 1, _pool_window_lanes((tm, PW))).astype(F32)
    diff = wsum / cnt - u
    y_c = _dot(diff.astype(BF16), poolw_ref[...]) * pscale_ref[...]
    ptail = ubuf[tm:tm + POOL_HIST, :]
    ubuf[0:POOL_HIST, :] = ptail
    poolo_ref[0] = ptail
    pump(N_IN)

    out_acc = out_acc + _dot(y_c.astype(BF16), wout_ref[LRU_W + MWP:K_OUT, :])
    x_new = x + gt1 * out_acc
    xo_ref[0] = x_new
    return x_new


def _prompt_mixer_kernel(x_ref, xn_ref, mod_ref, modn_ref, mod2_ref, g1_ref, g2_ref, win_ref, w1q_ref, w2q_ref,
                         *rest, tm):
    n_scratch = 10
    *weights_outs, fpart_ref = rest[:-n_scratch]
    h_sc, xbuf, ubuf, c_sc, n_sc, m_sc, proj_a, proj_b, h2_sc, hid_sc = rest[-n_scratch:]
    j = pl.program_id(1)
    t = pl.program_id(0) * pl.num_programs(1) + j

    @pl.when(j == 0)
    def _():
        h_sc[...] = jnp.zeros_like(h_sc)
        xbuf[0:CONV_HIST, :] = jnp.zeros((CONV_HIST, LRU_W), F32)
        ubuf[0:POOL_HIST, :] = jnp.zeros((POOL_HIST, PW), F32)
        c_sc[...] = jnp.zeros_like(c_sc)
        n_sc[...] = jnp.zeros_like(n_sc)
        m_sc[...] = jnp.zeros_like(m_sc)

    @pl.when(t == 0)
    def _():
        proj_a[...] = _project(x_ref[0], mod_ref[...], g1_ref[...], win_ref[...])
        h2_sc[...] = jnp.zeros_like(h2_sc)

    n_ffn_pieces = FFN_CHUNK // PROJ_PIECE

    def step(proj_cur, proj_nxt):
        done = [0]
        ffn_done = [0]
        lhs = []

        def ffn_piece():
            k = ffn_done[0]
            if k >= 2 * n_ffn_pieces:
                return
            c0 = (k % n_ffn_pieces) * PROJ_PIECE
            if k < n_ffn_pieces:
                hid = _dot(h2_sc[...], w1q_ref[:, c0:c0 + PROJ_PIECE])
                hid_sc[:, c0:c0 + PROJ_PIECE] = jnp.square(jnp.maximum(hid, 0.0)).astype(BF16)
            else:
                fpart_ref[:, c0:c0 + PROJ_PIECE] = _dot(hid_sc[...], w2q_ref[:, c0:c0 + PROJ_PIECE])
            ffn_done[0] = k + 1

        def pump(n):
            if not lhs:
                lhs.append(_ada_norm(xn_ref[0], g1_ref[...], modn_ref[...]).astype(BF16))
            for _ in range(n):
                c0 = done[0]
                if c0 >= N_IN:
                    break
                c1 = min(c0 + PROJ_PIECE, N_IN)
                proj_nxt[:, c0:c1] = _dot(lhs[0], win_ref[:, c0:c1])
                done[0] = c1
            if n > 0:
                ffn_piece()

        gt1 = mod_ref[:, 2 * D_MODEL:3 * D_MODEL]
        x_new = _mix_tile(proj_cur, x_ref[0], gt1, j, tm, *weights_outs, h_sc, xbuf, ubuf, c_sc, n_sc, m_sc,
                          pump=pump)
        h2_new = _ada_norm(x_new, g2_ref[...], mod2_ref[...]).astype(BF16)
        for _ in range(2 * n_ffn_pieces):
            ffn_piece()
        h2_sc[...] = h2_new

    @pl.when(t % 2 == 0)
    def _():
        step(proj_a, proj_b)

    @pl.when(t % 2 == 1)
    def _():
        step(proj_b, proj_a)


def _prompt_mixer(x, modp, l, wts, w1, w2, tm):
    nb, seq, _ = x.shape
    n_t = seq // tm

    def next_tile(b, j):
        t1 = jnp.minimum(b * n_t + j + 1, nb * n_t - 1)
        return t1 // n_t, t1 % n_t

    lsel3 = lambda b, j: (l, 0, 0)
    once = pl.Buffered(1)
    wspec = lambda a: pl.BlockSpec((None,) + a.shape[1:], lsel3, pipeline_mode=once)
    names = ("conv_w", "conv_b", "w_gate", "b_gate", "lam", "gbias", "ng", "pool_w", "pool_scale", "w_out")
    out_shapes = [
        jax.ShapeDtypeStruct((nb, seq, D_MODEL), F32),
        jax.ShapeDtypeStruct((nb, 1, LRU_W), F32),
        jax.ShapeDtypeStruct((nb, CONV_HIST, LRU_W), F32),
        jax.ShapeDtypeStruct((nb, NH, HD, HD), F32),
        jax.ShapeDtypeStruct((nb, SUBLANES, HP), F32),
        jax.ShapeDtypeStruct((nb, SUBLANES, HP), F32),
        jax.ShapeDtypeStruct((nb, POOL_HIST, PW), F32),
        jax.ShapeDtypeStruct(((nb * n_t - 1) * tm, D_MODEL), F32),
    ]
    out_specs = [
        pl.BlockSpec((1, tm, D_MODEL), lambda b, j: (b, j, 0)),
        pl.BlockSpec((1, 1, LRU_W), lambda b, j: (b, 0, 0)),
        pl.BlockSpec((1, CONV_HIST, LRU_W), lambda b, j: (b, 0, 0)),
        pl.BlockSpec((1, NH, HD, HD), lambda b, j: (b, 0, 0, 0)),
        pl.BlockSpec((1, SUBLANES, HP), lambda b, j: (b, 0, 0)),
        pl.BlockSpec((1, SUBLANES, HP), lambda b, j: (b, 0, 0)),
        pl.BlockSpec((1, POOL_HIST, PW), lambda b, j: (b, 0, 0)),
        pl.BlockSpec((tm, D_MODEL), lambda b, j: (jnp.maximum(b * n_t + j - 1, 0), 0)),
    ]
    return pl.pallas_call(
        functools.partial(_prompt_mixer_kernel, tm=tm),
        grid=(nb, seq // tm),
        in_specs=[
            pl.BlockSpec((1, tm, D_MODEL), lambda b, j: (b, j, 0)),
            pl.BlockSpec((1, tm, D_MODEL), lambda b, j: next_tile(b, j) + (0,)),
            pl.BlockSpec((None, None, 1, 3 * D_MODEL), lambda b, j: (l, b, 0, 0)),
            pl.BlockSpec((None, None, 1, 3 * D_MODEL), lambda b, j: (l, next_tile(b, j)[0], 0, 0)),
            pl.BlockSpec((None, None, 1, 3 * D_MODEL), lambda b, j: (l, b, 0, 1)),
            wspec(wts["g1"]), wspec(wts["g2"]), wspec(wts["w_in"]),
            pl.BlockSpec((D_MODEL, FFN_CHUNK), lambda b, j: (0, 0), pipeline_mode=once),
            pl.BlockSpec((FFN_CHUNK, D_MODEL), lambda b, j: (0, 0), pipeline_mode=once),
        ] + [wspec(wts[n]) for n in names],
        out_specs=out_specs,
        out_shape=out_shapes,
        scratch_shapes=[
            pltpu.VMEM((1, LRU_W), F32),
            pltpu.VMEM((CONV_HIST + tm, LRU_W), F32),
            pltpu.VMEM((POOL_HIST + tm, PW), F32),
            pltpu.VMEM((NH, HP, HP), F32),
            pltpu.VMEM((SUBLANES, HP), F32),
            pltpu.VMEM((SUBLANES, HP), F32),
            pltpu.VMEM((tm, N_IN), F32),
            pltpu.VMEM((tm, N_IN), F32),
            pltpu.VMEM((tm, D_MODEL), BF16),
            pltpu.VMEM((tm, FFN_CHUNK), BF16),
        ],
        compiler_params=pltpu.CompilerParams(
            dimension_semantics=("arbitrary", "arbitrary"), vmem_limit_bytes=VMEM_LIMIT),
        name="prompt_mixer",
    )(x, x, modp, modp, modp, wts["g1"], wts["g2"], wts["w_in"], w1, w2, *[wts[n] for n in names])


def _ffn_kernel(xp_ref, xs_ref, part_ref, modp_ref, mods_ref, g2_ref, w1_ref, w2_ref, gf_ref, *rest,
                n_prompt_steps, final):
    if len(rest) == 6:
        w1n_ref, w2n_ref, op_ref, os_ref, w1o_ref, w2o_ref = rest
    else:
        (op_ref, os_ref), w1n_ref = rest, None

    def ffn(x_ref, mod_ref, o_ref, first_chunk_ref=None):
        x = x_ref[...]
        mod = mod_ref[...]
        gt2 = mod[:, 2 * D_MODEL:3 * D_MODEL]
        h2 = _ada_norm(x, g2_ref[...], mod).astype(BF16)
        step = FFN_CHUNK
        if first_chunk_ref is None:
            acc, chunks = jnp.zeros(x.shape, F32), range(D_FF // step)
        else:
            acc, chunks = first_chunk_ref[...], range(1, D_FF // step)
        for c in chunks:
            hid = _dot(h2, w1_ref[:, c * step:(c + 1) * step])
            hid = jnp.square(jnp.maximum(hid, 0.0)).astype(BF16)
            acc = acc + _dot(hid, w2_ref[c * step:(c + 1) * step, :])
        y = x + gt2 * acc
        if final:
            y = _rms(y, gf_ref[...])
        o_ref[...] = y

    def cast_next_weights():
        if w1n_ref is not None:
            w1o_ref[...] = w1n_ref[...].astype(BF16)
            w2o_ref[...] = w2n_ref[...].astype(BF16)

    i = pl.program_id(0)

    @pl.when(i < n_prompt_steps - 1)
    def _():
        ffn(xp_ref, modp_ref, op_ref, part_ref)
        cast_next_weights()

    @pl.when(i == n_prompt_steps - 1)
    def _():
        ffn(xp_ref, modp_ref, op_ref)
        cast_next_weights()

    @pl.when(i == n_prompt_steps)
    def _():
        ffn(xs_ref, mods_ref, os_ref)


def _ffn(xp2d, xs, part, modp, mods, l, g2, w1, w2, next_w, final_g, tm, seq, final):
    n_tok = xp2d.shape[0]
    ns = xs.shape[0]
    n_steps = n_tok // tm
    assert part.shape[0] >= (n_steps - 1) * tm
    lsel3 = lambda i: (l, 0, 0)
    wspec = lambda a: pl.BlockSpec(a.shape, lambda i: (0, 0), pipeline_mode=pl.Buffered(1))
    tile = lambda i: jnp.minimum(i, n_steps - 1)
    in_specs = [
        pl.BlockSpec((tm, D_MODEL), lambda i: (tile(i), 0)),
        pl.BlockSpec((ns, D_MODEL), lambda i: (0, 0)),
        pl.BlockSpec((tm, D_MODEL), lambda i: (jnp.minimum(i, n_steps - 2), 0)),
        pl.BlockSpec((None, None, 1, 3 * D_MODEL), lambda i: (l, tile(i) // (seq // tm), 0, 1)),
        pl.BlockSpec((None, ns, 3 * D_MODEL), lambda i: (l, 0, 1)),
        pl.BlockSpec((None, 1, D_MODEL), lsel3),
        wspec(w1),
        wspec(w2),
        pl.BlockSpec((1, D_MODEL), lambda i: (0, 0)),
    ]
    out_specs = [
        pl.BlockSpec((tm, D_MODEL), lambda i: (tile(i), 0)),
        pl.BlockSpec((ns, D_MODEL), lambda i: (0, 0)),
    ]
    out_shape = [
        jax.ShapeDtypeStruct((n_tok, D_MODEL), F32),
        jax.ShapeDtypeStruct((ns, D_MODEL), F32),
    ]
    args = [xp2d, xs, part, modp, mods, g2, w1, w2, final_g]
    if next_w is not None:
        slab = D_FF // n_steps
        in_specs += [
            pl.BlockSpec((None, D_MODEL, slab), lambda i: (l + 1, 0, tile(i))),
            pl.BlockSpec((None, slab, D_MODEL), lambda i: (l + 1, tile(i), 0)),
        ]
        out_specs += [
            pl.BlockSpec((D_MODEL, slab), lambda i: (0, tile(i))),
            pl.BlockSpec((slab, D_MODEL), lambda i: (tile(i), 0)),
        ]
        out_shape += [
            jax.ShapeDtypeStruct((D_MODEL, D_FF), BF16),
            jax.ShapeDtypeStruct((D_FF, D_MODEL), BF16),
        ]
        args += list(next_w)
    return pl.pallas_call(
        functools.partial(_ffn_kernel, n_prompt_steps=n_steps, final=final),
        grid=(n_steps + 1,),
        in_specs=in_specs,
        out_specs=out_specs,
        out_shape=out_shape,
        compiler_params=pltpu.CompilerParams(
            dimension_semantics=("arbitrary",), vmem_limit_bytes=VMEM_LIMIT),
        name="ffn",
    )(*args)


def _sample_in_kernel(x_ref, mod_ref, g1_ref, win_ref, gbias_ref, m0_ref,
                      proj_ref, qt_ref, kst_ref, vt_ref, wvt_ref, gs_ref, dec_ref, mo_ref):
    proj = _project(x_ref[...], mod_ref[...], g1_ref[...], win_ref[...])
    proj_ref[...] = proj
    g_t = (proj[:, C_Q:C_Q + HP] + gbias_ref[...]).T
    ib = g_t[G_OFF:G_OFF + NH]
    g = _log_sigmoid(g_t[G_OFF + NH:G_OFF + 2 * NH]) + m0_ref[...]
    m_t = jnp.maximum(g, ib)
    inter = jnp.exp(g - m_t)
    p = jnp.exp(ib - m_t)
    mo_ref[...] = m_t
    gs_ref[0] = inter
    gs_ref[1] = p
    gs_ref[2] = jnp.exp(-m_t)
    q_t = proj[:, C_Q:C_Q + MWP].T
    ks_t = (proj[:, C_K:C_K + MWP] * (1.0 / math.sqrt(HD))).T
    v_t = proj[:, C_V:C_V + MWP].T
    for h in range(NH):
        hs = slice(h * HP, (h + 1) * HP)
        qt_ref[h] = q_t[hs]
        kst_ref[h] = ks_t[hs]
        vt_ref[h] = v_t[hs]
        wvt_ref[h] = p[h:h + 1] * v_t[hs]
        dec_ref[h] = inter[h:h + 1]


def _sample_in(xs, mods, l, wts, m0_t):
    n = xs.shape[0]
    lsel3 = lambda i: (l, 0, 0)
    head_t = jax.ShapeDtypeStruct((NH, HP, n), F32)
    out_shapes = [
        jax.ShapeDtypeStruct((n, N_IN), F32),
        head_t, head_t, head_t, head_t,
        jax.ShapeDtypeStruct((3, NH, n), F32),
        jax.ShapeDtypeStruct((NH, 1, n), F32),
        jax.ShapeDtypeStruct((NH, n), F32),
    ]
    return pl.pallas_call(
        _sample_in_kernel,
        grid=(1,),
        in_specs=[
            pl.BlockSpec((n, D_MODEL), lambda i: (0, 0)),
            pl.BlockSpec((None, n, 2 * D_MODEL), lsel3),
            pl.BlockSpec((None, 1, D_MODEL), lsel3),
            pl.BlockSpec((None, D_MODEL, N_IN), lsel3),
            pl.BlockSpec((None, 1, HP), lsel3),
            pl.BlockSpec((None, NH, n), lsel3),
        ],
        out_specs=[pl.BlockSpec(s.shape, lambda i, nd=len(s.shape): (0,) * nd) for s in out_shapes],
        out_shape=out_shapes,
        compiler_params=pltpu.CompilerParams(
            dimension_semantics=("arbitrary",), vmem_limit_bytes=VMEM_LIMIT),
        name="sample_in_proj",
    )(xs, mods, wts["g1"], wts["w_in"], wts["gbias"], m0_t)


def _sample_state_kernel(c0_ref, qt_ref, kst_ref, wvt_ref, dec_ref, *rest, vb, n_fill):
    cnew_ref, cq_ref = rest[-2:]
    g = pl.program_id(0)

    @pl.when(g < n_fill)
    def _():
        cnew_ref[...] = jnp.zeros_like(cnew_ref)

    @pl.when(g == n_fill)
    def _():
        q = qt_ref[0:HD, :]
        ks = kst_ref[0:HD, :]
        dec = dec_ref[...]
        for vi in range(vb):
            c_old = c0_ref[vi]
            cq_ref[vi:vi + 1, :] = jnp.sum(c_old * q, axis=0, keepdims=True)
            cnew_ref[vi] = dec * c_old + wvt_ref[vi:vi + 1, :] * ks


def _sample_state(c0_t, c_out_prev, qt, kst, wvt, dec, l, vb):
    depth = c0_t.shape[0]
    n = c0_t.shape[-1]
    if c_out_prev is None:
        assert l == 0
        n_fill = depth - 1
    else:
        n_fill = 0
    real = lambda g: g // max(n_fill, 1) if n_fill else 1
    layer = lambda g: (g + 1) % depth if n_fill else l
    head_spec = pl.BlockSpec((None, HP, n), lambda g, h, i: (h, 0, 0))
    in_specs = [
        pl.BlockSpec((None, None, vb, HD, n), lambda g, h, i: (l, h * real(g), i * real(g), 0, 0)),
        head_spec, head_spec,
        pl.BlockSpec((None, vb, n), lambda g, h, i: (h, i, 0)),
        pl.BlockSpec((None, 1, n), lambda g, h, i: (h, 0, 0)),
    ]
    args = [c0_t, qt, kst, wvt, dec]
    aliases = {}
    if c_out_prev is not None:
        in_specs.append(pl.BlockSpec(memory_space=pl.ANY))
        args.append(c_out_prev)
        aliases = {len(args) - 1: 0}
    return pl.pallas_call(
        functools.partial(_sample_state_kernel, vb=vb, n_fill=n_fill),
        grid=(n_fill + 1, NH, HD // vb),
        in_specs=in_specs,
        out_specs=[
            pl.BlockSpec((None, None, vb, HD, n), lambda g, h, i: (layer(g), h, i, 0, 0)),
            pl.BlockSpec((None, vb, n), lambda g, h, i: (h * real(g), i * real(g), 0)),
        ],
        out_shape=[
            jax.ShapeDtypeStruct(c0_t.shape, F32),
            jax.ShapeDtypeStruct((NH, HD, n), F32),
        ],
        input_output_aliases=aliases,
        compiler_params=pltpu.CompilerParams(
            dimension_semantics=("arbitrary", "arbitrary", "arbitrary"), vmem_limit_bytes=VMEM_LIMIT),
        name="sample_matrix_memory",
    )(*args)


def _sample_mix_kernel(x_ref, gt_ref, proj_ref, qt_ref, kst_ref, vt_ref, cq_ref, gs_ref,
                       h0_ref, conv_ref, n0_ref, pool_ref,
                       convw_ref, convb_ref, wgate_ref, bgate_ref, lam_ref, ng_ref,
                       poolw_ref, pscale_ref, wout_ref,
                       xo_ref, ho_ref, convo_ref, no_ref, poolo_ref):
    proj = proj_ref[...]
    n_tok = proj.shape[0]

    xr = proj[:, C_XR:C_XR + LRU_W]
    gr = proj[:, C_GR:C_GR + LRU_W]
    cw = convw_ref[...]
    xc = (convb_ref[...] + cw[3:4] * xr + cw[2:3] * conv_ref[2]
          + cw[1:2] * conv_ref[1] + cw[0:1] * conv_ref[0])
    convo_ref[0] = conv_ref[1]
    convo_ref[1] = conv_ref[2]
    convo_ref[2] = xr
    gates = _dot(xc.astype(BF16), wgate_ref[...]) + bgate_ref[...]
    a, mult, gate_i = _lru_coeffs(gates, LRU_C * _log_sigmoid(lam_ref[...]))
    h_new = a * h0_ref[...] + mult * gate_i * xc
    ho_ref[...] = h_new
    y_a = h_new * _gelu_tanh(gr)

    normed = []
    for h in range(NH):
        q = qt_ref[h, 0:HD, :]
        ks = kst_ref[h, 0:HD, :]
        v = vt_ref[h, 0:HD, :]
        n0 = n0_ref[h]
        inter, p, floor = gs_ref[0, h:h + 1, :], gs_ref[1, h:h + 1, :], gs_ref[2, h:h + 1, :]
        s = jnp.sum(q * ks, axis=0, keepdims=True) * p
        num = s * v + inter * cq_ref[h]
        den = s + inter * jnp.sum(n0 * q, axis=0, keepdims=True)
        hout = num / jnp.maximum(jnp.abs(den), floor)
        no_ref[h] = inter * n0 + p * ks
        ms = jnp.sum(hout * hout, axis=0, keepdims=True) * (1.0 / HD)
        normed.append(hout * lax.rsqrt(ms + EPS))
        normed.append(jnp.zeros((HP - HD, n_tok), F32))
    hm = jnp.concatenate(normed, axis=0).T
    y_b = _sigmoid(proj[:, C_O:C_O + MWP]) * (hm * ng_ref[...])

    u = proj[:, C_U:C_U + PW]
    sums, acc, nxt = {}, u, 1
    for win in POOL_WINDOWS:
        while nxt < win:
            acc = acc + pool_ref[POOL_BUF - nxt]
            nxt += 1
        sums[win] = acc
    wsum = _pool_select(sums)
    cnt = jnp.minimum(PAST_LEN + 1, _pool_window_lanes(u.shape)).astype(F32)
    diff = wsum / cnt - u
    y_c = _dot(diff.astype(BF16), poolw_ref[...]) * pscale_ref[...]
    for r in range(POOL_BUF - 1):
        poolo_ref[r] = pool_ref[r + 1]
    poolo_ref[POOL_BUF - 1] = u

    mix = jnp.concatenate([y_a, y_b, y_c], axis=1).astype(BF16)
    xo_ref[...] = x_ref[...] + gt_ref[...] * _dot(mix, wout_ref[...])


def _sample_mix(xs, mods, proj, qt, kst, vt, cq, gs, l, st, wts):
    n = xs.shape[0]
    zeros = lambda nd: (lambda i: (0,) * nd)
    full = lambda a: pl.BlockSpec(a.shape, zeros(a.ndim))
    lspec = lambda a: pl.BlockSpec((None,) + a.shape[1:], lambda i, nd=a.ndim: (l,) + (0,) * (nd - 1))
    snames = ("lru_h", "conv", "n", "pool")
    wnames = ("conv_w", "conv_b", "w_gate", "b_gate", "lam", "ng", "pool_w", "pool_scale", "w_out")
    out_shapes = [
        jax.ShapeDtypeStruct((n, D_MODEL), F32),
        jax.ShapeDtypeStruct((n, LRU_W), F32),
        jax.ShapeDtypeStruct((3, n, LRU_W), F32),
        jax.ShapeDtypeStruct((NH, HD, n), F32),
        jax.ShapeDtypeStruct((POOL_BUF, n, PW), F32),
    ]
    return pl.pallas_call(
        _sample_mix_kernel,
        grid=(1,),
        in_specs=[
            full(xs),
            pl.BlockSpec((None, n, D_MODEL), lambda i: (l, 0, 2)),
            full(proj), full(qt), full(kst), full(vt), full(cq), full(gs),
        ] + [lspec(st[k]) for k in snames] + [lspec(wts[k]) for k in wnames],
        out_specs=[pl.BlockSpec(s.shape, zeros(len(s.shape))) for s in out_shapes],
        out_shape=out_shapes,
        compiler_params=pltpu.CompilerParams(
            dimension_semantics=("arbitrary",), vmem_limit_bytes=VMEM_LIMIT),
        name="sample_mixer",
    )(xs, mods, proj, qt, kst, vt, cq, gs, *[st[k] for k in snames], *[wts[k] for k in wnames])


def _pad_heads(w):
    lead = w.shape[:-1]
    w = w.reshape(lead + (NH, HD))
    w = jnp.pad(w, [(0, 0)] * len(lead) + [(0, 0), (0, HP - HD)])
    return w.reshape(lead + (MWP,))


def _block_diag(w):
    depth, g, n, _ = w.shape
    eye = jnp.eye(g, dtype=w.dtype)
    return (w[:, :, :, None, :] * eye[None, :, None, :, None]).reshape(depth, g * n, g * n)


def _permute_kernel(wint_ref, wout_ref, winp_ref, woutp_ref):
    o0 = 2 * LRU_W
    g0 = o0 + 4 * MW

    def put(dst_col, parts):
        blocks = [wint_ref[r0:r0 + n, :] for r0, n in parts]
        n_rows = sum(n for _, n in parts)
        if n_rows < HP:
            blocks.append(jnp.zeros((HP - n_rows, D_MODEL), F32))
        blk = jnp.concatenate(blocks, axis=0) if len(blocks) > 1 else blocks[0]
        winp_ref[:, dst_col:dst_col + HP] = blk.T.astype(BF16)

    for b in range(o0 // HP):
        put(b * HP, [(b * HP, HP)])
    for k, dst in enumerate((C_Q, C_K, C_V, C_O)):
        for h in range(NH):
            parts = [(o0 + k * MW + h * HD, HD)]
            if k == 0 and h == 0:
                parts.append((g0, 2 * NH))
            put(dst + h * HP, parts)
    for b in range(PW // HP):
        put(C_U + b * HP, [(g0 + 2 * NH + b * HP, HP)])

    woutp_ref[0:LRU_W, :] = wout_ref[0:LRU_W, :].astype(BF16)
    for h in range(NH):
        r0 = LRU_W + h * HP
        woutp_ref[r0:r0 + HD, :] = wout_ref[LRU_W + h * HD:LRU_W + (h + 1) * HD, :].astype(BF16)
        woutp_ref[r0 + HD:r0 + HP, :] = jnp.zeros((HP - HD, D_MODEL), BF16)
    woutp_ref[LRU_W + MWP:K_OUT, :] = wout_ref[LRU_W + MW:D_MODEL, :].astype(BF16)


def _permute_projections(w_in, w_out):
    depth, _, in_cols = w_in.shape
    lsel3 = lambda l: (l, 0, 0)
    return pl.pallas_call(
        _permute_kernel,
        grid=(depth,),
        in_specs=[
            pl.BlockSpec((None, in_cols, D_MODEL), lsel3),
            pl.BlockSpec((None, D_MODEL, D_MODEL), lsel3),
        ],
        out_specs=[
            pl.BlockSpec((None, D_MODEL, N_IN), lsel3),
            pl.BlockSpec((None, K_OUT, D_MODEL), lsel3),
        ],
        out_shape=[
            jax.ShapeDtypeStruct((depth, D_MODEL, N_IN), BF16),
            jax.ShapeDtypeStruct((depth, K_OUT, D_MODEL), BF16),
        ],
        compiler_params=pltpu.CompilerParams(
            dimension_semantics=("arbitrary",), vmem_limit_bytes=VMEM_LIMIT),
        name="permute_projections",
    )(jnp.transpose(w_in, (0, 2, 1)), w_out)


def _prepare_weights(norm1_g, norm2_g, w_in, conv_w, conv_b, lru_wa, lru_ba, lru_wx, lru_bx, lru_lam,
                     mlstm_bi, mlstm_bf, mlstm_norm_g, pool_w, pool_scale, w_out):
    depth = w_in.shape[0]
    w_in_p, w_out_p = _permute_projections(w_in, w_out)
    row = lambda v: v.reshape(depth, 1, -1)
    return {
        "g1": row(norm1_g), "g2": row(norm2_g),
        "w_in": w_in_p, "w_out": w_out_p,
        "conv_w": conv_w, "conv_b": row(conv_b),
        "w_gate": jnp.concatenate([_block_diag(lru_wa), _block_diag(lru_wx)], axis=-1).astype(BF16),
        "b_gate": jnp.concatenate([row(lru_ba), row(lru_bx)], axis=-1),
        "lam": row(lru_lam),
        "gbias": jnp.pad(jnp.concatenate([mlstm_bi, mlstm_bf], axis=-1),
                         ((0, 0), (G_OFF, HP - G_OFF - 2 * NH)))[:, None, :],
        "ng": row(_pad_heads(mlstm_norm_g.reshape(depth, MW))),
        "pool_w": _block_diag(pool_w).astype(BF16),
        "pool_scale": row(pool_scale),
    }


def kernel(x_prompt, x_sample, state_lru_h, state_lru_conv, state_mlstm_C, state_mlstm_n, state_mlstm_m,
           state_pool, c_prompt, c_sample, norm1_g, norm2_g, w_ada, b_ada, w_in, conv_w, conv_b, lru_wa,
           lru_ba, lru_wx, lru_bx, lru_lam, mlstm_bi, mlstm_bf, mlstm_norm_g, pool_w, pool_scale, w_out,
           w_ff1, w_ff2, final_g):
    nb, seq, _ = x_prompt.shape
    ns = x_sample.shape[0]
    depth = w_in.shape[0]
    wts = _prepare_weights(norm1_g, norm2_g, w_in, conv_w, conv_b, lru_wa, lru_ba, lru_wx, lru_bx, lru_lam,
                           mlstm_bi, mlstm_bf, mlstm_norm_g, pool_w, pool_scale, w_out)
    w1_bf, w2_bf = w_ff1[0].astype(BF16), w_ff2[0].astype(BF16)
    final_row = final_g.reshape(1, D_MODEL)
    modp, mods = _modulation(jnp.concatenate([c_prompt, c_sample], axis=0), w_ada,
                             b_ada.reshape(depth, 1, -1), nb)

    sample_state = {
        "lru_h": state_lru_h,
        "conv": jnp.transpose(state_lru_conv, (0, 2, 1, 3)),
        "n": jnp.transpose(state_mlstm_n, (0, 2, 3, 1)),
        "pool": jnp.transpose(state_pool, (0, 2, 1, 3)),
    }
    m0_t = jnp.transpose(state_mlstm_m, (0, 2, 1))
    c0_t = jnp.transpose(state_mlstm_C, (0, 2, 3, 4, 1))
    c_new_t = None

    xp = x_prompt
    xs = x_sample.reshape(ns, D_MODEL)
    p_out = [[] for _ in range(6)]
    s_out = [[] for _ in range(5)]
    for l in range(depth):
        last = l == depth - 1
        xp, h_l, conv_l, c_l, n_l, m_l, pool_l, ffn_part = _prompt_mixer(xp, modp, l, wts, w1_bf, w2_bf, TM_MIX)
        for lst, val in zip(p_out, (h_l.reshape(nb, LRU_W), conv_l[:, CONV_HIST - (CONV_TAPS - 1):], c_l, n_l[:, :NH, :HD],
                                    m_l[:, :NH, 0], pool_l[:, POOL_HIST - POOL_BUF:])):
            lst.append(val)
        proj, qt, kst, vt, wvt, gs, dec, m_s = _sample_in(xs, mods, l, wts, m0_t)
        c_new_t, cq = _sample_state(c0_t, c_new_t, qt, kst, wvt, dec, l, VB_STATE)
        xs, h_s, conv_s, n_s, pool_s = _sample_mix(xs, mods, proj, qt, kst, vt, cq, gs, l, sample_state, wts)
        for lst, val in zip(s_out, (h_s, conv_s, n_s, m_s, pool_s)):
            lst.append(val)
        res = _ffn(xp.reshape(nb * seq, D_MODEL), xs, ffn_part, modp, mods, l, wts["g2"], w1_bf, w2_bf,
                   None if last else (w_ff1, w_ff2), final_row, TM_FFN, seq, last)
        xp, xs = res[0].reshape(nb, seq, D_MODEL), res[1]
        if not last:
            w1_bf, w2_bf = res[2], res[3]

    stack = lambda lst: jnp.stack(lst, axis=0)
    h_s, conv_s, n_s, m_s, pool_s = [stack(v) for v in s_out]
    return (xp, xs.reshape(ns, 1, D_MODEL),
            *[stack(v) for v in p_out],
            h_s,
            jnp.transpose(conv_s, (0, 2, 1, 3)),
            jnp.transpose(c_new_t, (0, 4, 1, 2, 3)),
            jnp.transpose(n_s, (0, 3, 1, 2)),
            jnp.transpose(m_s, (0, 2, 1)),
            jnp.transpose(pool_s, (0, 2, 1, 3)))
```

```python
import functools
import math

import jax
import jax.numpy as jnp
from jax import lax
from jax.experimental import pallas as pl
from jax.experimental.pallas import tpu as pltpu

F32 = jnp.float32
BF16 = jnp.bfloat16

LANES = 128
SUBLANES = 8
MXU_TILE = 256
VMEM_BYTES = 64 * 1024 * 1024

D_MODEL = 1024
LRU_W = 384
LRU_C = 8.0
NH = 4
HD = 96
HP = LANES
MW = NH * HD
MWP = NH * HP
CHUNK = 128
PW = 256
POOL_WINDOWS = (2, 4, 8, 16)
POOL_GROUP = PW // len(POOL_WINDOWS)
POOL_BUF = 15
CONV_TAPS = 4
CONV_HIST = SUBLANES
POOL_HIST = 2 * SUBLANES
D_FF = 4096
EPS = 1e-6
PAST_LEN = 16384

C_XR, C_GR, C_Q, C_K, C_V, C_O, C_U = 0, 384, 768, 1280, 1792, 2304, 2816
N_IN = 3072
G_OFF = HD
K_OUT = LRU_W + MWP + PW

VMEM_LIMIT = VMEM_BYTES - 8 * 1024 * 1024
TM_MIX = 512
PROJ_PIECE = MXU_TILE
TM_FFN = 1024
FFN_CHUNK = 1024
MOD_TN = 1536
VB_STATE = HD


def _log_sigmoid(x):
    return jnp.minimum(x, 0.0) - jnp.log1p(jnp.exp(-jnp.abs(x)))


def _sigmoid(x):
    return 1.0 / (1.0 + jnp.exp(-x))


def _gelu_tanh(x):
    c = math.sqrt(2.0 / math.pi)
    return 0.5 * x * (1.0 + jnp.tanh(c * (x + 0.044715 * (x * x * x))))


def _rms(x, g):
    ms = jnp.mean(x * x, axis=-1, keepdims=True)
    return x * lax.rsqrt(ms + EPS) * g


def _ada_norm(x, g, mod):
    ms = jnp.mean(x * x, axis=-1, keepdims=True)
    gain = g * (1.0 + mod[:, D_MODEL:2 * D_MODEL])
    return x * lax.rsqrt(ms + EPS) * gain + mod[:, 0:D_MODEL]


def _dot(a, b):
    return jnp.dot(a, b, preferred_element_type=F32)


def _dot_nt(a, b):
    return lax.dot_general(a, b, (((1,), (1,)), ((), ())), preferred_element_type=F32)


def _lru_coeffs(gates, log_lam):
    r = _sigmoid(gates[:, :LRU_W])
    i = _sigmoid(gates[:, LRU_W:])
    log_a = r * log_lam
    a = jnp.exp(log_a)
    th = jnp.tanh(log_a)
    mult = jnp.sqrt(-2.0 * th / (1.0 - th))
    return a, mult, i


def _scan_rows(a, b, h_prev):
    n, w = a.shape
    groups = n // SUBLANES
    a3 = a.reshape(groups, SUBLANES, w)
    b3 = b.reshape(groups, SUBLANES, w)
    row = lax.broadcasted_iota(jnp.int32, a3.shape, 1)
    s = 1
    while s < SUBLANES:
        keep = row >= s
        b3 = a3 * jnp.where(keep, pltpu.roll(b3, s, 1), 0.0) + b3
        a3 = a3 * jnp.where(keep, pltpu.roll(a3, s, 1), 1.0)
        s *= 2
    carry = h_prev
    outs = []
    last = SUBLANES - 1
    for g in range(groups):
        outs.append(a3[g] * carry + b3[g])
        carry = a3[g, last:last + 1] * carry + b3[g, last:last + 1]
    return jnp.concatenate(outs, axis=0), carry


def _head_norm_gate(hout, o, ng):
    ms = jnp.sum(hout * hout, axis=-1, keepdims=True) * (1.0 / HD)
    return _sigmoid(o) * (hout * lax.rsqrt(ms + EPS) * ng)


def _pool_by_group(shape, values):
    lane = lax.broadcasted_iota(jnp.int32, shape, 1)
    out = values[-1]
    for g in range(len(values) - 2, -1, -1):
        out = jnp.where(lane < (g + 1) * POOL_GROUP, values[g], out)
    return out


def _pool_select(sums):
    vals = [sums[w] for w in POOL_WINDOWS]
    return _pool_by_group(vals[0].shape, vals)


def _pool_window_lanes(shape):
    return _pool_by_group(shape, list(POOL_WINDOWS))


def _mod_kernel(c_ref, w_ref, b_ref, modp_ref, mods_ref, *, n_prompt):
    c = c_ref[...]
    act = (c * _sigmoid(c)).astype(BF16)
    out = _dot(act, w_ref[...].astype(BF16)) + b_ref[...]
    for r in range(n_prompt):
        modp_ref[r] = out[r:r + 1]
    mods_ref[...] = out[n_prompt:]


def _modulation(c_all, w_ada, b_ada, n_prompt):
    depth = w_ada.shape[0]
    n_all = c_all.shape[0]
    n_mod = w_ada.shape[2]
    tn = MOD_TN
    return pl.pallas_call(
        functools.partial(_mod_kernel, n_prompt=n_prompt),
        grid=(depth, n_mod // tn),
        in_specs=[
            pl.BlockSpec((n_all, D_MODEL), lambda l, j: (0, 0)),
            pl.BlockSpec((None, D_MODEL, tn), lambda l, j: (l, 0, j)),
            pl.BlockSpec((None, 1, tn), lambda l, j: (l, 0, j)),
        ],
        out_specs=[
            pl.BlockSpec((None, n_prompt, 1, tn), lambda l, j: (l, 0, 0, j)),
            pl.BlockSpec((None, n_all - n_prompt, tn), lambda l, j: (l, 0, j)),
        ],
        out_shape=[
            jax.ShapeDtypeStruct((depth, n_prompt, 1, n_mod), F32),
            jax.ShapeDtypeStruct((depth, n_all - n_prompt, n_mod), F32),
        ],
        compiler_params=pltpu.CompilerParams(
            dimension_semantics=("arbitrary", "arbitrary"), vmem_limit_bytes=VMEM_LIMIT),
        name="adaln_modulation",
    )(c_all, w_ada, b_ada)


def _mlstm_tile(proj, gl, gl_t, state, n_chunks, pump):
    pairs = [(c, h) for c in range(n_chunks) for h in range(NH)]
    t_idx = lax.broadcasted_iota(jnp.int32, (CHUNK, CHUNK), 0)
    s_idx = lax.broadcasted_iota(jnp.int32, (CHUNK, CHUNK), 1)
    causal = s_idx <= t_idx
    upper = t_idx <= s_idx
    rows = lambda c: slice(c * CHUNK, (c + 1) * CHUNK)
    head = lambda base, h: slice(base + h * HP, base + (h + 1) * HP)

    bcum_col, dmat, rowmax, ib_col = {}, {}, {}, {}
    for c, h in pairs:
        lf_row = gl_t[NH + h:NH + h + 1, rows(c)]
        lf_col = gl[rows(c), NH + h:NH + h + 1]
        ib_col[c, h] = gl[rows(c), h:h + 1]
        bcum_col[c, h] = jnp.sum(jnp.where(causal, lf_row, 0.0), axis=1, keepdims=True)
        bcum_row = jnp.sum(jnp.where(upper, lf_col, 0.0), axis=0, keepdims=True)
        dmat[c, h] = jnp.where(causal, bcum_col[c, h] - bcum_row + gl_t[h:h + 1, rows(c)], -jnp.inf)
        rowmax[c, h] = jnp.max(dmat[c, h], axis=1, keepdims=True)
    pump(1)

    m_start, m_t = {}, {}
    m_run = [state[h][2] for h in range(NH)]
    for c, h in pairs:
        m_start[c, h] = m_run[h]
        m_t[c, h] = jnp.maximum(bcum_col[c, h] + m_run[h], rowmax[c, h])
        m_run[h] = m_t[c, h][CHUNK - 1:CHUNK, :]
    inter, pmat, floor, decay, w_col = {}, {}, {}, {}, {}
    for c, h in pairs:
        b_last = bcum_col[c, h][CHUNK - 1:CHUNK, :]
        inter[c, h] = jnp.exp(bcum_col[c, h] + m_start[c, h] - m_t[c, h])
        pmat[c, h] = jnp.exp(dmat[c, h] - m_t[c, h])
        floor[c, h] = jnp.exp(-m_t[c, h])
        m_new = m_t[c, h][CHUNK - 1:CHUNK, :]
        decay[c, h] = jnp.exp(b_last + m_start[c, h] - m_new)
        w_col[c, h] = jnp.exp(b_last - bcum_col[c, h] + ib_col[c, h] - m_new)
    pump(1)

    qb, kb, s_mat, sv, wvk, wk = {}, {}, {}, {}, {}, {}
    for c, h in pairs:
        qb[c, h] = proj[rows(c), head(C_Q, h)].astype(BF16)
        ks = proj[rows(c), head(C_K, h)] * (1.0 / math.sqrt(HD))
        kb[c, h] = ks.astype(BF16)
        wk[c, h] = jnp.sum(w_col[c, h] * ks, axis=0, keepdims=True)
        s_mat[c, h] = _dot_nt(qb[c, h], kb[c, h]) * pmat[c, h]
    pump(1)
    for c, h in pairs:
        v = proj[rows(c), head(C_V, h)]
        sv[c, h] = _dot(s_mat[c, h].astype(BF16), v.astype(BF16))
        wvk[c, h] = _dot((w_col[c, h] * v).T.astype(BF16), kb[c, h])
    pump(1)

    outs = [[] for _ in range(NH)]
    c_run = [state[h][0] for h in range(NH)]
    n_run = [state[h][1] for h in range(NH)]
    for c, h in pairs:
        q = proj[rows(c), head(C_Q, h)]
        qc = _dot_nt(qb[c, h], c_run[h].astype(BF16))
        qn = jnp.sum(q * n_run[h], axis=1, keepdims=True)
        num = sv[c, h] + inter[c, h] * qc
        den = jnp.sum(s_mat[c, h], axis=1, keepdims=True) + inter[c, h] * qn
        outs[h].append(num / jnp.maximum(jnp.abs(den), floor[c, h]))
        c_run[h] = decay[c, h] * c_run[h] + wvk[c, h]
        n_run[h] = decay[c, h] * n_run[h] + wk[c, h]
    return outs, [(c_run[h], n_run[h], m_run[h]) for h in range(NH)]


def _project(x, mod, g1, w_in):
    return _dot(_ada_norm(x, g1, mod).astype(BF16), w_in)


def _mix_tile(proj, x, gt1, j, tm, convw_ref, convb_ref, wgate_ref, bgate_ref,
              lam_ref, gbias_ref, ng_ref, poolw_ref, pscale_ref, wout_ref,
              xo_ref, hlast_ref, convo_ref, co_ref, no_ref, mo_ref, poolo_ref,
              h_sc, xbuf, ubuf, c_sc, n_sc, m_sc, pump=lambda n: None):
    row = lax.broadcasted_iota(jnp.int32, (tm, 1), 0)

    xr = proj[:, C_XR:C_XR + LRU_W]
    gr = proj[:, C_GR:C_GR + LRU_W]
    xbuf[CONV_HIST:CONV_HIST + tm, :] = xr
    cw = convw_ref[...]
    xc = convb_ref[...] + cw[CONV_TAPS - 1:CONV_TAPS] * xr
    for back in range(1, CONV_TAPS):
        xc = xc + cw[CONV_TAPS - 1 - back:CONV_TAPS - back] * xbuf[CONV_HIST - back:CONV_HIST - back + tm, :]
    tail = xbuf[tm:tm + CONV_HIST, :]
    xbuf[0:CONV_HIST, :] = tail
    convo_ref[0] = tail
    gates = _dot(xc.astype(BF16), wgate_ref[...]) + bgate_ref[...]
    pump(0)
    log_lam = LRU_C * _log_sigmoid(lam_ref[...])
    h_run = h_sc[...]
    y_a_parts = []
    n_split = 2
    for r in range(n_split):
        rs = slice(r * tm // n_split, (r + 1) * tm // n_split)
        a, mult, gate_i = _lru_coeffs(gates[rs], log_lam)
        if r == 0:
            mult = jnp.where(jnp.logical_and(row[rs] == 0, j == 0), 1.0, mult)
        pump(2)
        hseq, h_run = _scan_rows(a, mult * gate_i * xc[rs], h_run)
        pump(1)
        y_a_parts.append(hseq * _gelu_tanh(gr[rs]))
    h_sc[...] = h_run
    hlast_ref[0] = h_run
    y_a = jnp.concatenate(y_a_parts, axis=0)
    out_acc = _dot(y_a.astype(BF16), wout_ref[0:LRU_W, :])

    g_rows = (proj[:, C_Q:C_Q + HP] + gbias_ref[...]).T[G_OFF:G_OFF + 2 * NH]
    row8 = lax.broadcasted_iota(jnp.int32, g_rows.shape, 0)
    gl_t = jnp.where(row8 >= NH, _log_sigmoid(g_rows), g_rows)
    gl = jnp.concatenate([gl_t, jnp.zeros((HP - 2 * NH, tm), F32)], axis=0).T
    ng = ng_ref[...]
    state = [(c_sc[h], n_sc[h:h + 1, :], m_sc[h:h + 1, 0:1]) for h in range(NH)]
    outs, state = _mlstm_tile(proj, gl, gl_t, state, tm // CHUNK, pump)
    y_b_heads = []
    for h in range(NH):
        c_h, n_h, m_h = state[h]
        c_sc[h] = c_h
        n_sc[h:h + 1, :] = n_h
        m_sc[h:h + 1, :] = jnp.broadcast_to(m_h, (1, HP))
        co_ref[0, h] = c_h[0:HD, 0:HD]
        hout_all = jnp.concatenate(outs[h], axis=0) if len(outs[h]) > 1 else outs[h][0]
        o = proj[:, C_O + h * HP:C_O + (h + 1) * HP]
        y_b_heads.append(_head_norm_gate(hout_all, o, ng[:, h * HP:(h + 1) * HP]))
        pump(h % 2)
    no_ref[0] = n_sc[...]
    mo_ref[0] = m_sc[...]
    y_b = jnp.concatenate(y_b_heads, axis=1).astype(BF16)
    out_acc = out_acc + _dot(y_b, wout_ref[LRU_W:LRU_W + MWP, :])

    u = proj[:, C_U:C_U + PW]
    ubuf[POOL_HIST:POOL_HIST + tm, :] = u
    sums, acc, win = {}, ubuf[...], 1
    while win < POOL_WINDOWS[-1]:
        acc = acc + pltpu.roll(acc, win, 0)
        win *= 2
        sums[win] = acc[POOL_HIST:]
    wsum = _pool_select(sums)
    pos = row + j * tm
    cnt = jnp.minimum(pos + 1, _pool_window_lanes((tm, PW))).astype(F32)
    diff = wsum / cnt - u
    y_c = _dot(diff.astype(BF16), poolw_ref[...]) * pscale_ref[...]
    ptail = ubuf[tm:tm + POOL_HIST, :]
    ubuf[0:POOL_HIST, :] = ptail
    poolo_ref[0] = ptail
    pump(N_IN)

    out_acc = out_acc + _dot(y_c.astype(BF16), wout_ref[LRU_W + MWP:K_OUT, :])
    xo_ref[0] = x + gt1 * out_acc


def _prompt_mixer_kernel(x_ref, xn_ref, mod_ref, modn_ref, g1_ref, win_ref, *rest, tm):
    n_scratch = 8
    weights_outs = rest[:-n_scratch]
    h_sc, xbuf, ubuf, c_sc, n_sc, m_sc, proj_a, proj_b = rest[-n_scratch:]
    j = pl.program_id(1)
    t = pl.program_id(0) * pl.num_programs(1) + j

    @pl.when(j == 0)
    def _():
        h_sc[...] = jnp.zeros_like(h_sc)
        xbuf[0:CONV_HIST, :] = jnp.zeros((CONV_HIST, LRU_W), F32)
        ubuf[0:POOL_HIST, :] = jnp.zeros((POOL_HIST, PW), F32)
        c_sc[...] = jnp.zeros_like(c_sc)
        n_sc[...] = jnp.zeros_like(n_sc)
        m_sc[...] = jnp.zeros_like(m_sc)

    @pl.when(t == 0)
    def _():
        proj_a[...] = _project(x_ref[0], mod_ref[...], g1_ref[...], win_ref[...])

    def step(proj_cur, proj_nxt):
        done = [0]
        lhs = []

        def pump(n):
            if not lhs:
                lhs.append(_ada_norm(xn_ref[0], g1_ref[...], modn_ref[...]).astype(BF16))
            for _ in range(n):
                c0 = done[0]
                if c0 >= N_IN:
                    return
                c1 = min(c0 + PROJ_PIECE, N_IN)
                proj_nxt[:, c0:c1] = _dot(lhs[0], win_ref[:, c0:c1])
                done[0] = c1

        gt1 = mod_ref[:, 2 * D_MODEL:3 * D_MODEL]
        _mix_tile(proj_cur, x_ref[0], gt1, j, tm, *weights_outs, h_sc, xbuf, ubuf, c_sc, n_sc, m_sc,
                  pump=pump)

    @pl.when(t % 2 == 0)
    def _():
        step(proj_a, proj_b)

    @pl.when(t % 2 == 1)
    def _():
        step(proj_b, proj_a)


def _prompt_mixer(x, modp, l, wts, tm):
    nb, seq, _ = x.shape
    n_t = seq // tm

    def next_tile(b, j):
        t1 = jnp.minimum(b * n_t + j + 1, nb * n_t - 1)
        return t1 // n_t, t1 % n_t

    lsel3 = lambda b, j: (l, 0, 0)
    wspec = lambda a: pl.BlockSpec((None,) + a.shape[1:], lsel3, pipeline_mode=pl.Buffered(1))
    names = ("g1", "w_in", "conv_w", "conv_b", "w_gate", "b_gate", "lam", "gbias", "ng", "pool_w",
             "pool_scale", "w_out")
    out_shapes = [
        jax.ShapeDtypeStruct((nb, seq, D_MODEL), F32),
        jax.ShapeDtypeStruct((nb, 1, LRU_W), F32),
        jax.ShapeDtypeStruct((nb, CONV_HIST, LRU_W), F32),
        jax.ShapeDtypeStruct((nb, NH, HD, HD), F32),
        jax.ShapeDtypeStruct((nb, SUBLANES, HP), F32),
        jax.ShapeDtypeStruct((nb, SUBLANES, HP), F32),
        jax.ShapeDtypeStruct((nb, POOL_HIST, PW), F32),
    ]
    out_specs = [
        pl.BlockSpec((1, tm, D_MODEL), lambda b, j: (b, j, 0)),
        pl.BlockSpec((1, 1, LRU_W), lambda b, j: (b, 0, 0)),
        pl.BlockSpec((1, CONV_HIST, LRU_W), lambda b, j: (b, 0, 0)),
        pl.BlockSpec((1, NH, HD, HD), lambda b, j: (b, 0, 0, 0)),
        pl.BlockSpec((1, SUBLANES, HP), lambda b, j: (b, 0, 0)),
        pl.BlockSpec((1, SUBLANES, HP), lambda b, j: (b, 0, 0)),
        pl.BlockSpec((1, POOL_HIST, PW), lambda b, j: (b, 0, 0)),
    ]
    return pl.pallas_call(
        functools.partial(_prompt_mixer_kernel, tm=tm),
        grid=(nb, seq // tm),
        in_specs=[
            pl.BlockSpec((1, tm, D_MODEL), lambda b, j: (b, j, 0)),
            pl.BlockSpec((1, tm, D_MODEL), lambda b, j: next_tile(b, j) + (0,)),
            pl.BlockSpec((None, None, 1, 3 * D_MODEL), lambda b, j: (l, b, 0, 0)),
            pl.BlockSpec((None, None, 1, 3 * D_MODEL), lambda b, j: (l, next_tile(b, j)[0], 0, 0)),
        ] + [wspec(wts[n]) for n in names],
        out_specs=out_specs,
        out_shape=out_shapes,
        scratch_shapes=[
            pltpu.VMEM((1, LRU_W), F32),
            pltpu.VMEM((CONV_HIST + tm, LRU_W), F32),
            pltpu.VMEM((POOL_HIST + tm, PW), F32),
            pltpu.VMEM((NH, HP, HP), F32),
            pltpu.VMEM((SUBLANES, HP), F32),
            pltpu.VMEM((SUBLANES, HP), F32),
            pltpu.VMEM((tm, N_IN), F32),
            pltpu.VMEM((tm, N_IN), F32),
        ],
        compiler_params=pltpu.CompilerParams(
            dimension_semantics=("arbitrary", "arbitrary"), vmem_limit_bytes=VMEM_LIMIT),
        name="prompt_mixer",
    )(x, x, modp, modp, *[wts[n] for n in names])


def _ffn_kernel(xp_ref, xs_ref, modp_ref, mods_ref, g2_ref, w1_ref, w2_ref, gf_ref, *rest,
                n_prompt_steps, final):
    if len(rest) == 6:
        w1n_ref, w2n_ref, op_ref, os_ref, w1o_ref, w2o_ref = rest
    else:
        (op_ref, os_ref), w1n_ref = rest, None

    def ffn(x_ref, mod_ref, o_ref):
        x = x_ref[...]
        mod = mod_ref[...]
        gt2 = mod[:, 2 * D_MODEL:3 * D_MODEL]
        h2 = _ada_norm(x, g2_ref[...], mod).astype(BF16)
        acc = jnp.zeros(x.shape, F32)
        step = FFN_CHUNK
        for c in range(D_FF // step):
            hid = _dot(h2, w1_ref[:, c * step:(c + 1) * step])
            hid = jnp.square(jnp.maximum(hid, 0.0)).astype(BF16)
            acc = acc + _dot(hid, w2_ref[c * step:(c + 1) * step, :])
        y = x + gt2 * acc
        if final:
            y = _rms(y, gf_ref[...])
        o_ref[...] = y

    i = pl.program_id(0)

    @pl.when(i < n_prompt_steps)
    def _():
        ffn(xp_ref, modp_ref, op_ref)
        if w1n_ref is not None:
            w1o_ref[...] = w1n_ref[...].astype(BF16)
            w2o_ref[...] = w2n_ref[...].astype(BF16)

    @pl.when(i == n_prompt_steps)
    def _():
        ffn(xs_ref, mods_ref, os_ref)


def _ffn(xp2d, xs, modp, mods, l, g2, w1, w2, next_w, final_g, tm, seq, final):
    n_tok = xp2d.shape[0]
    ns = xs.shape[0]
    n_steps = n_tok // tm
    lsel3 = lambda i: (l, 0, 0)
    wspec = lambda a: pl.BlockSpec(a.shape, lambda i: (0, 0), pipeline_mode=pl.Buffered(1))
    tile = lambda i: jnp.minimum(i, n_steps - 1)
    in_specs = [
        pl.BlockSpec((tm, D_MODEL), lambda i: (tile(i), 0)),
        pl.BlockSpec((ns, D_MODEL), lambda i: (0, 0)),
        pl.BlockSpec((None, None, 1, 3 * D_MODEL), lambda i: (l, tile(i) // (seq // tm), 0, 1)),
        pl.BlockSpec((None, ns, 3 * D_MODEL), lambda i: (l, 0, 1)),
        pl.BlockSpec((None, 1, D_MODEL), lsel3),
        wspec(w1),
        wspec(w2),
        pl.BlockSpec((1, D_MODEL), lambda i: (0, 0)),
    ]
    out_specs = [
        pl.BlockSpec((tm, D_MODEL), lambda i: (tile(i), 0)),
        pl.BlockSpec((ns, D_MODEL), lambda i: (0, 0)),
    ]
    out_shape = [
        jax.ShapeDtypeStruct((n_tok, D_MODEL), F32),
        jax.ShapeDtypeStruct((ns, D_MODEL), F32),
    ]
    args = [xp2d, xs, modp, mods, g2, w1, w2, final_g]
    if next_w is not None:
        slab = D_FF // n_steps
        in_specs += [
            pl.BlockSpec((None, D_MODEL, slab), lambda i: (l + 1, 0, tile(i))),
            pl.BlockSpec((None, slab, D_MODEL), lambda i: (l + 1, tile(i), 0)),
        ]
        out_specs += [
            pl.BlockSpec((D_MODEL, slab), lambda i: (0, tile(i))),
            pl.BlockSpec((slab, D_MODEL), lambda i: (tile(i), 0)),
        ]
        out_shape += [
            jax.ShapeDtypeStruct((D_MODEL, D_FF), BF16),
            jax.ShapeDtypeStruct((D_FF, D_MODEL), BF16),
        ]
        args += list(next_w)
    return pl.pallas_call(
        functools.partial(_ffn_kernel, n_prompt_steps=n_steps, final=final),
        grid=(n_steps + 1,),
        in_specs=in_specs,
        out_specs=out_specs,
        out_shape=out_shape,
        compiler_params=pltpu.CompilerParams(
            dimension_semantics=("arbitrary",), vmem_limit_bytes=VMEM_LIMIT),
        name="ffn",
    )(*args)


def _sample_in_kernel(x_ref, mod_ref, g1_ref, win_ref, gbias_ref, m0_ref,
                      proj_ref, qt_ref, kst_ref, vt_ref, wvt_ref, gs_ref, dec_ref, mo_ref):
    proj = _project(x_ref[...], mod_ref[...], g1_ref[...], win_ref[...])
    proj_ref[...] = proj
    g_t = (proj[:, C_Q:C_Q + HP] + gbias_ref[...]).T
    ib = g_t[G_OFF:G_OFF + NH]
    g = _log_sigmoid(g_t[G_OFF + NH:G_OFF + 2 * NH]) + m0_ref[...]
    m_t = jnp.maximum(g, ib)
    inter = jnp.exp(g - m_t)
    p = jnp.exp(ib - m_t)
    mo_ref[...] = m_t
    gs_ref[0] = inter
    gs_ref[1] = p
    gs_ref[2] = jnp.exp(-m_t)
    q_t = proj[:, C_Q:C_Q + MWP].T
    ks_t = (proj[:, C_K:C_K + MWP] * (1.0 / math.sqrt(HD))).T
    v_t = proj[:, C_V:C_V + MWP].T
    for h in range(NH):
        hs = slice(h * HP, (h + 1) * HP)
        qt_ref[h] = q_t[hs]
        kst_ref[h] = ks_t[hs]
        vt_ref[h] = v_t[hs]
        wvt_ref[h] = p[h:h + 1] * v_t[hs]
        dec_ref[h] = inter[h:h + 1]


def _sample_in(xs, mods, l, wts, m0_t):
    n = xs.shape[0]
    lsel3 = lambda i: (l, 0, 0)
    head_t = jax.ShapeDtypeStruct((NH, HP, n), F32)
    out_shapes = [
        jax.ShapeDtypeStruct((n, N_IN), F32),
        head_t, head_t, head_t, head_t,
        jax.ShapeDtypeStruct((3, NH, n), F32),
        jax.ShapeDtypeStruct((NH, 1, n), F32),
        jax.ShapeDtypeStruct((NH, n), F32),
    ]
    return pl.pallas_call(
        _sample_in_kernel,
        grid=(1,),
        in_specs=[
            pl.BlockSpec((n, D_MODEL), lambda i: (0, 0)),
            pl.BlockSpec((None, n, 2 * D_MODEL), lsel3),
            pl.BlockSpec((None, 1, D_MODEL), lsel3),
            pl.BlockSpec((None, D_MODEL, N_IN), lsel3),
            pl.BlockSpec((None, 1, HP), lsel3),
            pl.BlockSpec((None, NH, n), lsel3),
        ],
        out_specs=[pl.BlockSpec(s.shape, lambda i, nd=len(s.shape): (0,) * nd) for s in out_shapes],
        out_shape=out_shapes,
        compiler_params=pltpu.CompilerParams(
            dimension_semantics=("arbitrary",), vmem_limit_bytes=VMEM_LIMIT),
        name="sample_in_proj",
    )(xs, mods, wts["g1"], wts["w_in"], wts["gbias"], m0_t)


def _sample_state_kernel(c0_ref, qt_ref, kst_ref, wvt_ref, dec_ref, *rest, vb, n_fill):
    cnew_ref, cq_ref = rest[-2:]
    g = pl.program_id(0)

    @pl.when(g < n_fill)
    def _():
        cnew_ref[...] = jnp.zeros_like(cnew_ref)

    @pl.when(g == n_fill)
    def _():
        q = qt_ref[0:HD, :]
        ks = kst_ref[0:HD, :]
        dec = dec_ref[...]
        for vi in range(vb):
            c_old = c0_ref[vi]
            cq_ref[vi:vi + 1, :] = jnp.sum(c_old * q, axis=0, keepdims=True)
            cnew_ref[vi] = dec * c_old + wvt_ref[vi:vi + 1, :] * ks


def _sample_state(c0_t, c_out_prev, qt, kst, wvt, dec, l, vb):
    depth = c0_t.shape[0]
    n = c0_t.shape[-1]
    if c_out_prev is None:
        assert l == 0
        n_fill = depth - 1
    else:
        n_fill = 0
    real = lambda g: g // max(n_fill, 1) if n_fill else 1
    layer = lambda g: (g + 1) % depth if n_fill else l
    head_spec = pl.BlockSpec((None, HP, n), lambda g, h, i: (h, 0, 0))
    in_specs = [
        pl.BlockSpec((None, None, vb, HD, n), lambda g, h, i: (l, h * real(g), i * real(g), 0, 0)),
        head_spec, head_spec,
        pl.BlockSpec((None, vb, n), lambda g, h, i: (h, i, 0)),
        pl.BlockSpec((None, 1, n), lambda g, h, i: (h, 0, 0)),
    ]
    args = [c0_t, qt, kst, wvt, dec]
    aliases = {}
    if c_out_prev is not None:
        in_specs.append(pl.BlockSpec(memory_space=pl.ANY))
        args.append(c_out_prev)
        aliases = {len(args) - 1: 0}
    return pl.pallas_call(
        functools.partial(_sample_state_kernel, vb=vb, n_fill=n_fill),
        grid=(n_fill + 1, NH, HD // vb),
        in_specs=in_specs,
        out_specs=[
            pl.BlockSpec((None, None, vb, HD, n), lambda g, h, i: (layer(g), h, i, 0, 0)),
            pl.BlockSpec((None, vb, n), lambda g, h, i: (h * real(g), i * real(g), 0)),
        ],
        out_shape=[
            jax.ShapeDtypeStruct(c0_t.shape, F32),
            jax.ShapeDtypeStruct((NH, HD, n), F32),
        ],
        input_output_aliases=aliases,
        compiler_params=pltpu.CompilerParams(
            dimension_semantics=("arbitrary", "arbitrary", "arbitrary"), vmem_limit_bytes=VMEM_LIMIT),
        name="sample_matrix_memory",
    )(*args)


def _sample_mix_kernel(x_ref, gt_ref, proj_ref, qt_ref, kst_ref, vt_ref, cq_ref, gs_ref,
                       h0_ref, conv_ref, n0_ref, pool_ref,
                       convw_ref, convb_ref, wgate_ref, bgate_ref, lam_ref, ng_ref,
                       poolw_ref, pscale_ref, wout_ref,
                       xo_ref, ho_ref, convo_ref, no_ref, poolo_ref):
    proj = proj_ref[...]
    n_tok = proj.shape[0]

    xr = proj[:, C_XR:C_XR + LRU_W]
    gr = proj[:, C_GR:C_GR + LRU_W]
    cw = convw_ref[...]
    xc = (convb_ref[...] + cw[3:4] * xr + cw[2:3] * conv_ref[2]
          + cw[1:2] * conv_ref[1] + cw[0:1] * conv_ref[0])
    convo_ref[0] = conv_ref[1]
    convo_ref[1] = conv_ref[2]
    convo_ref[2] = xr
    gates = _dot(xc.astype(BF16), wgate_ref[...]) + bgate_ref[...]
    a, mult, gate_i = _lru_coeffs(gates, LRU_C * _log_sigmoid(lam_ref[...]))
    h_new = a * h0_ref[...] + mult * gate_i * xc
    ho_ref[...] = h_new
    y_a = h_new * _gelu_tanh(gr)

    normed = []
    for h in range(NH):
        q = qt_ref[h, 0:HD, :]
        ks = kst_ref[h, 0:HD, :]
        v = vt_ref[h, 0:HD, :]
        n0 = n0_ref[h]
        inter, p, floor = gs_ref[0, h:h + 1, :], gs_ref[1, h:h + 1, :], gs_ref[2, h:h + 1, :]
        s = jnp.sum(q * ks, axis=0, keepdims=True) * p
        num = s * v + inter * cq_ref[h]
        den = s + inter * jnp.sum(n0 * q, axis=0, keepdims=True)
        hout = num / jnp.maximum(jnp.abs(den), floor)
        no_ref[h] = inter * n0 + p * ks
        ms = jnp.sum(hout * hout, axis=0, keepdims=True) * (1.0 / HD)
        normed.append(hout * lax.rsqrt(ms + EPS))
        normed.append(jnp.zeros((HP - HD, n_tok), F32))
    hm = jnp.concatenate(normed, axis=0).T
    y_b = _sigmoid(proj[:, C_O:C_O + MWP]) * (hm * ng_ref[...])

    u = proj[:, C_U:C_U + PW]
    sums, acc, nxt = {}, u, 1
    for win in POOL_WINDOWS:
        while nxt < win:
            acc = acc + pool_ref[POOL_BUF - nxt]
            nxt += 1
        sums[win] = acc
    wsum = _pool_select(sums)
    cnt = jnp.minimum(PAST_LEN + 1, _pool_window_lanes(u.shape)).astype(F32)
    diff = wsum / cnt - u
    y_c = _dot(diff.astype(BF16), poolw_ref[...]) * pscale_ref[...]
    for r in range(POOL_BUF - 1):
        poolo_ref[r] = pool_ref[r + 1]
    poolo_ref[POOL_BUF - 1] = u

    mix = jnp.concatenate([y_a, y_b, y_c], axis=1).astype(BF16)
    xo_ref[...] = x_ref[...] + gt_ref[...] * _dot(mix, wout_ref[...])


def _sample_mix(xs, mods, proj, qt, kst, vt, cq, gs, l, st, wts):
    n = xs.shape[0]
    zeros = lambda nd: (lambda i: (0,) * nd)
    full = lambda a: pl.BlockSpec(a.shape, zeros(a.ndim))
    lspec = lambda a: pl.BlockSpec((None,) + a.shape[1:], lambda i, nd=a.ndim: (l,) + (0,) * (nd - 1))
    snames = ("lru_h", "conv", "n", "pool")
    wnames = ("conv_w", "conv_b", "w_gate", "b_gate", "lam", "ng", "pool_w", "pool_scale", "w_out")
    out_shapes = [
        jax.ShapeDtypeStruct((n, D_MODEL), F32),
        jax.ShapeDtypeStruct((n, LRU_W), F32),
        jax.ShapeDtypeStruct((3, n, LRU_W), F32),
        jax.ShapeDtypeStruct((NH, HD, n), F32),
        jax.ShapeDtypeStruct((POOL_BUF, n, PW), F32),
    ]
    return pl.pallas_call(
        _sample_mix_kernel,
        grid=(1,),
        in_specs=[
            full(xs),
            pl.BlockSpec((None, n, D_MODEL), lambda i: (l, 0, 2)),
            full(proj), full(qt), full(kst), full(vt), full(cq), full(gs),
        ] + [lspec(st[k]) for k in snames] + [lspec(wts[k]) for k in wnames],
        out_specs=[pl.BlockSpec(s.shape, zeros(len(s.shape))) for s in out_shapes],
        out_shape=out_shapes,
        compiler_params=pltpu.CompilerParams(
            dimension_semantics=("arbitrary",), vmem_limit_bytes=VMEM_LIMIT),
        name="sample_mixer",
    )(xs, mods, proj, qt, kst, vt, cq, gs, *[st[k] for k in snames], *[wts[k] for k in wnames])


def _pad_heads(w):
    lead = w.shape[:-1]
    w = w.reshape(lead + (NH, HD))
    w = jnp.pad(w, [(0, 0)] * len(lead) + [(0, 0), (0, HP - HD)])
    return w.reshape(lead + (MWP,))


def _block_diag(w):
    depth, g, n, _ = w.shape
    eye = jnp.eye(g, dtype=w.dtype)
    return (w[:, :, :, None, :] * eye[None, :, None, :, None]).reshape(depth, g * n, g * n)


def _permute_kernel(wint_ref, wout_ref, winp_ref, woutp_ref):
    o0 = 2 * LRU_W
    g0 = o0 + 4 * MW

    def put(dst_col, parts):
        blocks = [wint_ref[r0:r0 + n, :] for r0, n in parts]
        n_rows = sum(n for _, n in parts)
        if n_rows < HP:
            blocks.append(jnp.zeros((HP - n_rows, D_MODEL), F32))
        blk = jnp.concatenate(blocks, axis=0) if len(blocks) > 1 else blocks[0]
        winp_ref[:, dst_col:dst_col + HP] = blk.T.astype(BF16)

    for b in range(o0 // HP):
        put(b * HP, [(b * HP, HP)])
    for k, dst in enumerate((C_Q, C_K, C_V, C_O)):
        for h in range(NH):
            parts = [(o0 + k * MW + h * HD, HD)]
            if k == 0 and h == 0:
                parts.append((g0, 2 * NH))
            put(dst + h * HP, parts)
    for b in range(PW // HP):
        put(C_U + b * HP, [(g0 + 2 * NH + b * HP, HP)])

    woutp_ref[0:LRU_W, :] = wout_ref[0:LRU_W, :].astype(BF16)
    for h in range(NH):
        r0 = LRU_W + h * HP
        woutp_ref[r0:r0 + HD, :] = wout_ref[LRU_W + h * HD:LRU_W + (h + 1) * HD, :].astype(BF16)
        woutp_ref[r0 + HD:r0 + HP, :] = jnp.zeros((HP - HD, D_MODEL), BF16)
    woutp_ref[LRU_W + MWP:K_OUT, :] = wout_ref[LRU_W + MW:D_MODEL, :].astype(BF16)


def _permute_projections(w_in, w_out):
    depth, _, in_cols = w_in.shape
    lsel3 = lambda l: (l, 0, 0)
    return pl.pallas_call(
        _permute_kernel,
        grid=(depth,),
        in_specs=[
            pl.BlockSpec((None, in_cols, D_MODEL), lsel3),
            pl.BlockSpec((None, D_MODEL, D_MODEL), lsel3),
        ],
        out_specs=[
            pl.BlockSpec((None, D_MODEL, N_IN), lsel3),
            pl.BlockSpec((None, K_OUT, D_MODEL), lsel3),
        ],
        out_shape=[
            jax.ShapeDtypeStruct((depth, D_MODEL, N_IN), BF16),
            jax.ShapeDtypeStruct((depth, K_OUT, D_MODEL), BF16),
        ],
        compiler_params=pltpu.CompilerParams(
            dimension_semantics=("arbitrary",), vmem_limit_bytes=VMEM_LIMIT),
        name="permute_projections",
    )(jnp.transpose(w_in, (0, 2, 1)), w_out)


def _prepare_weights(norm1_g, norm2_g, w_in, conv_w, conv_b, lru_wa, lru_ba, lru_wx, lru_bx, lru_lam,
                     mlstm_bi, mlstm_bf, mlstm_norm_g, pool_w, pool_scale, w_out):
    depth = w_in.shape[0]
    w_in_p, w_out_p = _permute_projections(w_in, w_out)
    row = lambda v: v.reshape(depth, 1, -1)
    return {
        "g1": row(norm1_g), "g2": row(norm2_g),
        "w_in": w_in_p, "w_out": w_out_p,
        "conv_w": conv_w, "conv_b": row(conv_b),
        "w_gate": jnp.concatenate([_block_diag(lru_wa), _block_diag(lru_wx)], axis=-1).astype(BF16),
        "b_gate": jnp.concatenate([row(lru_ba), row(lru_bx)], axis=-1),
        "lam": row(lru_lam),
        "gbias": jnp.pad(jnp.concatenate([mlstm_bi, mlstm_bf], axis=-1),
                         ((0, 0), (G_OFF, HP - G_OFF - 2 * NH)))[:, None, :],
        "ng": row(_pad_heads(mlstm_norm_g.reshape(depth, MW))),
        "pool_w": _block_diag(pool_w).astype(BF16),
        "pool_scale": row(pool_scale),
    }


def kernel(x_prompt, x_sample, state_lru_h, state_lru_conv, state_mlstm_C, state_mlstm_n, state_mlstm_m,
           state_pool, c_prompt, c_sample, norm1_g, norm2_g, w_ada, b_ada, w_in, conv_w, conv_b, lru_wa,
           lru_ba, lru_wx, lru_bx, lru_lam, mlstm_bi, mlstm_bf, mlstm_norm_g, pool_w, pool_scale, w_out,
           w_ff1, w_ff2, final_g):
    nb, seq, _ = x_prompt.shape
    ns = x_sample.shape[0]
    depth = w_in.shape[0]
    wts = _prepare_weights(norm1_g, norm2_g, w_in, conv_w, conv_b, lru_wa, lru_ba, lru_wx, lru_bx, lru_lam,
                           mlstm_bi, mlstm_bf, mlstm_norm_g, pool_w, pool_scale, w_out)
    w1_bf, w2_bf = w_ff1[0].astype(BF16), w_ff2[0].astype(BF16)
    final_row = final_g.reshape(1, D_MODEL)
    modp, mods = _modulation(jnp.concatenate([c_prompt, c_sample], axis=0), w_ada,
                             b_ada.reshape(depth, 1, -1), nb)

    sample_state = {
        "lru_h": state_lru_h,
        "conv": jnp.transpose(state_lru_conv, (0, 2, 1, 3)),
        "n": jnp.transpose(state_mlstm_n, (0, 2, 3, 1)),
        "pool": jnp.transpose(state_pool, (0, 2, 1, 3)),
    }
    m0_t = jnp.transpose(state_mlstm_m, (0, 2, 1))
    c0_t = jnp.transpose(state_mlstm_C, (0, 2, 3, 4, 1))
    c_new_t = None

    xp = x_prompt
    xs = x_sample.reshape(ns, D_MODEL)
    p_out = [[] for _ in range(6)]
    s_out = [[] for _ in range(5)]
    for l in range(depth):
        last = l == depth - 1
        xp, h_l, conv_l, c_l, n_l, m_l, pool_l = _prompt_mixer(xp, modp, l, wts, TM_MIX)
        for lst, val in zip(p_out, (h_l.reshape(nb, LRU_W), conv_l[:, CONV_HIST - (CONV_TAPS - 1):], c_l, n_l[:, :NH, :HD],
                                    m_l[:, :NH, 0], pool_l[:, POOL_HIST - POOL_BUF:])):
            lst.append(val)
        proj, qt, kst, vt, wvt, gs, dec, m_s = _sample_in(xs, mods, l, wts, m0_t)
        c_new_t, cq = _sample_state(c0_t, c_new_t, qt, kst, wvt, dec, l, VB_STATE)
        xs, h_s, conv_s, n_s, pool_s = _sample_mix(xs, mods, proj, qt, kst, vt, cq, gs, l, sample_state, wts)
        for lst, val in zip(s_out, (h_s, conv_s, n_s, m_s, pool_s)):
            lst.append(val)
        res = _ffn(xp.reshape(nb * seq, D_MODEL), xs, modp, mods, l, wts["g2"], w1_bf, w2_bf,
                   None if last else (w_ff1, w_ff2), final_row, TM_FFN, seq, last)
        xp, xs = res[0].reshape(nb, seq, D_MODEL), res[1]
        if not last:
            w1_bf, w2_bf = res[2], res[3]

    stack = lambda lst: jnp.stack(lst, axis=0)
    h_s, conv_s, n_s, m_s, pool_s = [stack(v) for v in s_out]
    return (xp, xs.reshape(ns, 1, D_MODEL),
            *[stack(v) for v in p_out],
            h_s,
            jnp.transpose(conv_s, (0, 2, 1, 3)),
            jnp.transpose(c_new_t, (0, 4, 1, 2, 3)),
            jnp.transpose(n_s, (0, 3, 1, 2)),
            jnp.transpose(m_s, (0, 2, 1)),
            jnp.transpose(pool_s, (0, 2, 1, 3)))
```

```python
import functools
import math

import jax
import jax.numpy as jnp
from jax import lax
from jax.experimental import pallas as pl
from jax.experimental.pallas import tpu as pltpu

F32 = jnp.float32
BF16 = jnp.bfloat16

LANES = 128
SUBLANES = 8
MXU_TILE = 256
VMEM_BYTES = 64 * 1024 * 1024

D_MODEL = 1024
LRU_W = 384
LRU_C = 8.0
NH = 4
HD = 96
HP = LANES
MW = NH * HD
MWP = NH * HP
CHUNK = 128
PW = 256
POOL_WINDOWS = (2, 4, 8, 16)
POOL_GROUP = PW // len(POOL_WINDOWS)
POOL_BUF = 15
CONV_TAPS = 4
CONV_HIST = SUBLANES
POOL_HIST = 2 * SUBLANES
D_FF = 4096
EPS = 1e-6
PAST_LEN = 16384

C_XR, C_GR, C_Q, C_K, C_V, C_O, C_U = 0, 384, 768, 1280, 1792, 2304, 2816
N_IN = 3072
G_OFF = HD
K_OUT = LRU_W + MWP + PW

VMEM_LIMIT = VMEM_BYTES - 8 * 1024 * 1024
TM_MIX = 512
PROJ_PIECE = MXU_TILE
TM_FFN = 1024
FFN_CHUNK = 512
MOD_TN = 1536
VB_STATE = HD


def _log_sigmoid(x):
    return jnp.minimum(x, 0.0) - jnp.log1p(jnp.exp(-jnp.abs(x)))


def _sigmoid(x):
    return 1.0 / (1.0 + jnp.exp(-x))


def _gelu_tanh(x):
    c = math.sqrt(2.0 / math.pi)
    return 0.5 * x * (1.0 + jnp.tanh(c * (x + 0.044715 * (x * x * x))))


def _rms(x, g):
    ms = jnp.mean(x * x, axis=-1, keepdims=True)
    return x * lax.rsqrt(ms + EPS) * g


def _ada_norm(x, g, mod):
    ms = jnp.mean(x * x, axis=-1, keepdims=True)
    gain = g * (1.0 + mod[:, D_MODEL:2 * D_MODEL])
    return x * lax.rsqrt(ms + EPS) * gain + mod[:, 0:D_MODEL]


def _dot(a, b):
    return jnp.dot(a, b, preferred_element_type=F32)


def _dot_nt(a, b):
    return lax.dot_general(a, b, (((1,), (1,)), ((), ())), preferred_element_type=F32)


def _lru_coeffs(gates, log_lam):
    r = _sigmoid(gates[:, :LRU_W])
    i = _sigmoid(gates[:, LRU_W:])
    log_a = r * log_lam
    a = jnp.exp(log_a)
    th = jnp.tanh(log_a)
    mult = jnp.sqrt(-2.0 * th / (1.0 - th))
    return a, mult, i


def _scan_rows(a, b, h_prev):
    n, w = a.shape
    groups = n // SUBLANES
    a3 = a.reshape(groups, SUBLANES, w)
    b3 = b.reshape(groups, SUBLANES, w)
    row = lax.broadcasted_iota(jnp.int32, a3.shape, 1)
    s = 1
    while s < SUBLANES:
        keep = row >= s
        b3 = a3 * jnp.where(keep, pltpu.roll(b3, s, 1), 0.0) + b3
        a3 = a3 * jnp.where(keep, pltpu.roll(a3, s, 1), 1.0)
        s *= 2
    carry = h_prev
    outs = []
    last = SUBLANES - 1
    for g in range(groups):
        outs.append(a3[g] * carry + b3[g])
        carry = a3[g, last:last + 1] * carry + b3[g, last:last + 1]
    return jnp.concatenate(outs, axis=0), carry


def _head_norm_gate(hout, o, ng):
    ms = jnp.sum(hout * hout, axis=-1, keepdims=True) * (1.0 / HD)
    return _sigmoid(o) * (hout * lax.rsqrt(ms + EPS) * ng)


def _pool_by_group(shape, values):
    lane = lax.broadcasted_iota(jnp.int32, shape, 1)
    out = values[-1]
    for g in range(len(values) - 2, -1, -1):
        out = jnp.where(lane < (g + 1) * POOL_GROUP, values[g], out)
    return out


def _pool_select(sums):
    vals = [sums[w] for w in POOL_WINDOWS]
    return _pool_by_group(vals[0].shape, vals)


def _pool_window_lanes(shape):
    return _pool_by_group(shape, list(POOL_WINDOWS))


def _mod_kernel(c_ref, w_ref, b_ref, modp_ref, mods_ref, *, n_prompt):
    c = c_ref[...]
    act = (c * _sigmoid(c)).astype(BF16)
    out = _dot(act, w_ref[...].astype(BF16)) + b_ref[...]
    for r in range(n_prompt):
        modp_ref[r] = out[r:r + 1]
    mods_ref[...] = out[n_prompt:]


def _modulation(c_all, w_ada, b_ada, n_prompt):
    depth = w_ada.shape[0]
    n_all = c_all.shape[0]
    n_mod = w_ada.shape[2]
    tn = MOD_TN
    return pl.pallas_call(
        functools.partial(_mod_kernel, n_prompt=n_prompt),
        grid=(depth, n_mod // tn),
        in_specs=[
            pl.BlockSpec((n_all, D_MODEL), lambda l, j: (0, 0)),
            pl.BlockSpec((None, D_MODEL, tn), lambda l, j: (l, 0, j)),
            pl.BlockSpec((None, 1, tn), lambda l, j: (l, 0, j)),
        ],
        out_specs=[
            pl.BlockSpec((None, n_prompt, 1, tn), lambda l, j: (l, 0, 0, j)),
            pl.BlockSpec((None, n_all - n_prompt, tn), lambda l, j: (l, 0, j)),
        ],
        out_shape=[
            jax.ShapeDtypeStruct((depth, n_prompt, 1, n_mod), F32),
            jax.ShapeDtypeStruct((depth, n_all - n_prompt, n_mod), F32),
        ],
        compiler_params=pltpu.CompilerParams(
            dimension_semantics=("arbitrary", "arbitrary"), vmem_limit_bytes=VMEM_LIMIT),
        name="adaln_modulation",
    )(c_all, w_ada, b_ada)


def _mlstm_tile(proj, gl, gl_t, state, n_chunks, pump):
    pairs = [(c, h) for c in range(n_chunks) for h in range(NH)]
    t_idx = lax.broadcasted_iota(jnp.int32, (CHUNK, CHUNK), 0)
    s_idx = lax.broadcasted_iota(jnp.int32, (CHUNK, CHUNK), 1)
    causal = s_idx <= t_idx
    upper = t_idx <= s_idx
    rows = lambda c: slice(c * CHUNK, (c + 1) * CHUNK)
    head = lambda base, h: slice(base + h * HP, base + (h + 1) * HP)

    bcum_col, dmat, rowmax, ib_col = {}, {}, {}, {}
    for c, h in pairs:
        lf_row = gl_t[NH + h:NH + h + 1, rows(c)]
        lf_col = gl[rows(c), NH + h:NH + h + 1]
        ib_col[c, h] = gl[rows(c), h:h + 1]
        bcum_col[c, h] = jnp.sum(jnp.where(causal, lf_row, 0.0), axis=1, keepdims=True)
        bcum_row = jnp.sum(jnp.where(upper, lf_col, 0.0), axis=0, keepdims=True)
        dmat[c, h] = jnp.where(causal, bcum_col[c, h] - bcum_row + gl_t[h:h + 1, rows(c)], -jnp.inf)
        rowmax[c, h] = jnp.max(dmat[c, h], axis=1, keepdims=True)
    pump(1)

    m_start, m_t = {}, {}
    m_run = [state[h][2] for h in range(NH)]
    for c, h in pairs:
        m_start[c, h] = m_run[h]
        m_t[c, h] = jnp.maximum(bcum_col[c, h] + m_run[h], rowmax[c, h])
        m_run[h] = m_t[c, h][CHUNK - 1:CHUNK, :]
    inter, pmat, floor, decay, w_col = {}, {}, {}, {}, {}
    for c, h in pairs:
        b_last = bcum_col[c, h][CHUNK - 1:CHUNK, :]
        inter[c, h] = jnp.exp(bcum_col[c, h] + m_start[c, h] - m_t[c, h])
        pmat[c, h] = jnp.exp(dmat[c, h] - m_t[c, h])
        floor[c, h] = jnp.exp(-m_t[c, h])
        m_new = m_t[c, h][CHUNK - 1:CHUNK, :]
        decay[c, h] = jnp.exp(b_last + m_start[c, h] - m_new)
        w_col[c, h] = jnp.exp(b_last - bcum_col[c, h] + ib_col[c, h] - m_new)
    pump(1)

    qb, kb, s_mat, sv, wvk, wk = {}, {}, {}, {}, {}, {}
    for c, h in pairs:
        qb[c, h] = proj[rows(c), head(C_Q, h)].astype(BF16)
        ks = proj[rows(c), head(C_K, h)] * (1.0 / math.sqrt(HD))
        kb[c, h] = ks.astype(BF16)
        wk[c, h] = jnp.sum(w_col[c, h] * ks, axis=0, keepdims=True)
        s_mat[c, h] = _dot_nt(qb[c, h], kb[c, h]) * pmat[c, h]
    pump(1)
    for c, h in pairs:
        v = proj[rows(c), head(C_V, h)]
        sv[c, h] = _dot(s_mat[c, h].astype(BF16), v.astype(BF16))
        wvk[c, h] = _dot((w_col[c, h] * v).T.astype(BF16), kb[c, h])
    pump(1)

    outs = [[] for _ in range(NH)]
    c_run = [state[h][0] for h in range(NH)]
    n_run = [state[h][1] for h in range(NH)]
    for c, h in pairs:
        q = proj[rows(c), head(C_Q, h)]
        qc = _dot_nt(qb[c, h], c_run[h].astype(BF16))
        qn = jnp.sum(q * n_run[h], axis=1, keepdims=True)
        num = sv[c, h] + inter[c, h] * qc
        den = jnp.sum(s_mat[c, h], axis=1, keepdims=True) + inter[c, h] * qn
        outs[h].append(num / jnp.maximum(jnp.abs(den), floor[c, h]))
        c_run[h] = decay[c, h] * c_run[h] + wvk[c, h]
        n_run[h] = decay[c, h] * n_run[h] + wk[c, h]
    return outs, [(c_run[h], n_run[h], m_run[h]) for h in range(NH)]


def _project(x, mod, g1, w_in):
    return _dot(_ada_norm(x, g1, mod).astype(BF16), w_in)


def _mix_tile(proj, x, gt1, j, tm, convw_ref, convb_ref, wgate_ref, bgate_ref,
              lam_ref, gbias_ref, ng_ref, poolw_ref, pscale_ref, wout_ref,
              xo_ref, hlast_ref, convo_ref, co_ref, no_ref, mo_ref, poolo_ref,
              h_sc, xbuf, ubuf, c_sc, n_sc, m_sc, pump=lambda n: None):
    row = lax.broadcasted_iota(jnp.int32, (tm, 1), 0)

    xr = proj[:, C_XR:C_XR + LRU_W]
    gr = proj[:, C_GR:C_GR + LRU_W]
    xbuf[CONV_HIST:CONV_HIST + tm, :] = xr
    cw = convw_ref[...]
    xc = convb_ref[...] + cw[CONV_TAPS - 1:CONV_TAPS] * xr
    for back in range(1, CONV_TAPS):
        xc = xc + cw[CONV_TAPS - 1 - back:CONV_TAPS - back] * xbuf[CONV_HIST - back:CONV_HIST - back + tm, :]
    tail = xbuf[tm:tm + CONV_HIST, :]
    xbuf[0:CONV_HIST, :] = tail
    convo_ref[0] = tail
    gates = _dot(xc.astype(BF16), wgate_ref[...]) + bgate_ref[...]
    pump(0)
    log_lam = LRU_C * _log_sigmoid(lam_ref[...])
    h_run = h_sc[...]
    y_a_parts = []
    n_split = 2
    for r in range(n_split):
        rs = slice(r * tm // n_split, (r + 1) * tm // n_split)
        a, mult, gate_i = _lru_coeffs(gates[rs], log_lam)
        if r == 0:
            mult = jnp.where(jnp.logical_and(row[rs] == 0, j == 0), 1.0, mult)
        pump(2)
        hseq, h_run = _scan_rows(a, mult * gate_i * xc[rs], h_run)
        pump(1)
        y_a_parts.append(hseq * _gelu_tanh(gr[rs]))
    h_sc[...] = h_run
    hlast_ref[0] = h_run
    y_a = jnp.concatenate(y_a_parts, axis=0)
    out_acc = _dot(y_a.astype(BF16), wout_ref[0:LRU_W, :])

    g_rows = (proj[:, C_Q:C_Q + HP] + gbias_ref[...]).T[G_OFF:G_OFF + 2 * NH]
    row8 = lax.broadcasted_iota(jnp.int32, g_rows.shape, 0)
    gl_t = jnp.where(row8 >= NH, _log_sigmoid(g_rows), g_rows)
    gl = jnp.concatenate([gl_t, jnp.zeros((HP - 2 * NH, tm), F32)], axis=0).T
    ng = ng_ref[...]
    state = [(c_sc[h], n_sc[h:h + 1, :], m_sc[h:h + 1, 0:1]) for h in range(NH)]
    outs, state = _mlstm_tile(proj, gl, gl_t, state, tm // CHUNK, pump)
    y_b_heads = []
    for h in range(NH):
        c_h, n_h, m_h = state[h]
        c_sc[h] = c_h
        n_sc[h:h + 1, :] = n_h
        m_sc[h:h + 1, :] = jnp.broadcast_to(m_h, (1, HP))
        co_ref[0, h] = c_h[0:HD, 0:HD]
        hout_all = jnp.concatenate(outs[h], axis=0) if len(outs[h]) > 1 else outs[h][0]
        o = proj[:, C_O + h * HP:C_O + (h + 1) * HP]
        y_b_heads.append(_head_norm_gate(hout_all, o, ng[:, h * HP:(h + 1) * HP]))
        pump(h % 2)
    no_ref[0] = n_sc[...]
    mo_ref[0] = m_sc[...]
    y_b = jnp.concatenate(y_b_heads, axis=1).astype(BF16)
    out_acc = out_acc + _dot(y_b, wout_ref[LRU_W:LRU_W + MWP, :])

    u = proj[:, C_U:C_U + PW]
    ubuf[POOL_HIST:POOL_HIST + tm, :] = u
    sums, acc, win = {}, ubuf[...], 1
    while win < POOL_WINDOWS[-1]:
        acc = acc + pltpu.roll(acc, win, 0)
        win *= 2
        sums[win] = acc[POOL_HIST:]
    wsum = _pool_select(sums)
    pos = row + j * tm
    cnt = jnp.minimum(pos + 1, _pool_window_lanes((tm, PW))).astype(F32)
    diff = wsum / cnt - u
    y_c = _dot(diff.astype(BF16), poolw_ref[...]) * pscale_ref[...]
    ptail = ubuf[tm:tm + POOL_HIST, :]
    ubuf[0:POOL_HIST, :] = ptail
    poolo_ref[0] = ptail
    pump(N_IN)

    out_acc = out_acc + _dot(y_c.astype(BF16), wout_ref[LRU_W + MWP:K_OUT, :])
    xo_ref[0] = x[...] + gt1 * out_acc


def _prompt_mixer_kernel(x_ref, xn_ref, mod_ref, modn_ref, g1_ref, win_ref, *rest, tm):
    n_scratch = 9
    weights_outs = rest[:-n_scratch]
    h_sc, xbuf, ubuf, c_sc, n_sc, m_sc, proj_a, proj_b, x_keep = rest[-n_scratch:]
    j = pl.program_id(1)
    t = pl.program_id(0) * pl.num_programs(1) + j

    @pl.when(j == 0)
    def _():
        h_sc[...] = jnp.zeros_like(h_sc)
        xbuf[0:CONV_HIST, :] = jnp.zeros((CONV_HIST, LRU_W), F32)
        ubuf[0:POOL_HIST, :] = jnp.zeros((POOL_HIST, PW), F32)
        c_sc[...] = jnp.zeros_like(c_sc)
        n_sc[...] = jnp.zeros_like(n_sc)
        m_sc[...] = jnp.zeros_like(m_sc)

    @pl.when(t == 0)
    def _():
        x_keep[...] = x_ref[0]
        proj_a[...] = _project(x_ref[0], mod_ref[...], g1_ref[...], win_ref[...])

    def step(proj_cur, proj_nxt):
        done = [0]
        lhs = []

        def pump(n):
            if not lhs:
                lhs.append(_ada_norm(xn_ref[0], g1_ref[...], modn_ref[...]).astype(BF16))
            for _ in range(n):
                c0 = done[0]
                if c0 >= N_IN:
                    return
                c1 = min(c0 + PROJ_PIECE, N_IN)
                proj_nxt[:, c0:c1] = _dot(lhs[0], win_ref[:, c0:c1])
                done[0] = c1

        gt1 = mod_ref[:, 2 * D_MODEL:3 * D_MODEL]
        _mix_tile(proj_cur, x_keep, gt1, j, tm, *weights_outs, h_sc, xbuf, ubuf, c_sc, n_sc, m_sc,
                  pump=pump)
        x_keep[...] = xn_ref[0]

    @pl.when(t % 2 == 0)
    def _():
        step(proj_a, proj_b)

    @pl.when(t % 2 == 1)
    def _():
        step(proj_b, proj_a)


def _prompt_mixer(x, modp, l, wts, tm):
    nb, seq, _ = x.shape
    n_t = seq // tm

    def next_tile(b, j):
        t1 = jnp.minimum(b * n_t + j + 1, nb * n_t - 1)
        return t1 // n_t, t1 % n_t

    lsel3 = lambda b, j: (l, 0, 0)
    wspec = lambda a: pl.BlockSpec((None,) + a.shape[1:], lsel3, pipeline_mode=pl.Buffered(1))
    names = ("g1", "w_in", "conv_w", "conv_b", "w_gate", "b_gate", "lam", "gbias", "ng", "pool_w",
             "pool_scale", "w_out")
    out_shapes = [
        jax.ShapeDtypeStruct((nb, seq, D_MODEL), F32),
        jax.ShapeDtypeStruct((nb, 1, LRU_W), F32),
        jax.ShapeDtypeStruct((nb, CONV_HIST, LRU_W), F32),
        jax.ShapeDtypeStruct((nb, NH, HD, HD), F32),
        jax.ShapeDtypeStruct((nb, SUBLANES, HP), F32),
        jax.ShapeDtypeStruct((nb, SUBLANES, HP), F32),
        jax.ShapeDtypeStruct((nb, POOL_HIST, PW), F32),
    ]
    out_specs = [
        pl.BlockSpec((1, tm, D_MODEL), lambda b, j: (b, j, 0)),
        pl.BlockSpec((1, 1, LRU_W), lambda b, j: (b, 0, 0)),
        pl.BlockSpec((1, CONV_HIST, LRU_W), lambda b, j: (b, 0, 0)),
        pl.BlockSpec((1, NH, HD, HD), lambda b, j: (b, 0, 0, 0)),
        pl.BlockSpec((1, SUBLANES, HP), lambda b, j: (b, 0, 0)),
        pl.BlockSpec((1, SUBLANES, HP), lambda b, j: (b, 0, 0)),
        pl.BlockSpec((1, POOL_HIST, PW), lambda b, j: (b, 0, 0)),
    ]
    return pl.pallas_call(
        functools.partial(_prompt_mixer_kernel, tm=tm),
        grid=(nb, seq // tm),
        in_specs=[
            pl.BlockSpec((1, tm, D_MODEL), lambda b, j: (0, 0, 0)),
            pl.BlockSpec((1, tm, D_MODEL), lambda b, j: next_tile(b, j) + (0,)),
            pl.BlockSpec((None, None, 1, 3 * D_MODEL), lambda b, j: (l, b, 0, 0)),
            pl.BlockSpec((None, None, 1, 3 * D_MODEL), lambda b, j: (l, next_tile(b, j)[0], 0, 0)),
        ] + [wspec(wts[n]) for n in names],
        out_specs=out_specs,
        out_shape=out_shapes,
        scratch_shapes=[
            pltpu.VMEM((1, LRU_W), F32),
            pltpu.VMEM((CONV_HIST + tm, LRU_W), F32),
            pltpu.VMEM((POOL_HIST + tm, PW), F32),
            pltpu.VMEM((NH, HP, HP), F32),
            pltpu.VMEM((SUBLANES, HP), F32),
            pltpu.VMEM((SUBLANES, HP), F32),
            pltpu.VMEM((tm, N_IN), F32),
            pltpu.VMEM((tm, N_IN), F32),
            pltpu.VMEM((tm, D_MODEL), F32),
        ],
        compiler_params=pltpu.CompilerParams(
            dimension_semantics=("arbitrary", "arbitrary"), vmem_limit_bytes=VMEM_LIMIT),
        name="prompt_mixer",
    )(x, x, modp, modp, *[wts[n] for n in names])


def _ffn_kernel(xp_ref, xs_ref, modp_ref, mods_ref, g2_ref, w1_ref, w2_ref, gf_ref, *rest,
                n_prompt_steps, final):
    if len(rest) == 6:
        w1n_ref, w2n_ref, op_ref, os_ref, w1o_ref, w2o_ref = rest
    else:
        (op_ref, os_ref), w1n_ref = rest, None

    def ffn(x_ref, mod_ref, o_ref):
        x = x_ref[...]
        mod = mod_ref[...]
        gt2 = mod[:, 2 * D_MODEL:3 * D_MODEL]
        h2 = _ada_norm(x, g2_ref[...], mod).astype(BF16)
        acc = jnp.zeros(x.shape, F32)
        step = FFN_CHUNK
        for c in range(D_FF // step):
            hid = _dot(h2, w1_ref[:, c * step:(c + 1) * step])
            hid = jnp.square(jnp.maximum(hid, 0.0)).astype(BF16)
            acc = acc + _dot(hid, w2_ref[c * step:(c + 1) * step, :])
        y = x + gt2 * acc
        if final:
            y = _rms(y, gf_ref[...])
        o_ref[...] = y

    i = pl.program_id(0)

    @pl.when(i < n_prompt_steps)
    def _():
        ffn(xp_ref, modp_ref, op_ref)
        if w1n_ref is not None:
            w1o_ref[...] = w1n_ref[...].astype(BF16)
            w2o_ref[...] = w2n_ref[...].astype(BF16)

    @pl.when(i == n_prompt_steps)
    def _():
        ffn(xs_ref, mods_ref, os_ref)


def _ffn(xp2d, xs, modp, mods, l, g2, w1, w2, next_w, final_g, tm, seq, final):
    n_tok = xp2d.shape[0]
    ns = xs.shape[0]
    n_steps = n_tok // tm
    lsel3 = lambda i: (l, 0, 0)
    wspec = lambda a: pl.BlockSpec(a.shape, lambda i: (0, 0), pipeline_mode=pl.Buffered(1))
    tile = lambda i: jnp.minimum(i, n_steps - 1)
    in_specs = [
        pl.BlockSpec((tm, D_MODEL), lambda i: (tile(i), 0)),
        pl.BlockSpec((ns, D_MODEL), lambda i: (0, 0)),
        pl.BlockSpec((None, None, 1, 3 * D_MODEL), lambda i: (l, tile(i) // (seq // tm), 0, 1)),
        pl.BlockSpec((None, ns, 3 * D_MODEL), lambda i: (l, 0, 1)),
        pl.BlockSpec((None, 1, D_MODEL), lsel3),
        wspec(w1),
        wspec(w2),
        pl.BlockSpec((1, D_MODEL), lambda i: (0, 0)),
    ]
    out_specs = [
        pl.BlockSpec((tm, D_MODEL), lambda i: (tile(i), 0)),
        pl.BlockSpec((ns, D_MODEL), lambda i: (0, 0)),
    ]
    out_shape = [
        jax.ShapeDtypeStruct((n_tok, D_MODEL), F32),
        jax.ShapeDtypeStruct((ns, D_MODEL), F32),
    ]
    args = [xp2d, xs, modp, mods, g2, w1, w2, final_g]
    if next_w is not None:
        slab = D_FF // n_steps
        in_specs += [
            pl.BlockSpec((None, D_MODEL, slab), lambda i: (l + 1, 0, tile(i))),
            pl.BlockSpec((None, slab, D_MODEL), lambda i: (l + 1, tile(i), 0)),
        ]
        out_specs += [
            pl.BlockSpec((D_MODEL, slab), lambda i: (0, tile(i))),
            pl.BlockSpec((slab, D_MODEL), lambda i: (tile(i), 0)),
        ]
        out_shape += [
            jax.ShapeDtypeStruct((D_MODEL, D_FF), BF16),
            jax.ShapeDtypeStruct((D_FF, D_MODEL), BF16),
        ]
        args += list(next_w)
    return pl.pallas_call(
        functools.partial(_ffn_kernel, n_prompt_steps=n_steps, final=final),
        grid=(n_steps + 1,),
        in_specs=in_specs,
        out_specs=out_specs,
        out_shape=out_shape,
        compiler_params=pltpu.CompilerParams(
            dimension_semantics=("arbitrary",), vmem_limit_bytes=VMEM_LIMIT),
        name="ffn",
    )(*args)


def _sample_in_kernel(x_ref, mod_ref, g1_ref, win_ref, gbias_ref, m0_ref,
                      proj_ref, qt_ref, kst_ref, vt_ref, wvt_ref, gs_ref, dec_ref, mo_ref):
    proj = _project(x_ref[...], mod_ref[...], g1_ref[...], win_ref[...])
    proj_ref[...] = proj
    g_t = (proj[:, C_Q:C_Q + HP] + gbias_ref[...]).T
    ib = g_t[G_OFF:G_OFF + NH]
    g = _log_sigmoid(g_t[G_OFF + NH:G_OFF + 2 * NH]) + m0_ref[...]
    m_t = jnp.maximum(g, ib)
    inter = jnp.exp(g - m_t)
    p = jnp.exp(ib - m_t)
    mo_ref[...] = m_t
    gs_ref[0] = inter
    gs_ref[1] = p
    gs_ref[2] = jnp.exp(-m_t)
    q_t = proj[:, C_Q:C_Q + MWP].T
    ks_t = (proj[:, C_K:C_K + MWP] * (1.0 / math.sqrt(HD))).T
    v_t = proj[:, C_V:C_V + MWP].T
    for h in range(NH):
        hs = slice(h * HP, (h + 1) * HP)
        qt_ref[h] = q_t[hs]
        kst_ref[h] = ks_t[hs]
        vt_ref[h] = v_t[hs]
        wvt_ref[h] = p[h:h + 1] * v_t[hs]
        dec_ref[h] = inter[h:h + 1]


def _sample_in(xs, mods, l, wts, m0_t):
    n = xs.shape[0]
    lsel3 = lambda i: (l, 0, 0)
    head_t = jax.ShapeDtypeStruct((NH, HP, n), F32)
    out_shapes = [
        jax.ShapeDtypeStruct((n, N_IN), F32),
        head_t, head_t, head_t, head_t,
        jax.ShapeDtypeStruct((3, NH, n), F32),
        jax.ShapeDtypeStruct((NH, 1, n), F32),
        jax.ShapeDtypeStruct((NH, n), F32),
    ]
    return pl.pallas_call(
        _sample_in_kernel,
        grid=(1,),
        in_specs=[
            pl.BlockSpec((n, D_MODEL), lambda i: (0, 0)),
            pl.BlockSpec((None, n, 2 * D_MODEL), lsel3),
            pl.BlockSpec((None, 1, D_MODEL), lsel3),
            pl.BlockSpec((None, D_MODEL, N_IN), lsel3),
            pl.BlockSpec((None, 1, HP), lsel3),
            pl.BlockSpec((None, NH, n), lsel3),
        ],
        out_specs=[pl.BlockSpec(s.shape, lambda i, nd=len(s.shape): (0,) * nd) for s in out_shapes],
        out_shape=out_shapes,
        compiler_params=pltpu.CompilerParams(
            dimension_semantics=("arbitrary",), vmem_limit_bytes=VMEM_LIMIT),
        name="sample_in_proj",
    )(xs, mods, wts["g1"], wts["w_in"], wts["gbias"], m0_t)


def _sample_state_kernel(c0_ref, qt_ref, kst_ref, wvt_ref, dec_ref, *rest, vb, n_fill):
    cnew_ref, cq_ref = rest[-2:]
    g = pl.program_id(0)

    @pl.when(g < n_fill)
    def _():
        cnew_ref[...] = jnp.zeros_like(cnew_ref)

    @pl.when(g == n_fill)
    def _():
        q = qt_ref[0:HD, :]
        ks = kst_ref[0:HD, :]
        dec = dec_ref[...]
        for vi in range(vb):
            c_old = c0_ref[vi]
            cq_ref[vi:vi + 1, :] = jnp.sum(c_old * q, axis=0, keepdims=True)
            cnew_ref[vi] = dec * c_old + wvt_ref[vi:vi + 1, :] * ks


def _sample_state(c0_t, c_out_prev, qt, kst, wvt, dec, l, vb):
    depth = c0_t.shape[0]
    n = c0_t.shape[-1]
    if c_out_prev is None:
        assert l == 0
        n_fill = depth - 1
    else:
        n_fill = 0
    real = lambda g: g // max(n_fill, 1) if n_fill else 1
    layer = lambda g: (g + 1) % depth if n_fill else l
    head_spec = pl.BlockSpec((None, HP, n), lambda g, h, i: (h, 0, 0))
    in_specs = [
        pl.BlockSpec((None, None, vb, HD, n), lambda g, h, i: (l, h * real(g), i * real(g), 0, 0)),
        head_spec, head_spec,
        pl.BlockSpec((None, vb, n), lambda g, h, i: (h, i, 0)),
        pl.BlockSpec((None, 1, n), lambda g, h, i: (h, 0, 0)),
    ]
    args = [c0_t, qt, kst, wvt, dec]
    aliases = {}
    if c_out_prev is not None:
        in_specs.append(pl.BlockSpec(memory_space=pl.ANY))
        args.append(c_out_prev)
        aliases = {len(args) - 1: 0}
    return pl.pallas_call(
        functools.partial(_sample_state_kernel, vb=vb, n_fill=n_fill),
        grid=(n_fill + 1, NH, HD // vb),
        in_specs=in_specs,
        out_specs=[
            pl.BlockSpec((None, None, vb, HD, n), lambda g, h, i: (layer(g), h, i, 0, 0)),
            pl.BlockSpec((None, vb, n), lambda g, h, i: (h * real(g), i * real(g), 0)),
        ],
        out_shape=[
            jax.ShapeDtypeStruct(c0_t.shape, F32),
            jax.ShapeDtypeStruct((NH, HD, n), F32),
        ],
        input_output_aliases=aliases,
        compiler_params=pltpu.CompilerParams(
            dimension_semantics=("arbitrary", "arbitrary", "arbitrary"), vmem_limit_bytes=VMEM_LIMIT),
        name="sample_matrix_memory",
    )(*args)


def _sample_mix_kernel(x_ref, gt_ref, proj_ref, qt_ref, kst_ref, vt_ref, cq_ref, gs_ref,
                       h0_ref, conv_ref, n0_ref, pool_ref,
                       convw_ref, convb_ref, wgate_ref, bgate_ref, lam_ref, ng_ref,
                       poolw_ref, pscale_ref, wout_ref,
                       xo_ref, ho_ref, convo_ref, no_ref, poolo_ref):
    proj = proj_ref[...]
    n_tok = proj.shape[0]

    xr = proj[:, C_XR:C_XR + LRU_W]
    gr = proj[:, C_GR:C_GR + LRU_W]
    cw = convw_ref[...]
    xc = (convb_ref[...] + cw[3:4] * xr + cw[2:3] * conv_ref[2]
          + cw[1:2] * conv_ref[1] + cw[0:1] * conv_ref[0])
    convo_ref[0] = conv_ref[1]
    convo_ref[1] = conv_ref[2]
    convo_ref[2] = xr
    gates = _dot(xc.astype(BF16), wgate_ref[...]) + bgate_ref[...]
    a, mult, gate_i = _lru_coeffs(gates, LRU_C * _log_sigmoid(lam_ref[...]))
    h_new = a * h0_ref[...] + mult * gate_i * xc
    ho_ref[...] = h_new
    y_a = h_new * _gelu_tanh(gr)

    normed = []
    for h in range(NH):
        q = qt_ref[h, 0:HD, :]
        ks = kst_ref[h, 0:HD, :]
        v = vt_ref[h, 0:HD, :]
        n0 = n0_ref[h]
        inter, p, floor = gs_ref[0, h:h + 1, :], gs_ref[1, h:h + 1, :], gs_ref[2, h:h + 1, :]
        s = jnp.sum(q * ks, axis=0, keepdims=True) * p
        num = s * v + inter * cq_ref[h]
        den = s + inter * jnp.sum(n0 * q, axis=0, keepdims=True)
        hout = num / jnp.maximum(jnp.abs(den), floor)
        no_ref[h] = inter * n0 + p * ks
        ms = jnp.sum(hout * hout, axis=0, keepdims=True) * (1.0 / HD)
        normed.append(hout * lax.rsqrt(ms + EPS))
        normed.append(jnp.zeros((HP - HD, n_tok), F32))
    hm = jnp.concatenate(normed, axis=0).T
    y_b = _sigmoid(proj[:, C_O:C_O + MWP]) * (hm * ng_ref[...])

    u = proj[:, C_U:C_U + PW]
    sums, acc, nxt = {}, u, 1
    for win in POOL_WINDOWS:
        while nxt < win:
            acc = acc + pool_ref[POOL_BUF - nxt]
            nxt += 1
        sums[win] = acc
    wsum = _pool_select(sums)
    cnt = jnp.minimum(PAST_LEN + 1, _pool_window_lanes(u.shape)).astype(F32)
    diff = wsum / cnt - u
    y_c = _dot(diff.astype(BF16), poolw_ref[...]) * pscale_ref[...]
    for r in range(POOL_BUF - 1):
        poolo_ref[r] = pool_ref[r + 1]
    poolo_ref[POOL_BUF - 1] = u

    mix = jnp.concatenate([y_a, y_b, y_c], axis=1).astype(BF16)
    xo_ref[...] = x_ref[...] + gt_ref[...] * _dot(mix, wout_ref[...])


def _sample_mix(xs, mods, proj, qt, kst, vt, cq, gs, l, st, wts):
    n = xs.shape[0]
    zeros = lambda nd: (lambda i: (0,) * nd)
    full = lambda a: pl.BlockSpec(a.shape, zeros(a.ndim))
    lspec = lambda a: pl.BlockSpec((None,) + a.shape[1:], lambda i, nd=a.ndim: (l,) + (0,) * (nd - 1))
    snames = ("lru_h", "conv", "n", "pool")
    wnames = ("conv_w", "conv_b", "w_gate", "b_gate", "lam", "ng", "pool_w", "pool_scale", "w_out")
    out_shapes = [
        jax.ShapeDtypeStruct((n, D_MODEL), F32),
        jax.ShapeDtypeStruct((n, LRU_W), F32),
        jax.ShapeDtypeStruct((3, n, LRU_W), F32),
        jax.ShapeDtypeStruct((NH, HD, n), F32),
        jax.ShapeDtypeStruct((POOL_BUF, n, PW), F32),
    ]
    return pl.pallas_call(
        _sample_mix_kernel,
        grid=(1,),
        in_specs=[
            full(xs),
            pl.BlockSpec((None, n, D_MODEL), lambda i: (l, 0, 2)),
            full(proj), full(qt), full(kst), full(vt), full(cq), full(gs),
        ] + [lspec(st[k]) for k in snames] + [lspec(wts[k]) for k in wnames],
        out_specs=[pl.BlockSpec(s.shape, zeros(len(s.shape))) for s in out_shapes],
        out_shape=out_shapes,
        compiler_params=pltpu.CompilerParams(
            dimension_semantics=("arbitrary",), vmem_limit_bytes=VMEM_LIMIT),
        name="sample_mixer",
    )(xs, mods, proj, qt, kst, vt, cq, gs, *[st[k] for k in snames], *[wts[k] for k in wnames])


def _pad_heads(w):
    lead = w.shape[:-1]
    w = w.reshape(lead + (NH, HD))
    w = jnp.pad(w, [(0, 0)] * len(lead) + [(0, 0), (0, HP - HD)])
    return w.reshape(lead + (MWP,))


def _block_diag(w):
    depth, g, n, _ = w.shape
    eye = jnp.eye(g, dtype=w.dtype)
    return (w[:, :, :, None, :] * eye[None, :, None, :, None]).reshape(depth, g * n, g * n)


def _permute_kernel(wint_ref, wout_ref, winp_ref, woutp_ref):
    o0 = 2 * LRU_W
    g0 = o0 + 4 * MW

    def put(dst_col, parts):
        blocks = [wint_ref[r0:r0 + n, :] for r0, n in parts]
        n_rows = sum(n for _, n in parts)
        if n_rows < HP:
            blocks.append(jnp.zeros((HP - n_rows, D_MODEL), F32))
        blk = jnp.concatenate(blocks, axis=0) if len(blocks) > 1 else blocks[0]
        winp_ref[:, dst_col:dst_col + HP] = blk.T.astype(BF16)

    for b in range(o0 // HP):
        put(b * HP, [(b * HP, HP)])
    for k, dst in enumerate((C_Q, C_K, C_V, C_O)):
        for h in range(NH):
            parts = [(o0 + k * MW + h * HD, HD)]
            if k == 0 and h == 0:
                parts.append((g0, 2 * NH))
            put(dst + h * HP, parts)
    for b in range(PW // HP):
        put(C_U + b * HP, [(g0 + 2 * NH + b * HP, HP)])

    woutp_ref[0:LRU_W, :] = wout_ref[0:LRU_W, :].astype(BF16)
    for h in range(NH):
        r0 = LRU_W + h * HP
        woutp_ref[r0:r0 + HD, :] = wout_ref[LRU_W + h * HD:LRU_W + (h + 1) * HD, :].astype(BF16)
        woutp_ref[r0 + HD:r0 + HP, :] = jnp.zeros((HP - HD, D_MODEL), BF16)
    woutp_ref[LRU_W + MWP:K_OUT, :] = wout_ref[LRU_W + MW:D_MODEL, :].astype(BF16)


def _permute_projections(w_in, w_out):
    depth, _, in_cols = w_in.shape
    lsel3 = lambda l: (l, 0, 0)
    return pl.pallas_call(
        _permute_kernel,
        grid=(depth,),
        in_specs=[
            pl.BlockSpec((None, in_cols, D_MODEL), lsel3),
            pl.BlockSpec((None, D_MODEL, D_MODEL), lsel3),
        ],
        out_specs=[
            pl.BlockSpec((None, D_MODEL, N_IN), lsel3),
            pl.BlockSpec((None, K_OUT, D_MODEL), lsel3),
        ],
        out_shape=[
            jax.ShapeDtypeStruct((depth, D_MODEL, N_IN), BF16),
            jax.ShapeDtypeStruct((depth, K_OUT, D_MODEL), BF16),
        ],
        compiler_params=pltpu.CompilerParams(
            dimension_semantics=("arbitrary",), vmem_limit_bytes=VMEM_LIMIT),
        name="permute_projections",
    )(jnp.transpose(w_in, (0, 2, 1)), w_out)


def _prepare_weights(norm1_g, norm2_g, w_in, conv_w, conv_b, lru_wa, lru_ba, lru_wx, lru_bx, lru_lam,
                     mlstm_bi, mlstm_bf, mlstm_norm_g, pool_w, pool_scale, w_out):
    depth = w_in.shape[0]
    w_in_p, w_out_p = _permute_projections(w_in, w_out)
    row = lambda v: v.reshape(depth, 1, -1)
    return {
        "g1": row(norm1_g), "g2": row(norm2_g),
        "w_in": w_in_p, "w_out": w_out_p,
        "conv_w": conv_w, "conv_b": row(conv_b),
        "w_gate": jnp.concatenate([_block_diag(lru_wa), _block_diag(lru_wx)], axis=-1).astype(BF16),
        "b_gate": jnp.concatenate([row(lru_ba), row(lru_bx)], axis=-1),
        "lam": row(lru_lam),
        "gbias": jnp.pad(jnp.concatenate([mlstm_bi, mlstm_bf], axis=-1),
                         ((0, 0), (G_OFF, HP - G_OFF - 2 * NH)))[:, None, :],
        "ng": row(_pad_heads(mlstm_norm_g.reshape(depth, MW))),
        "pool_w": _block_diag(pool_w).astype(BF16),
        "pool_scale": row(pool_scale),
    }


def kernel(x_prompt, x_sample, state_lru_h, state_lru_conv, state_mlstm_C, state_mlstm_n, state_mlstm_m,
           state_pool, c_prompt, c_sample, norm1_g, norm2_g, w_ada, b_ada, w_in, conv_w, conv_b, lru_wa,
           lru_ba, lru_wx, lru_bx, lru_lam, mlstm_bi, mlstm_bf, mlstm_norm_g, pool_w, pool_scale, w_out,
           w_ff1, w_ff2, final_g):
    nb, seq, _ = x_prompt.shape
    ns = x_sample.shape[0]
    depth = w_in.shape[0]
    wts = _prepare_weights(norm1_g, norm2_g, w_in, conv_w, conv_b, lru_wa, lru_ba, lru_wx, lru_bx, lru_lam,
                           mlstm_bi, mlstm_bf, mlstm_norm_g, pool_w, pool_scale, w_out)
    w1_bf, w2_bf = w_ff1[0].astype(BF16), w_ff2[0].astype(BF16)
    final_row = final_g.reshape(1, D_MODEL)
    modp, mods = _modulation(jnp.concatenate([c_prompt, c_sample], axis=0), w_ada,
                             b_ada.reshape(depth, 1, -1), nb)

    sample_state = {
        "lru_h": state_lru_h,
        "conv": jnp.transpose(state_lru_conv, (0, 2, 1, 3)),
        "n": jnp.transpose(state_mlstm_n, (0, 2, 3, 1)),
        "pool": jnp.transpose(state_pool, (0, 2, 1, 3)),
    }
    m0_t = jnp.transpose(state_mlstm_m, (0, 2, 1))
    c0_t = jnp.transpose(state_mlstm_C, (0, 2, 3, 4, 1))
    c_new_t = None

    xp = x_prompt
    xs = x_sample.reshape(ns, D_MODEL)
    p_out = [[] for _ in range(6)]
    s_out = [[] for _ in range(5)]
    for l in range(depth):
        last = l == depth - 1
        xp, h_l, conv_l, c_l, n_l, m_l, pool_l = _prompt_mixer(xp, modp, l, wts, TM_MIX)
        for lst, val in zip(p_out, (h_l.reshape(nb, LRU_W), conv_l[:, CONV_HIST - (CONV_TAPS - 1):], c_l, n_l[:, :NH, :HD],
                                    m_l[:, :NH, 0], pool_l[:, POOL_HIST - POOL_BUF:])):
            lst.append(val)
        proj, qt, kst, vt, wvt, gs, dec, m_s = _sample_in(xs, mods, l, wts, m0_t)
        c_new_t, cq = _sample_state(c0_t, c_new_t, qt, kst, wvt, dec, l, VB_STATE)
        xs, h_s, conv_s, n_s, pool_s = _sample_mix(xs, mods, proj, qt, kst, vt, cq, gs, l, sample_state, wts)
        for lst, val in zip(s_out, (h_s, conv_s, n_s, m_s, pool_s)):
            lst.append(val)
        res = _ffn(xp.reshape(nb * seq, D_MODEL), xs, modp, mods, l, wts["g2"], w1_bf, w2_bf,
                   None if last else (w_ff1, w_ff2), final_row, TM_FFN, seq, last)
        xp, xs = res[0].reshape(nb, seq, D_MODEL), res[1]
        if not last:
            w1_bf, w2_bf = res[2], res[3]

    stack = lambda lst: jnp.stack(lst, axis=0)
    h_s, conv_s, n_s, m_s, pool_s = [stack(v) for v in s_out]
    return (xp, xs.reshape(ns, 1, D_MODEL),
            *[stack(v) for v in p_out],
            h_s,
            jnp.transpose(conv_s, (0, 2, 1, 3)),
            jnp.transpose(c_new_t, (0, 4, 1, 2, 3)),
            jnp.transpose(n_s, (0, 3, 1, 2)),
            jnp.transpose(m_s, (0, 2, 1)),
            jnp.transpose(pool_s, (0, 2, 1, 3)))
```

```python
import functools
import math

import jax
import jax.numpy as jnp
from jax import lax
from jax.experimental import pallas as pl
from jax.experimental.pallas import tpu as pltpu

F32 = jnp.float32
BF16 = jnp.bfloat16

LANES = 128
SUBLANES = 8
MXU_TILE = 256
VMEM_BYTES = 64 * 1024 * 1024

D_MODEL = 1024
LRU_W = 384
LRU_C = 8.0
NH = 4
HD = 96
HP = LANES
MW = NH * HD
MWP = NH * HP
CHUNK = 128
PW = 256
POOL_WINDOWS = (2, 4, 8, 16)
POOL_GROUP = PW // len(POOL_WINDOWS)
POOL_BUF = 15
CONV_TAPS = 4
CONV_HIST = SUBLANES
POOL_HIST = 2 * SUBLANES
D_FF = 4096
EPS = 1e-6
PAST_LEN = 16384

C_XR, C_GR, C_Q, C_K, C_V, C_O, C_U = 0, 384, 768, 1280, 1792, 2304, 2816
N_IN = 3072
G_OFF = HD
K_OUT = LRU_W + MWP + PW

VMEM_LIMIT = VMEM_BYTES - 8 * 1024 * 1024
TM_MIX = 512
PROJ_PIECE = MXU_TILE
TM_FFN = 1024
FFN_CHUNK = 2048
MOD_TN = 1536
VB_STATE = HD


def _log_sigmoid(x):
    return jnp.minimum(x, 0.0) - jnp.log1p(jnp.exp(-jnp.abs(x)))


def _sigmoid(x):
    return 1.0 / (1.0 + jnp.exp(-x))


def _gelu_tanh(x):
    c = math.sqrt(2.0 / math.pi)
    return 0.5 * x * (1.0 + jnp.tanh(c * (x + 0.044715 * (x * x * x))))


def _rms(x, g):
    ms = jnp.mean(x * x, axis=-1, keepdims=True)
    return x * lax.rsqrt(ms + EPS) * g


def _ada_norm(x, g, mod):
    ms = jnp.mean(x * x, axis=-1, keepdims=True)
    gain = g * (1.0 + mod[:, D_MODEL:2 * D_MODEL])
    return x * lax.rsqrt(ms + EPS) * gain + mod[:, 0:D_MODEL]


def _dot(a, b):
    return jnp.dot(a, b, preferred_element_type=F32)


def _dot_nt(a, b):
    return lax.dot_general(a, b, (((1,), (1,)), ((), ())), preferred_element_type=F32)


def _lru_coeffs(gates, log_lam):
    r = _sigmoid(gates[:, :LRU_W])
    i = _sigmoid(gates[:, LRU_W:])
    log_a = r * log_lam
    a = jnp.exp(log_a)
    th = jnp.tanh(log_a)
    mult = jnp.sqrt(-2.0 * th / (1.0 - th))
    return a, mult, i


def _scan_rows(a, b, h_prev):
    n, w = a.shape
    groups = n // SUBLANES
    a3 = a.reshape(groups, SUBLANES, w)
    b3 = b.reshape(groups, SUBLANES, w)
    row = lax.broadcasted_iota(jnp.int32, a3.shape, 1)
    s = 1
    while s < SUBLANES:
        keep = row >= s
        b3 = a3 * jnp.where(keep, pltpu.roll(b3, s, 1), 0.0) + b3
        a3 = a3 * jnp.where(keep, pltpu.roll(a3, s, 1), 1.0)
        s *= 2
    carry = h_prev
    outs = []
    last = SUBLANES - 1
    for g in range(groups):
        outs.append(a3[g] * carry + b3[g])
        carry = a3[g, last:last + 1] * carry + b3[g, last:last + 1]
    return jnp.concatenate(outs, axis=0), carry


def _head_norm_gate(hout, o, ng):
    ms = jnp.sum(hout * hout, axis=-1, keepdims=True) * (1.0 / HD)
    return _sigmoid(o) * (hout * lax.rsqrt(ms + EPS) * ng)


def _pool_by_group(shape, values):
    lane = lax.broadcasted_iota(jnp.int32, shape, 1)
    out = values[-1]
    for g in range(len(values) - 2, -1, -1):
        out = jnp.where(lane < (g + 1) * POOL_GROUP, values[g], out)
    return out


def _pool_select(sums):
    vals = [sums[w] for w in POOL_WINDOWS]
    return _pool_by_group(vals[0].shape, vals)


def _pool_window_lanes(shape):
    return _pool_by_group(shape, list(POOL_WINDOWS))


def _mod_kernel(c_ref, w_ref, b_ref, modp_ref, mods_ref, *, n_prompt):
    c = c_ref[...]
    act = (c * _sigmoid(c)).astype(BF16)
    out = _dot(act, w_ref[...].astype(BF16)) + b_ref[...]
    for r in range(n_prompt):
        modp_ref[r] = out[r:r + 1]
    mods_ref[...] = out[n_prompt:]


def _modulation(c_all, w_ada, b_ada, n_prompt):
    depth = w_ada.shape[0]
    n_all = c_all.shape[0]
    n_mod = w_ada.shape[2]
    tn = MOD_TN
    return pl.pallas_call(
        functools.partial(_mod_kernel, n_prompt=n_prompt),
        grid=(depth, n_mod // tn),
        in_specs=[
            pl.BlockSpec((n_all, D_MODEL), lambda l, j: (0, 0)),
            pl.BlockSpec((None, D_MODEL, tn), lambda l, j: (l, 0, j)),
            pl.BlockSpec((None, 1, tn), lambda l, j: (l, 0, j)),
        ],
        out_specs=[
            pl.BlockSpec((None, n_prompt, 1, tn), lambda l, j: (l, 0, 0, j)),
            pl.BlockSpec((None, n_all - n_prompt, tn), lambda l, j: (l, 0, j)),
        ],
        out_shape=[
            jax.ShapeDtypeStruct((depth, n_prompt, 1, n_mod), F32),
            jax.ShapeDtypeStruct((depth, n_all - n_prompt, n_mod), F32),
        ],
        compiler_params=pltpu.CompilerParams(
            dimension_semantics=("arbitrary", "arbitrary"), vmem_limit_bytes=VMEM_LIMIT),
        name="adaln_modulation",
    )(c_all, w_ada, b_ada)


def _mlstm_tile(proj, gl, gl_t, state, n_chunks, pump):
    pairs = [(c, h) for c in range(n_chunks) for h in range(NH)]
    t_idx = lax.broadcasted_iota(jnp.int32, (CHUNK, CHUNK), 0)
    s_idx = lax.broadcasted_iota(jnp.int32, (CHUNK, CHUNK), 1)
    causal = s_idx <= t_idx
    upper = t_idx <= s_idx
    rows = lambda c: slice(c * CHUNK, (c + 1) * CHUNK)
    head = lambda base, h: slice(base + h * HP, base + (h + 1) * HP)

    bcum_col, dmat, rowmax, ib_col = {}, {}, {}, {}
    for c, h in pairs:
        lf_row = gl_t[NH + h:NH + h + 1, rows(c)]
        lf_col = gl[rows(c), NH + h:NH + h + 1]
        ib_col[c, h] = gl[rows(c), h:h + 1]
        bcum_col[c, h] = jnp.sum(jnp.where(causal, lf_row, 0.0), axis=1, keepdims=True)
        bcum_row = jnp.sum(jnp.where(upper, lf_col, 0.0), axis=0, keepdims=True)
        dmat[c, h] = jnp.where(causal, bcum_col[c, h] - bcum_row + gl_t[h:h + 1, rows(c)], -jnp.inf)
        rowmax[c, h] = jnp.max(dmat[c, h], axis=1, keepdims=True)
    pump(1)

    m_start, m_t = {}, {}
    m_run = [state[h][2] for h in range(NH)]
    for c, h in pairs:
        m_start[c, h] = m_run[h]
        m_t[c, h] = jnp.maximum(bcum_col[c, h] + m_run[h], rowmax[c, h])
        m_run[h] = m_t[c, h][CHUNK - 1:CHUNK, :]
    inter, pmat, floor, decay, w_col = {}, {}, {}, {}, {}
    for c, h in pairs:
        b_last = bcum_col[c, h][CHUNK - 1:CHUNK, :]
        inter[c, h] = jnp.exp(bcum_col[c, h] + m_start[c, h] - m_t[c, h])
        pmat[c, h] = jnp.exp(dmat[c, h] - m_t[c, h])
        floor[c, h] = jnp.exp(-m_t[c, h])
        m_new = m_t[c, h][CHUNK - 1:CHUNK, :]
        decay[c, h] = jnp.exp(b_last + m_start[c, h] - m_new)
        w_col[c, h] = jnp.exp(b_last - bcum_col[c, h] + ib_col[c, h] - m_new)
    pump(1)

    qb, kb, s_mat, sv, wvk, wk = {}, {}, {}, {}, {}, {}
    for c, h in pairs:
        qb[c, h] = proj[rows(c), head(C_Q, h)].astype(BF16)
        ks = proj[rows(c), head(C_K, h)] * (1.0 / math.sqrt(HD))
        kb[c, h] = ks.astype(BF16)
        wk[c, h] = jnp.sum(w_col[c, h] * ks, axis=0, keepdims=True)
        s_mat[c, h] = _dot_nt(qb[c, h], kb[c, h]) * pmat[c, h]
    pump(1)
    for c, h in pairs:
        v = proj[rows(c), head(C_V, h)]
        sv[c, h] = _dot(s_mat[c, h].astype(BF16), v.astype(BF16))
        wvk[c, h] = _dot((w_col[c, h] * v).T.astype(BF16), kb[c, h])
    pump(1)

    outs = [[] for _ in range(NH)]
    c_run = [state[h][0] for h in range(NH)]
    n_run = [state[h][1] for h in range(NH)]
    for c, h in pairs:
        q = proj[rows(c), head(C_Q, h)]
        qc = _dot_nt(qb[c, h], c_run[h].astype(BF16))
        qn = jnp.sum(q * n_run[h], axis=1, keepdims=True)
        num = sv[c, h] + inter[c, h] * qc
        den = jnp.sum(s_mat[c, h], axis=1, keepdims=True) + inter[c, h] * qn
        outs[h].append(num / jnp.maximum(jnp.abs(den), floor[c, h]))
        c_run[h] = decay[c, h] * c_run[h] + wvk[c, h]
        n_run[h] = decay[c, h] * n_run[h] + wk[c, h]
    return outs, [(c_run[h], n_run[h], m_run[h]) for h in range(NH)]


def _project(x, mod, g1, w_in):
    return _dot(_ada_norm(x, g1, mod).astype(BF16), w_in)


def _mix_tile(proj, x, gt1, j, tm, convw_ref, convb_ref, wgate_ref, bgate_ref,
              lam_ref, gbias_ref, ng_ref, poolw_ref, pscale_ref, wout_ref,
              xo_ref, hlast_ref, convo_ref, co_ref, no_ref, mo_ref, poolo_ref,
              h_sc, xbuf, ubuf, c_sc, n_sc, m_sc, pump=lambda n: None):
    row = lax.broadcasted_iota(jnp.int32, (tm, 1), 0)

    xr = proj[:, C_XR:C_XR + LRU_W]
    gr = proj[:, C_GR:C_GR + LRU_W]
    xbuf[CONV_HIST:CONV_HIST + tm, :] = xr
    cw = convw_ref[...]
    xc = convb_ref[...] + cw[CONV_TAPS - 1:CONV_TAPS] * xr
    for back in range(1, CONV_TAPS):
        xc = xc + cw[CONV_TAPS - 1 - back:CONV_TAPS - back] * xbuf[CONV_HIST - back:CONV_HIST - back + tm, :]
    tail = xbuf[tm:tm + CONV_HIST, :]
    xbuf[0:CONV_HIST, :] = tail
    convo_ref[0] = tail
    gates = _dot(xc.astype(BF16), wgate_ref[...]) + bgate_ref[...]
    pump(0)
    log_lam = LRU_C * _log_sigmoid(lam_ref[...])
    h_run = h_sc[...]
    y_a_parts = []
    n_split = 2
    for r in range(n_split):
        rs = slice(r * tm // n_split, (r + 1) * tm // n_split)
        a, mult, gate_i = _lru_coeffs(gates[rs], log_lam)
        if r == 0:
            mult = jnp.where(jnp.logical_and(row[rs] == 0, j == 0), 1.0, mult)
        pump(2)
        hseq, h_run = _scan_rows(a, mult * gate_i * xc[rs], h_run)
        pump(1)
        y_a_parts.append(hseq * _gelu_tanh(gr[rs]))
    h_sc[...] = h_run
    hlast_ref[0] = h_run
    y_a = jnp.concatenate(y_a_parts, axis=0)
    out_acc = _dot(y_a.astype(BF16), wout_ref[0:LRU_W, :])

    g_rows = (proj[:, C_Q:C_Q + HP] + gbias_ref[...]).T[G_OFF:G_OFF + 2 * NH]
    row8 = lax.broadcasted_iota(jnp.int32, g_rows.shape, 0)
    gl_t = jnp.where(row8 >= NH, _log_sigmoid(g_rows), g_rows)
    gl = jnp.concatenate([gl_t, jnp.zeros((HP - 2 * NH, tm), F32)], axis=0).T
    ng = ng_ref[...]
    state = [(c_sc[h], n_sc[h:h + 1, :], m_sc[h:h + 1, 0:1]) for h in range(NH)]
    outs, state = _mlstm_tile(proj, gl, gl_t, state, tm // CHUNK, pump)
    y_b_heads = []
    for h in range(NH):
        c_h, n_h, m_h = state[h]
        c_sc[h] = c_h
        n_sc[h:h + 1, :] = n_h
        m_sc[h:h + 1, :] = jnp.broadcast_to(m_h, (1, HP))
        co_ref[0, h] = c_h[0:HD, 0:HD]
        hout_all = jnp.concatenate(outs[h], axis=0) if len(outs[h]) > 1 else outs[h][0]
        o = proj[:, C_O + h * HP:C_O + (h + 1) * HP]
        y_b_heads.append(_head_norm_gate(hout_all, o, ng[:, h * HP:(h + 1) * HP]))
        pump(h % 2)
    no_ref[0] = n_sc[...]
    mo_ref[0] = m_sc[...]
    y_b = jnp.concatenate(y_b_heads, axis=1).astype(BF16)
    out_acc = out_acc + _dot(y_b, wout_ref[LRU_W:LRU_W + MWP, :])

    u = proj[:, C_U:C_U + PW]
    ubuf[POOL_HIST:POOL_HIST + tm, :] = u
    sums, acc, win = {}, ubuf[...], 1
    while win < POOL_WINDOWS[-1]:
        acc = acc + pltpu.roll(acc, win, 0)
        win *= 2
        sums[win] = acc[POOL_HIST:]
    wsum = _pool_select(sums)
    pos = row + j * tm
    cnt = jnp.minimum(pos + 1, _pool_window_lanes((tm, PW))).astype(F32)
    diff = wsum / cnt - u
    y_c = _dot(diff.astype(BF16), poolw_ref[...]) * pscale_ref[...]
    ptail = ubuf[tm:tm + POOL_HIST, :]
    ubuf[0:POOL_HIST, :] = ptail
    poolo_ref[0] = ptail
    pump(N_IN)

    out_acc = out_acc + _dot(y_c.astype(BF16), wout_ref[LRU_W + MWP:K_OUT, :])
    xo_ref[0] = x[...] + gt1 * out_acc


def _prompt_mixer_kernel(x_ref, xn_ref, mod_ref, modn_ref, g1_ref, win_ref, *rest, tm):
    n_scratch = 9
    weights_outs = rest[:-n_scratch]
    h_sc, xbuf, ubuf, c_sc, n_sc, m_sc, proj_a, proj_b, x_keep = rest[-n_scratch:]
    j = pl.program_id(1)
    t = pl.program_id(0) * pl.num_programs(1) + j

    @pl.when(j == 0)
    def _():
        h_sc[...] = jnp.zeros_like(h_sc)
        xbuf[0:CONV_HIST, :] = jnp.zeros((CONV_HIST, LRU_W), F32)
        ubuf[0:POOL_HIST, :] = jnp.zeros((POOL_HIST, PW), F32)
        c_sc[...] = jnp.zeros_like(c_sc)
        n_sc[...] = jnp.zeros_like(n_sc)
        m_sc[...] = jnp.zeros_like(m_sc)

    @pl.when(t == 0)
    def _():
        x_keep[...] = x_ref[0]
        proj_a[...] = _project(x_ref[0], mod_ref[...], g1_ref[...], win_ref[...])

    def step(proj_cur, proj_nxt):
        done = [0]
        lhs = []

        def pump(n):
            if not lhs:
                lhs.append(_ada_norm(xn_ref[0], g1_ref[...], modn_ref[...]).astype(BF16))
            for _ in range(n):
                c0 = done[0]
                if c0 >= N_IN:
                    return
                c1 = min(c0 + PROJ_PIECE, N_IN)
                proj_nxt[:, c0:c1] = _dot(lhs[0], win_ref[:, c0:c1])
                done[0] = c1

        gt1 = mod_ref[:, 2 * D_MODEL:3 * D_MODEL]
        _mix_tile(proj_cur, x_keep, gt1, j, tm, *weights_outs, h_sc, xbuf, ubuf, c_sc, n_sc, m_sc,
                  pump=pump)
        x_keep[...] = xn_ref[0]

    @pl.when(t % 2 == 0)
    def _():
        step(proj_a, proj_b)

    @pl.when(t % 2 == 1)
    def _():
        step(proj_b, proj_a)


def _prompt_mixer(x, modp, l, wts, tm):
    nb, seq, _ = x.shape
    n_t = seq // tm

    def next_tile(b, j):
        t1 = jnp.minimum(b * n_t + j + 1, nb * n_t - 1)
        return t1 // n_t, t1 % n_t

    lsel3 = lambda b, j: (l, 0, 0)
    wspec = lambda a: pl.BlockSpec((None,) + a.shape[1:], lsel3, pipeline_mode=pl.Buffered(1))
    names = ("g1", "w_in", "conv_w", "conv_b", "w_gate", "b_gate", "lam", "gbias", "ng", "pool_w",
             "pool_scale", "w_out")
    out_shapes = [
        jax.ShapeDtypeStruct((nb, seq, D_MODEL), F32),
        jax.ShapeDtypeStruct((nb, 1, LRU_W), F32),
        jax.ShapeDtypeStruct((nb, CONV_HIST, LRU_W), F32),
        jax.ShapeDtypeStruct((nb, NH, HD, HD), F32),
        jax.ShapeDtypeStruct((nb, SUBLANES, HP), F32),
        jax.ShapeDtypeStruct((nb, SUBLANES, HP), F32),
        jax.ShapeDtypeStruct((nb, POOL_HIST, PW), F32),
    ]
    out_specs = [
        pl.BlockSpec((1, tm, D_MODEL), lambda b, j: (b, j, 0)),
        pl.BlockSpec((1, 1, LRU_W), lambda b, j: (b, 0, 0)),
        pl.BlockSpec((1, CONV_HIST, LRU_W), lambda b, j: (b, 0, 0)),
        pl.BlockSpec((1, NH, HD, HD), lambda b, j: (b, 0, 0, 0)),
        pl.BlockSpec((1, SUBLANES, HP), lambda b, j: (b, 0, 0)),
        pl.BlockSpec((1, SUBLANES, HP), lambda b, j: (b, 0, 0)),
        pl.BlockSpec((1, POOL_HIST, PW), lambda b, j: (b, 0, 0)),
    ]
    return pl.pallas_call(
        functools.partial(_prompt_mixer_kernel, tm=tm),
        grid=(nb, seq // tm),
        in_specs=[
            pl.BlockSpec((1, tm, D_MODEL), lambda b, j: (0, 0, 0)),
            pl.BlockSpec((1, tm, D_MODEL), lambda b, j: next_tile(b, j) + (0,)),
            pl.BlockSpec((None, None, 1, 3 * D_MODEL), lambda b, j: (l, b, 0, 0)),
            pl.BlockSpec((None, None, 1, 3 * D_MODEL), lambda b, j: (l, next_tile(b, j)[0], 0, 0)),
        ] + [wspec(wts[n]) for n in names],
        out_specs=out_specs,
        out_shape=out_shapes,
        scratch_shapes=[
            pltpu.VMEM((1, LRU_W), F32),
            pltpu.VMEM((CONV_HIST + tm, LRU_W), F32),
            pltpu.VMEM((POOL_HIST + tm, PW), F32),
            pltpu.VMEM((NH, HP, HP), F32),
            pltpu.VMEM((SUBLANES, HP), F32),
            pltpu.VMEM((SUBLANES, HP), F32),
            pltpu.VMEM((tm, N_IN), F32),
            pltpu.VMEM((tm, N_IN), F32),
            pltpu.VMEM((tm, D_MODEL), F32),
        ],
        compiler_params=pltpu.CompilerParams(
            dimension_semantics=("arbitrary", "arbitrary"), vmem_limit_bytes=VMEM_LIMIT),
        name="prompt_mixer",
    )(x, x, modp, modp, *[wts[n] for n in names])


def _ffn_kernel(xp_ref, xs_ref, modp_ref, mods_ref, g2_ref, w1_ref, w2_ref, gf_ref, *rest,
                n_prompt_steps, final):
    if len(rest) == 6:
        w1n_ref, w2n_ref, op_ref, os_ref, w1o_ref, w2o_ref = rest
    else:
        (op_ref, os_ref), w1n_ref = rest, None

    def ffn(x_ref, mod_ref, o_ref):
        x = x_ref[...]
        mod = mod_ref[...]
        gt2 = mod[:, 2 * D_MODEL:3 * D_MODEL]
        h2 = _ada_norm(x, g2_ref[...], mod).astype(BF16)
        acc = jnp.zeros(x.shape, F32)
        step = FFN_CHUNK
        for c in range(D_FF // step):
            hid = _dot(h2, w1_ref[:, c * step:(c + 1) * step])
            hid = jnp.square(jnp.maximum(hid, 0.0)).astype(BF16)
            acc = acc + _dot(hid, w2_ref[c * step:(c + 1) * step, :])
        y = x + gt2 * acc
        if final:
            y = _rms(y, gf_ref[...])
        o_ref[...] = y

    i = pl.program_id(0)

    @pl.when(i < n_prompt_steps)
    def _():
        ffn(xp_ref, modp_ref, op_ref)
        if w1n_ref is not None:
            w1o_ref[...] = w1n_ref[...].astype(BF16)
            w2o_ref[...] = w2n_ref[...].astype(BF16)

    @pl.when(i == n_prompt_steps)
    def _():
        ffn(xs_ref, mods_ref, os_ref)


def _ffn(xp2d, xs, modp, mods, l, g2, w1, w2, next_w, final_g, tm, seq, final):
    n_tok = xp2d.shape[0]
    ns = xs.shape[0]
    n_steps = n_tok // tm
    lsel3 = lambda i: (l, 0, 0)
    wspec = lambda a: pl.BlockSpec(a.shape, lambda i: (0, 0), pipeline_mode=pl.Buffered(1))
    tile = lambda i: jnp.minimum(i, n_steps - 1)
    in_specs = [
        pl.BlockSpec((tm, D_MODEL), lambda i: (tile(i), 0)),
        pl.BlockSpec((ns, D_MODEL), lambda i: (0, 0)),
        pl.BlockSpec((None, None, 1, 3 * D_MODEL), lambda i: (l, tile(i) // (seq // tm), 0, 1)),
        pl.BlockSpec((None, ns, 3 * D_MODEL), lambda i: (l, 0, 1)),
        pl.BlockSpec((None, 1, D_MODEL), lsel3),
        wspec(w1),
        wspec(w2),
        pl.BlockSpec((1, D_MODEL), lambda i: (0, 0)),
    ]
    out_specs = [
        pl.BlockSpec((tm, D_MODEL), lambda i: (tile(i), 0)),
        pl.BlockSpec((ns, D_MODEL), lambda i: (0, 0)),
    ]
    out_shape = [
        jax.ShapeDtypeStruct((n_tok, D_MODEL), F32),
        jax.ShapeDtypeStruct((ns, D_MODEL), F32),
    ]
    args = [xp2d, xs, modp, mods, g2, w1, w2, final_g]
    if next_w is not None:
        slab = D_FF // n_steps
        in_specs += [
            pl.BlockSpec((None, D_MODEL, slab), lambda i: (l + 1, 0, tile(i))),
            pl.BlockSpec((None, slab, D_MODEL), lambda i: (l + 1, tile(i), 0)),
        ]
        out_specs += [
            pl.BlockSpec((D_MODEL, slab), lambda i: (0, tile(i))),
            pl.BlockSpec((slab, D_MODEL), lambda i: (tile(i), 0)),
        ]
        out_shape += [
            jax.ShapeDtypeStruct((D_MODEL, D_FF), BF16),
            jax.ShapeDtypeStruct((D_FF, D_MODEL), BF16),
        ]
        args += list(next_w)
    return pl.pallas_call(
        functools.partial(_ffn_kernel, n_prompt_steps=n_steps, final=final),
        grid=(n_steps + 1,),
        in_specs=in_specs,
        out_specs=out_specs,
        out_shape=out_shape,
        compiler_params=pltpu.CompilerParams(
            dimension_semantics=("arbitrary",), vmem_limit_bytes=VMEM_LIMIT),
        name="ffn",
    )(*args)


def _sample_in_kernel(x_ref, mod_ref, g1_ref, win_ref, gbias_ref, m0_ref,
                      proj_ref, qt_ref, kst_ref, vt_ref, wvt_ref, gs_ref, dec_ref, mo_ref):
    proj = _project(x_ref[...], mod_ref[...], g1_ref[...], win_ref[...])
    proj_ref[...] = proj
    g_t = (proj[:, C_Q:C_Q + HP] + gbias_ref[...]).T
    ib = g_t[G_OFF:G_OFF + NH]
    g = _log_sigmoid(g_t[G_OFF + NH:G_OFF + 2 * NH]) + m0_ref[...]
    m_t = jnp.maximum(g, ib)
    inter = jnp.exp(g - m_t)
    p = jnp.exp(ib - m_t)
    mo_ref[...] = m_t
    gs_ref[0] = inter
    gs_ref[1] = p
    gs_ref[2] = jnp.exp(-m_t)
    q_t = proj[:, C_Q:C_Q + MWP].T
    ks_t = (proj[:, C_K:C_K + MWP] * (1.0 / math.sqrt(HD))).T
    v_t = proj[:, C_V:C_V + MWP].T
    for h in range(NH):
        hs = slice(h * HP, (h + 1) * HP)
        qt_ref[h] = q_t[hs]
        kst_ref[h] = ks_t[hs]
        vt_ref[h] = v_t[hs]
        wvt_ref[h] = p[h:h + 1] * v_t[hs]
        dec_ref[h] = inter[h:h + 1]


def _sample_in(xs, mods, l, wts, m0_t):
    n = xs.shape[0]
    lsel3 = lambda i: (l, 0, 0)
    head_t = jax.ShapeDtypeStruct((NH, HP, n), F32)
    out_shapes = [
        jax.ShapeDtypeStruct((n, N_IN), F32),
        head_t, head_t, head_t, head_t,
        jax.ShapeDtypeStruct((3, NH, n), F32),
        jax.ShapeDtypeStruct((NH, 1, n), F32),
        jax.ShapeDtypeStruct((NH, n), F32),
    ]
    return pl.pallas_call(
        _sample_in_kernel,
        grid=(1,),
        in_specs=[
            pl.BlockSpec((n, D_MODEL), lambda i: (0, 0)),
            pl.BlockSpec((None, n, 2 * D_MODEL), lsel3),
            pl.BlockSpec((None, 1, D_MODEL), lsel3),
            pl.BlockSpec((None, D_MODEL, N_IN), lsel3),
            pl.BlockSpec((None, 1, HP), lsel3),
            pl.BlockSpec((None, NH, n), lsel3),
        ],
        out_specs=[pl.BlockSpec(s.shape, lambda i, nd=len(s.shape): (0,) * nd) for s in out_shapes],
        out_shape=out_shapes,
        compiler_params=pltpu.CompilerParams(
            dimension_semantics=("arbitrary",), vmem_limit_bytes=VMEM_LIMIT),
        name="sample_in_proj",
    )(xs, mods, wts["g1"], wts["w_in"], wts["gbias"], m0_t)


def _sample_state_kernel(c0_ref, qt_ref, kst_ref, wvt_ref, dec_ref, *rest, vb, n_fill):
    cnew_ref, cq_ref = rest[-2:]
    g = pl.program_id(0)

    @pl.when(g < n_fill)
    def _():
        cnew_ref[...] = jnp.zeros_like(cnew_ref)

    @pl.when(g == n_fill)
    def _():
        q = qt_ref[0:HD, :]
        ks = kst_ref[0:HD, :]
        dec = dec_ref[...]
        for vi in range(vb):
            c_old = c0_ref[vi]
            cq_ref[vi:vi + 1, :] = jnp.sum(c_old * q, axis=0, keepdims=True)
            cnew_ref[vi] = dec * c_old + wvt_ref[vi:vi + 1, :] * ks


def _sample_state(c0_t, c_out_prev, qt, kst, wvt, dec, l, vb):
    depth = c0_t.shape[0]
    n = c0_t.shape[-1]
    if c_out_prev is None:
        assert l == 0
        n_fill = depth - 1
    else:
        n_fill = 0
    real = lambda g: g // max(n_fill, 1) if n_fill else 1
    layer = lambda g: (g + 1) % depth if n_fill else l
    head_spec = pl.BlockSpec((None, HP, n), lambda g, h, i: (h, 0, 0))
    in_specs = [
        pl.BlockSpec((None, None, vb, HD, n), lambda g, h, i: (l, h * real(g), i * real(g), 0, 0)),
        head_spec, head_spec,
        pl.BlockSpec((None, vb, n), lambda g, h, i: (h, i, 0)),
        pl.BlockSpec((None, 1, n), lambda g, h, i: (h, 0, 0)),
    ]
    args = [c0_t, qt, kst, wvt, dec]
    aliases = {}
    if c_out_prev is not None:
        in_specs.append(pl.BlockSpec(memory_space=pl.ANY))
        args.append(c_out_prev)
        aliases = {len(args) - 1: 0}
    return pl.pallas_call(
        functools.partial(_sample_state_kernel, vb=vb, n_fill=n_fill),
        grid=(n_fill + 1, NH, HD // vb),
        in_specs=in_specs,
        out_specs=[
            pl.BlockSpec((None, None, vb, HD, n), lambda g, h, i: (layer(g), h, i, 0, 0)),
            pl.BlockSpec((None, vb, n), lambda g, h, i: (h * real(g), i * real(g), 0)),
        ],
        out_shape=[
            jax.ShapeDtypeStruct(c0_t.shape, F32),
            jax.ShapeDtypeStruct((NH, HD, n), F32),
        ],
        input_output_aliases=aliases,
        compiler_params=pltpu.CompilerParams(
            dimension_semantics=("arbitrary", "arbitrary", "arbitrary"), vmem_limit_bytes=VMEM_LIMIT),
        name="sample_matrix_memory",
    )(*args)


def _sample_mix_kernel(x_ref, gt_ref, proj_ref, qt_ref, kst_ref, vt_ref, cq_ref, gs_ref,
                       h0_ref, conv_ref, n0_ref, pool_ref,
                       convw_ref, convb_ref, wgate_ref, bgate_ref, lam_ref, ng_ref,
                       poolw_ref, pscale_ref, wout_ref,
                       xo_ref, ho_ref, convo_ref, no_ref, poolo_ref):
    proj = proj_ref[...]
    n_tok = proj.shape[0]

    xr = proj[:, C_XR:C_XR + LRU_W]
    gr = proj[:, C_GR:C_GR + LRU_W]
    cw = convw_ref[...]
    xc = (convb_ref[...] + cw[3:4] * xr + cw[2:3] * conv_ref[2]
          + cw[1:2] * conv_ref[1] + cw[0:1] * conv_ref[0])
    convo_ref[0] = conv_ref[1]
    convo_ref[1] = conv_ref[2]
    convo_ref[2] = xr
    gates = _dot(xc.astype(BF16), wgate_ref[...]) + bgate_ref[...]
    a, mult, gate_i = _lru_coeffs(gates, LRU_C * _log_sigmoid(lam_ref[...]))
    h_new = a * h0_ref[...] + mult * gate_i * xc
    ho_ref[...] = h_new
    y_a = h_new * _gelu_tanh(gr)

    normed = []
    for h in range(NH):
        q = qt_ref[h, 0:HD, :]
        ks = kst_ref[h, 0:HD, :]
        v = vt_ref[h, 0:HD, :]
        n0 = n0_ref[h]
        inter, p, floor = gs_ref[0, h:h + 1, :], gs_ref[1, h:h + 1, :], gs_ref[2, h:h + 1, :]
        s = jnp.sum(q * ks, axis=0, keepdims=True) * p
        num = s * v + inter * cq_ref[h]
        den = s + inter * jnp.sum(n0 * q, axis=0, keepdims=True)
        hout = num / jnp.maximum(jnp.abs(den), floor)
        no_ref[h] = inter * n0 + p * ks
        ms = jnp.sum(hout * hout, axis=0, keepdims=True) * (1.0 / HD)
        normed.append(hout * lax.rsqrt(ms + EPS))
        normed.append(jnp.zeros((HP - HD, n_tok), F32))
    hm = jnp.concatenate(normed, axis=0).T
    y_b = _sigmoid(proj[:, C_O:C_O + MWP]) * (hm * ng_ref[...])

    u = proj[:, C_U:C_U + PW]
    sums, acc, nxt = {}, u, 1
    for win in POOL_WINDOWS:
        while nxt < win:
            acc = acc + pool_ref[POOL_BUF - nxt]
            nxt += 1
        sums[win] = acc
    wsum = _pool_select(sums)
    cnt = jnp.minimum(PAST_LEN + 1, _pool_window_lanes(u.shape)).astype(F32)
    diff = wsum / cnt - u
    y_c = _dot(diff.astype(BF16), poolw_ref[...]) * pscale_ref[...]
    for r in range(POOL_BUF - 1):
        poolo_ref[r] = pool_ref[r + 1]
    poolo_ref[POOL_BUF - 1] = u

    mix = jnp.concatenate([y_a, y_b, y_c], axis=1).astype(BF16)
    xo_ref[...] = x_ref[...] + gt_ref[...] * _dot(mix, wout_ref[...])


def _sample_mix(xs, mods, proj, qt, kst, vt, cq, gs, l, st, wts):
    n = xs.shape[0]
    zeros = lambda nd: (lambda i: (0,) * nd)
    full = lambda a: pl.BlockSpec(a.shape, zeros(a.ndim))
    lspec = lambda a: pl.BlockSpec((None,) + a.shape[1:], lambda i, nd=a.ndim: (l,) + (0,) * (nd - 1))
    snames = ("lru_h", "conv", "n", "pool")
    wnames = ("conv_w", "conv_b", "w_gate", "b_gate", "lam", "ng", "pool_w", "pool_scale", "w_out")
    out_shapes = [
        jax.ShapeDtypeStruct((n, D_MODEL), F32),
        jax.ShapeDtypeStruct((n, LRU_W), F32),
        jax.ShapeDtypeStruct((3, n, LRU_W), F32),
        jax.ShapeDtypeStruct((NH, HD, n), F32),
        jax.ShapeDtypeStruct((POOL_BUF, n, PW), F32),
    ]
    return pl.pallas_call(
        _sample_mix_kernel,
        grid=(1,),
        in_specs=[
            full(xs),
            pl.BlockSpec((None, n, D_MODEL), lambda i: (l, 0, 2)),
            full(proj), full(qt), full(kst), full(vt), full(cq), full(gs),
        ] + [lspec(st[k]) for k in snames] + [lspec(wts[k]) for k in wnames],
        out_specs=[pl.BlockSpec(s.shape, zeros(len(s.shape))) for s in out_shapes],
        out_shape=out_shapes,
        compiler_params=pltpu.CompilerParams(
            dimension_semantics=("arbitrary",), vmem_limit_bytes=VMEM_LIMIT),
        name="sample_mixer",
    )(xs, mods, proj, qt, kst, vt, cq, gs, *[st[k] for k in snames], *[wts[k] for k in wnames])


def _pad_heads(w):
    lead = w.shape[:-1]
    w = w.reshape(lead + (NH, HD))
    w = jnp.pad(w, [(0, 0)] * len(lead) + [(0, 0), (0, HP - HD)])
    return w.reshape(lead + (MWP,))


def _block_diag(w):
    depth, g, n, _ = w.shape
    eye = jnp.eye(g, dtype=w.dtype)
    return (w[:, :, :, None, :] * eye[None, :, None, :, None]).reshape(depth, g * n, g * n)


def _permute_kernel(wint_ref, wout_ref, winp_ref, woutp_ref):
    o0 = 2 * LRU_W
    g0 = o0 + 4 * MW

    def put(dst_col, parts):
        blocks = [wint_ref[r0:r0 + n, :] for r0, n in parts]
        n_rows = sum(n for _, n in parts)
        if n_rows < HP:
            blocks.append(jnp.zeros((HP - n_rows, D_MODEL), F32))
        blk = jnp.concatenate(blocks, axis=0) if len(blocks) > 1 else blocks[0]
        winp_ref[:, dst_col:dst_col + HP] = blk.T.astype(BF16)

    for b in range(o0 // HP):
        put(b * HP, [(b * HP, HP)])
    for k, dst in enumerate((C_Q, C_K, C_V, C_O)):
        for h in range(NH):
            parts = [(o0 + k * MW + h * HD, HD)]
            if k == 0 and h == 0:
                parts.append((g0, 2 * NH))
            put(dst + h * HP, parts)
    for b in range(PW // HP):
        put(C_U + b * HP, [(g0 + 2 * NH + b * HP, HP)])

    woutp_ref[0:LRU_W, :] = wout_ref[0:LRU_W, :].astype(BF16)
    for h in range(NH):
        r0 = LRU_W + h * HP
        woutp_ref[r0:r0 + HD, :] = wout_ref[LRU_W + h * HD:LRU_W + (h + 1) * HD, :].astype(BF16)
        woutp_ref[r0 + HD:r0 + HP, :] = jnp.zeros((HP - HD, D_MODEL), BF16)
    woutp_ref[LRU_W + MWP:K_OUT, :] = wout_ref[LRU_W + MW:D_MODEL, :].astype(BF16)


def _permute_projections(w_in, w_out):
    depth, _, in_cols = w_in.shape
    lsel3 = lambda l: (l, 0, 0)
    return pl.pallas_call(
        _permute_kernel,
        grid=(depth,),
        in_specs=[
            pl.BlockSpec((None, in_cols, D_MODEL), lsel3),
            pl.BlockSpec((None, D_MODEL, D_MODEL), lsel3),
        ],
        out_specs=[
            pl.BlockSpec((None, D_MODEL, N_IN), lsel3),
            pl.BlockSpec((None, K_OUT, D_MODEL), lsel3),
        ],
        out_shape=[
            jax.ShapeDtypeStruct((depth, D_MODEL, N_IN), BF16),
            jax.ShapeDtypeStruct((depth, K_OUT, D_MODEL), BF16),
        ],
        compiler_params=pltpu.CompilerParams(
            dimension_semantics=("arbitrary",), vmem_limit_bytes=VMEM_LIMIT),
        name="permute_projections",
    )(jnp.transpose(w_in, (0, 2, 1)), w_out)


def _prepare_weights(norm1_g, norm2_g, w_in, conv_w, conv_b, lru_wa, lru_ba, lru_wx, lru_bx, lru_lam,
                     mlstm_bi, mlstm_bf, mlstm_norm_g, pool_w, pool_scale, w_out):
    depth = w_in.shape[0]
    w_in_p, w_out_p = _permute_projections(w_in, w_out)
    row = lambda v: v.reshape(depth, 1, -1)
    return {
        "g1": row(norm1_g), "g2": row(norm2_g),
        "w_in": w_in_p, "w_out": w_out_p,
        "conv_w": conv_w, "conv_b": row(conv_b),
        "w_gate": jnp.concatenate([_block_diag(lru_wa), _block_diag(lru_wx)], axis=-1).astype(BF16),
        "b_gate": jnp.concatenate([row(lru_ba), row(lru_bx)], axis=-1),
        "lam": row(lru_lam),
        "gbias": jnp.pad(jnp.concatenate([mlstm_bi, mlstm_bf], axis=-1),
                         ((0, 0), (G_OFF, HP - G_OFF - 2 * NH)))[:, None, :],
        "ng": row(_pad_heads(mlstm_norm_g.reshape(depth, MW))),
        "pool_w": _block_diag(pool_w).astype(BF16),
        "pool_scale": row(pool_scale),
    }


def kernel(x_prompt, x_sample, state_lru_h, state_lru_conv, state_mlstm_C, state_mlstm_n, state_mlstm_m,
           state_pool, c_prompt, c_sample, norm1_g, norm2_g, w_ada, b_ada, w_in, conv_w, conv_b, lru_wa,
           lru_ba, lru_wx, lru_bx, lru_lam, mlstm_bi, mlstm_bf, mlstm_norm_g, pool_w, pool_scale, w_out,
           w_ff1, w_ff2, final_g):
    nb, seq, _ = x_prompt.shape
    ns = x_sample.shape[0]
    depth = w_in.shape[0]
    wts = _prepare_weights(norm1_g, norm2_g, w_in, conv_w, conv_b, lru_wa, lru_ba, lru_wx, lru_bx, lru_lam,
                           mlstm_bi, mlstm_bf, mlstm_norm_g, pool_w, pool_scale, w_out)
    w1_bf, w2_bf = w_ff1[0].astype(BF16), w_ff2[0].astype(BF16)
    final_row = final_g.reshape(1, D_MODEL)
    modp, mods = _modulation(jnp.concatenate([c_prompt, c_sample], axis=0), w_ada,
                             b_ada.reshape(depth, 1, -1), nb)

    sample_state = {
        "lru_h": state_lru_h,
        "conv": jnp.transpose(state_lru_conv, (0, 2, 1, 3)),
        "n": jnp.transpose(state_mlstm_n, (0, 2, 3, 1)),
        "pool": jnp.transpose(state_pool, (0, 2, 1, 3)),
    }
    m0_t = jnp.transpose(state_mlstm_m, (0, 2, 1))
    c0_t = jnp.transpose(state_mlstm_C, (0, 2, 3, 4, 1))
    c_new_t = None

    xp = x_prompt
    xs = x_sample.reshape(ns, D_MODEL)
    p_out = [[] for _ in range(6)]
    s_out = [[] for _ in range(5)]
    for l in range(depth):
        last = l == depth - 1
        xp, h_l, conv_l, c_l, n_l, m_l, pool_l = _prompt_mixer(xp, modp, l, wts, TM_MIX)
        for lst, val in zip(p_out, (h_l.reshape(nb, LRU_W), conv_l[:, CONV_HIST - (CONV_TAPS - 1):], c_l, n_l[:, :NH, :HD],
                                    m_l[:, :NH, 0], pool_l[:, POOL_HIST - POOL_BUF:])):
            lst.append(val)
        proj, qt, kst, vt, wvt, gs, dec, m_s = _sample_in(xs, mods, l, wts, m0_t)
        c_new_t, cq = _sample_state(c0_t, c_new_t, qt, kst, wvt, dec, l, VB_STATE)
        xs, h_s, conv_s, n_s, pool_s = _sample_mix(xs, mods, proj, qt, kst, vt, cq, gs, l, sample_state, wts)
        for lst, val in zip(s_out, (h_s, conv_s, n_s, m_s, pool_s)):
            lst.append(val)
        res = _ffn(xp.reshape(nb * seq, D_MODEL), xs, modp, mods, l, wts["g2"], w1_bf, w2_bf,
                   None if last else (w_ff1, w_ff2), final_row, TM_FFN, seq, last)
        xp, xs = res[0].reshape(nb, seq, D_MODEL), res[1]
        if not last:
            w1_bf, w2_bf = res[2], res[3]

    stack = lambda lst: jnp.stack(lst, axis=0)
    h_s, conv_s, n_s, m_s, pool_s = [stack(v) for v in s_out]
    return (xp, xs.reshape(ns, 1, D_MODEL),
            *[stack(v) for v in p_out],
            h_s,
            jnp.transpose(conv_s, (0, 2, 1, 3)),
            jnp.transpose(c_new_t, (0, 4, 1, 2, 3)),
            jnp.transpose(n_s, (0, 3, 1, 2)),
            jnp.transpose(m_s, (0, 2, 1)),
            jnp.transpose(pool_s, (0, 2, 1, 3)))
```

```python
import functools
import math

import jax
import jax.numpy as jnp
from jax import lax
from jax.experimental import pallas as pl
from jax.experimental.pallas import tpu as pltpu

F32 = jnp.float32
BF16 = jnp.bfloat16

LANES = 128
SUBLANES = 8
MXU_TILE = 256
VMEM_BYTES = 64 * 1024 * 1024

D_MODEL = 1024
LRU_W = 384
LRU_C = 8.0
NH = 4
HD = 96
HP = LANES
MW = NH * HD
MWP = NH * HP
CHUNK = 128
PW = 256
POOL_WINDOWS = (2, 4, 8, 16)
POOL_GROUP = PW // len(POOL_WINDOWS)
POOL_BUF = 15
CONV_TAPS = 4
CONV_HIST = SUBLANES
POOL_HIST = 2 * SUBLANES
D_FF = 4096
EPS = 1e-6
PAST_LEN = 16384

C_XR, C_GR, C_Q, C_K, C_V, C_O, C_U = 0, 384, 768, 1280, 1792, 2304, 2816
N_IN = 3072
G_OFF = HD
K_OUT = LRU_W + MWP + PW

VMEM_LIMIT = VMEM_BYTES - 8 * 1024 * 1024
TM_MIX = 512
PROJ_PIECE = MXU_TILE
TM_FFN = 1024
FFN_CHUNK = 1024
MOD_TN = 3072
VB_STATE = HD


def _log_sigmoid(x):
    return jnp.minimum(x, 0.0) - jnp.log1p(jnp.exp(-jnp.abs(x)))


def _sigmoid(x):
    return 1.0 / (1.0 + jnp.exp(-x))


def _gelu_tanh(x):
    c = math.sqrt(2.0 / math.pi)
    return 0.5 * x * (1.0 + jnp.tanh(c * (x + 0.044715 * (x * x * x))))


def _rms(x, g):
    ms = jnp.mean(x * x, axis=-1, keepdims=True)
    return x * lax.rsqrt(ms + EPS) * g


def _ada_norm(x, g, mod):
    ms = jnp.mean(x * x, axis=-1, keepdims=True)
    gain = g * (1.0 + mod[:, D_MODEL:2 * D_MODEL])
    return x * lax.rsqrt(ms + EPS) * gain + mod[:, 0:D_MODEL]


def _dot(a, b):
    return jnp.dot(a, b, preferred_element_type=F32)


def _dot_nt(a, b):
    return lax.dot_general(a, b, (((1,), (1,)), ((), ())), preferred_element_type=F32)


def _lru_coeffs(gates, log_lam):
    r = _sigmoid(gates[:, :LRU_W])
    i = _sigmoid(gates[:, LRU_W:])
    log_a = r * log_lam
    a = jnp.exp(log_a)
    th = jnp.tanh(log_a)
    mult = jnp.sqrt(-2.0 * th / (1.0 - th))
    return a, mult, i


def _scan_rows(a, b, h_prev):
    n, w = a.shape
    groups = n // SUBLANES
    a3 = a.reshape(groups, SUBLANES, w)
    b3 = b.reshape(groups, SUBLANES, w)
    row = lax.broadcasted_iota(jnp.int32, a3.shape, 1)
    s = 1
    while s < SUBLANES:
        keep = row >= s
        b3 = a3 * jnp.where(keep, pltpu.roll(b3, s, 1), 0.0) + b3
        a3 = a3 * jnp.where(keep, pltpu.roll(a3, s, 1), 1.0)
        s *= 2
    carry = h_prev
    outs = []
    last = SUBLANES - 1
    for g in range(groups):
        outs.append(a3[g] * carry + b3[g])
        carry = a3[g, last:last + 1] * carry + b3[g, last:last + 1]
    return jnp.concatenate(outs, axis=0), carry


def _head_norm_gate(hout, o, ng):
    ms = jnp.sum(hout * hout, axis=-1, keepdims=True) * (1.0 / HD)
    return _sigmoid(o) * (hout * lax.rsqrt(ms + EPS) * ng)


def _pool_by_group(shape, values):
    lane = lax.broadcasted_iota(jnp.int32, shape, 1)
    out = values[-1]
    for g in range(len(values) - 2, -1, -1):
        out = jnp.where(lane < (g + 1) * POOL_GROUP, values[g], out)
    return out


def _pool_select(sums):
    vals = [sums[w] for w in POOL_WINDOWS]
    return _pool_by_group(vals[0].shape, vals)


def _pool_window_lanes(shape):
    return _pool_by_group(shape, list(POOL_WINDOWS))


def _mod_kernel(c_ref, w_ref, b_ref, modp_ref, mods_ref, *, n_prompt):
    c = c_ref[...]
    act = (c * _sigmoid(c)).astype(BF16)
    out = _dot(act, w_ref[...].astype(BF16)) + b_ref[...]
    for r in range(n_prompt):
        modp_ref[r] = out[r:r + 1]
    mods_ref[...] = out[n_prompt:]


def _modulation(c_all, w_ada, b_ada, n_prompt):
    depth = w_ada.shape[0]
    n_all = c_all.shape[0]
    n_mod = w_ada.shape[2]
    tn = MOD_TN
    return pl.pallas_call(
        functools.partial(_mod_kernel, n_prompt=n_prompt),
        grid=(depth, n_mod // tn),
        in_specs=[
            pl.BlockSpec((n_all, D_MODEL), lambda l, j: (0, 0)),
            pl.BlockSpec((None, D_MODEL, tn), lambda l, j: (l, 0, j)),
            pl.BlockSpec((None, 1, tn), lambda l, j: (l, 0, j)),
        ],
        out_specs=[
            pl.BlockSpec((None, n_prompt, 1, tn), lambda l, j: (l, 0, 0, j)),
            pl.BlockSpec((None, n_all - n_prompt, tn), lambda l, j: (l, 0, j)),
        ],
        out_shape=[
            jax.ShapeDtypeStruct((depth, n_prompt, 1, n_mod), F32),
            jax.ShapeDtypeStruct((depth, n_all - n_prompt, n_mod), F32),
        ],
        compiler_params=pltpu.CompilerParams(
            dimension_semantics=("arbitrary", "arbitrary"), vmem_limit_bytes=VMEM_LIMIT),
        name="adaln_modulation",
    )(c_all, w_ada, b_ada)


def _mlstm_tile(proj, gl, gl_t, state, n_chunks, pump):
    pairs = [(c, h) for c in range(n_chunks) for h in range(NH)]
    t_idx = lax.broadcasted_iota(jnp.int32, (CHUNK, CHUNK), 0)
    s_idx = lax.broadcasted_iota(jnp.int32, (CHUNK, CHUNK), 1)
    causal = s_idx <= t_idx
    upper = t_idx <= s_idx
    rows = lambda c: slice(c * CHUNK, (c + 1) * CHUNK)
    head = lambda base, h: slice(base + h * HP, base + (h + 1) * HP)

    bcum_col, dmat, rowmax, ib_col = {}, {}, {}, {}
    for c, h in pairs:
        lf_row = gl_t[NH + h:NH + h + 1, rows(c)]
        lf_col = gl[rows(c), NH + h:NH + h + 1]
        ib_col[c, h] = gl[rows(c), h:h + 1]
        bcum_col[c, h] = jnp.sum(jnp.where(causal, lf_row, 0.0), axis=1, keepdims=True)
        bcum_row = jnp.sum(jnp.where(upper, lf_col, 0.0), axis=0, keepdims=True)
        dmat[c, h] = jnp.where(causal, bcum_col[c, h] - bcum_row + gl_t[h:h + 1, rows(c)], -jnp.inf)
        rowmax[c, h] = jnp.max(dmat[c, h], axis=1, keepdims=True)
    pump(1)

    m_start, m_t = {}, {}
    m_run = [state[h][2] for h in range(NH)]
    for c, h in pairs:
        m_start[c, h] = m_run[h]
        m_t[c, h] = jnp.maximum(bcum_col[c, h] + m_run[h], rowmax[c, h])
        m_run[h] = m_t[c, h][CHUNK - 1:CHUNK, :]
    inter, pmat, floor, decay, w_col = {}, {}, {}, {}, {}
    for c, h in pairs:
        b_last = bcum_col[c, h][CHUNK - 1:CHUNK, :]
        inter[c, h] = jnp.exp(bcum_col[c, h] + m_start[c, h] - m_t[c, h])
        pmat[c, h] = jnp.exp(dmat[c, h] - m_t[c, h])
        floor[c, h] = jnp.exp(-m_t[c, h])
        m_new = m_t[c, h][CHUNK - 1:CHUNK, :]
        decay[c, h] = jnp.exp(b_last + m_start[c, h] - m_new)
        w_col[c, h] = jnp.exp(b_last - bcum_col[c, h] + ib_col[c, h] - m_new)
    pump(1)

    qb, kb, s_mat, sv, wvk, wk = {}, {}, {}, {}, {}, {}
    for c, h in pairs:
        qb[c, h] = proj[rows(c), head(C_Q, h)].astype(BF16)
        ks = proj[rows(c), head(C_K, h)] * (1.0 / math.sqrt(HD))
        kb[c, h] = ks.astype(BF16)
        wk[c, h] = jnp.sum(w_col[c, h] * ks, axis=0, keepdims=True)
        s_mat[c, h] = _dot_nt(qb[c, h], kb[c, h]) * pmat[c, h]
    pump(1)
    for c, h in pairs:
        v = proj[rows(c), head(C_V, h)]
        sv[c, h] = _dot(s_mat[c, h].astype(BF16), v.astype(BF16))
        wvk[c, h] = _dot((w_col[c, h] * v).T.astype(BF16), kb[c, h])
    pump(1)

    outs = [[] for _ in range(NH)]
    c_run = [state[h][0] for h in range(NH)]
    n_run = [state[h][1] for h in range(NH)]
    for c, h in pairs:
        q = proj[rows(c), head(C_Q, h)]
        qc = _dot_nt(qb[c, h], c_run[h].astype(BF16))
        qn = jnp.sum(q * n_run[h], axis=1, keepdims=True)
        num = sv[c, h] + inter[c, h] * qc
        den = jnp.sum(s_mat[c, h], axis=1, keepdims=True) + inter[c, h] * qn
        outs[h].append(num / jnp.maximum(jnp.abs(den), floor[c, h]))
        c_run[h] = decay[c, h] * c_run[h] + wvk[c, h]
        n_run[h] = decay[c, h] * n_run[h] + wk[c, h]
    return outs, [(c_run[h], n_run[h], m_run[h]) for h in range(NH)]


def _project(x, mod, g1, w_in):
    return _dot(_ada_norm(x, g1, mod).astype(BF16), w_in)


def _mix_tile(proj, x, gt1, j, tm, convw_ref, convb_ref, wgate_ref, bgate_ref,
              lam_ref, gbias_ref, ng_ref, poolw_ref, pscale_ref, wout_ref,
              xo_ref, hlast_ref, convo_ref, co_ref, no_ref, mo_ref, poolo_ref,
              h_sc, xbuf, ubuf, c_sc, n_sc, m_sc, pump=lambda n: None):
    row = lax.broadcasted_iota(jnp.int32, (tm, 1), 0)

    xr = proj[:, C_XR:C_XR + LRU_W]
    gr = proj[:, C_GR:C_GR + LRU_W]
    xbuf[CONV_HIST:CONV_HIST + tm, :] = xr
    cw = convw_ref[...]
    xc = convb_ref[...] + cw[CONV_TAPS - 1:CONV_TAPS] * xr
    for back in range(1, CONV_TAPS):
        xc = xc + cw[CONV_TAPS - 1 - back:CONV_TAPS - back] * xbuf[CONV_HIST - back:CONV_HIST - back + tm, :]
    tail = xbuf[tm:tm + CONV_HIST, :]
    xbuf[0:CONV_HIST, :] = tail
    convo_ref[0] = tail
    gates = _dot(xc.astype(BF16), wgate_ref[...]) + bgate_ref[...]
    pump(0)
    log_lam = LRU_C * _log_sigmoid(lam_ref[...])
    h_run = h_sc[...]
    y_a_parts = []
    n_split = 2
    for r in range(n_split):
        rs = slice(r * tm // n_split, (r + 1) * tm // n_split)
        a, mult, gate_i = _lru_coeffs(gates[rs], log_lam)
        if r == 0:
            mult = jnp.where(jnp.logical_and(row[rs] == 0, j == 0), 1.0, mult)
        pump(2)
        hseq, h_run = _scan_rows(a, mult * gate_i * xc[rs], h_run)
        pump(1)
        y_a_parts.append(hseq * _gelu_tanh(gr[rs]))
    h_sc[...] = h_run
    hlast_ref[0] = h_run
    y_a = jnp.concatenate(y_a_parts, axis=0)
    out_acc = _dot(y_a.astype(BF16), wout_ref[0:LRU_W, :])

    g_rows = (proj[:, C_Q:C_Q + HP] + gbias_ref[...]).T[G_OFF:G_OFF + 2 * NH]
    row8 = lax.broadcasted_iota(jnp.int32, g_rows.shape, 0)
    gl_t = jnp.where(row8 >= NH, _log_sigmoid(g_rows), g_rows)
    gl = jnp.concatenate([gl_t, jnp.zeros((HP - 2 * NH, tm), F32)], axis=0).T
    ng = ng_ref[...]
    state = [(c_sc[h], n_sc[h:h + 1, :], m_sc[h:h + 1, 0:1]) for h in range(NH)]
    outs, state = _mlstm_tile(proj, gl, gl_t, state, tm // CHUNK, pump)
    y_b_heads = []
    for h in range(NH):
        c_h, n_h, m_h = state[h]
        c_sc[h] = c_h
        n_sc[h:h + 1, :] = n_h
        m_sc[h:h + 1, :] = jnp.broadcast_to(m_h, (1, HP))
        co_ref[0, h] = c_h[0:HD, 0:HD]
        hout_all = jnp.concatenate(outs[h], axis=0) if len(outs[h]) > 1 else outs[h][0]
        o = proj[:, C_O + h * HP:C_O + (h + 1) * HP]
        y_b_heads.append(_head_norm_gate(hout_all, o, ng[:, h * HP:(h + 1) * HP]))
        pump(h % 2)
    no_ref[0] = n_sc[...]
    mo_ref[0] = m_sc[...]
    y_b = jnp.concatenate(y_b_heads, axis=1).astype(BF16)
    out_acc = out_acc + _dot(y_b, wout_ref[LRU_W:LRU_W + MWP, :])

    u = proj[:, C_U:C_U + PW]
    ubuf[POOL_HIST:POOL_HIST + tm, :] = u
    sums, acc, win = {}, ubuf[...], 1
    while win < POOL_WINDOWS[-1]:
        acc = acc + pltpu.roll(acc, win, 0)
        win *= 2
        sums[win] = acc[POOL_HIST:]
    wsum = _pool_select(sums)
    pos = row + j * tm
    cnt = jnp.minimum(pos + 1, _pool_window_lanes((tm, PW))).astype(F32)
    diff = wsum / cnt - u
    y_c = _dot(diff.astype(BF16), poolw_ref[...]) * pscale_ref[...]
    ptail = ubuf[tm:tm + POOL_HIST, :]
    ubuf[0:POOL_HIST, :] = ptail
    poolo_ref[0] = ptail
    pump(N_IN)

    out_acc = out_acc + _dot(y_c.astype(BF16), wout_ref[LRU_W + MWP:K_OUT, :])
    xo_ref[0] = x[...] + gt1 * out_acc


def _prompt_mixer_kernel(x_ref, xn_ref, mod_ref, modn_ref, g1_ref, win_ref, *rest, tm):
    n_scratch = 9
    weights_outs = rest[:-n_scratch]
    h_sc, xbuf, ubuf, c_sc, n_sc, m_sc, proj_a, proj_b, x_keep = rest[-n_scratch:]
    j = pl.program_id(1)
    t = pl.program_id(0) * pl.num_programs(1) + j

    @pl.when(j == 0)
    def _():
        h_sc[...] = jnp.zeros_like(h_sc)
        xbuf[0:CONV_HIST, :] = jnp.zeros((CONV_HIST, LRU_W), F32)
        ubuf[0:POOL_HIST, :] = jnp.zeros((POOL_HIST, PW), F32)
        c_sc[...] = jnp.zeros_like(c_sc)
        n_sc[...] = jnp.zeros_like(n_sc)
        m_sc[...] = jnp.zeros_like(m_sc)

    @pl.when(t == 0)
    def _():
        x_keep[...] = x_ref[0]
        proj_a[...] = _project(x_ref[0], mod_ref[...], g1_ref[...], win_ref[...])

    def step(proj_cur, proj_nxt):
        done = [0]
        lhs = []

        def pump(n):
            if not lhs:
                lhs.append(_ada_norm(xn_ref[0], g1_ref[...], modn_ref[...]).astype(BF16))
            for _ in range(n):
                c0 = done[0]
                if c0 >= N_IN:
                    return
                c1 = min(c0 + PROJ_PIECE, N_IN)
                proj_nxt[:, c0:c1] = _dot(lhs[0], win_ref[:, c0:c1])
                done[0] = c1

        gt1 = mod_ref[:, 2 * D_MODEL:3 * D_MODEL]
        _mix_tile(proj_cur, x_keep, gt1, j, tm, *weights_outs, h_sc, xbuf, ubuf, c_sc, n_sc, m_sc,
                  pump=pump)
        x_keep[...] = xn_ref[0]

    @pl.when(t % 2 == 0)
    def _():
        step(proj_a, proj_b)

    @pl.when(t % 2 == 1)
    def _():
        step(proj_b, proj_a)


def _prompt_mixer(x, modp, l, wts, tm):
    nb, seq, _ = x.shape
    n_t = seq // tm

    def next_tile(b, j):
        t1 = jnp.minimum(b * n_t + j + 1, nb * n_t - 1)
        return t1 // n_t, t1 % n_t

    lsel3 = lambda b, j: (l, 0, 0)
    wspec = lambda a: pl.BlockSpec((None,) + a.shape[1:], lsel3, pipeline_mode=pl.Buffered(1))
    names = ("g1", "w_in", "conv_w", "conv_b", "w_gate", "b_gate", "lam", "gbias", "ng", "pool_w",
             "pool_scale", "w_out")
    out_shapes = [
        jax.ShapeDtypeStruct((nb, seq, D_MODEL), F32),
        jax.ShapeDtypeStruct((nb, 1, LRU_W), F32),
        jax.ShapeDtypeStruct((nb, CONV_HIST, LRU_W), F32),
        jax.ShapeDtypeStruct((nb, NH, HD, HD), F32),
        jax.ShapeDtypeStruct((nb, SUBLANES, HP), F32),
        jax.ShapeDtypeStruct((nb, SUBLANES, HP), F32),
        jax.ShapeDtypeStruct((nb, POOL_HIST, PW), F32),
    ]
    out_specs = [
        pl.BlockSpec((1, tm, D_MODEL), lambda b, j: (b, j, 0)),
        pl.BlockSpec((1, 1, LRU_W), lambda b, j: (b, 0, 0)),
        pl.BlockSpec((1, CONV_HIST, LRU_W), lambda b, j: (b, 0, 0)),
        pl.BlockSpec((1, NH, HD, HD), lambda b, j: (b, 0, 0, 0)),
        pl.BlockSpec((1, SUBLANES, HP), lambda b, j: (b, 0, 0)),
        pl.BlockSpec((1, SUBLANES, HP), lambda b, j: (b, 0, 0)),
        pl.BlockSpec((1, POOL_HIST, PW), lambda b, j: (b, 0, 0)),
    ]
    return pl.pallas_call(
        functools.partial(_prompt_mixer_kernel, tm=tm),
        grid=(nb, seq // tm),
        in_specs=[
            pl.BlockSpec((1, tm, D_MODEL), lambda b, j: (0, 0, 0)),
            pl.BlockSpec((1, tm, D_MODEL), lambda b, j: next_tile(b, j) + (0,)),
            pl.BlockSpec((None, None, 1, 3 * D_MODEL), lambda b, j: (l, b, 0, 0)),
            pl.BlockSpec((None, None, 1, 3 * D_MODEL), lambda b, j: (l, next_tile(b, j)[0], 0, 0)),
        ] + [wspec(wts[n]) for n in names],
        out_specs=out_specs,
        out_shape=out_shapes,
        scratch_shapes=[
            pltpu.VMEM((1, LRU_W), F32),
            pltpu.VMEM((CONV_HIST + tm, LRU_W), F32),
            pltpu.VMEM((POOL_HIST + tm, PW), F32),
            pltpu.VMEM((NH, HP, HP), F32),
            pltpu.VMEM((SUBLANES, HP), F32),
            pltpu.VMEM((SUBLANES, HP), F32),
            pltpu.VMEM((tm, N_IN), F32),
            pltpu.VMEM((tm, N_IN), F32),
            pltpu.VMEM((tm, D_MODEL), F32),
        ],
        compiler_params=pltpu.CompilerParams(
            dimension_semantics=("arbitrary", "arbitrary"), vmem_limit_bytes=VMEM_LIMIT),
        name="prompt_mixer",
    )(x, x, modp, modp, *[wts[n] for n in names])


def _ffn_kernel(xp_ref, xs_ref, modp_ref, mods_ref, g2_ref, w1_ref, w2_ref, gf_ref, *rest,
                n_prompt_steps, final):
    if len(rest) == 6:
        w1n_ref, w2n_ref, op_ref, os_ref, w1o_ref, w2o_ref = rest
    else:
        (op_ref, os_ref), w1n_ref = rest, None

    def ffn(x_ref, mod_ref, o_ref):
        x = x_ref[...]
        mod = mod_ref[...]
        gt2 = mod[:, 2 * D_MODEL:3 * D_MODEL]
        h2 = _ada_norm(x, g2_ref[...], mod).astype(BF16)
        acc = jnp.zeros(x.shape, F32)
        step = FFN_CHUNK
        for c in range(D_FF // step):
            hid = _dot(h2, w1_ref[:, c * step:(c + 1) * step])
            hid = jnp.square(jnp.maximum(hid, 0.0)).astype(BF16)
            acc = acc + _dot(hid, w2_ref[c * step:(c + 1) * step, :])
        y = x + gt2 * acc
        if final:
            y = _rms(y, gf_ref[...])
        o_ref[...] = y

    i = pl.program_id(0)

    @pl.when(i < n_prompt_steps)
    def _():
        ffn(xp_ref, modp_ref, op_ref)
        if w1n_ref is not None:
            w1o_ref[...] = w1n_ref[...].astype(BF16)
            w2o_ref[...] = w2n_ref[...].astype(BF16)

    @pl.when(i == n_prompt_steps)
    def _():
        ffn(xs_ref, mods_ref, os_ref)


def _ffn(xp2d, xs, modp, mods, l, g2, w1, w2, next_w, final_g, tm, seq, final):
    n_tok = xp2d.shape[0]
    ns = xs.shape[0]
    n_steps = n_tok // tm
    lsel3 = lambda i: (l, 0, 0)
    wspec = lambda a: pl.BlockSpec(a.shape, lambda i: (0, 0), pipeline_mode=pl.Buffered(1))
    tile = lambda i: jnp.minimum(i, n_steps - 1)
    in_specs = [
        pl.BlockSpec((tm, D_MODEL), lambda i: (tile(i), 0)),
        pl.BlockSpec((ns, D_MODEL), lambda i: (0, 0)),
        pl.BlockSpec((None, None, 1, 3 * D_MODEL), lambda i: (l, tile(i) // (seq // tm), 0, 1)),
        pl.BlockSpec((None, ns, 3 * D_MODEL), lambda i: (l, 0, 1)),
        pl.BlockSpec((None, 1, D_MODEL), lsel3),
        wspec(w1),
        wspec(w2),
        pl.BlockSpec((1, D_MODEL), lambda i: (0, 0)),
    ]
    out_specs = [
        pl.BlockSpec((tm, D_MODEL), lambda i: (tile(i), 0)),
        pl.BlockSpec((ns, D_MODEL), lambda i: (0, 0)),
    ]
    out_shape = [
        jax.ShapeDtypeStruct((n_tok, D_MODEL), F32),
        jax.ShapeDtypeStruct((ns, D_MODEL), F32),
    ]
    args = [xp2d, xs, modp, mods, g2, w1, w2, final_g]
    if next_w is not None:
        slab = D_FF // n_steps
        in_specs += [
            pl.BlockSpec((None, D_MODEL, slab), lambda i: (l + 1, 0, tile(i))),
            pl.BlockSpec((None, slab, D_MODEL), lambda i: (l + 1, tile(i), 0)),
        ]
        out_specs += [
            pl.BlockSpec((D_MODEL, slab), lambda i: (0, tile(i))),
            pl.BlockSpec((slab, D_MODEL), lambda i: (tile(i), 0)),
        ]
        out_shape += [
            jax.ShapeDtypeStruct((D_MODEL, D_FF), BF16),
            jax.ShapeDtypeStruct((D_FF, D_MODEL), BF16),
        ]
        args += list(next_w)
    return pl.pallas_call(
        functools.partial(_ffn_kernel, n_prompt_steps=n_steps, final=final),
        grid=(n_steps + 1,),
        in_specs=in_specs,
        out_specs=out_specs,
        out_shape=out_shape,
        compiler_params=pltpu.CompilerParams(
            dimension_semantics=("arbitrary",), vmem_limit_bytes=VMEM_LIMIT),
        name="ffn",
    )(*args)


def _sample_in_kernel(x_ref, mod_ref, g1_ref, win_ref, gbias_ref, m0_ref,
                      proj_ref, qt_ref, kst_ref, vt_ref, wvt_ref, gs_ref, dec_ref, mo_ref):
    proj = _project(x_ref[...], mod_ref[...], g1_ref[...], win_ref[...])
    proj_ref[...] = proj
    g_t = (proj[:, C_Q:C_Q + HP] + gbias_ref[...]).T
    ib = g_t[G_OFF:G_OFF + NH]
    g = _log_sigmoid(g_t[G_OFF + NH:G_OFF + 2 * NH]) + m0_ref[...]
    m_t = jnp.maximum(g, ib)
    inter = jnp.exp(g - m_t)
    p = jnp.exp(ib - m_t)
    mo_ref[...] = m_t
    gs_ref[0] = inter
    gs_ref[1] = p
    gs_ref[2] = jnp.exp(-m_t)
    q_t = proj[:, C_Q:C_Q + MWP].T
    ks_t = (proj[:, C_K:C_K + MWP] * (1.0 / math.sqrt(HD))).T
    v_t = proj[:, C_V:C_V + MWP].T
    for h in range(NH):
        hs = slice(h * HP, (h + 1) * HP)
        qt_ref[h] = q_t[hs]
        kst_ref[h] = ks_t[hs]
        vt_ref[h] = v_t[hs]
        wvt_ref[h] = p[h:h + 1] * v_t[hs]
        dec_ref[h] = inter[h:h + 1]


def _sample_in(xs, mods, l, wts, m0_t):
    n = xs.shape[0]
    lsel3 = lambda i: (l, 0, 0)
    head_t = jax.ShapeDtypeStruct((NH, HP, n), F32)
    out_shapes = [
        jax.ShapeDtypeStruct((n, N_IN), F32),
        head_t, head_t, head_t, head_t,
        jax.ShapeDtypeStruct((3, NH, n), F32),
        jax.ShapeDtypeStruct((NH, 1, n), F32),
        jax.ShapeDtypeStruct((NH, n), F32),
    ]
    return pl.pallas_call(
        _sample_in_kernel,
        grid=(1,),
        in_specs=[
            pl.BlockSpec((n, D_MODEL), lambda i: (0, 0)),
            pl.BlockSpec((None, n, 2 * D_MODEL), lsel3),
            pl.BlockSpec((None, 1, D_MODEL), lsel3),
            pl.BlockSpec((None, D_MODEL, N_IN), lsel3),
            pl.BlockSpec((None, 1, HP), lsel3),
            pl.BlockSpec((None, NH, n), lsel3),
        ],
        out_specs=[pl.BlockSpec(s.shape, lambda i, nd=len(s.shape): (0,) * nd) for s in out_shapes],
        out_shape=out_shapes,
        compiler_params=pltpu.CompilerParams(
            dimension_semantics=("arbitrary",), vmem_limit_bytes=VMEM_LIMIT),
        name="sample_in_proj",
    )(xs, mods, wts["g1"], wts["w_in"], wts["gbias"], m0_t)


def _sample_state_kernel(c0_ref, qt_ref, kst_ref, wvt_ref, dec_ref, *rest, vb, n_fill):
    cnew_ref, cq_ref = rest[-2:]
    g = pl.program_id(0)

    @pl.when(g < n_fill)
    def _():
        cnew_ref[...] = jnp.zeros_like(cnew_ref)

    @pl.when(g == n_fill)
    def _():
        q = qt_ref[0:HD, :]
        ks = kst_ref[0:HD, :]
        dec = dec_ref[...]
        for vi in range(vb):
            c_old = c0_ref[vi]
            cq_ref[vi:vi + 1, :] = jnp.sum(c_old * q, axis=0, keepdims=True)
            cnew_ref[vi] = dec * c_old + wvt_ref[vi:vi + 1, :] * ks


def _sample_state(c0_t, c_out_prev, qt, kst, wvt, dec, l, vb):
    depth = c0_t.shape[0]
    n = c0_t.shape[-1]
    if c_out_prev is None:
        assert l == 0
        n_fill = depth - 1
    else:
        n_fill = 0
    real = lambda g: g // max(n_fill, 1) if n_fill else 1
    layer = lambda g: (g + 1) % depth if n_fill else l
    head_spec = pl.BlockSpec((None, HP, n), lambda g, h, i: (h, 0, 0))
    in_specs = [
        pl.BlockSpec((None, None, vb, HD, n), lambda g, h, i: (l, h * real(g), i * real(g), 0, 0)),
        head_spec, head_spec,
        pl.BlockSpec((None, vb, n), lambda g, h, i: (h, i, 0)),
        pl.BlockSpec((None, 1, n), lambda g, h, i: (h, 0, 0)),
    ]
    args = [c0_t, qt, kst, wvt, dec]
    aliases = {}
    if c_out_prev is not None:
        in_specs.append(pl.BlockSpec(memory_space=pl.ANY))
        args.append(c_out_prev)
        aliases = {len(args) - 1: 0}
    return pl.pallas_call(
        functools.partial(_sample_state_kernel, vb=vb, n_fill=n_fill),
        grid=(n_fill + 1, NH, HD // vb),
        in_specs=in_specs,
        out_specs=[
            pl.BlockSpec((None, None, vb, HD, n), lambda g, h, i: (layer(g), h, i, 0, 0)),
            pl.BlockSpec((None, vb, n), lambda g, h, i: (h * real(g), i * real(g), 0)),
        ],
        out_shape=[
            jax.ShapeDtypeStruct(c0_t.shape, F32),
            jax.ShapeDtypeStruct((NH, HD, n), F32),
        ],
        input_output_aliases=aliases,
        compiler_params=pltpu.CompilerParams(
            dimension_semantics=("arbitrary", "arbitrary", "arbitrary"), vmem_limit_bytes=VMEM_LIMIT),
        name="sample_matrix_memory",
    )(*args)


def _sample_mix_kernel(x_ref, gt_ref, proj_ref, qt_ref, kst_ref, vt_ref, cq_ref, gs_ref,
                       h0_ref, conv_ref, n0_ref, pool_ref,
                       convw_ref, convb_ref, wgate_ref, bgate_ref, lam_ref, ng_ref,
                       poolw_ref, pscale_ref, wout_ref,
                       xo_ref, ho_ref, convo_ref, no_ref, poolo_ref):
    proj = proj_ref[...]
    n_tok = proj.shape[0]

    xr = proj[:, C_XR:C_XR + LRU_W]
    gr = proj[:, C_GR:C_GR + LRU_W]
    cw = convw_ref[...]
    xc = (convb_ref[...] + cw[3:4] * xr + cw[2:3] * conv_ref[2]
          + cw[1:2] * conv_ref[1] + cw[0:1] * conv_ref[0])
    convo_ref[0] = conv_ref[1]
    convo_ref[1] = conv_ref[2]
    convo_ref[2] = xr
    gates = _dot(xc.astype(BF16), wgate_ref[...]) + bgate_ref[...]
    a, mult, gate_i = _lru_coeffs(gates, LRU_C * _log_sigmoid(lam_ref[...]))
    h_new = a * h0_ref[...] + mult * gate_i * xc
    ho_ref[...] = h_new
    y_a = h_new * _gelu_tanh(gr)

    normed = []
    for h in range(NH):
        q = qt_ref[h, 0:HD, :]
        ks = kst_ref[h, 0:HD, :]
        v = vt_ref[h, 0:HD, :]
        n0 = n0_ref[h]
        inter, p, floor = gs_ref[0, h:h + 1, :], gs_ref[1, h:h + 1, :], gs_ref[2, h:h + 1, :]
        s = jnp.sum(q * ks, axis=0, keepdims=True) * p
        num = s * v + inter * cq_ref[h]
        den = s + inter * jnp.sum(n0 * q, axis=0, keepdims=True)
        hout = num / jnp.maximum(jnp.abs(den), floor)
        no_ref[h] = inter * n0 + p * ks
        ms = jnp.sum(hout * hout, axis=0, keepdims=True) * (1.0 / HD)
        normed.append(hout * lax.rsqrt(ms + EPS))
        normed.append(jnp.zeros((HP - HD, n_tok), F32))
    hm = jnp.concatenate(normed, axis=0).T
    y_b = _sigmoid(proj[:, C_O:C_O + MWP]) * (hm * ng_ref[...])

    u = proj[:, C_U:C_U + PW]
    sums, acc, nxt = {}, u, 1
    for win in POOL_WINDOWS:
        while nxt < win:
            acc = acc + pool_ref[POOL_BUF - nxt]
            nxt += 1
        sums[win] = acc
    wsum = _pool_select(sums)
    cnt = jnp.minimum(PAST_LEN + 1, _pool_window_lanes(u.shape)).astype(F32)
    diff = wsum / cnt - u
    y_c = _dot(diff.astype(BF16), poolw_ref[...]) * pscale_ref[...]
    for r in range(POOL_BUF - 1):
        poolo_ref[r] = pool_ref[r + 1]
    poolo_ref[POOL_BUF - 1] = u

    mix = jnp.concatenate([y_a, y_b, y_c], axis=1).astype(BF16)
    xo_ref[...] = x_ref[...] + gt_ref[...] * _dot(mix, wout_ref[...])


def _sample_mix(xs, mods, proj, qt, kst, vt, cq, gs, l, st, wts):
    n = xs.shape[0]
    zeros = lambda nd: (lambda i: (0,) * nd)
    full = lambda a: pl.BlockSpec(a.shape, zeros(a.ndim))
    lspec = lambda a: pl.BlockSpec((None,) + a.shape[1:], lambda i, nd=a.ndim: (l,) + (0,) * (nd - 1))
    snames = ("lru_h", "conv", "n", "pool")
    wnames = ("conv_w", "conv_b", "w_gate", "b_gate", "lam", "ng", "pool_w", "pool_scale", "w_out")
    out_shapes = [
        jax.ShapeDtypeStruct((n, D_MODEL), F32),
        jax.ShapeDtypeStruct((n, LRU_W), F32),
        jax.ShapeDtypeStruct((3, n, LRU_W), F32),
        jax.ShapeDtypeStruct((NH, HD, n), F32),
        jax.ShapeDtypeStruct((POOL_BUF, n, PW), F32),
    ]
    return pl.pallas_call(
        _sample_mix_kernel,
        grid=(1,),
        in_specs=[
            full(xs),
            pl.BlockSpec((None, n, D_MODEL), lambda i: (l, 0, 2)),
            full(proj), full(qt), full(kst), full(vt), full(cq), full(gs),
        ] + [lspec(st[k]) for k in snames] + [lspec(wts[k]) for k in wnames],
        out_specs=[pl.BlockSpec(s.shape, zeros(len(s.shape))) for s in out_shapes],
        out_shape=out_shapes,
        compiler_params=pltpu.CompilerParams(
            dimension_semantics=("arbitrary",), vmem_limit_bytes=VMEM_LIMIT),
        name="sample_mixer",
    )(xs, mods, proj, qt, kst, vt, cq, gs, *[st[k] for k in snames], *[wts[k] for k in wnames])


def _pad_heads(w):
    lead = w.shape[:-1]
    w = w.reshape(lead + (NH, HD))
    w = jnp.pad(w, [(0, 0)] * len(lead) + [(0, 0), (0, HP - HD)])
    return w.reshape(lead + (MWP,))


def _block_diag(w):
    depth, g, n, _ = w.shape
    eye = jnp.eye(g, dtype=w.dtype)
    return (w[:, :, :, None, :] * eye[None, :, None, :, None]).reshape(depth, g * n, g * n)


def _permute_kernel(wint_ref, wout_ref, winp_ref, woutp_ref):
    o0 = 2 * LRU_W
    g0 = o0 + 4 * MW

    def put(dst_col, parts):
        blocks = [wint_ref[r0:r0 + n, :] for r0, n in parts]
        n_rows = sum(n for _, n in parts)
        if n_rows < HP:
            blocks.append(jnp.zeros((HP - n_rows, D_MODEL), F32))
        blk = jnp.concatenate(blocks, axis=0) if len(blocks) > 1 else blocks[0]
        winp_ref[:, dst_col:dst_col + HP] = blk.T.astype(BF16)

    for b in range(o0 // HP):
        put(b * HP, [(b * HP, HP)])
    for k, dst in enumerate((C_Q, C_K, C_V, C_O)):
        for h in range(NH):
            parts = [(o0 + k * MW + h * HD, HD)]
            if k == 0 and h == 0:
                parts.append((g0, 2 * NH))
            put(dst + h * HP, parts)
    for b in range(PW // HP):
        put(C_U + b * HP, [(g0 + 2 * NH + b * HP, HP)])

    woutp_ref[0:LRU_W, :] = wout_ref[0:LRU_W, :].astype(BF16)
    for h in range(NH):
        r0 = LRU_W + h * HP
        woutp_ref[r0:r0 + HD, :] = wout_ref[LRU_W + h * HD:LRU_W + (h + 1) * HD, :].astype(BF16)
        woutp_ref[r0 + HD:r0 + HP, :] = jnp.zeros((HP - HD, D_MODEL), BF16)
    woutp_ref[LRU_W + MWP:K_OUT, :] = wout_ref[LRU_W + MW:D_MODEL, :].astype(BF16)


def _permute_projections(w_in, w_out):
    depth, _, in_cols = w_in.shape
    lsel3 = lambda l: (l, 0, 0)
    return pl.pallas_call(
        _permute_kernel,
        grid=(depth,),
        in_specs=[
            pl.BlockSpec((None, in_cols, D_MODEL), lsel3),
            pl.BlockSpec((None, D_MODEL, D_MODEL), lsel3),
        ],
        out_specs=[
            pl.BlockSpec((None, D_MODEL, N_IN), lsel3),
            pl.BlockSpec((None, K_OUT, D_MODEL), lsel3),
        ],
        out_shape=[
            jax.ShapeDtypeStruct((depth, D_MODEL, N_IN), BF16),
            jax.ShapeDtypeStruct((depth, K_OUT, D_MODEL), BF16),
        ],
        compiler_params=pltpu.CompilerParams(
            dimension_semantics=("arbitrary",), vmem_limit_bytes=VMEM_LIMIT),
        name="permute_projections",
    )(jnp.transpose(w_in, (0, 2, 1)), w_out)


def _prepare_weights(norm1_g, norm2_g, w_in, conv_w, conv_b, lru_wa, lru_ba, lru_wx, lru_bx, lru_lam,
                     mlstm_bi, mlstm_bf, mlstm_norm_g, pool_w, pool_scale, w_out):
    depth = w_in.shape[0]
    w_in_p, w_out_p = _permute_projections(w_in, w_out)
    row = lambda v: v.reshape(depth, 1, -1)
    return {
        "g1": row(norm1_g), "g2": row(norm2_g),
        "w_in": w_in_p, "w_out": w_out_p,
        "conv_w": conv_w, "conv_b": row(conv_b),
        "w_gate": jnp.concatenate([_block_diag(lru_wa), _block_diag(lru_wx)], axis=-1).astype(BF16),
        "b_gate": jnp.concatenate([row(lru_ba), row(lru_bx)], axis=-1),
        "lam": row(lru_lam),
        "gbias": jnp.pad(jnp.concatenate([mlstm_bi, mlstm_bf], axis=-1),
                         ((0, 0), (G_OFF, HP - G_OFF - 2 * NH)))[:, None, :],
        "ng": row(_pad_heads(mlstm_norm_g.reshape(depth, MW))),
        "pool_w": _block_diag(pool_w).astype(BF16),
        "pool_scale": row(pool_scale),
    }


def kernel(x_prompt, x_sample, state_lru_h, state_lru_conv, state_mlstm_C, state_mlstm_n, state_mlstm_m,
           state_pool, c_prompt, c_sample, norm1_g, norm2_g, w_ada, b_ada, w_in, conv_w, conv_b, lru_wa,
           lru_ba, lru_wx, lru_bx, lru_lam, mlstm_bi, mlstm_bf, mlstm_norm_g, pool_w, pool_scale, w_out,
           w_ff1, w_ff2, final_g):
    nb, seq, _ = x_prompt.shape
    ns = x_sample.shape[0]
    depth = w_in.shape[0]
    wts = _prepare_weights(norm1_g, norm2_g, w_in, conv_w, conv_b, lru_wa, lru_ba, lru_wx, lru_bx, lru_lam,
                           mlstm_bi, mlstm_bf, mlstm_norm_g, pool_w, pool_scale, w_out)
    w1_bf, w2_bf = w_ff1[0].astype(BF16), w_ff2[0].astype(BF16)
    final_row = final_g.reshape(1, D_MODEL)
    modp, mods = _modulation(jnp.concatenate([c_prompt, c_sample], axis=0), w_ada,
                             b_ada.reshape(depth, 1, -1), nb)

    sample_state = {
        "lru_h": state_lru_h,
        "conv": jnp.transpose(state_lru_conv, (0, 2, 1, 3)),
        "n": jnp.transpose(state_mlstm_n, (0, 2, 3, 1)),
        "pool": jnp.transpose(state_pool, (0, 2, 1, 3)),
    }
    m0_t = jnp.transpose(state_mlstm_m, (0, 2, 1))
    c0_t = jnp.transpose(state_mlstm_C, (0, 2, 3, 4, 1))
    c_new_t = None

    xp = x_prompt
    xs = x_sample.reshape(ns, D_MODEL)
    p_out = [[] for _ in range(6)]
    s_out = [[] for _ in range(5)]
    for l in range(depth):
        last = l == depth - 1
        xp, h_l, conv_l, c_l, n_l, m_l, pool_l = _prompt_mixer(xp, modp, l, wts, TM_MIX)
        for lst, val in zip(p_out, (h_l.reshape(nb, LRU_W), conv_l[:, CONV_HIST - (CONV_TAPS - 1):], c_l, n_l[:, :NH, :HD],
                                    m_l[:, :NH, 0], pool_l[:, POOL_HIST - POOL_BUF:])):
            lst.append(val)
        proj, qt, kst, vt, wvt, gs, dec, m_s = _sample_in(xs, mods, l, wts, m0_t)
        c_new_t, cq = _sample_state(c0_t, c_new_t, qt, kst, wvt, dec, l, VB_STATE)
        xs, h_s, conv_s, n_s, pool_s = _sample_mix(xs, mods, proj, qt, kst, vt, cq, gs, l, sample_state, wts)
        for lst, val in zip(s_out, (h_s, conv_s, n_s, m_s, pool_s)):
            lst.append(val)
        res = _ffn(xp.reshape(nb * seq, D_MODEL), xs, modp, mods, l, wts["g2"], w1_bf, w2_bf,
                   None if last else (w_ff1, w_ff2), final_row, TM_FFN, seq, last)
        xp, xs = res[0].reshape(nb, seq, D_MODEL), res[1]
        if not last:
            w1_bf, w2_bf = res[2], res[3]

    stack = lambda lst: jnp.stack(lst, axis=0)
    h_s, conv_s, n_s, m_s, pool_s = [stack(v) for v in s_out]
    return (xp, xs.reshape(ns, 1, D_MODEL),
            *[stack(v) for v in p_out],
            h_s,
            jnp.transpose(conv_s, (0, 2, 1, 3)),
            jnp.transpose(c_new_t, (0, 4, 1, 2, 3)),
            jnp.transpose(n_s, (0, 3, 1, 2)),
            jnp.transpose(m_s, (0, 2, 1)),
            jnp.transpose(pool_s, (0, 2, 1, 3)))
```

```python
import functools
import math

import jax
import jax.numpy as jnp
from jax import lax
from jax.experimental import pallas as pl
from jax.experimental.pallas import tpu as pltpu

F32 = jnp.float32
BF16 = jnp.bfloat16

LANES = 128
SUBLANES = 8
MXU_TILE = 256
VMEM_BYTES = 64 * 1024 * 1024

D_MODEL = 1024
LRU_W = 384
LRU_C = 8.0
NH = 4
HD = 96
HP = LANES
MW = NH * HD
MWP = NH * HP
CHUNK = 128
PW = 256
POOL_WINDOWS = (2, 4, 8, 16)
POOL_GROUP = PW // len(POOL_WINDOWS)
POOL_BUF = 15
CONV_TAPS = 4
CONV_HIST = SUBLANES
POOL_HIST = 2 * SUBLANES
D_FF = 4096
EPS = 1e-6
PAST_LEN = 16384

C_XR, C_GR, C_Q, C_K, C_V, C_O, C_U = 0, 384, 768, 1280, 1792, 2304, 2816
N_IN = 3072
G_OFF = HD
K_OUT = LRU_W + MWP + PW

VMEM_LIMIT = VMEM_BYTES - 8 * 1024 * 1024
TM_MIX = 512
PROJ_PIECE = MXU_TILE
TM_FFN = 1024
FFN_CHUNK = 1024
MOD_TN = 1536
VB_STATE = HD


def _log_sigmoid(x):
    return jnp.minimum(x, 0.0) - jnp.log1p(jnp.exp(-jnp.abs(x)))


def _sigmoid(x):
    return 1.0 / (1.0 + jnp.exp(-x))


def _gelu_tanh(x):
    c = math.sqrt(2.0 / math.pi)
    return 0.5 * x * (1.0 + jnp.tanh(c * (x + 0.044715 * (x * x * x))))


def _rms(x, g):
    ms = jnp.mean(x * x, axis=-1, keepdims=True)
    return x * lax.rsqrt(ms + EPS) * g


def _ada_norm(x, g, mod):
    ms = jnp.mean(x * x, axis=-1, keepdims=True)
    gain = g * (1.0 + mod[:, D_MODEL:2 * D_MODEL])
    return x * lax.rsqrt(ms + EPS) * gain + mod[:, 0:D_MODEL]


def _dot(a, b):
    return jnp.dot(a, b, preferred_element_type=F32)


def _dot_nt(a, b):
    return lax.dot_general(a, b, (((1,), (1,)), ((), ())), preferred_element_type=F32)


def _lru_coeffs(gates, log_lam):
    r = _sigmoid(gates[:, :LRU_W])
    i = _sigmoid(gates[:, LRU_W:])
    log_a = r * log_lam
    a = jnp.exp(log_a)
    th = jnp.tanh(log_a)
    mult = jnp.sqrt(-2.0 * th / (1.0 - th))
    return a, mult, i


def _scan_rows(a, b, h_prev):
    n, w = a.shape
    groups = n // SUBLANES
    a3 = a.reshape(groups, SUBLANES, w)
    b3 = b.reshape(groups, SUBLANES, w)
    row = lax.broadcasted_iota(jnp.int32, a3.shape, 1)
    s = 1
    while s < SUBLANES:
        keep = row >= s
        b3 = a3 * jnp.where(keep, pltpu.roll(b3, s, 1), 0.0) + b3
        a3 = a3 * jnp.where(keep, pltpu.roll(a3, s, 1), 1.0)
        s *= 2
    carry = h_prev
    outs = []
    last = SUBLANES - 1
    for g in range(groups):
        outs.append(a3[g] * carry + b3[g])
        carry = a3[g, last:last + 1] * carry + b3[g, last:last + 1]
    return jnp.concatenate(outs, axis=0), carry


def _head_norm_gate(hout, o, ng):
    ms = jnp.sum(hout * hout, axis=-1, keepdims=True) * (1.0 / HD)
    return _sigmoid(o) * (hout * lax.rsqrt(ms + EPS) * ng)


def _pool_by_group(shape, values):
    lane = lax.broadcasted_iota(jnp.int32, shape, 1)
    out = values[-1]
    for g in range(len(values) - 2, -1, -1):
        out = jnp.where(lane < (g + 1) * POOL_GROUP, values[g], out)
    return out


def _pool_select(sums):
    vals = [sums[w] for w in POOL_WINDOWS]
    return _pool_by_group(vals[0].shape, vals)


def _pool_window_lanes(shape):
    return _pool_by_group(shape, list(POOL_WINDOWS))


def _mod_kernel(c_ref, w_ref, b_ref, modp_ref, mods_ref, *, n_prompt):
    c = c_ref[...]
    act = (c * _sigmoid(c)).astype(BF16)
    out = _dot(act, w_ref[...].astype(BF16)) + b_ref[...]
    for r in range(n_prompt):
        modp_ref[r] = out[r:r + 1]
    mods_ref[...] = out[n_prompt:]


def _modulation(c_all, w_ada, b_ada, n_prompt):
    depth = w_ada.shape[0]
    n_all = c_all.shape[0]
    n_mod = w_ada.shape[2]
    tn = MOD_TN
    return pl.pallas_call(
        functools.partial(_mod_kernel, n_prompt=n_prompt),
        grid=(depth, n_mod // tn),
        in_specs=[
            pl.BlockSpec((n_all, D_MODEL), lambda l, j: (0, 0)),
            pl.BlockSpec((None, D_MODEL, tn), lambda l, j: (l, 0, j)),
            pl.BlockSpec((None, 1, tn), lambda l, j: (l, 0, j)),
        ],
        out_specs=[
            pl.BlockSpec((None, n_prompt, 1, tn), lambda l, j: (l, 0, 0, j)),
            pl.BlockSpec((None, n_all - n_prompt, tn), lambda l, j: (l, 0, j)),
        ],
        out_shape=[
            jax.ShapeDtypeStruct((depth, n_prompt, 1, n_mod), F32),
            jax.ShapeDtypeStruct((depth, n_all - n_prompt, n_mod), F32),
        ],
        compiler_params=pltpu.CompilerParams(
            dimension_semantics=("arbitrary", "arbitrary"), vmem_limit_bytes=VMEM_LIMIT),
        name="adaln_modulation",
    )(c_all, w_ada, b_ada)


def _mlstm_tile(proj, gl, gl_t, state, n_chunks, pump):
    pairs = [(c, h) for c in range(n_chunks) for h in range(NH)]
    t_idx = lax.broadcasted_iota(jnp.int32, (CHUNK, CHUNK), 0)
    s_idx = lax.broadcasted_iota(jnp.int32, (CHUNK, CHUNK), 1)
    causal = s_idx <= t_idx
    upper = t_idx <= s_idx
    rows = lambda c: slice(c * CHUNK, (c + 1) * CHUNK)
    head = lambda base, h: slice(base + h * HP, base + (h + 1) * HP)

    bcum_col, dmat, rowmax, ib_col = {}, {}, {}, {}
    for c, h in pairs:
        lf_row = gl_t[NH + h:NH + h + 1, rows(c)]
        lf_col = gl[rows(c), NH + h:NH + h + 1]
        ib_col[c, h] = gl[rows(c), h:h + 1]
        bcum_col[c, h] = jnp.sum(jnp.where(causal, lf_row, 0.0), axis=1, keepdims=True)
        bcum_row = jnp.sum(jnp.where(upper, lf_col, 0.0), axis=0, keepdims=True)
        dmat[c, h] = jnp.where(causal, bcum_col[c, h] - bcum_row + gl_t[h:h + 1, rows(c)], -jnp.inf)
        rowmax[c, h] = jnp.max(dmat[c, h], axis=1, keepdims=True)
    pump(1)

    m_start, m_t = {}, {}
    m_run = [state[h][2] for h in range(NH)]
    for c, h in pairs:
        m_start[c, h] = m_run[h]
        m_t[c, h] = jnp.maximum(bcum_col[c, h] + m_run[h], rowmax[c, h])
        m_run[h] = m_t[c, h][CHUNK - 1:CHUNK, :]
    inter, pmat, floor, decay, w_col = {}, {}, {}, {}, {}
    for c, h in pairs:
        b_last = bcum_col[c, h][CHUNK - 1:CHUNK, :]
        inter[c, h] = jnp.exp(bcum_col[c, h] + m_start[c, h] - m_t[c, h])
        pmat[c, h] = jnp.exp(dmat[c, h] - m_t[c, h])
        floor[c, h] = jnp.exp(-m_t[c, h])
        m_new = m_t[c, h][CHUNK - 1:CHUNK, :]
        decay[c, h] = jnp.exp(b_last + m_start[c, h] - m_new)
        w_col[c, h] = jnp.exp(b_last - bcum_col[c, h] + ib_col[c, h] - m_new)
    pump(1)

    qb, kb, s_mat, sv, wvk, wk = {}, {}, {}, {}, {}, {}
    for c, h in pairs:
        qb[c, h] = proj[rows(c), head(C_Q, h)].astype(BF16)
        ks = proj[rows(c), head(C_K, h)] * (1.0 / math.sqrt(HD))
        kb[c, h] = ks.astype(BF16)
        wk[c, h] = jnp.sum(w_col[c, h] * ks, axis=0, keepdims=True)
        s_mat[c, h] = _dot_nt(qb[c, h], kb[c, h]) * pmat[c, h]
    pump(1)
    for c, h in pairs:
        v = proj[rows(c), head(C_V, h)]
        sv[c, h] = _dot(s_mat[c, h].astype(BF16), v.astype(BF16))
        wvk[c, h] = _dot((w_col[c, h] * v).T.astype(BF16), kb[c, h])
    pump(1)

    outs = [[] for _ in range(NH)]
    c_run = [state[h][0] for h in range(NH)]
    n_run = [state[h][1] for h in range(NH)]
    for c, h in pairs:
        q = proj[rows(c), head(C_Q, h)]
        qc = _dot_nt(qb[c, h], c_run[h].astype(BF16))
        qn = jnp.sum(q * n_run[h], axis=1, keepdims=True)
        num = sv[c, h] + inter[c, h] * qc
        den = jnp.sum(s_mat[c, h], axis=1, keepdims=True) + inter[c, h] * qn
        outs[h].append(num / jnp.maximum(jnp.abs(den), floor[c, h]))
        c_run[h] = decay[c, h] * c_run[h] + wvk[c, h]
        n_run[h] = decay[c, h] * n_run[h] + wk[c, h]
    return outs, [(c_run[h], n_run[h], m_run[h]) for h in range(NH)]


def _project(x, mod, g1, w_in):
    return _dot(_ada_norm(x, g1, mod).astype(BF16), w_in)


def _mix_tile(proj, x, gt1, j, tm, convw_ref, convb_ref, wgate_ref, bgate_ref,
              lam_ref, gbias_ref, ng_ref, poolw_ref, pscale_ref, wout_ref,
              xo_ref, hlast_ref, convo_ref, co_ref, no_ref, mo_ref, poolo_ref,
              h_sc, xbuf, ubuf, c_sc, n_sc, m_sc, pump=lambda n: None):
    row = lax.broadcasted_iota(jnp.int32, (tm, 1), 0)

    xr = proj[:, C_XR:C_XR + LRU_W]
    gr = proj[:, C_GR:C_GR + LRU_W]
    xbuf[CONV_HIST:CONV_HIST + tm, :] = xr
    cw = convw_ref[...]
    xc = convb_ref[...] + cw[CONV_TAPS - 1:CONV_TAPS] * xr
    for back in range(1, CONV_TAPS):
        xc = xc + cw[CONV_TAPS - 1 - back:CONV_TAPS - back] * xbuf[CONV_HIST - back:CONV_HIST - back + tm, :]
    tail = xbuf[tm:tm + CONV_HIST, :]
    xbuf[0:CONV_HIST, :] = tail
    convo_ref[0] = tail
    gates = _dot(xc.astype(BF16), wgate_ref[...]) + bgate_ref[...]
    pump(0)
    log_lam = LRU_C * _log_sigmoid(lam_ref[...])
    h_run = h_sc[...]
    y_a_parts = []
    n_split = 2
    for r in range(n_split):
        rs = slice(r * tm // n_split, (r + 1) * tm // n_split)
        a, mult, gate_i = _lru_coeffs(gates[rs], log_lam)
        if r == 0:
            mult = jnp.where(jnp.logical_and(row[rs] == 0, j == 0), 1.0, mult)
        pump(2)
        hseq, h_run = _scan_rows(a, mult * gate_i * xc[rs], h_run)
        pump(1)
        y_a_parts.append(hseq * _gelu_tanh(gr[rs]))
    h_sc[...] = h_run
    hlast_ref[0] = h_run
    y_a = jnp.concatenate(y_a_parts, axis=0).astype(BF16)

    g_rows = (proj[:, C_Q:C_Q + HP] + gbias_ref[...]).T[G_OFF:G_OFF + 2 * NH]
    row8 = lax.broadcasted_iota(jnp.int32, g_rows.shape, 0)
    gl_t = jnp.where(row8 >= NH, _log_sigmoid(g_rows), g_rows)
    gl = jnp.concatenate([gl_t, jnp.zeros((HP - 2 * NH, tm), F32)], axis=0).T
    ng = ng_ref[...]
    state = [(c_sc[h], n_sc[h:h + 1, :], m_sc[h:h + 1, 0:1]) for h in range(NH)]
    outs, state = _mlstm_tile(proj, gl, gl_t, state, tm // CHUNK, pump)
    y_b_heads = []
    for h in range(NH):
        c_h, n_h, m_h = state[h]
        c_sc[h] = c_h
        n_sc[h:h + 1, :] = n_h
        m_sc[h:h + 1, :] = jnp.broadcast_to(m_h, (1, HP))
        co_ref[0, h] = c_h[0:HD, 0:HD]
        hout_all = jnp.concatenate(outs[h], axis=0) if len(outs[h]) > 1 else outs[h][0]
        o = proj[:, C_O + h * HP:C_O + (h + 1) * HP]
        y_b_heads.append(_head_norm_gate(hout_all, o, ng[:, h * HP:(h + 1) * HP]))
        pump(h % 2)
    no_ref[0] = n_sc[...]
    mo_ref[0] = m_sc[...]
    y_b = jnp.concatenate(y_b_heads, axis=1).astype(BF16)

    u = proj[:, C_U:C_U + PW]
    ubuf[POOL_HIST:POOL_HIST + tm, :] = u
    sums, acc, win = {}, ubuf[...], 1
    while win < POOL_WINDOWS[-1]:
        acc = acc + pltpu.roll(acc, win, 0)
        win *= 2
        sums[win] = acc[POOL_HIST:]
    wsum = _pool_select(sums)
    pos = row + j * tm
    cnt = jnp.minimum(pos + 1, _pool_window_lanes((tm, PW))).astype(F32)
    diff = wsum / cnt - u
    y_c = _dot(diff.astype(BF16), poolw_ref[...]) * pscale_ref[...]
    ptail = ubuf[tm:tm + POOL_HIST, :]
    ubuf[0:POOL_HIST, :] = ptail
    poolo_ref[0] = ptail
    pump(N_IN)

    mix = jnp.concatenate([y_a, y_b, y_c.astype(BF16)], axis=1)
    xo_ref[0] = x[...] + gt1 * _dot(mix, wout_ref[...])


def _prompt_mixer_kernel(x_ref, xn_ref, mod_ref, modn_ref, g1_ref, win_ref, *rest, tm, with_fill):
    n_scratch = 9
    weights_outs = rest[:-n_scratch]
    if with_fill:
        *weights_outs, fill_ref = weights_outs
        fill_ref[...] = jnp.zeros_like(fill_ref)
    h_sc, xbuf, ubuf, c_sc, n_sc, m_sc, proj_a, proj_b, x_keep = rest[-n_scratch:]
    j = pl.program_id(1)
    t = pl.program_id(0) * pl.num_programs(1) + j

    @pl.when(j == 0)
    def _():
        h_sc[...] = jnp.zeros_like(h_sc)
        xbuf[0:CONV_HIST, :] = jnp.zeros((CONV_HIST, LRU_W), F32)
        ubuf[0:POOL_HIST, :] = jnp.zeros((POOL_HIST, PW), F32)
        c_sc[...] = jnp.zeros_like(c_sc)
        n_sc[...] = jnp.zeros_like(n_sc)
        m_sc[...] = jnp.zeros_like(m_sc)

    @pl.when(t == 0)
    def _():
        x_keep[...] = x_ref[0]
        proj_a[...] = _project(x_ref[0], mod_ref[...], g1_ref[...], win_ref[...])

    def step(proj_cur, proj_nxt):
        done = [0]
        lhs = []

        def pump(n):
            if not lhs:
                lhs.append(_ada_norm(xn_ref[0], g1_ref[...], modn_ref[...]).astype(BF16))
            for _ in range(n):
                c0 = done[0]
                if c0 >= N_IN:
                    return
                c1 = min(c0 + PROJ_PIECE, N_IN)
                proj_nxt[:, c0:c1] = _dot(lhs[0], win_ref[:, c0:c1])
                done[0] = c1

        gt1 = mod_ref[:, 2 * D_MODEL:3 * D_MODEL]
        _mix_tile(proj_cur, x_keep, gt1, j, tm, *weights_outs, h_sc, xbuf, ubuf, c_sc, n_sc, m_sc,
                  pump=pump)
        x_keep[...] = xn_ref[0]

    @pl.when(t % 2 == 0)
    def _():
        step(proj_a, proj_b)

    @pl.when(t % 2 == 1)
    def _():
        step(proj_b, proj_a)


def _prompt_mixer(x, modp, l, wts, tm, fill_shape=None):
    nb, seq, _ = x.shape
    n_t = seq // tm

    def next_tile(b, j):
        t1 = jnp.minimum(b * n_t + j + 1, nb * n_t - 1)
        return t1 // n_t, t1 % n_t

    lsel3 = lambda b, j: (l, 0, 0)
    wspec = lambda a: pl.BlockSpec((None,) + a.shape[1:], lsel3, pipeline_mode=pl.Buffered(1))
    names = ("g1", "w_in", "conv_w", "conv_b", "w_gate", "b_gate", "lam", "gbias", "ng", "pool_w",
             "pool_scale", "w_out")
    out_shapes = [
        jax.ShapeDtypeStruct((nb, seq, D_MODEL), F32),
        jax.ShapeDtypeStruct((nb, 1, LRU_W), F32),
        jax.ShapeDtypeStruct((nb, CONV_HIST, LRU_W), F32),
        jax.ShapeDtypeStruct((nb, NH, HD, HD), F32),
        jax.ShapeDtypeStruct((nb, SUBLANES, HP), F32),
        jax.ShapeDtypeStruct((nb, SUBLANES, HP), F32),
        jax.ShapeDtypeStruct((nb, POOL_HIST, PW), F32),
    ]
    out_specs = [
        pl.BlockSpec((1, tm, D_MODEL), lambda b, j: (b, j, 0)),
        pl.BlockSpec((1, 1, LRU_W), lambda b, j: (b, 0, 0)),
        pl.BlockSpec((1, CONV_HIST, LRU_W), lambda b, j: (b, 0, 0)),
        pl.BlockSpec((1, NH, HD, HD), lambda b, j: (b, 0, 0, 0)),
        pl.BlockSpec((1, SUBLANES, HP), lambda b, j: (b, 0, 0)),
        pl.BlockSpec((1, SUBLANES, HP), lambda b, j: (b, 0, 0)),
        pl.BlockSpec((1, POOL_HIST, PW), lambda b, j: (b, 0, 0)),
    ]
    if fill_shape is not None:
        rows, fa, fb = fill_shape
        slab = rows // (nb * n_t)
        assert slab * nb * n_t == rows
        out_shapes.append(jax.ShapeDtypeStruct(fill_shape, F32))
        out_specs.append(pl.BlockSpec((slab, fa, fb), lambda b, j: (b * n_t + j, 0, 0)))
    return pl.pallas_call(
        functools.partial(_prompt_mixer_kernel, tm=tm, with_fill=fill_shape is not None),
        grid=(nb, seq // tm),
        in_specs=[
            pl.BlockSpec((1, tm, D_MODEL), lambda b, j: (0, 0, 0)),
            pl.BlockSpec((1, tm, D_MODEL), lambda b, j: next_tile(b, j) + (0,)),
            pl.BlockSpec((None, None, 1, 3 * D_MODEL), lambda b, j: (l, b, 0, 0)),
            pl.BlockSpec((None, None, 1, 3 * D_MODEL), lambda b, j: (l, next_tile(b, j)[0], 0, 0)),
        ] + [wspec(wts[n]) for n in names],
        out_specs=out_specs,
        out_shape=out_shapes,
        scratch_shapes=[
            pltpu.VMEM((1, LRU_W), F32),
            pltpu.VMEM((CONV_HIST + tm, LRU_W), F32),
            pltpu.VMEM((POOL_HIST + tm, PW), F32),
            pltpu.VMEM((NH, HP, HP), F32),
            pltpu.VMEM((SUBLANES, HP), F32),
            pltpu.VMEM((SUBLANES, HP), F32),
            pltpu.VMEM((tm, N_IN), F32),
            pltpu.VMEM((tm, N_IN), F32),
            pltpu.VMEM((tm, D_MODEL), F32),
        ],
        compiler_params=pltpu.CompilerParams(
            dimension_semantics=("arbitrary", "arbitrary"), vmem_limit_bytes=VMEM_LIMIT),
        name="prompt_mixer",
    )(x, x, modp, modp, *[wts[n] for n in names])


def _ffn_kernel(xp_ref, xs_ref, modp_ref, mods_ref, g2_ref, w1_ref, w2_ref, gf_ref, *rest,
                n_prompt_steps, final):
    if len(rest) == 6:
        w1n_ref, w2n_ref, op_ref, os_ref, w1o_ref, w2o_ref = rest
    else:
        (op_ref, os_ref), w1n_ref = rest, None

    def ffn(x_ref, mod_ref, o_ref):
        x = x_ref[...]
        mod = mod_ref[...]
        gt2 = mod[:, 2 * D_MODEL:3 * D_MODEL]
        h2 = _ada_norm(x, g2_ref[...], mod).astype(BF16)
        acc = jnp.zeros(x.shape, F32)
        step = FFN_CHUNK
        for c in range(D_FF // step):
            hid = _dot(h2, w1_ref[:, c * step:(c + 1) * step])
            hid = jnp.square(jnp.maximum(hid, 0.0)).astype(BF16)
            acc = acc + _dot(hid, w2_ref[c * step:(c + 1) * step, :])
        y = x + gt2 * acc
        if final:
            y = _rms(y, gf_ref[...])
        o_ref[...] = y

    i = pl.program_id(0)

    @pl.when(i < n_prompt_steps)
    def _():
        ffn(xp_ref, modp_ref, op_ref)
        if w1n_ref is not None:
            w1o_ref[...] = w1n_ref[...].astype(BF16)
            w2o_ref[...] = w2n_ref[...].astype(BF16)

    @pl.when(i == n_prompt_steps)
    def _():
        ffn(xs_ref, mods_ref, os_ref)


def _ffn(xp2d, xs, modp, mods, l, g2, w1, w2, next_w, final_g, tm, seq, final):
    n_tok = xp2d.shape[0]
    ns = xs.shape[0]
    n_steps = n_tok // tm
    lsel3 = lambda i: (l, 0, 0)
    wspec = lambda a: pl.BlockSpec(a.shape, lambda i: (0, 0), pipeline_mode=pl.Buffered(1))
    tile = lambda i: jnp.minimum(i, n_steps - 1)
    in_specs = [
        pl.BlockSpec((tm, D_MODEL), lambda i: (tile(i), 0)),
        pl.BlockSpec((ns, D_MODEL), lambda i: (0, 0)),
        pl.BlockSpec((None, None, 1, 3 * D_MODEL), lambda i: (l, tile(i) // (seq // tm), 0, 1)),
        pl.BlockSpec((None, ns, 3 * D_MODEL), lambda i: (l, 0, 1)),
        pl.BlockSpec((None, 1, D_MODEL), lsel3),
        wspec(w1),
        wspec(w2),
        pl.BlockSpec((1, D_MODEL), lambda i: (0, 0)),
    ]
    out_specs = [
        pl.BlockSpec((tm, D_MODEL), lambda i: (tile(i), 0)),
        pl.BlockSpec((ns, D_MODEL), lambda i: (0, 0)),
    ]
    out_shape = [
        jax.ShapeDtypeStruct((n_tok, D_MODEL), F32),
        jax.ShapeDtypeStruct((ns, D_MODEL), F32),
    ]
    args = [xp2d, xs, modp, mods, g2, w1, w2, final_g]
    if next_w is not None:
        slab = D_FF // n_steps
        in_specs += [
            pl.BlockSpec((None, D_MODEL, slab), lambda i: (l + 1, 0, tile(i))),
            pl.BlockSpec((None, slab, D_MODEL), lambda i: (l + 1, tile(i), 0)),
        ]
        out_specs += [
            pl.BlockSpec((D_MODEL, slab), lambda i: (0, tile(i))),
            pl.BlockSpec((slab, D_MODEL), lambda i: (tile(i), 0)),
        ]
        out_shape += [
            jax.ShapeDtypeStruct((D_MODEL, D_FF), BF16),
            jax.ShapeDtypeStruct((D_FF, D_MODEL), BF16),
        ]
        args += list(next_w)
    return pl.pallas_call(
        functools.partial(_ffn_kernel, n_prompt_steps=n_steps, final=final),
        grid=(n_steps + 1,),
        in_specs=in_specs,
        out_specs=out_specs,
        out_shape=out_shape,
        compiler_params=pltpu.CompilerParams(
            dimension_semantics=("arbitrary",), vmem_limit_bytes=VMEM_LIMIT),
        name="ffn",
    )(*args)


def _sample_in_kernel(x_ref, mod_ref, g1_ref, win_ref, gbias_ref, m0_ref,
                      proj_ref, qt_ref, kst_ref, vt_ref, wvt_ref, gs_ref, dec_ref, mo_ref):
    proj = _project(x_ref[...], mod_ref[...], g1_ref[...], win_ref[...])
    proj_ref[...] = proj
    g_t = (proj[:, C_Q:C_Q + HP] + gbias_ref[...]).T
    ib = g_t[G_OFF:G_OFF + NH]
    g = _log_sigmoid(g_t[G_OFF + NH:G_OFF + 2 * NH]) + m0_ref[...]
    m_t = jnp.maximum(g, ib)
    inter = jnp.exp(g - m_t)
    p = jnp.exp(ib - m_t)
    mo_ref[...] = m_t
    gs_ref[0] = inter
    gs_ref[1] = p
    gs_ref[2] = jnp.exp(-m_t)
    q_t = proj[:, C_Q:C_Q + MWP].T
    ks_t = (proj[:, C_K:C_K + MWP] * (1.0 / math.sqrt(HD))).T
    v_t = proj[:, C_V:C_V + MWP].T
    for h in range(NH):
        hs = slice(h * HP, (h + 1) * HP)
        qt_ref[h] = q_t[hs]
        kst_ref[h] = ks_t[hs]
        vt_ref[h] = v_t[hs]
        wvt_ref[h] = p[h:h + 1] * v_t[hs]
        dec_ref[h] = inter[h:h + 1]


def _sample_in(xs, mods, l, wts, m0_t):
    n = xs.shape[0]
    lsel3 = lambda i: (l, 0, 0)
    head_t = jax.ShapeDtypeStruct((NH, HP, n), F32)
    out_shapes = [
        jax.ShapeDtypeStruct((n, N_IN), F32),
        head_t, head_t, head_t, head_t,
        jax.ShapeDtypeStruct((3, NH, n), F32),
        jax.ShapeDtypeStruct((NH, 1, n), F32),
        jax.ShapeDtypeStruct((NH, n), F32),
    ]
    return pl.pallas_call(
        _sample_in_kernel,
        grid=(1,),
        in_specs=[
            pl.BlockSpec((n, D_MODEL), lambda i: (0, 0)),
            pl.BlockSpec((None, n, 2 * D_MODEL), lsel3),
            pl.BlockSpec((None, 1, D_MODEL), lsel3),
            pl.BlockSpec((None, D_MODEL, N_IN), lsel3),
            pl.BlockSpec((None, 1, HP), lsel3),
            pl.BlockSpec((None, NH, n), lsel3),
        ],
        out_specs=[pl.BlockSpec(s.shape, lambda i, nd=len(s.shape): (0,) * nd) for s in out_shapes],
        out_shape=out_shapes,
        compiler_params=pltpu.CompilerParams(
            dimension_semantics=("arbitrary",), vmem_limit_bytes=VMEM_LIMIT),
        name="sample_in_proj",
    )(xs, mods, wts["g1"], wts["w_in"], wts["gbias"], m0_t)


def _sample_state_kernel(c0_ref, qt_ref, kst_ref, wvt_ref, dec_ref, *rest, vb, n_fill):
    cnew_ref, cq_ref = rest[-2:]
    g = pl.program_id(0)

    @pl.when(g < n_fill)
    def _():
        cnew_ref[...] = jnp.zeros_like(cnew_ref)

    @pl.when(g == n_fill)
    def _():
        q = qt_ref[0:HD, :]
        ks = kst_ref[0:HD, :]
        dec = dec_ref[...]
        for vi in range(vb):
            c_old = c0_ref[vi]
            cq_ref[vi:vi + 1, :] = jnp.sum(c_old * q, axis=0, keepdims=True)
            cnew_ref[vi] = dec * c_old + wvt_ref[vi:vi + 1, :] * ks


def _sample_state(c0_t, c_out_prev, qt, kst, wvt, dec, l, vb):
    depth = c0_t.shape[0]
    n = c0_t.shape[-1]
    if c_out_prev is None:
        assert l == 0
        n_fill = depth - 1
    else:
        n_fill = 0
    real = lambda g: g // max(n_fill, 1) if n_fill else 1
    layer = lambda g: (g + 1) % depth if n_fill else l
    head_spec = pl.BlockSpec((None, HP, n), lambda g, h, i: (h, 0, 0))
    in_specs = [
        pl.BlockSpec((None, None, vb, HD, n), lambda g, h, i: (l, h * real(g), i * real(g), 0, 0)),
        head_spec, head_spec,
        pl.BlockSpec((None, vb, n), lambda g, h, i: (h, i, 0)),
        pl.BlockSpec((None, 1, n), lambda g, h, i: (h, 0, 0)),
    ]
    args = [c0_t, qt, kst, wvt, dec]
    aliases = {}
    if c_out_prev is not None:
        in_specs.append(pl.BlockSpec(memory_space=pl.ANY))
        args.append(c_out_prev)
        aliases = {len(args) - 1: 0}
    return pl.pallas_call(
        functools.partial(_sample_state_kernel, vb=vb, n_fill=n_fill),
        grid=(n_fill + 1, NH, HD // vb),
        in_specs=in_specs,
        out_specs=[
            pl.BlockSpec((None, None, vb, HD, n), lambda g, h, i: (layer(g), h, i, 0, 0)),
            pl.BlockSpec((None, vb, n), lambda g, h, i: (h * real(g), i * real(g), 0)),
        ],
        out_shape=[
            jax.ShapeDtypeStruct(c0_t.shape, F32),
            jax.ShapeDtypeStruct((NH, HD, n), F32),
        ],
        input_output_aliases=aliases,
        compiler_params=pltpu.CompilerParams(
            dimension_semantics=("arbitrary", "arbitrary", "arbitrary"), vmem_limit_bytes=VMEM_LIMIT),
        name="sample_matrix_memory",
    )(*args)


def _sample_mix_kernel(x_ref, gt_ref, proj_ref, qt_ref, kst_ref, vt_ref, cq_ref, gs_ref,
                       h0_ref, conv_ref, n0_ref, pool_ref,
                       convw_ref, convb_ref, wgate_ref, bgate_ref, lam_ref, ng_ref,
                       poolw_ref, pscale_ref, wout_ref,
                       xo_ref, ho_ref, convo_ref, no_ref, poolo_ref):
    proj = proj_ref[...]
    n_tok = proj.shape[0]

    xr = proj[:, C_XR:C_XR + LRU_W]
    gr = proj[:, C_GR:C_GR + LRU_W]
    cw = convw_ref[...]
    xc = (convb_ref[...] + cw[3:4] * xr + cw[2:3] * conv_ref[2]
          + cw[1:2] * conv_ref[1] + cw[0:1] * conv_ref[0])
    convo_ref[0] = conv_ref[1]
    convo_ref[1] = conv_ref[2]
    convo_ref[2] = xr
    gates = _dot(xc.astype(BF16), wgate_ref[...]) + bgate_ref[...]
    a, mult, gate_i = _lru_coeffs(gates, LRU_C * _log_sigmoid(lam_ref[...]))
    h_new = a * h0_ref[...] + mult * gate_i * xc
    ho_ref[...] = h_new
    y_a = h_new * _gelu_tanh(gr)

    normed = []
    for h in range(NH):
        q = qt_ref[h, 0:HD, :]
        ks = kst_ref[h, 0:HD, :]
        v = vt_ref[h, 0:HD, :]
        n0 = n0_ref[h]
        inter, p, floor = gs_ref[0, h:h + 1, :], gs_ref[1, h:h + 1, :], gs_ref[2, h:h + 1, :]
        s = jnp.sum(q * ks, axis=0, keepdims=True) * p
        num = s * v + inter * cq_ref[h]
        den = s + inter * jnp.sum(n0 * q, axis=0, keepdims=True)
        hout = num / jnp.maximum(jnp.abs(den), floor)
        no_ref[h] = inter * n0 + p * ks
        ms = jnp.sum(hout * hout, axis=0, keepdims=True) * (1.0 / HD)
        normed.append(hout * lax.rsqrt(ms + EPS))
        normed.append(jnp.zeros((HP - HD, n_tok), F32))
    hm = jnp.concatenate(normed, axis=0).T
    y_b = _sigmoid(proj[:, C_O:C_O + MWP]) * (hm * ng_ref[...])

    u = proj[:, C_U:C_U + PW]
    sums, acc, nxt = {}, u, 1
    for win in POOL_WINDOWS:
        while nxt < win:
            acc = acc + pool_ref[POOL_BUF - nxt]
            nxt += 1
        sums[win] = acc
    wsum = _pool_select(sums)
    cnt = jnp.minimum(PAST_LEN + 1, _pool_window_lanes(u.shape)).astype(F32)
    diff = wsum / cnt - u
    y_c = _dot(diff.astype(BF16), poolw_ref[...]) * pscale_ref[...]
    for r in range(POOL_BUF - 1):
        poolo_ref[r] = pool_ref[r + 1]
    poolo_ref[POOL_BUF - 1] = u

    mix = jnp.concatenate([y_a, y_b, y_c], axis=1).astype(BF16)
    xo_ref[...] = x_ref[...] + gt_ref[...] * _dot(mix, wout_ref[...])


def _sample_mix(xs, mods, proj, qt, kst, vt, cq, gs, l, st, wts):
    n = xs.shape[0]
    zeros = lambda nd: (lambda i: (0,) * nd)
    full = lambda a: pl.BlockSpec(a.shape, zeros(a.ndim))
    lspec = lambda a: pl.BlockSpec((None,) + a.shape[1:], lambda i, nd=a.ndim: (l,) + (0,) * (nd - 1))
    snames = ("lru_h", "conv", "n", "pool")
    wnames = ("conv_w", "conv_b", "w_gate", "b_gate", "lam", "ng", "pool_w", "pool_scale", "w_out")
    out_shapes = [
        jax.ShapeDtypeStruct((n, D_MODEL), F32),
        jax.ShapeDtypeStruct((n, LRU_W), F32),
        jax.ShapeDtypeStruct((3, n, LRU_W), F32),
        jax.ShapeDtypeStruct((NH, HD, n), F32),
        jax.ShapeDtypeStruct((POOL_BUF, n, PW), F32),
    ]
    return pl.pallas_call(
        _sample_mix_kernel,
        grid=(1,),
        in_specs=[
            full(xs),
            pl.BlockSpec((None, n, D_MODEL), lambda i: (l, 0, 2)),
            full(proj), full(qt), full(kst), full(vt), full(cq), full(gs),
        ] + [lspec(st[k]) for k in snames] + [lspec(wts[k]) for k in wnames],
        out_specs=[pl.BlockSpec(s.shape, zeros(len(s.shape))) for s in out_shapes],
        out_shape=out_shapes,
        compiler_params=pltpu.CompilerParams(
            dimension_semantics=("arbitrary",), vmem_limit_bytes=VMEM_LIMIT),
        name="sample_mixer",
    )(xs, mods, proj, qt, kst, vt, cq, gs, *[st[k] for k in snames], *[wts[k] for k in wnames])


def _pad_heads(w):
    lead = w.shape[:-1]
    w = w.reshape(lead + (NH, HD))
    w = jnp.pad(w, [(0, 0)] * len(lead) + [(0, 0), (0, HP - HD)])
    return w.reshape(lead + (MWP,))


def _block_diag(w):
    depth, g, n, _ = w.shape
    eye = jnp.eye(g, dtype=w.dtype)
    return (w[:, :, :, None, :] * eye[None, :, None, :, None]).reshape(depth, g * n, g * n)


def _permute_kernel(wint_ref, wout_ref, winp_ref, woutp_ref):
    o0 = 2 * LRU_W
    g0 = o0 + 4 * MW

    def put(dst_col, parts):
        blocks = [wint_ref[r0:r0 + n, :] for r0, n in parts]
        n_rows = sum(n for _, n in parts)
        if n_rows < HP:
            blocks.append(jnp.zeros((HP - n_rows, D_MODEL), F32))
        blk = jnp.concatenate(blocks, axis=0) if len(blocks) > 1 else blocks[0]
        winp_ref[:, dst_col:dst_col + HP] = blk.T.astype(BF16)

    for b in range(o0 // HP):
        put(b * HP, [(b * HP, HP)])
    for k, dst in enumerate((C_Q, C_K, C_V, C_O)):
        for h in range(NH):
            parts = [(o0 + k * MW + h * HD, HD)]
            if k == 0 and h == 0:
                parts.append((g0, 2 * NH))
            put(dst + h * HP, parts)
    for b in range(PW // HP):
        put(C_U + b * HP, [(g0 + 2 * NH + b * HP, HP)])

    woutp_ref[0:LRU_W, :] = wout_ref[0:LRU_W, :].astype(BF16)
    for h in range(NH):
        r0 = LRU_W + h * HP
        woutp_ref[r0:r0 + HD, :] = wout_ref[LRU_W + h * HD:LRU_W + (h + 1) * HD, :].astype(BF16)
        woutp_ref[r0 + HD:r0 + HP, :] = jnp.zeros((HP - HD, D_MODEL), BF16)
    woutp_ref[LRU_W + MWP:K_OUT, :] = wout_ref[LRU_W + MW:D_MODEL, :].astype(BF16)


def _permute_projections(w_in, w_out):
    depth, _, in_cols = w_in.shape
    lsel3 = lambda l: (l, 0, 0)
    return pl.pallas_call(
        _permute_kernel,
        grid=(depth,),
        in_specs=[
            pl.BlockSpec((None, in_cols, D_MODEL), lsel3),
            pl.BlockSpec((None, D_MODEL, D_MODEL), lsel3),
        ],
        out_specs=[
            pl.BlockSpec((None, D_MODEL, N_IN), lsel3),
            pl.BlockSpec((None, K_OUT, D_MODEL), lsel3),
        ],
        out_shape=[
            jax.ShapeDtypeStruct((depth, D_MODEL, N_IN), BF16),
            jax.ShapeDtypeStruct((depth, K_OUT, D_MODEL), BF16),
        ],
        compiler_params=pltpu.CompilerParams(
            dimension_semantics=("arbitrary",), vmem_limit_bytes=VMEM_LIMIT),
        name="permute_projections",
    )(jnp.transpose(w_in, (0, 2, 1)), w_out)


def _prepare_weights(norm1_g, norm2_g, w_in, conv_w, conv_b, lru_wa, lru_ba, lru_wx, lru_bx, lru_lam,
                     mlstm_bi, mlstm_bf, mlstm_norm_g, pool_w, pool_scale, w_out):
    depth = w_in.shape[0]
    w_in_p, w_out_p = _permute_projections(w_in, w_out)
    row = lambda v: v.reshape(depth, 1, -1)
    return {
        "g1": row(norm1_g), "g2": row(norm2_g),
        "w_in": w_in_p, "w_out": w_out_p,
        "conv_w": conv_w, "conv_b": row(conv_b),
        "w_gate": jnp.concatenate([_block_diag(lru_wa), _block_diag(lru_wx)], axis=-1).astype(BF16),
        "b_gate": jnp.concatenate([row(lru_ba), row(lru_bx)], axis=-1),
        "lam": row(lru_lam),
        "gbias": jnp.pad(jnp.concatenate([mlstm_bi, mlstm_bf], axis=-1),
                         ((0, 0), (G_OFF, HP - G_OFF - 2 * NH)))[:, None, :],
        "ng": row(_pad_heads(mlstm_norm_g.reshape(depth, MW))),
        "pool_w": _block_diag(pool_w).astype(BF16),
        "pool_scale": row(pool_scale),
    }


def kernel(x_prompt, x_sample, state_lru_h, state_lru_conv, state_mlstm_C, state_mlstm_n, state_mlstm_m,
           state_pool, c_prompt, c_sample, norm1_g, norm2_g, w_ada, b_ada, w_in, conv_w, conv_b, lru_wa,
           lru_ba, lru_wx, lru_bx, lru_lam, mlstm_bi, mlstm_bf, mlstm_norm_g, pool_w, pool_scale, w_out,
           w_ff1, w_ff2, final_g):
    nb, seq, _ = x_prompt.shape
    ns = x_sample.shape[0]
    depth = w_in.shape[0]
    wts = _prepare_weights(norm1_g, norm2_g, w_in, conv_w, conv_b, lru_wa, lru_ba, lru_wx, lru_bx, lru_lam,
                           mlstm_bi, mlstm_bf, mlstm_norm_g, pool_w, pool_scale, w_out)
    w1_bf, w2_bf = w_ff1[0].astype(BF16), w_ff2[0].astype(BF16)
    final_row = final_g.reshape(1, D_MODEL)
    modp, mods = _modulation(jnp.concatenate([c_prompt, c_sample], axis=0), w_ada,
                             b_ada.reshape(depth, 1, -1), nb)

    sample_state = {
        "lru_h": state_lru_h,
        "conv": jnp.transpose(state_lru_conv, (0, 2, 1, 3)),
        "n": jnp.transpose(state_mlstm_n, (0, 2, 3, 1)),
        "pool": jnp.transpose(state_pool, (0, 2, 1, 3)),
    }
    m0_t = jnp.transpose(state_mlstm_m, (0, 2, 1))
    c0_t = jnp.transpose(state_mlstm_C, (0, 2, 3, 4, 1))
    c_new_t = None

    xp = x_prompt
    xs = x_sample.reshape(ns, D_MODEL)
    p_out = [[] for _ in range(6)]
    s_out = [[] for _ in range(5)]
    for l in range(depth):
        last = l == depth - 1
        if l == 0:
            *mix_out, c_new_t = _prompt_mixer(xp, modp, l, wts, TM_MIX, (depth * NH * HD, HD, ns))
            c_new_t = c_new_t.reshape(c0_t.shape)
        else:
            mix_out = _prompt_mixer(xp, modp, l, wts, TM_MIX)
        xp, h_l, conv_l, c_l, n_l, m_l, pool_l = mix_out
        for lst, val in zip(p_out, (h_l.reshape(nb, LRU_W), conv_l[:, CONV_HIST - (CONV_TAPS - 1):], c_l, n_l[:, :NH, :HD],
                                    m_l[:, :NH, 0], pool_l[:, POOL_HIST - POOL_BUF:])):
            lst.append(val)
        proj, qt, kst, vt, wvt, gs, dec, m_s = _sample_in(xs, mods, l, wts, m0_t)
        c_new_t, cq = _sample_state(c0_t, c_new_t, qt, kst, wvt, dec, l, VB_STATE)
        xs, h_s, conv_s, n_s, pool_s = _sample_mix(xs, mods, proj, qt, kst, vt, cq, gs, l, sample_state, wts)
        for lst, val in zip(s_out, (h_s, conv_s, n_s, m_s, pool_s)):
            lst.append(val)
        res = _ffn(xp.reshape(nb * seq, D_MODEL), xs, modp, mods, l, wts["g2"], w1_bf, w2_bf,
                   None if last else (w_ff1, w_ff2), final_row, TM_FFN, seq, last)
        xp, xs = res[0].reshape(nb, seq, D_MODEL), res[1]
        if not last:
            w1_bf, w2_bf = res[2], res[3]

    stack = lambda lst: jnp.stack(lst, axis=0)
    h_s, conv_s, n_s, m_s, pool_s = [stack(v) for v in s_out]
    return (xp, xs.reshape(ns, 1, D_MODEL),
            *[stack(v) for v in p_out],
            h_s,
            jnp.transpose(conv_s, (0, 2, 1, 3)),
            jnp.transpose(c_new_t, (0, 4, 1, 2, 3)),
            jnp.transpose(n_s, (0, 3, 1, 2)),
            jnp.transpose(m_s, (0, 2, 1)),
            jnp.transpose(pool_s, (0, 2, 1, 3)))
```
